```python
import math
import jax, jax.numpy as jnp
from jax import lax
import numpy as np

D_MODEL = 1024
BATCH = 4
SEQ = 8192
DEPTH = 2

GRID_W = 64
CTX_LEN = 256
EPS = 1e-6
ROPE_THETA = 10000.0
Q_BLOCK = 128
HEAD_DIM = 64
A_HEADS = 6
A_KV_HEADS = 2
A_GROUP = A_HEADS // A_KV_HEADS
A_Q = A_HEADS * HEAD_DIM
A_KV = A_KV_HEADS * HEAD_DIM
B_HEADS = 6
B_SUB = 32
B_VDIM = 2 * B_SUB
B_QK = B_HEADS * 2 * B_SUB
B_V = B_HEADS * B_VDIM
C_GROUPS = 16
C_GROUP_CH = 16
C_WIDTH = C_GROUPS * C_GROUP_CH
C_STATE = 64
IN_WIDTH = A_Q + 2 * A_KV + 2 * B_QK + B_V + C_WIDTH
MIX_WIDTH = A_Q + B_V + C_WIDTH
PEER_HEADS = 8
N_KEYS = 128
N_EXPERTS = N_KEYS * N_KEYS
PEER_TOPK = 16
PEER_HALF = 128
TOKEN_CHUNK = 128

kernel_name = 'hybrid_gqa_diffattn_s5_peer_dit'


def _rms(x, g):
    xf = x.astype(jnp.float32)
    y = xf * lax.rsqrt(jnp.mean(xf * xf, axis=-1, keepdims=True) + EPS)
    return (y * g.astype(jnp.float32)).astype(x.dtype)


def _modulate(h, shift, scale):
    return h * (1 + scale) + shift


def _rope_2d(x, row, col):
    d = x.shape[-1]
    half = d // 2
    inv = 1.0 / (ROPE_THETA ** (jnp.arange(0, half, 2, dtype=jnp.float32) / half))
    ang = jnp.concatenate([row.astype(jnp.float32)[:, None] * inv[None, :],
                           col.astype(jnp.float32)[:, None] * inv[None, :]], axis=-1)
    cos = jnp.cos(ang)[None, :, None, :]
    sin = jnp.sin(ang)[None, :, None, :]
    xr = x.astype(jnp.float32).reshape(x.shape[:-1] + (half, 2))
    x0, x1 = xr[..., 0], xr[..., 1]
    out = jnp.stack([x0 * cos - x1 * sin, x0 * sin + x1 * cos], axis=-1)
    return out.reshape(x.shape).astype(x.dtype)


def _gqa_block(q, k, v):
    s = jnp.einsum('bqkgd,bskd->bkgqs', q, k).astype(jnp.float32) * (HEAD_DIM ** -0.5)
    p = jax.nn.softmax(s, axis=-1).astype(v.dtype)
    return jnp.einsum('bkgqs,bskd->bqkgd', p, v)


def _diff_block(q, k, v, lam):
    s = jnp.einsum('bqhtd,bshtd->bthqs', q, k).astype(jnp.float32) * (B_SUB ** -0.5)
    p = jax.nn.softmax(s, axis=-1)
    a = (p[:, 0] - lam * p[:, 1]).astype(v.dtype)
    return jnp.einsum('bhqs,bshd->bqhd', a, v)


def _sweep(block_fn, qs, *kv):
    B, L = qs[0].shape[:2]
    nb = L // Q_BLOCK
    qb = tuple(jnp.moveaxis(q.reshape((B, nb, Q_BLOCK) + q.shape[2:]), 1, 0) for q in qs)
    out = lax.map(lambda xs: block_fn(*xs, *kv), qb)
    return jnp.moveaxis(out, 0, 1).reshape((B, L) + out.shape[3:])


def _scan_combine(e1, e2):
    a1, b1 = e1
    a2, b2 = e2
    return a2 * a1, a2 * b1 + b2


def _lin_scan(bu, abar, reverse):
    if reverse:
        bu = jnp.flip(bu, axis=1)
    a = jnp.broadcast_to(abar, (1, bu.shape[1]) + abar.shape)
    _, h = lax.associative_scan(_scan_combine, (a, bu), axis=1)
    return jnp.flip(h, axis=1) if reverse else h


def _s5_mixer(u, uc, a_re, a_im, log_dt, b_re, b_im, c_re, c_im, d_skip, w_glu, with_ctx):
    f32 = jnp.float32
    B, L, _ = u.shape
    Lc = uc.shape[1]
    ug = u.astype(f32).reshape(B, L, C_GROUPS, C_GROUP_CH).astype(jnp.complex64)
    ucg = uc.astype(f32).reshape(B, Lc, C_GROUPS, C_GROUP_CH).astype(jnp.complex64)
    ys, ycs = [], []
    for dirn in range(2):
        rev = dirn == 1
        A = lax.complex(a_re[dirn].astype(f32), a_im[dirn].astype(f32))
        adt = A * jnp.exp(log_dt[dirn].astype(f32))[:, None]
        abar = jnp.exp(adt)
        bbar = ((abar - 1) / A)[..., None] * lax.complex(b_re[dirn].astype(f32), b_im[dirn].astype(f32))
        cmat = lax.complex(c_re[dirn].astype(f32), c_im[dirn].astype(f32))
        hc = _lin_scan(jnp.einsum('blgp,gnp->blgn', ucg, bbar), abar, rev)
        h0 = hc[:, 0] if rev else hc[:, -1]
        steps = (L - jnp.arange(L)) if rev else (jnp.arange(L) + 1)
        decay = jnp.exp(adt[None] * steps.astype(f32)[:, None, None])
        h = _lin_scan(jnp.einsum('blgp,gnp->blgn', ug, bbar), abar, rev) + decay[None] * h0[:, None]
        ys.append(jnp.einsum('blgn,gpn->blgp', h, cmat).real)
        if with_ctx:
            ycs.append(jnp.einsum('blgn,gpn->blgp', hc, cmat).real)

    def _out(y_dirs, inp, n):
        y = (y_dirs[0] + y_dirs[1]).reshape(B, n, C_WIDTH) + d_skip.astype(f32) * inp.astype(f32)
        g = jax.nn.gelu(y)
        return (g * jax.nn.sigmoid(g @ w_glu.astype(f32))).astype(inp.dtype)

    y_lat = _out(ys, u, L)
    y_ctx = _out(ycs, uc, Lc) if with_ctx else None
    return y_lat, y_ctx


def _token_mixers(h, hc, row, col, layer_idx, w_in, qn_a, kn_a, qn_b, kn_b,
                  lam_q1, lam_k1, lam_q2, lam_k2, subln_g,
                  a_re, a_im, log_dt, b_re, b_im, c_re, c_im, d_skip, w_glu, with_ctx):
    B, L, _ = h.shape
    Lc = hc.shape[1]
    cuts = [A_Q, A_Q + A_KV, A_Q + 2 * A_KV, A_Q + 2 * A_KV + B_QK,
            A_Q + 2 * A_KV + 2 * B_QK, A_Q + 2 * A_KV + 2 * B_QK + B_V]
    qa, ka, va, qb, kb, vb, us = jnp.split(h @ w_in, cuts, axis=-1)
    qac, kac, vac, qbc, kbc, vbc, usc = jnp.split(hc @ w_in, cuts, axis=-1)

    qa = _rope_2d(_rms(qa.reshape(B, L, A_HEADS, HEAD_DIM), qn_a), row, col)
    qa = qa.reshape(B, L, A_KV_HEADS, A_GROUP, HEAD_DIM)
    ka = _rope_2d(_rms(ka.reshape(B, L, A_KV_HEADS, HEAD_DIM), kn_a), row, col)
    va = va.reshape(B, L, A_KV_HEADS, HEAD_DIM)
    kac = _rms(kac.reshape(B, Lc, A_KV_HEADS, HEAD_DIM), kn_a)
    vac = vac.reshape(B, Lc, A_KV_HEADS, HEAD_DIM)
    ka_all = jnp.concatenate([kac, ka], axis=1)
    va_all = jnp.concatenate([vac, va], axis=1)
    o_a = _sweep(_gqa_block, (qa,), ka_all, va_all).reshape(B, L, A_Q)

    lambda_init = 0.8 - 0.6 * math.exp(-0.3 * layer_idx)
    lam = (jnp.exp(jnp.sum(lam_q1.astype(jnp.float32) * lam_k1.astype(jnp.float32)))
           - jnp.exp(jnp.sum(lam_q2.astype(jnp.float32) * lam_k2.astype(jnp.float32))) + lambda_init)
    qb = _rope_2d(_rms(qb.reshape(B, L, 2 * B_HEADS, B_SUB), qn_b), row, col).reshape(B, L, B_HEADS, 2, B_SUB)
    kb = _rope_2d(_rms(kb.reshape(B, L, 2 * B_HEADS, B_SUB), kn_b), row, col).reshape(B, L, B_HEADS, 2, B_SUB)
    vb = vb.reshape(B, L, B_HEADS, B_VDIM)
    kbc = _rms(kbc.reshape(B, Lc, 2 * B_HEADS, B_SUB), kn_b).reshape(B, Lc, B_HEADS, 2, B_SUB)
    vbc = vbc.reshape(B, Lc, B_HEADS, B_VDIM)
    kb_all = jnp.concatenate([kbc, kb], axis=1)
    vb_all = jnp.concatenate([vbc, vb], axis=1)
    o_b = _sweep(_diff_block, (qb,), kb_all, vb_all, lam)
    o_b = (_rms(o_b, subln_g) * (1 - lambda_init)).reshape(B, L, B_V)

    o_c, o_cc = _s5_mixer(us, usc, a_re, a_im, log_dt, b_re, b_im, c_re, c_im, d_skip, w_glu, with_ctx)

    mix = jnp.concatenate([o_a, o_b, o_c], axis=-1)
    if not with_ctx:
        return mix, None
    qac = _rms(qac.reshape(B, Lc, A_HEADS, HEAD_DIM), qn_a).reshape(B, Lc, A_KV_HEADS, A_GROUP, HEAD_DIM)
    o_ac = _gqa_block(qac, kac, vac).reshape(B, Lc, A_Q)
    qbc = _rms(qbc.reshape(B, Lc, 2 * B_HEADS, B_SUB), qn_b).reshape(B, Lc, B_HEADS, 2, B_SUB)
    o_bc = _diff_block(qbc, kbc, vbc, lam)
    o_bc = (_rms(o_bc, subln_g) * (1 - lambda_init)).reshape(B, Lc, B_V)
    mixc = jnp.concatenate([o_ac, o_bc, o_cc], axis=-1)
    return mix, mixc


def _peer(h, w_q, sub_keys, u_emb, v_emb):
    T, D = h.shape

    def chunk(hc):
        q = (hc @ w_q).reshape(TOKEN_CHUNK, PEER_HEADS, 2, PEER_HALF)
        s = jnp.einsum('chtd,htnd->chtn', q, sub_keys).astype(jnp.float32)
        s1, i1 = lax.top_k(s[:, :, 0], PEER_TOPK)
        s2, i2 = lax.top_k(s[:, :, 1], PEER_TOPK)
        cand = (s1[..., :, None] + s2[..., None, :]).reshape(TOKEN_CHUNK, PEER_HEADS, PEER_TOPK * PEER_TOPK)
        sc, ci = lax.top_k(cand, PEER_TOPK)
        idx = (jnp.take_along_axis(i1, ci // PEER_TOPK, axis=-1) * N_KEYS
               + jnp.take_along_axis(i2, ci % PEER_TOPK, axis=-1))
        g = jax.nn.softmax(sc, axis=-1)
        u = jnp.take(u_emb, idx, axis=0)
        act = jax.nn.gelu(jnp.einsum('chkd,cd->chk', u, hc).astype(jnp.float32))
        w = (g * act).astype(hc.dtype)
        return jnp.einsum('chk,chkd->cd', w, jnp.take(v_emb, idx, axis=0))

    out = lax.map(chunk, h.reshape(T // TOKEN_CHUNK, TOKEN_CHUNK, D))
    return out.reshape(T, D)


def setup_inputs(seed: int = 0) -> dict:
    key = jax.random.key(seed)
    ks = iter(jax.random.split(key, 40))
    f32 = jnp.float32

    def nrm(shape, s):
        return jax.random.normal(next(ks), shape, f32) * s

    D = D_MODEL
    x = nrm((BATCH, SEQ, D), 1.0)
    c = nrm((BATCH, D), 1.0)
    ctx = nrm((BATCH, CTX_LEN, D), 1.0)
    c_ctx = nrm((D,), 1.0)
    w_ada = nrm((DEPTH, D, 6 * D), D ** -0.5)
    b_ada = nrm((DEPTH, 6 * D), 0.01)
    norm1_g = 1.0 + nrm((DEPTH, D), 0.02)
    norm2_g = 1.0 + nrm((DEPTH, D), 0.02)
    w_in = nrm((DEPTH, D, IN_WIDTH), D ** -0.5)
    w_out = nrm((DEPTH, MIX_WIDTH, D), MIX_WIDTH ** -0.5)
    qn_a = 1.0 + nrm((DEPTH, HEAD_DIM), 0.02)
    kn_a = 1.0 + nrm((DEPTH, HEAD_DIM), 0.02)
    qn_b = 1.0 + nrm((DEPTH, B_SUB), 0.02)
    kn_b = 1.0 + nrm((DEPTH, B_SUB), 0.02)
    lam_q1 = nrm((DEPTH, B_SUB), 0.1)
    lam_k1 = nrm((DEPTH, B_SUB), 0.1)
    lam_q2 = nrm((DEPTH, B_SUB), 0.1)
    lam_k2 = nrm((DEPTH, B_SUB), 0.1)
    subln_g = 1.0 + nrm((DEPTH, B_VDIM), 0.02)
    n_idx = jnp.arange(C_STATE, dtype=f32)
    s5_a_re = -0.5 + nrm((DEPTH, 2, C_GROUPS, C_STATE), 0.01)
    s5_a_im = math.pi * n_idx + nrm((DEPTH, 2, C_GROUPS, C_STATE), 0.01)
    s5_log_dt = jax.random.uniform(next(ks), (DEPTH, 2, C_GROUPS), f32, math.log(1e-3), math.log(1e-1))
    s5_b_re = nrm((DEPTH, 2, C_GROUPS, C_STATE, C_GROUP_CH), (2 * C_GROUP_CH) ** -0.5)
    s5_b_im = nrm((DEPTH, 2, C_GROUPS, C_STATE, C_GROUP_CH), (2 * C_GROUP_CH) ** -0.5)
    s5_c_re = nrm((DEPTH, 2, C_GROUPS, C_GROUP_CH, C_STATE), C_STATE ** -0.5)
    s5_c_im = nrm((DEPTH, 2, C_GROUPS, C_GROUP_CH, C_STATE), C_STATE ** -0.5)
    s5_d = nrm((DEPTH, C_WIDTH), 1.0)
    s5_w_glu = nrm((DEPTH, C_WIDTH, C_WIDTH), C_WIDTH ** -0.5)
    peer_wq = nrm((DEPTH, D, PEER_HEADS * 2 * PEER_HALF), D ** -0.5)
    peer_keys = nrm((DEPTH, PEER_HEADS, 2, N_KEYS, PEER_HALF), PEER_HALF ** -0.5)
    peer_u = nrm((DEPTH, N_EXPERTS, D), D ** -0.5)
    peer_v = nrm((DEPTH, N_EXPERTS, D), 0.5)
    return {'x': x, 'c': c, 'ctx': ctx, 'c_ctx': c_ctx, 'w_ada': w_ada, 'b_ada': b_ada,
            'norm1_g': norm1_g, 'norm2_g': norm2_g, 'w_in': w_in, 'w_out': w_out,
            'qn_a': qn_a, 'kn_a': kn_a, 'qn_b': qn_b, 'kn_b': kn_b,
            'lam_q1': lam_q1, 'lam_k1': lam_k1, 'lam_q2': lam_q2, 'lam_k2': lam_k2, 'subln_g': subln_g,
            's5_a_re': s5_a_re, 's5_a_im': s5_a_im, 's5_log_dt': s5_log_dt,
            's5_b_re': s5_b_re, 's5_b_im': s5_b_im, 's5_c_re': s5_c_re, 's5_c_im': s5_c_im,
            's5_d': s5_d, 's5_w_glu': s5_w_glu,
            'peer_wq': peer_wq, 'peer_keys': peer_keys, 'peer_u': peer_u, 'peer_v': peer_v}


def reference(x, c, ctx, c_ctx, w_ada, b_ada, norm1_g, norm2_g, w_in, w_out,
              qn_a, kn_a, qn_b, kn_b, lam_q1, lam_k1, lam_q2, lam_k2, subln_g,
              s5_a_re, s5_a_im, s5_log_dt, s5_b_re, s5_b_im, s5_c_re, s5_c_im,
              s5_d, s5_w_glu, peer_wq, peer_keys, peer_u, peer_v):
    B, L, D = x.shape
    Lc = ctx.shape[1]
    ROWS = L // GRID_W
    row = jnp.repeat(jnp.arange(ROWS, dtype=jnp.int32), GRID_W)
    col = jnp.tile(jnp.arange(GRID_W, dtype=jnp.int32), ROWS)
    s_c = jax.nn.silu(c)
    s_cc = jax.nn.silu(c_ctx)
    for l in range(DEPTH):
        with_ctx = l < DEPTH - 1
        sh1, sc1, g1, sh2, sc2, g2 = jnp.split((s_c @ w_ada[l] + b_ada[l])[:, None, :], 6, axis=-1)
        csh1, csc1, cg1, csh2, csc2, cg2 = jnp.split(s_cc @ w_ada[l] + b_ada[l], 6, axis=-1)

        h = _modulate(_rms(x, norm1_g[l]), sh1, sc1)
        hc = _modulate(_rms(ctx, norm1_g[l]), csh1, csc1)
        mix, mixc = _token_mixers(h, hc, row, col, l, w_in[l], qn_a[l], kn_a[l], qn_b[l], kn_b[l],
                                  lam_q1[l], lam_k1[l], lam_q2[l], lam_k2[l], subln_g[l],
                                  s5_a_re[l], s5_a_im[l], s5_log_dt[l], s5_b_re[l], s5_b_im[l],
                                  s5_c_re[l], s5_c_im[l], s5_d[l], s5_w_glu[l], with_ctx)
        x = x + g1 * (mix @ w_out[l])

        h2 = _modulate(_rms(x, norm2_g[l]), sh2, sc2)
        x = x + g2 * _peer(h2.reshape(B * L, D), peer_wq[l], peer_keys[l], peer_u[l], peer_v[l]).reshape(B, L, D)

        if with_ctx:
            ctx = ctx + cg1 * (mixc @ w_out[l])
            h2c = _modulate(_rms(ctx, norm2_g[l]), csh2, csc2)
            ctx = ctx + cg2 * _peer(h2c.reshape(B * Lc, D), peer_wq[l], peer_keys[l], peer_u[l], peer_v[l]).reshape(B, Lc, D)
    return x
```

```python
import functools
import math

import jax
import jax.numpy as jnp
from jax import lax
from jax.experimental import pallas as pl
from jax.experimental.pallas import tpu as pltpu

D_MODEL = 1024
GRID_W = 64
EPS = 1e-6
ROPE_THETA = 10000.0
HEAD_DIM = 64
A_HEADS = 6
A_KV_HEADS = 2
A_GROUP = A_HEADS // A_KV_HEADS
A_Q = A_HEADS * HEAD_DIM
A_KV = A_KV_HEADS * HEAD_DIM
B_HEADS = 6
B_SUB = 32
B_VDIM = 2 * B_SUB
B_QK = B_HEADS * 2 * B_SUB
B_V = B_HEADS * B_VDIM
C_GROUPS = 16
C_GROUP_CH = 16
C_WIDTH = C_GROUPS * C_GROUP_CH
C_STATE = 64
PEER_HEADS = 8
N_KEYS = 128
PEER_TOPK = 16
PEER_HALF = 128
TOKEN_CHUNK = 128

LANES = 128
V_EXT = LANES
LOG2E = math.log2(math.e)
NEG_BIG = -1e30
VMEM_LIMIT = 48 * 1024 * 1024


def _online_softmax_step(q, k_t, v, m_ref, acc_ref):
    s = jnp.dot(q, k_t, preferred_element_type=jnp.float32)
    m_old = m_ref[...]
    m_new = jnp.maximum(m_old, jnp.max(s, axis=-1, keepdims=True))
    alpha = jnp.exp2(m_old - m_new)
    p = jnp.exp2(s - m_new).astype(jnp.bfloat16)
    acc_ref[...] = alpha * acc_ref[...] + jnp.dot(p, v, preferred_element_type=jnp.float32)
    m_ref[...] = m_new


def _attend(q, kt_at, v_at, m_ref, acc_ref, n_keys, tk):
    m_ref[...] = jnp.full(m_ref.shape, NEG_BIG, jnp.float32)
    acc_ref[...] = jnp.zeros(acc_ref.shape, jnp.float32)

    def body(j, carry):
        off = pl.multiple_of(j * tk, tk)
        _online_softmax_step(q, kt_at(off), v_at(off), m_ref, acc_ref)
        return carry

    lax.fori_loop(0, n_keys // tk, body, 0)
    return acc_ref[...]


def _gqa_kernel(q_ref, kt_ref, v_ref, o_ref, m_ref, acc_ref, *, tq, tk, n_keys):
    outs = []
    for kvh in range(A_KV_HEADS):
        q = jnp.concatenate(
            [q_ref[0, :, (kvh * A_GROUP + g) * HEAD_DIM:(kvh * A_GROUP + g + 1) * HEAD_DIM]
             for g in range(A_GROUP)], axis=0)
        acc = _attend(q,
                      lambda off: kt_ref[0, kvh, :, pl.ds(off, tk)],
                      lambda off: v_ref[0, kvh, pl.ds(off, tk), :],
                      m_ref, acc_ref, n_keys, tk)
        o = acc[:, :HEAD_DIM] / acc[:, HEAD_DIM:HEAD_DIM + 1]
        outs += [o[g * tq:(g + 1) * tq] for g in range(A_GROUP)]
    o_ref[0] = jnp.concatenate(outs, axis=-1).astype(o_ref.dtype)


def _diff_kernel(lam_ref, q_ref, kt_ref, v_ref, g_ref, o_ref, m_ref, acc_ref, *, tq, tk, n_keys, out_scale):
    lam = lam_ref[0]
    outs = []
    for h in range(2):
        q12 = jnp.concatenate(
            [q_ref[0, :, (2 * h + t) * B_SUB:(2 * h + t + 1) * B_SUB] for t in range(2)], axis=0)

        def step_keys(off, h=h):
            return kt_ref[0, 2 * h, :, pl.ds(off, tk)], kt_ref[0, 2 * h + 1, :, pl.ds(off, tk)]

        m_ref[...] = jnp.full(m_ref.shape, NEG_BIG, jnp.float32)
        acc_ref[...] = jnp.zeros(acc_ref.shape, jnp.float32)

        def body(j, carry, h=h, q12=q12):
            off = pl.multiple_of(j * tk, tk)
            k1, k2 = step_keys(off)
            s = jnp.concatenate(
                [jnp.dot(q12[:tq], k1, preferred_element_type=jnp.float32),
                 jnp.dot(q12[tq:], k2, preferred_element_type=jnp.float32)], axis=0)
            m_old = m_ref[...]
            m_new = jnp.maximum(m_old, jnp.max(s, axis=-1, keepdims=True))
            alpha = jnp.exp2(m_old - m_new)
            p = jnp.exp2(s - m_new).astype(jnp.bfloat16)
            acc_ref[...] = alpha * acc_ref[...] + jnp.dot(p, v_ref[0, h, pl.ds(off, tk), :],
                                                         preferred_element_type=jnp.float32)
            m_ref[...] = m_new
            return carry

        lax.fori_loop(0, n_keys // tk, body, 0)
        acc = acc_ref[...]
        o12 = acc[:, :B_VDIM] / acc[:, B_VDIM:B_VDIM + 1]
        o = o12[:tq] - lam * o12[tq:]
        o = o * lax.rsqrt(jnp.mean(o * o, axis=-1, keepdims=True) + EPS)
        outs.append(o * g_ref[...] * out_scale)
    o_ref[0] = jnp.concatenate(outs, axis=-1).astype(o_ref.dtype)


def _gqa_attention(q, k_t, v_ext, *, tq=256, tk=768):
    B, L, _ = q.shape
    S = k_t.shape[-1]
    kern = functools.partial(_gqa_kernel, tq=tq, tk=tk, n_keys=S)
    return pl.pallas_call(
        kern,
        grid=(B, L // tq),
        in_specs=[
            pl.BlockSpec((1, tq, A_Q), lambda b, i: (b, i, 0)),
            pl.BlockSpec((1, A_KV_HEADS, HEAD_DIM, S), lambda b, i: (b, 0, 0, 0)),
            pl.BlockSpec((1, A_KV_HEADS, S, V_EXT), lambda b, i: (b, 0, 0, 0)),
        ],
        out_specs=pl.BlockSpec((1, tq, A_Q), lambda b, i: (b, i, 0)),
        out_shape=jax.ShapeDtypeStruct((B, L, A_Q), jnp.bfloat16),
        scratch_shapes=[pltpu.VMEM((A_GROUP * tq, 1), jnp.float32),
                        pltpu.VMEM((A_GROUP * tq, V_EXT), jnp.float32)],
        compiler_params=pltpu.CompilerParams(
            dimension_semantics=("arbitrary", "arbitrary"), vmem_limit_bytes=VMEM_LIMIT),
        name="gqa_attention",
    )(q, k_t, v_ext)


def _diff_attention(q, k_t, v_ext, lam, subln_g, out_scale, *, tq=256, tk=768):
    B, L, _ = q.shape
    S = k_t.shape[-1]
    kern = functools.partial(_diff_kernel, tq=tq, tk=tk, n_keys=S, out_scale=out_scale)
    return pl.pallas_call(
        kern,
        grid=(B, B_HEADS // 2, L // tq),
        in_specs=[
            pl.BlockSpec(memory_space=pltpu.SMEM),
            pl.BlockSpec((1, tq, 2 * B_VDIM), lambda b, h, i: (b, i, h)),
            pl.BlockSpec((1, 4, B_SUB, S), lambda b, h, i: (b, h, 0, 0)),
            pl.BlockSpec((1, 2, S, V_EXT), lambda b, h, i: (b, h, 0, 0)),
            pl.BlockSpec((1, B_VDIM), lambda b, h, i: (0, 0)),
        ],
        out_specs=pl.BlockSpec((1, tq, 2 * B_VDIM), lambda b, h, i: (b, i, h)),
        out_shape=jax.ShapeDtypeStruct((B, L, B_V), jnp.bfloat16),
        scratch_shapes=[pltpu.VMEM((2 * tq, 1), jnp.float32),
                        pltpu.VMEM((2 * tq, V_EXT), jnp.float32)],
        compiler_params=pltpu.CompilerParams(
            dimension_semantics=("arbitrary", "arbitrary", "arbitrary"), vmem_limit_bytes=VMEM_LIMIT),
        name="diff_attention",
    )(lam.reshape(1).astype(jnp.float32), q, k_t, v_ext, subln_g.reshape(1, B_VDIM).astype(jnp.float32))


def _value_ext(v):
    B, S, H, d = v.shape
    ones = jnp.ones((B, S, H, 1), v.dtype)
    pad = jnp.zeros((B, S, H, V_EXT - d - 1), v.dtype)
    return jnp.transpose(jnp.concatenate([v, ones, pad], axis=-1), (0, 2, 1, 3)).astype(jnp.bfloat16)


def _rms(x, g):
    xf = x.astype(jnp.float32)
    y = xf * lax.rsqrt(jnp.mean(xf * xf, axis=-1, keepdims=True) + EPS)
    return (y * g.astype(jnp.float32)).astype(x.dtype)


def _modulate(h, shift, scale):
    return h * (1 + scale) + shift


def _rope_2d(x, row, col):
    d = x.shape[-1]
    half = d // 2
    inv = 1.0 / (ROPE_THETA ** (jnp.arange(0, half, 2, dtype=jnp.float32) / half))
    ang = jnp.concatenate([row.astype(jnp.float32)[:, None] * inv[None, :],
                           col.astype(jnp.float32)[:, None] * inv[None, :]], axis=-1)
    cos = jnp.cos(ang)[None, :, None, :]
    sin = jnp.sin(ang)[None, :, None, :]
    xr = x.astype(jnp.float32).reshape(x.shape[:-1] + (half, 2))
    x0, x1 = xr[..., 0], xr[..., 1]
    out = jnp.stack([x0 * cos - x1 * sin, x0 * sin + x1 * cos], axis=-1)
    return out.reshape(x.shape).astype(x.dtype)


def _gqa_block(q, k, v):
    s = jnp.einsum('bqkgd,bskd->bkgqs', q, k).astype(jnp.float32) * (HEAD_DIM ** -0.5)
    p = jax.nn.softmax(s, axis=-1).astype(v.dtype)
    return jnp.einsum('bkgqs,bskd->bqkgd', p, v)


def _diff_block(q, k, v, lam):
    s = jnp.einsum('bqhtd,bshtd->bthqs', q, k).astype(jnp.float32) * (B_SUB ** -0.5)
    p = jax.nn.softmax(s, axis=-1)
    a = (p[:, 0] - lam * p[:, 1]).astype(v.dtype)
    return jnp.einsum('bhqs,bshd->bqhd', a, v)


def _scan_combine(e1, e2):
    a1, b1 = e1
    a2, b2 = e2
    return a2 * a1, a2 * b1 + b2


def _lin_scan(bu, abar, reverse):
    if reverse:
        bu = jnp.flip(bu, axis=1)
    a = jnp.broadcast_to(abar, (1, bu.shape[1]) + abar.shape)
    _, h = lax.associative_scan(_scan_combine, (a, bu), axis=1)
    return jnp.flip(h, axis=1) if reverse else h


def _s5_mixer(u, uc, a_re, a_im, log_dt, b_re, b_im, c_re, c_im, d_skip, w_glu, with_ctx):
    f32 = jnp.float32
    B, L, _ = u.shape
    Lc = uc.shape[1]
    ug = u.astype(f32).reshape(B, L, C_GROUPS, C_GROUP_CH).astype(jnp.complex64)
    ucg = uc.astype(f32).reshape(B, Lc, C_GROUPS, C_GROUP_CH).astype(jnp.complex64)
    ys, ycs = [], []
    for dirn in range(2):
        rev = dirn == 1
        A = lax.complex(a_re[dirn].astype(f32), a_im[dirn].astype(f32))
        adt = A * jnp.exp(log_dt[dirn].astype(f32))[:, None]
        abar = jnp.exp(adt)
        bbar = ((abar - 1) / A)[..., None] * lax.complex(b_re[dirn].astype(f32), b_im[dirn].astype(f32))
        cmat = lax.complex(c_re[dirn].astype(f32), c_im[dirn].astype(f32))
        hc = _lin_scan(jnp.einsum('blgp,gnp->blgn', ucg, bbar), abar, rev)
        h0 = hc[:, 0] if rev else hc[:, -1]
        steps = (L - jnp.arange(L)) if rev else (jnp.arange(L) + 1)
        decay = jnp.exp(adt[None] * steps.astype(f32)[:, None, None])
        h = _lin_scan(jnp.einsum('blgp,gnp->blgn', ug, bbar), abar, rev) + decay[None] * h0[:, None]
        ys.append(jnp.einsum('blgn,gpn->blgp', h, cmat).real)
        if with_ctx:
            ycs.append(jnp.einsum('blgn,gpn->blgp', hc, cmat).real)

    def _out(y_dirs, inp, n):
        y = (y_dirs[0] + y_dirs[1]).reshape(B, n, C_WIDTH) + d_skip.astype(f32) * inp.astype(f32)
        g = jax.nn.gelu(y)
        return (g * jax.nn.sigmoid(g @ w_glu.astype(f32))).astype(inp.dtype)

    y_lat = _out(ys, u, L)
    y_ctx = _out(ycs, uc, Lc) if with_ctx else None
    return y_lat, y_ctx


def _token_mixers(h, hc, row, col, layer_idx, w_in, qn_a, kn_a, qn_b, kn_b,
                  lam_q1, lam_k1, lam_q2, lam_k2, subln_g,
                  a_re, a_im, log_dt, b_re, b_im, c_re, c_im, d_skip, w_glu, with_ctx):
    B, L, _ = h.shape
    Lc = hc.shape[1]
    cuts = [A_Q, A_Q + A_KV, A_Q + 2 * A_KV, A_Q + 2 * A_KV + B_QK,
            A_Q + 2 * A_KV + 2 * B_QK, A_Q + 2 * A_KV + 2 * B_QK + B_V]
    qa, ka, va, qb, kb, vb, us = jnp.split(h @ w_in, cuts, axis=-1)
    qac, kac, vac, qbc, kbc, vbc, usc = jnp.split(hc @ w_in, cuts, axis=-1)

    qa = _rope_2d(_rms(qa.reshape(B, L, A_HEADS, HEAD_DIM), qn_a), row, col)
    ka = _rope_2d(_rms(ka.reshape(B, L, A_KV_HEADS, HEAD_DIM), kn_a), row, col)
    va = va.reshape(B, L, A_KV_HEADS, HEAD_DIM)
    kac = _rms(kac.reshape(B, Lc, A_KV_HEADS, HEAD_DIM), kn_a)
    vac = vac.reshape(B, Lc, A_KV_HEADS, HEAD_DIM)
    ka_all = jnp.concatenate([kac, ka], axis=1)
    va_all = jnp.concatenate([vac, va], axis=1)
    qa_s = (qa * (HEAD_DIM ** -0.5 * LOG2E)).reshape(B, L, A_Q).astype(jnp.bfloat16)
    ka_t = jnp.transpose(ka_all, (0, 2, 3, 1)).astype(jnp.bfloat16)
    o_a = _gqa_attention(qa_s, ka_t, _value_ext(va_all))

    lambda_init = 0.8 - 0.6 * math.exp(-0.3 * layer_idx)
    lam = (jnp.exp(jnp.sum(lam_q1.astype(jnp.float32) * lam_k1.astype(jnp.float32)))
           - jnp.exp(jnp.sum(lam_q2.astype(jnp.float32) * lam_k2.astype(jnp.float32))) + lambda_init)
    qb = _rope_2d(_rms(qb.reshape(B, L, 2 * B_HEADS, B_SUB), qn_b), row, col)
    kb = _rope_2d(_rms(kb.reshape(B, L, 2 * B_HEADS, B_SUB), kn_b), row, col)
    vb = vb.reshape(B, L, B_HEADS, B_VDIM)
    kbc = _rms(kbc.reshape(B, Lc, 2 * B_HEADS, B_SUB), kn_b)
    vbc = vbc.reshape(B, Lc, B_HEADS, B_VDIM)
    kb_all = jnp.concatenate([kbc, kb], axis=1)
    vb_all = jnp.concatenate([vbc, vb], axis=1)
    qb_s = (qb * (B_SUB ** -0.5 * LOG2E)).reshape(B, L, B_QK).astype(jnp.bfloat16)
    kb_t = jnp.transpose(kb_all, (0, 2, 3, 1)).astype(jnp.bfloat16)
    o_b = _diff_attention(qb_s, kb_t, _value_ext(vb_all), lam, subln_g, 1 - lambda_init)

    o_c, o_cc = _s5_mixer(us, usc, a_re, a_im, log_dt, b_re, b_im, c_re, c_im, d_skip, w_glu, with_ctx)

    mix = jnp.concatenate([o_a.astype(h.dtype), o_b.astype(h.dtype), o_c], axis=-1)
    if not with_ctx:
        return mix, None
    qac = _rms(qac.reshape(B, Lc, A_HEADS, HEAD_DIM), qn_a).reshape(B, Lc, A_KV_HEADS, A_GROUP, HEAD_DIM)
    o_ac = _gqa_block(qac, kac, vac).reshape(B, Lc, A_Q)
    kbc5 = kbc.reshape(B, Lc, B_HEADS, 2, B_SUB)
    qbc = _rms(qbc.reshape(B, Lc, 2 * B_HEADS, B_SUB), qn_b).reshape(B, Lc, B_HEADS, 2, B_SUB)
    o_bc = _diff_block(qbc, kbc5, vbc, lam)
    o_bc = (_rms(o_bc, subln_g) * (1 - lambda_init)).reshape(B, Lc, B_V)
    mixc = jnp.concatenate([o_ac, o_bc, o_cc], axis=-1)
    return mix, mixc


def _peer(h, w_q, sub_keys, u_emb, v_emb):
    T, D = h.shape

    def chunk(hc):
        q = (hc @ w_q).reshape(TOKEN_CHUNK, PEER_HEADS, 2, PEER_HALF)
        s = jnp.einsum('chtd,htnd->chtn', q, sub_keys).astype(jnp.float32)
        s1, i1 = lax.top_k(s[:, :, 0], PEER_TOPK)
        s2, i2 = lax.top_k(s[:, :, 1], PEER_TOPK)
        cand = (s1[..., :, None] + s2[..., None, :]).reshape(TOKEN_CHUNK, PEER_HEADS, PEER_TOPK * PEER_TOPK)
        sc, ci = lax.top_k(cand, PEER_TOPK)
        idx = (jnp.take_along_axis(i1, ci // PEER_TOPK, axis=-1) * N_KEYS
               + jnp.take_along_axis(i2, ci % PEER_TOPK, axis=-1))
        g = jax.nn.softmax(sc, axis=-1)
        u = jnp.take(u_emb, idx, axis=0)
        act = jax.nn.gelu(jnp.einsum('chkd,cd->chk', u, hc).astype(jnp.float32))
        w = (g * act).astype(hc.dtype)
        return jnp.einsum('chk,chkd->cd', w, jnp.take(v_emb, idx, axis=0))

    out = lax.map(chunk, h.reshape(T // TOKEN_CHUNK, TOKEN_CHUNK, D))
    return out.reshape(T, D)


def kernel(x, c, ctx, c_ctx, w_ada, b_ada, norm1_g, norm2_g, w_in, w_out,
           qn_a, kn_a, qn_b, kn_b, lam_q1, lam_k1, lam_q2, lam_k2, subln_g,
           s5_a_re, s5_a_im, s5_log_dt, s5_b_re, s5_b_im, s5_c_re, s5_c_im,
           s5_d, s5_w_glu, peer_wq, peer_keys, peer_u, peer_v):
    B, L, D = x.shape
    Lc = ctx.shape[1]
    depth = w_in.shape[0]
    rows = L // GRID_W
    row = jnp.repeat(jnp.arange(rows, dtype=jnp.int32), GRID_W)
    col = jnp.tile(jnp.arange(GRID_W, dtype=jnp.int32), rows)
    s_c = jax.nn.silu(c)
    s_cc = jax.nn.silu(c_ctx)
    for l in range(depth):
        with_ctx = l < depth - 1
        sh1, sc1, g1, sh2, sc2, g2 = jnp.split((s_c @ w_ada[l] + b_ada[l])[:, None, :], 6, axis=-1)
        csh1, csc1, cg1, csh2, csc2, cg2 = jnp.split(s_cc @ w_ada[l] + b_ada[l], 6, axis=-1)

        h = _modulate(_rms(x, norm1_g[l]), sh1, sc1)
        hc = _modulate(_rms(ctx, norm1_g[l]), csh1, csc1)
        mix, mixc = _token_mixers(h, hc, row, col, l, w_in[l], qn_a[l], kn_a[l], qn_b[l], kn_b[l],
                                  lam_q1[l], lam_k1[l], lam_q2[l], lam_k2[l], subln_g[l],
                                  s5_a_re[l], s5_a_im[l], s5_log_dt[l], s5_b_re[l], s5_b_im[l],
                                  s5_c_re[l], s5_c_im[l], s5_d[l], s5_w_glu[l], with_ctx)
        x = x + g1 * (mix @ w_out[l])

        h2 = _modulate(_rms(x, norm2_g[l]), sh2, sc2)
        x = x + g2 * _peer(h2.reshape(B * L, D), peer_wq[l], peer_keys[l], peer_u[l], peer_v[l]).reshape(B, L, D)

        if with_ctx:
            ctx = ctx + cg1 * (mixc @ w_out[l])
            h2c = _modulate(_rms(ctx, norm2_g[l]), csh2, csc2)
            ctx = ctx + cg2 * _peer(h2c.reshape(B * Lc, D), peer_wq[l], peer_keys[l], peer_u[l],
                                    peer_v[l]).reshape(B, Lc, D)
    return x
```

```python
import functools
import math

import jax
import jax.numpy as jnp
from jax import lax
from jax.experimental import pallas as pl
from jax.experimental.pallas import tpu as pltpu
from jax.experimental.pallas import tpu_sc as plsc

D_MODEL = 1024
GRID_W = 64
EPS = 1e-6
ROPE_THETA = 10000.0
HEAD_DIM = 64
A_HEADS = 6
A_KV_HEADS = 2
A_GROUP = A_HEADS // A_KV_HEADS
A_Q = A_HEADS * HEAD_DIM
A_KV = A_KV_HEADS * HEAD_DIM
B_HEADS = 6
B_SUB = 32
B_VDIM = 2 * B_SUB
B_QK = B_HEADS * 2 * B_SUB
B_V = B_HEADS * B_VDIM
C_GROUPS = 16
C_GROUP_CH = 16
C_WIDTH = C_GROUPS * C_GROUP_CH
C_STATE = 64
PEER_HEADS = 8
N_KEYS = 128
PEER_TOPK = 16
PEER_HALF = 128
TOKEN_CHUNK = 128

LANES = 128
V_EXT = LANES
LOG2E = math.log2(math.e)
NEG_BIG = -1e30
VMEM_LIMIT = 48 * 1024 * 1024


def _online_softmax_step(q, k_t, v, m_ref, acc_ref):
    s = jnp.dot(q, k_t, preferred_element_type=jnp.float32)
    m_old = m_ref[...]
    m_new = jnp.maximum(m_old, jnp.max(s, axis=-1, keepdims=True))
    alpha = jnp.exp2(m_old - m_new)
    p = jnp.exp2(s - m_new).astype(jnp.bfloat16)
    acc_ref[...] = alpha * acc_ref[...] + jnp.dot(p, v, preferred_element_type=jnp.float32)
    m_ref[...] = m_new


def _attend(q, kt_at, v_at, m_ref, acc_ref, n_keys, tk):
    m_ref[...] = jnp.full(m_ref.shape, NEG_BIG, jnp.float32)
    acc_ref[...] = jnp.zeros(acc_ref.shape, jnp.float32)

    def body(j, carry):
        off = pl.multiple_of(j * tk, tk)
        _online_softmax_step(q, kt_at(off), v_at(off), m_ref, acc_ref)
        return carry

    lax.fori_loop(0, n_keys // tk, body, 0)
    return acc_ref[...]


def _gqa_kernel(q_ref, kt_ref, v_ref, o_ref, m_ref, acc_ref, *, tq, tk, n_keys):
    outs = []
    for kvh in range(A_KV_HEADS):
        q = jnp.concatenate(
            [q_ref[0, :, (kvh * A_GROUP + g) * HEAD_DIM:(kvh * A_GROUP + g + 1) * HEAD_DIM]
             for g in range(A_GROUP)], axis=0)
        acc = _attend(q,
                      lambda off: kt_ref[0, kvh, :, pl.ds(off, tk)],
                      lambda off: v_ref[0, kvh, pl.ds(off, tk), :],
                      m_ref, acc_ref, n_keys, tk)
        o = acc[:, :HEAD_DIM] / acc[:, HEAD_DIM:HEAD_DIM + 1]
        outs += [o[g * tq:(g + 1) * tq] for g in range(A_GROUP)]
    o_ref[0] = jnp.concatenate(outs, axis=-1).astype(o_ref.dtype)


def _diff_kernel(lam_ref, q_ref, kt_ref, v_ref, g_ref, o_ref, m_ref, acc_ref, *, tq, tk, n_keys, out_scale):
    lam = lam_ref[0]
    outs = []
    for h in range(2):
        q12 = jnp.concatenate(
            [q_ref[0, :, (2 * h + t) * B_SUB:(2 * h + t + 1) * B_SUB] for t in range(2)], axis=0)

        def step_keys(off, h=h):
            return kt_ref[0, 2 * h, :, pl.ds(off, tk)], kt_ref[0, 2 * h + 1, :, pl.ds(off, tk)]

        m_ref[...] = jnp.full(m_ref.shape, NEG_BIG, jnp.float32)
        acc_ref[...] = jnp.zeros(acc_ref.shape, jnp.float32)

        def body(j, carry, h=h, q12=q12):
            off = pl.multiple_of(j * tk, tk)
            k1, k2 = step_keys(off)
            s = jnp.concatenate(
                [jnp.dot(q12[:tq], k1, preferred_element_type=jnp.float32),
                 jnp.dot(q12[tq:], k2, preferred_element_type=jnp.float32)], axis=0)
            m_old = m_ref[...]
            m_new = jnp.maximum(m_old, jnp.max(s, axis=-1, keepdims=True))
            alpha = jnp.exp2(m_old - m_new)
            p = jnp.exp2(s - m_new).astype(jnp.bfloat16)
            acc_ref[...] = alpha * acc_ref[...] + jnp.dot(p, v_ref[0, h, pl.ds(off, tk), :],
                                                         preferred_element_type=jnp.float32)
            m_ref[...] = m_new
            return carry

        lax.fori_loop(0, n_keys // tk, body, 0)
        acc = acc_ref[...]
        o12 = acc[:, :B_VDIM] / acc[:, B_VDIM:B_VDIM + 1]
        o = o12[:tq] - lam * o12[tq:]
        o = o * lax.rsqrt(jnp.mean(o * o, axis=-1, keepdims=True) + EPS)
        outs.append(o * g_ref[...] * out_scale)
    o_ref[0] = jnp.concatenate(outs, axis=-1).astype(o_ref.dtype)


def _gqa_attention(q, k_t, v_ext, *, tq=256, tk=768):
    B, L, _ = q.shape
    S = k_t.shape[-1]
    kern = functools.partial(_gqa_kernel, tq=tq, tk=tk, n_keys=S)
    return pl.pallas_call(
        kern,
        grid=(B, L // tq),
        in_specs=[
            pl.BlockSpec((1, tq, A_Q), lambda b, i: (b, i, 0)),
            pl.BlockSpec((1, A_KV_HEADS, HEAD_DIM, S), lambda b, i: (b, 0, 0, 0)),
            pl.BlockSpec((1, A_KV_HEADS, S, V_EXT), lambda b, i: (b, 0, 0, 0)),
        ],
        out_specs=pl.BlockSpec((1, tq, A_Q), lambda b, i: (b, i, 0)),
        out_shape=jax.ShapeDtypeStruct((B, L, A_Q), jnp.bfloat16),
        scratch_shapes=[pltpu.VMEM((A_GROUP * tq, 1), jnp.float32),
                        pltpu.VMEM((A_GROUP * tq, V_EXT), jnp.float32)],
        compiler_params=pltpu.CompilerParams(
            dimension_semantics=("arbitrary", "arbitrary"), vmem_limit_bytes=VMEM_LIMIT),
        name="gqa_attention",
    )(q, k_t, v_ext)


def _diff_attention(q, k_t, v_ext, lam, subln_g, out_scale, *, tq=256, tk=768):
    B, L, _ = q.shape
    S = k_t.shape[-1]
    kern = functools.partial(_diff_kernel, tq=tq, tk=tk, n_keys=S, out_scale=out_scale)
    return pl.pallas_call(
        kern,
        grid=(B, B_HEADS // 2, L // tq),
        in_specs=[
            pl.BlockSpec(memory_space=pltpu.SMEM),
            pl.BlockSpec((1, tq, 2 * B_VDIM), lambda b, h, i: (b, i, h)),
            pl.BlockSpec((1, 4, B_SUB, S), lambda b, h, i: (b, h, 0, 0)),
            pl.BlockSpec((1, 2, S, V_EXT), lambda b, h, i: (b, h, 0, 0)),
            pl.BlockSpec((1, B_VDIM), lambda b, h, i: (0, 0)),
        ],
        out_specs=pl.BlockSpec((1, tq, 2 * B_VDIM), lambda b, h, i: (b, i, h)),
        out_shape=jax.ShapeDtypeStruct((B, L, B_V), jnp.bfloat16),
        scratch_shapes=[pltpu.VMEM((2 * tq, 1), jnp.float32),
                        pltpu.VMEM((2 * tq, V_EXT), jnp.float32)],
        compiler_params=pltpu.CompilerParams(
            dimension_semantics=("arbitrary", "arbitrary", "arbitrary"), vmem_limit_bytes=VMEM_LIMIT),
        name="diff_attention",
    )(lam.reshape(1).astype(jnp.float32), q, k_t, v_ext, subln_g.reshape(1, B_VDIM).astype(jnp.float32))


def _value_ext(v):
    B, S, H, d = v.shape
    ones = jnp.ones((B, S, H, 1), v.dtype)
    pad = jnp.zeros((B, S, H, V_EXT - d - 1), v.dtype)
    return jnp.transpose(jnp.concatenate([v, ones, pad], axis=-1), (0, 2, 1, 3)).astype(jnp.bfloat16)


def _rms(x, g):
    xf = x.astype(jnp.float32)
    y = xf * lax.rsqrt(jnp.mean(xf * xf, axis=-1, keepdims=True) + EPS)
    return (y * g.astype(jnp.float32)).astype(x.dtype)


def _modulate(h, shift, scale):
    return h * (1 + scale) + shift


def _rope_2d(x, row, col):
    d = x.shape[-1]
    half = d // 2
    inv = 1.0 / (ROPE_THETA ** (jnp.arange(0, half, 2, dtype=jnp.float32) / half))
    ang = jnp.concatenate([row.astype(jnp.float32)[:, None] * inv[None, :],
                           col.astype(jnp.float32)[:, None] * inv[None, :]], axis=-1)
    cos = jnp.cos(ang)[None, :, None, :]
    sin = jnp.sin(ang)[None, :, None, :]
    xr = x.astype(jnp.float32).reshape(x.shape[:-1] + (half, 2))
    x0, x1 = xr[..., 0], xr[..., 1]
    out = jnp.stack([x0 * cos - x1 * sin, x0 * sin + x1 * cos], axis=-1)
    return out.reshape(x.shape).astype(x.dtype)


def _gqa_block(q, k, v):
    s = jnp.einsum('bqkgd,bskd->bkgqs', q, k).astype(jnp.float32) * (HEAD_DIM ** -0.5)
    p = jax.nn.softmax(s, axis=-1).astype(v.dtype)
    return jnp.einsum('bkgqs,bskd->bqkgd', p, v)


def _diff_block(q, k, v, lam):
    s = jnp.einsum('bqhtd,bshtd->bthqs', q, k).astype(jnp.float32) * (B_SUB ** -0.5)
    p = jax.nn.softmax(s, axis=-1)
    a = (p[:, 0] - lam * p[:, 1]).astype(v.dtype)
    return jnp.einsum('bhqs,bshd->bqhd', a, v)


def _scan_combine(e1, e2):
    a1, b1 = e1
    a2, b2 = e2
    return a2 * a1, a2 * b1 + b2


def _lin_scan(bu, abar, reverse):
    if reverse:
        bu = jnp.flip(bu, axis=1)
    a = jnp.broadcast_to(abar, (1, bu.shape[1]) + abar.shape)
    _, h = lax.associative_scan(_scan_combine, (a, bu), axis=1)
    return jnp.flip(h, axis=1) if reverse else h


def _s5_mixer(u, uc, a_re, a_im, log_dt, b_re, b_im, c_re, c_im, d_skip, w_glu, with_ctx):
    f32 = jnp.float32
    B, L, _ = u.shape
    Lc = uc.shape[1]
    ug = u.astype(f32).reshape(B, L, C_GROUPS, C_GROUP_CH).astype(jnp.complex64)
    ucg = uc.astype(f32).reshape(B, Lc, C_GROUPS, C_GROUP_CH).astype(jnp.complex64)
    ys, ycs = [], []
    for dirn in range(2):
        rev = dirn == 1
        A = lax.complex(a_re[dirn].astype(f32), a_im[dirn].astype(f32))
        adt = A * jnp.exp(log_dt[dirn].astype(f32))[:, None]
        abar = jnp.exp(adt)
        bbar = ((abar - 1) / A)[..., None] * lax.complex(b_re[dirn].astype(f32), b_im[dirn].astype(f32))
        cmat = lax.complex(c_re[dirn].astype(f32), c_im[dirn].astype(f32))
        hc = _lin_scan(jnp.einsum('blgp,gnp->blgn', ucg, bbar), abar, rev)
        h0 = hc[:, 0] if rev else hc[:, -1]
        steps = (L - jnp.arange(L)) if rev else (jnp.arange(L) + 1)
        decay = jnp.exp(adt[None] * steps.astype(f32)[:, None, None])
        h = _lin_scan(jnp.einsum('blgp,gnp->blgn', ug, bbar), abar, rev) + decay[None] * h0[:, None]
        ys.append(jnp.einsum('blgn,gpn->blgp', h, cmat).real)
        if with_ctx:
            ycs.append(jnp.einsum('blgn,gpn->blgp', hc, cmat).real)

    def _out(y_dirs, inp, n):
        y = (y_dirs[0] + y_dirs[1]).reshape(B, n, C_WIDTH) + d_skip.astype(f32) * inp.astype(f32)
        g = jax.nn.gelu(y)
        return (g * jax.nn.sigmoid(g @ w_glu.astype(f32))).astype(inp.dtype)

    y_lat = _out(ys, u, L)
    y_ctx = _out(ycs, uc, Lc) if with_ctx else None
    return y_lat, y_ctx


def _token_mixers(h, hc, row, col, layer_idx, w_in, qn_a, kn_a, qn_b, kn_b,
                  lam_q1, lam_k1, lam_q2, lam_k2, subln_g,
                  a_re, a_im, log_dt, b_re, b_im, c_re, c_im, d_skip, w_glu, with_ctx):
    B, L, _ = h.shape
    Lc = hc.shape[1]
    cuts = [A_Q, A_Q + A_KV, A_Q + 2 * A_KV, A_Q + 2 * A_KV + B_QK,
            A_Q + 2 * A_KV + 2 * B_QK, A_Q + 2 * A_KV + 2 * B_QK + B_V]
    qa, ka, va, qb, kb, vb, us = jnp.split(h @ w_in, cuts, axis=-1)
    qac, kac, vac, qbc, kbc, vbc, usc = jnp.split(hc @ w_in, cuts, axis=-1)

    qa = _rope_2d(_rms(qa.reshape(B, L, A_HEADS, HEAD_DIM), qn_a), row, col)
    ka = _rope_2d(_rms(ka.reshape(B, L, A_KV_HEADS, HEAD_DIM), kn_a), row, col)
    va = va.reshape(B, L, A_KV_HEADS, HEAD_DIM)
    kac = _rms(kac.reshape(B, Lc, A_KV_HEADS, HEAD_DIM), kn_a)
    vac = vac.reshape(B, Lc, A_KV_HEADS, HEAD_DIM)
    ka_all = jnp.concatenate([kac, ka], axis=1)
    va_all = jnp.concatenate([vac, va], axis=1)
    qa_s = (qa * (HEAD_DIM ** -0.5 * LOG2E)).reshape(B, L, A_Q).astype(jnp.bfloat16)
    ka_t = jnp.transpose(ka_all, (0, 2, 3, 1)).astype(jnp.bfloat16)
    o_a = _gqa_attention(qa_s, ka_t, _value_ext(va_all))

    lambda_init = 0.8 - 0.6 * math.exp(-0.3 * layer_idx)
    lam = (jnp.exp(jnp.sum(lam_q1.astype(jnp.float32) * lam_k1.astype(jnp.float32)))
           - jnp.exp(jnp.sum(lam_q2.astype(jnp.float32) * lam_k2.astype(jnp.float32))) + lambda_init)
    qb = _rope_2d(_rms(qb.reshape(B, L, 2 * B_HEADS, B_SUB), qn_b), row, col)
    kb = _rope_2d(_rms(kb.reshape(B, L, 2 * B_HEADS, B_SUB), kn_b), row, col)
    vb = vb.reshape(B, L, B_HEADS, B_VDIM)
    kbc = _rms(kbc.reshape(B, Lc, 2 * B_HEADS, B_SUB), kn_b)
    vbc = vbc.reshape(B, Lc, B_HEADS, B_VDIM)
    kb_all = jnp.concatenate([kbc, kb], axis=1)
    vb_all = jnp.concatenate([vbc, vb], axis=1)
    qb_s = (qb * (B_SUB ** -0.5 * LOG2E)).reshape(B, L, B_QK).astype(jnp.bfloat16)
    kb_t = jnp.transpose(kb_all, (0, 2, 3, 1)).astype(jnp.bfloat16)
    o_b = _diff_attention(qb_s, kb_t, _value_ext(vb_all), lam, subln_g, 1 - lambda_init)

    o_c, o_cc = _s5_mixer(us, usc, a_re, a_im, log_dt, b_re, b_im, c_re, c_im, d_skip, w_glu, with_ctx)

    mix = jnp.concatenate([o_a.astype(h.dtype), o_b.astype(h.dtype), o_c], axis=-1)
    if not with_ctx:
        return mix, None
    qac = _rms(qac.reshape(B, Lc, A_HEADS, HEAD_DIM), qn_a).reshape(B, Lc, A_KV_HEADS, A_GROUP, HEAD_DIM)
    o_ac = _gqa_block(qac, kac, vac).reshape(B, Lc, A_Q)
    kbc5 = kbc.reshape(B, Lc, B_HEADS, 2, B_SUB)
    qbc = _rms(qbc.reshape(B, Lc, 2 * B_HEADS, B_SUB), qn_b).reshape(B, Lc, B_HEADS, 2, B_SUB)
    o_bc = _diff_block(qbc, kbc5, vbc, lam)
    o_bc = (_rms(o_bc, subln_g) * (1 - lambda_init)).reshape(B, Lc, B_V)
    mixc = jnp.concatenate([o_ac, o_bc, o_cc], axis=-1)
    return mix, mixc


PEER_TB = 256
PEER_EB = 8
N_SLOTS = PEER_HEADS * PEER_TOPK
HALF_D = D_MODEL // 2
CAND_ROWS = 80
INVALID_FLAT = 1.0e9
SC_CORES = 2
SC_SUBCORES = 16
SC_CHUNK = 32


def _cand_flat_table():
    f = [float(b) for b in range(16)]
    for a in range(1, 8):
        f += [float(a * 16 + b) if (a + 1) * (b + 1) <= 16 else INVALID_FLAT for b in range(8)]
    f += [float(a * 16) for a in range(8, 16)]
    return jnp.broadcast_to(jnp.asarray(f, jnp.float32)[:, None], (CAND_ROWS, PEER_TB))


def _pair_rows(first, second):
    blocks = [first[0:1] + second]
    blocks += [first[a:a + 1] + second[0:8] for a in range(1, 8)]
    blocks += [first[8:16] + second[0:1]]
    return jnp.concatenate(blocks, axis=0)


def _peer_route_kernel(x_ref, g_ref, sh_ref, sc_ref, wq_ref, keys_ref, ftab_ref,
                       idx_ref, gate_ref, h_ref, val_scr, id_scr, sc_scr, e_scr):
    f32 = jnp.float32
    x = x_ref[...]
    y = x * lax.rsqrt(jnp.mean(x * x, axis=-1, keepdims=True) + EPS) * g_ref[...]
    h2 = (y * (1.0 + sc_ref[0]) + sh_ref[0]).astype(jnp.bfloat16)
    h_ref[...] = h2
    q = jnp.dot(h2, wq_ref[...], preferred_element_type=f32)
    ftab = ftab_ref[...]
    key_id = lax.broadcasted_iota(jnp.int32, (N_KEYS, PEER_TB), 0).astype(f32)
    neg_inf = -jnp.inf
    for head in range(PEER_HEADS):
        for t in range(2):
            ht = head * 2 + t
            qh = q[:, ht * PEER_HALF:(ht + 1) * PEER_HALF].astype(jnp.bfloat16)
            s = lax.dot_general(keys_ref[ht], qh, (((1,), (1,)), ((), ())),
                                preferred_element_type=f32)
            for r in range(PEER_TOPK):
                m = jnp.max(s, axis=0, keepdims=True)
                pick = jnp.min(jnp.where(s == m, key_id, float(N_KEYS)), axis=0, keepdims=True)
                s = jnp.where(key_id == pick, neg_inf, s)
                val_scr[t, r:r + 1, :] = m
                id_scr[t, r:r + 1, :] = pick
        cand = _pair_rows(val_scr[0], val_scr[1])
        cand = jnp.where(ftab < float(PEER_TOPK * PEER_TOPK), cand, neg_inf)
        expert = _pair_rows(id_scr[0] * float(N_KEYS), id_scr[1])
        for r in range(PEER_TOPK):
            m = jnp.max(cand, axis=0, keepdims=True)
            pick = jnp.min(jnp.where(cand == m, ftab, INVALID_FLAT), axis=0, keepdims=True)
            hit = ftab == pick
            sc_scr[r:r + 1, :] = m
            e_scr[r:r + 1, :] = jnp.sum(jnp.where(hit, expert, 0.0), axis=0, keepdims=True)
            cand = jnp.where(hit, neg_inf, cand)
        sc = sc_scr[...]
        p = jnp.exp(sc - sc[0:1])
        gate = p / jnp.sum(p, axis=0, keepdims=True)
        idx_ref[head * PEER_TOPK:(head + 1) * PEER_TOPK, :] = e_scr[...].astype(jnp.int32)
        gate_ref[head * PEER_TOPK:(head + 1) * PEER_TOPK, :] = gate


def _peer_route(x, norm_g, shift, scale, wq_bf, keys_bf, tokens_per_mod):
    T, D = x.shape
    nq = wq_bf.shape[1]
    blocks_per_mod = tokens_per_mod // PEER_TB
    return pl.pallas_call(
        _peer_route_kernel,
        grid=(T // PEER_TB,),
        in_specs=[
            pl.BlockSpec((PEER_TB, D), lambda i: (i, 0)),
            pl.BlockSpec((1, D), lambda i: (0, 0)),
            pl.BlockSpec((1, 1, D), lambda i: (i // blocks_per_mod, 0, 0)),
            pl.BlockSpec((1, 1, D), lambda i: (i // blocks_per_mod, 0, 0)),
            pl.BlockSpec((D, nq), lambda i: (0, 0)),
            pl.BlockSpec((2 * PEER_HEADS, N_KEYS, PEER_HALF), lambda i: (0, 0, 0)),
            pl.BlockSpec((CAND_ROWS, PEER_TB), lambda i: (0, 0)),
        ],
        out_specs=[
            pl.BlockSpec((N_SLOTS, PEER_TB), lambda i: (0, i)),
            pl.BlockSpec((N_SLOTS, PEER_TB), lambda i: (0, i)),
            pl.BlockSpec((PEER_TB, D), lambda i: (i, 0)),
        ],
        out_shape=[
            jax.ShapeDtypeStruct((N_SLOTS, T), jnp.int32),
            jax.ShapeDtypeStruct((N_SLOTS, T), jnp.float32),
            jax.ShapeDtypeStruct((T, D), jnp.bfloat16),
        ],
        scratch_shapes=[pltpu.VMEM((2, PEER_TOPK, PEER_TB), jnp.float32),
                        pltpu.VMEM((2, PEER_TOPK, PEER_TB), jnp.float32),
                        pltpu.VMEM((PEER_TOPK, PEER_TB), jnp.float32),
                        pltpu.VMEM((PEER_TOPK, PEER_TB), jnp.float32)],
        compiler_params=pltpu.CompilerParams(dimension_semantics=("arbitrary",),
                                             vmem_limit_bytes=VMEM_LIMIT),
        name="peer_route",
    )(x, norm_g.reshape(1, D).astype(jnp.float32), shift, scale, wq_bf, keys_bf, _cand_flat_table())


def _gather_rows(table, idx2d):
    n_chunks_total, chunk = idx2d.shape
    width = table.shape[1]
    n_workers = SC_CORES * SC_SUBCORES
    n_chunks = n_chunks_total // n_workers
    mesh = plsc.VectorSubcoreMesh(core_axis_name="c", subcore_axis_name="s")

    def body(table_hbm, idx_hbm, out_hbm, idx_v, rows_v, gsem, wsem):
        wid = lax.axis_index("s") * SC_CORES + lax.axis_index("c")
        cbase = wid * n_chunks

        def gather_copy(slot):
            return pltpu.make_async_copy(table_hbm.at[idx_v.at[slot]], rows_v.at[slot], gsem.at[slot])

        def writeback_copy(r, slot):
            return pltpu.make_async_copy(rows_v.at[slot], out_hbm.at[pl.ds(r * chunk, chunk)], wsem.at[slot])

        pltpu.sync_copy(idx_hbm.at[cbase], idx_v.at[0])
        gather_copy(0).start()

        def step(i2, carry):
            for slot in range(2):
                i = i2 * 2 + slot
                r = cbase + i
                other = 1 - slot
                gather_copy(slot).wait()

                @pl.when(i >= 1)
                def _():
                    writeback_copy(r - 1, other).wait()

                @pl.when(i + 1 < n_chunks)
                def _():
                    pltpu.sync_copy(idx_hbm.at[r + 1], idx_v.at[other])
                    gather_copy(other).start()

                writeback_copy(r, slot).start()
            return carry

        lax.fori_loop(0, n_chunks // 2, step, 0)
        writeback_copy(cbase + n_chunks - 1, 1).wait()

    return pl.kernel(
        body, mesh=mesh,
        out_type=jax.ShapeDtypeStruct((n_chunks_total * chunk, width), jnp.int32),
        scratch_types=[pltpu.VMEM((2, chunk), jnp.int32),
                       pltpu.VMEM((2, chunk, width), jnp.int32),
                       pltpu.SemaphoreType.DMA((2,)),
                       pltpu.SemaphoreType.DMA((2,))],
        name="peer_gather",
    )(table, idx2d)


def _unpack_pair(words):
    lo = lax.bitcast_convert_type(words << 16, jnp.float32)
    hi = lax.bitcast_convert_type(words & jnp.int32(-65536), jnp.float32)
    return lo, hi


def _peer_expert_kernel(rows_ref, h_ref, gate_ref, x_ref, g2_ref, o_ref):
    f32 = jnp.float32
    gate = gate_ref[0]
    outs = []
    for t in range(PEER_EB):
        words = rows_ref[t]
        u_lo, u_hi = _unpack_pair(words[:, :HALF_D])
        h = h_ref[t:t + 1, :].astype(f32)
        act = jnp.sum(u_lo * h[:, :HALF_D] + u_hi * h[:, HALF_D:], axis=-1, keepdims=True)
        wgt = jax.nn.gelu(act) * gate[:, t:t + 1]
        v_lo, v_hi = _unpack_pair(words[:, HALF_D:])
        outs.append(jnp.concatenate([jnp.sum(wgt * v_lo, axis=0, keepdims=True),
                                     jnp.sum(wgt * v_hi, axis=0, keepdims=True)], axis=-1))
    o_ref[...] = x_ref[...] + g2_ref[0] * jnp.concatenate(outs, axis=0)


def _peer_expert(rows, h2, gate_t, x, g2, tokens_per_mod):
    T, D = x.shape
    nblk = T // PEER_EB
    blocks_per_mod = tokens_per_mod // PEER_EB
    gate_blk = jnp.transpose(gate_t.reshape(N_SLOTS, nblk, PEER_EB), (1, 0, 2))
    return pl.pallas_call(
        _peer_expert_kernel,
        grid=(nblk,),
        in_specs=[
            pl.BlockSpec((PEER_EB, N_SLOTS, D), lambda i: (i, 0, 0)),
            pl.BlockSpec((PEER_EB, D), lambda i: (i, 0)),
            pl.BlockSpec((1, N_SLOTS, PEER_EB), lambda i: (i, 0, 0)),
            pl.BlockSpec((PEER_EB, D), lambda i: (i, 0)),
            pl.BlockSpec((1, 1, D), lambda i: (i // blocks_per_mod, 0, 0)),
        ],
        out_specs=pl.BlockSpec((PEER_EB, D), lambda i: (i, 0)),
        out_shape=jax.ShapeDtypeStruct((T, D), jnp.float32),
        compiler_params=pltpu.CompilerParams(dimension_semantics=("arbitrary",),
                                             vmem_limit_bytes=VMEM_LIMIT),
        name="peer_expert",
    )(rows.reshape(T, N_SLOTS, D), h2, gate_blk, x, g2)


def _pack_tables(u_emb, v_emb):
    def pack(w):
        bits = lax.bitcast_convert_type(w.astype(jnp.bfloat16), jnp.uint16).astype(jnp.uint32)
        return lax.bitcast_convert_type(bits[:, :HALF_D] | (bits[:, HALF_D:] << 16), jnp.int32)
    return jnp.concatenate([pack(u_emb), pack(v_emb)], axis=1)


def _peer_residual(x, norm_g, shift, scale, gate2, wq_bf, keys_bf, table, tokens_per_mod, n_splits):
    T, D = x.shape
    idx_t, gate_t, h2 = _peer_route(x, norm_g, shift, scale, wq_bf, keys_bf, tokens_per_mod)
    idx_tok = jnp.transpose(idx_t)
    ts = T // n_splits
    mods_per_split = max(ts // tokens_per_mod, 1)
    outs = []
    for s in range(n_splits):
        rows = _gather_rows(table, idx_tok[s * ts:(s + 1) * ts].reshape(ts * N_SLOTS // SC_CHUNK, SC_CHUNK))
        m0 = (s * ts) // tokens_per_mod
        outs.append(_peer_expert(rows, h2[s * ts:(s + 1) * ts], gate_t[:, s * ts:(s + 1) * ts],
                                 x[s * ts:(s + 1) * ts], gate2[m0:m0 + mods_per_split], tokens_per_mod))
    return outs[0] if n_splits == 1 else jnp.concatenate(outs, axis=0)


def kernel(x, c, ctx, c_ctx, w_ada, b_ada, norm1_g, norm2_g, w_in, w_out,
           qn_a, kn_a, qn_b, kn_b, lam_q1, lam_k1, lam_q2, lam_k2, subln_g,
           s5_a_re, s5_a_im, s5_log_dt, s5_b_re, s5_b_im, s5_c_re, s5_c_im,
           s5_d, s5_w_glu, peer_wq, peer_keys, peer_u, peer_v):
    B, L, D = x.shape
    Lc = ctx.shape[1]
    depth = w_in.shape[0]
    rows = L // GRID_W
    row = jnp.repeat(jnp.arange(rows, dtype=jnp.int32), GRID_W)
    col = jnp.tile(jnp.arange(GRID_W, dtype=jnp.int32), rows)
    s_c = jax.nn.silu(c)
    s_cc = jax.nn.silu(c_ctx)
    for l in range(depth):
        with_ctx = l < depth - 1
        sh1, sc1, g1, sh2, sc2, g2 = jnp.split((s_c @ w_ada[l] + b_ada[l])[:, None, :], 6, axis=-1)
        csh1, csc1, cg1, csh2, csc2, cg2 = jnp.split(s_cc @ w_ada[l] + b_ada[l], 6, axis=-1)

        h = _modulate(_rms(x, norm1_g[l]), sh1, sc1)
        hc = _modulate(_rms(ctx, norm1_g[l]), csh1, csc1)
        mix, mixc = _token_mixers(h, hc, row, col, l, w_in[l], qn_a[l], kn_a[l], qn_b[l], kn_b[l],
                                  lam_q1[l], lam_k1[l], lam_q2[l], lam_k2[l], subln_g[l],
                                  s5_a_re[l], s5_a_im[l], s5_log_dt[l], s5_b_re[l], s5_b_im[l],
                                  s5_c_re[l], s5_c_im[l], s5_d[l], s5_w_glu[l], with_ctx)
        x = x + g1 * (mix @ w_out[l])

        wq_bf = peer_wq[l].astype(jnp.bfloat16)
        keys_bf = peer_keys[l].reshape(2 * PEER_HEADS, N_KEYS, PEER_HALF).astype(jnp.bfloat16)
        table = _pack_tables(peer_u[l], peer_v[l])
        x = _peer_residual(x.reshape(B * L, D), norm2_g[l], sh2, sc2, g2, wq_bf, keys_bf, table,
                           tokens_per_mod=L, n_splits=B).reshape(B, L, D)

        if with_ctx:
            ctx = ctx + cg1 * (mixc @ w_out[l])
            ctx = _peer_residual(ctx.reshape(B * Lc, D), norm2_g[l], csh2.reshape(1, 1, D), csc2.reshape(1, 1, D),
                                 cg2.reshape(1, 1, D), wq_bf, keys_bf, table,
                                 tokens_per_mod=B * Lc, n_splits=1).reshape(B, Lc, D)
    return x
```

```python
import functools
import math

import jax
import jax.numpy as jnp
from jax import lax
from jax.experimental import pallas as pl
from jax.experimental.pallas import tpu as pltpu
from jax.experimental.pallas import tpu_sc as plsc

D_MODEL = 1024
GRID_W = 64
EPS = 1e-6
ROPE_THETA = 10000.0
HEAD_DIM = 64
A_HEADS = 6
A_KV_HEADS = 2
A_GROUP = A_HEADS // A_KV_HEADS
A_Q = A_HEADS * HEAD_DIM
A_KV = A_KV_HEADS * HEAD_DIM
B_HEADS = 6
B_SUB = 32
B_VDIM = 2 * B_SUB
B_QK = B_HEADS * 2 * B_SUB
B_V = B_HEADS * B_VDIM
C_GROUPS = 16
C_GROUP_CH = 16
C_WIDTH = C_GROUPS * C_GROUP_CH
C_STATE = 64
PEER_HEADS = 8
N_KEYS = 128
PEER_TOPK = 16
PEER_HALF = 128
TOKEN_CHUNK = 128

LANES = 128
V_EXT = LANES
LOG2E = math.log2(math.e)
NEG_BIG = -1e30
VMEM_LIMIT = 48 * 1024 * 1024


def _online_softmax_step(q, k_t, v, m_ref, acc_ref):
    s = jnp.dot(q, k_t, preferred_element_type=jnp.float32)
    m_old = m_ref[...]
    m_new = jnp.maximum(m_old, jnp.max(s, axis=-1, keepdims=True))
    alpha = jnp.exp2(m_old - m_new)
    p = jnp.exp2(s - m_new).astype(jnp.bfloat16)
    acc_ref[...] = alpha * acc_ref[...] + jnp.dot(p, v, preferred_element_type=jnp.float32)
    m_ref[...] = m_new


def _attend(q, kt_at, v_at, m_ref, acc_ref, n_keys, tk):
    m_ref[...] = jnp.full(m_ref.shape, NEG_BIG, jnp.float32)
    acc_ref[...] = jnp.zeros(acc_ref.shape, jnp.float32)

    def body(j, carry):
        off = pl.multiple_of(j * tk, tk)
        _online_softmax_step(q, kt_at(off), v_at(off), m_ref, acc_ref)
        return carry

    lax.fori_loop(0, n_keys // tk, body, 0)
    return acc_ref[...]


def _gqa_kernel(q_ref, kt_ref, v_ref, o_ref, m_ref, acc_ref, *, tq, tk, n_keys):
    outs = []
    for kvh in range(A_KV_HEADS):
        q = jnp.concatenate(
            [q_ref[0, :, (kvh * A_GROUP + g) * HEAD_DIM:(kvh * A_GROUP + g + 1) * HEAD_DIM]
             for g in range(A_GROUP)], axis=0)
        acc = _attend(q,
                      lambda off: kt_ref[0, kvh, :, pl.ds(off, tk)],
                      lambda off: v_ref[0, kvh, pl.ds(off, tk), :],
                      m_ref, acc_ref, n_keys, tk)
        o = acc[:, :HEAD_DIM] / acc[:, HEAD_DIM:HEAD_DIM + 1]
        outs += [o[g * tq:(g + 1) * tq] for g in range(A_GROUP)]
    o_ref[0] = jnp.concatenate(outs, axis=-1).astype(o_ref.dtype)


def _diff_kernel(lam_ref, q_ref, kt_ref, v_ref, g_ref, o_ref, m_ref, acc_ref, *, tq, tk, n_keys, out_scale):
    lam = lam_ref[0]
    outs = []
    for h in range(2):
        q12 = jnp.concatenate(
            [q_ref[0, :, (2 * h + t) * B_SUB:(2 * h + t + 1) * B_SUB] for t in range(2)], axis=0)

        def step_keys(off, h=h):
            return kt_ref[0, 2 * h, :, pl.ds(off, tk)], kt_ref[0, 2 * h + 1, :, pl.ds(off, tk)]

        m_ref[...] = jnp.full(m_ref.shape, NEG_BIG, jnp.float32)
        acc_ref[...] = jnp.zeros(acc_ref.shape, jnp.float32)

        def body(j, carry, h=h, q12=q12):
            off = pl.multiple_of(j * tk, tk)
            k1, k2 = step_keys(off)
            s = jnp.concatenate(
                [jnp.dot(q12[:tq], k1, preferred_element_type=jnp.float32),
                 jnp.dot(q12[tq:], k2, preferred_element_type=jnp.float32)], axis=0)
            m_old = m_ref[...]
            m_new = jnp.maximum(m_old, jnp.max(s, axis=-1, keepdims=True))
            alpha = jnp.exp2(m_old - m_new)
            p = jnp.exp2(s - m_new).astype(jnp.bfloat16)
            acc_ref[...] = alpha * acc_ref[...] + jnp.dot(p, v_ref[0, h, pl.ds(off, tk), :],
                                                         preferred_element_type=jnp.float32)
            m_ref[...] = m_new
            return carry

        lax.fori_loop(0, n_keys // tk, body, 0)
        acc = acc_ref[...]
        o12 = acc[:, :B_VDIM] / acc[:, B_VDIM:B_VDIM + 1]
        o = o12[:tq] - lam * o12[tq:]
        o = o * lax.rsqrt(jnp.mean(o * o, axis=-1, keepdims=True) + EPS)
        outs.append(o * g_ref[...] * out_scale)
    o_ref[0] = jnp.concatenate(outs, axis=-1).astype(o_ref.dtype)


def _gqa_attention(q, k_t, v_ext, *, tq=256, tk=768):
    B, L, _ = q.shape
    S = k_t.shape[-1]
    kern = functools.partial(_gqa_kernel, tq=tq, tk=tk, n_keys=S)
    return pl.pallas_call(
        kern,
        grid=(B, L // tq),
        in_specs=[
            pl.BlockSpec((1, tq, A_Q), lambda b, i: (b, i, 0)),
            pl.BlockSpec((1, A_KV_HEADS, HEAD_DIM, S), lambda b, i: (b, 0, 0, 0)),
            pl.BlockSpec((1, A_KV_HEADS, S, V_EXT), lambda b, i: (b, 0, 0, 0)),
        ],
        out_specs=pl.BlockSpec((1, tq, A_Q), lambda b, i: (b, i, 0)),
        out_shape=jax.ShapeDtypeStruct((B, L, A_Q), jnp.bfloat16),
        scratch_shapes=[pltpu.VMEM((A_GROUP * tq, 1), jnp.float32),
                        pltpu.VMEM((A_GROUP * tq, V_EXT), jnp.float32)],
        compiler_params=pltpu.CompilerParams(
            dimension_semantics=("arbitrary", "arbitrary"), vmem_limit_bytes=VMEM_LIMIT),
        name="gqa_attention",
    )(q, k_t, v_ext)


def _diff_attention(q, k_t, v_ext, lam, subln_g, out_scale, *, tq=256, tk=768):
    B, L, _ = q.shape
    S = k_t.shape[-1]
    kern = functools.partial(_diff_kernel, tq=tq, tk=tk, n_keys=S, out_scale=out_scale)
    return pl.pallas_call(
        kern,
        grid=(B, B_HEADS // 2, L // tq),
        in_specs=[
            pl.BlockSpec(memory_space=pltpu.SMEM),
            pl.BlockSpec((1, tq, 2 * B_VDIM), lambda b, h, i: (b, i, h)),
            pl.BlockSpec((1, 4, B_SUB, S), lambda b, h, i: (b, h, 0, 0)),
            pl.BlockSpec((1, 2, S, V_EXT), lambda b, h, i: (b, h, 0, 0)),
            pl.BlockSpec((1, B_VDIM), lambda b, h, i: (0, 0)),
        ],
        out_specs=pl.BlockSpec((1, tq, 2 * B_VDIM), lambda b, h, i: (b, i, h)),
        out_shape=jax.ShapeDtypeStruct((B, L, B_V), jnp.bfloat16),
        scratch_shapes=[pltpu.VMEM((2 * tq, 1), jnp.float32),
                        pltpu.VMEM((2 * tq, V_EXT), jnp.float32)],
        compiler_params=pltpu.CompilerParams(
            dimension_semantics=("arbitrary", "arbitrary", "arbitrary"), vmem_limit_bytes=VMEM_LIMIT),
        name="diff_attention",
    )(lam.reshape(1).astype(jnp.float32), q, k_t, v_ext, subln_g.reshape(1, B_VDIM).astype(jnp.float32))


def _value_ext(v):
    B, S, H, d = v.shape
    ones = jnp.ones((B, S, H, 1), v.dtype)
    pad = jnp.zeros((B, S, H, V_EXT - d - 1), v.dtype)
    return jnp.transpose(jnp.concatenate([v, ones, pad], axis=-1), (0, 2, 1, 3)).astype(jnp.bfloat16)


def _rms(x, g):
    xf = x.astype(jnp.float32)
    y = xf * lax.rsqrt(jnp.mean(xf * xf, axis=-1, keepdims=True) + EPS)
    return (y * g.astype(jnp.float32)).astype(x.dtype)


def _modulate(h, shift, scale):
    return h * (1 + scale) + shift


def _rope_2d(x, row, col):
    d = x.shape[-1]
    half = d // 2
    inv = 1.0 / (ROPE_THETA ** (jnp.arange(0, half, 2, dtype=jnp.float32) / half))
    ang = jnp.concatenate([row.astype(jnp.float32)[:, None] * inv[None, :],
                           col.astype(jnp.float32)[:, None] * inv[None, :]], axis=-1)
    cos = jnp.cos(ang)[None, :, None, :]
    sin = jnp.sin(ang)[None, :, None, :]
    xr = x.astype(jnp.float32).reshape(x.shape[:-1] + (half, 2))
    x0, x1 = xr[..., 0], xr[..., 1]
    out = jnp.stack([x0 * cos - x1 * sin, x0 * sin + x1 * cos], axis=-1)
    return out.reshape(x.shape).astype(x.dtype)


def _gqa_block(q, k, v):
    s = jnp.einsum('bqkgd,bskd->bkgqs', q, k).astype(jnp.float32) * (HEAD_DIM ** -0.5)
    p = jax.nn.softmax(s, axis=-1).astype(v.dtype)
    return jnp.einsum('bkgqs,bskd->bqkgd', p, v)


def _diff_block(q, k, v, lam):
    s = jnp.einsum('bqhtd,bshtd->bthqs', q, k).astype(jnp.float32) * (B_SUB ** -0.5)
    p = jax.nn.softmax(s, axis=-1)
    a = (p[:, 0] - lam * p[:, 1]).astype(v.dtype)
    return jnp.einsum('bhqs,bshd->bqhd', a, v)


S5_TC = 64
S5_ROWS = 8
S5_CW = S5_TC * C_GROUP_CH
S5_SW = 2 * C_STATE


def _s5_operators(a_re, a_im, log_dt, b_re, b_im, c_re, c_im):
    f32 = jnp.float32
    tc, P, N, G = S5_TC, C_GROUP_CH, C_STATE, C_GROUPS
    j = jnp.arange(tc + 1, dtype=f32)
    toep, p_parts, q_parts, a_parts = [], [], [], []
    s_idx = jnp.arange(tc)[:, None]
    t_idx = jnp.arange(tc)[None, :]
    for dirn in range(2):
        A = lax.complex(a_re[dirn].astype(f32), a_im[dirn].astype(f32))
        adt = A * jnp.exp(log_dt[dirn].astype(f32))[:, None]
        abar = jnp.exp(adt)
        bbar = ((abar - 1) / A)[..., None] * lax.complex(b_re[dirn].astype(f32), b_im[dirn].astype(f32))
        cmat = lax.complex(c_re[dirn].astype(f32), c_im[dirn].astype(f32))
        pw = jnp.exp(adt[None] * j[:, None, None])
        kern = jnp.einsum('gpn,jgn,gnq->jgpq', cmat, pw[:tc], bbar).real
        lag = (s_idx - t_idx) if dirn == 1 else (t_idx - s_idx)
        tm = jnp.where((lag >= 0)[:, :, None, None, None], kern[jnp.clip(lag, 0, tc - 1)], 0.0)
        toep.append(jnp.transpose(tm, (2, 0, 4, 1, 3)).reshape(G, S5_CW, S5_CW))
        inj_pw = pw[:tc] if dirn == 1 else pw[:tc][::-1]
        inj = inj_pw[:, :, :, None] * bbar[None]
        p_parts.append(jnp.transpose(inj, (1, 0, 3, 2)).reshape(G, S5_CW, N))
        out_pw = pw[1:][::-1] if dirn == 1 else pw[1:]
        outm = cmat[None] * out_pw[:, :, None, :]
        q_parts.append(jnp.transpose(outm, (1, 3, 0, 2)).reshape(G, N, S5_CW))
        a_parts.append(pw[tc])
    zeros = jnp.zeros((G, N, S5_CW), f32)
    w_mat = jnp.concatenate([
        toep[0] + toep[1],
        q_parts[0].real, zeros, -q_parts[0].imag, zeros,
        zeros, q_parts[1].real, zeros, -q_parts[1].imag,
    ], axis=1)
    p_mat = jnp.concatenate([p_parts[0].real, p_parts[1].real, p_parts[0].imag, p_parts[1].imag], axis=-1)
    a_chunk = jnp.stack([jnp.concatenate([a_parts[0].real, a_parts[1].real], axis=-1),
                         jnp.concatenate([a_parts[0].imag, a_parts[1].imag], axis=-1)], axis=1)
    return p_mat.astype(jnp.bfloat16), w_mat.astype(jnp.bfloat16), a_chunk.astype(f32)


def _s5_scan_kernel(u_ref, p_ref, w_ref, a_ref, y_ref, s_scr, hs_scr, *, order_f, order_r):
    f32 = jnp.float32
    u = u_ref[0]
    s_scr[...] = jnp.dot(u, p_ref[0], preferred_element_type=f32)
    a_re = a_ref[0, 0:1, :]
    a_im = a_ref[0, 1:2, :]
    is_fwd = lax.broadcasted_iota(jnp.int32, (S5_ROWS, S5_SW), 1) < C_STATE
    h_re = jnp.zeros((S5_ROWS, S5_SW), f32)
    h_im = jnp.zeros((S5_ROWS, S5_SW), f32)
    for cf, cr in zip(order_f, order_r):
        rf = slice(cf * S5_ROWS, (cf + 1) * S5_ROWS)
        rr = slice(cr * S5_ROWS, (cr + 1) * S5_ROWS)
        hs_scr[rf, 0:S5_SW] = h_re
        hs_scr[rf, S5_SW:2 * S5_SW] = h_im
        hs_scr[rr, 2 * S5_SW:3 * S5_SW] = h_re
        hs_scr[rr, 3 * S5_SW:4 * S5_SW] = h_im
        s_re = jnp.where(is_fwd, s_scr[rf, 0:S5_SW], s_scr[rr, 0:S5_SW])
        s_im = jnp.where(is_fwd, s_scr[rf, S5_SW:2 * S5_SW], s_scr[rr, S5_SW:2 * S5_SW])
        h_re, h_im = a_re * h_re - a_im * h_im + s_re, a_re * h_im + a_im * h_re + s_im
    y = jnp.dot(u, w_ref[0, 0:S5_CW, :], preferred_element_type=f32)
    y += jnp.dot(hs_scr[...].astype(jnp.bfloat16), w_ref[0, S5_CW:, :], preferred_element_type=f32)
    y_ref[0] = y


def _s5_scan(u_seq, n_ctx_chunks, p_mat, w_mat, a_chunk):
    B, S, _ = u_seq.shape
    G, P = C_GROUPS, C_GROUP_CH
    nc = S // S5_TC
    R = nc * S5_ROWS
    ug = jnp.transpose(u_seq.reshape(B, nc, S5_TC, G, P), (3, 1, 0, 2, 4))
    ug = jnp.pad(ug, ((0, 0), (0, 0), (0, S5_ROWS - B), (0, 0), (0, 0)))
    ug = ug.reshape(G, R, S5_CW).astype(jnp.bfloat16)
    order_f = tuple(range(nc))
    order_r = tuple(range(n_ctx_chunks - 1, -1, -1)) + tuple(range(nc - 1, n_ctx_chunks - 1, -1))
    y = pl.pallas_call(
        functools.partial(_s5_scan_kernel, order_f=order_f, order_r=order_r),
        grid=(G,),
        in_specs=[
            pl.BlockSpec((1, R, S5_CW), lambda g: (g, 0, 0)),
            pl.BlockSpec((1, S5_CW, 2 * S5_SW), lambda g: (g, 0, 0)),
            pl.BlockSpec((1, S5_CW + 4 * S5_SW, S5_CW), lambda g: (g, 0, 0)),
            pl.BlockSpec((1, 2, S5_SW), lambda g: (g, 0, 0)),
        ],
        out_specs=pl.BlockSpec((1, R, S5_CW), lambda g: (g, 0, 0)),
        out_shape=jax.ShapeDtypeStruct((G, R, S5_CW), jnp.float32),
        scratch_shapes=[pltpu.VMEM((R, 2 * S5_SW), jnp.float32),
                        pltpu.VMEM((R, 4 * S5_SW), jnp.float32)],
        compiler_params=pltpu.CompilerParams(dimension_semantics=("arbitrary",),
                                             vmem_limit_bytes=VMEM_LIMIT),
        name="s5_scan",
    )(ug, p_mat, w_mat, a_chunk)
    y = y.reshape(G, nc, S5_ROWS, S5_TC, P)[:, :, :B]
    return jnp.transpose(y, (2, 1, 3, 0, 4)).reshape(B, S, C_WIDTH)


def _s5_glu_kernel(y_ref, u_ref, d_ref, w_ref, o_ref):
    y = y_ref[...] + d_ref[...] * u_ref[...]
    g = jax.nn.gelu(y)
    z = jnp.dot(g.astype(jnp.bfloat16), w_ref[...], preferred_element_type=jnp.float32)
    o_ref[...] = (g * jax.nn.sigmoid(z)).astype(o_ref.dtype)


def _s5_glu(y, u, d_skip, w_glu):
    T = y.shape[0]
    tb = math.gcd(T, 1024)
    return pl.pallas_call(
        _s5_glu_kernel,
        grid=(T // tb,),
        in_specs=[pl.BlockSpec((tb, C_WIDTH), lambda i: (i, 0)),
                  pl.BlockSpec((tb, C_WIDTH), lambda i: (i, 0)),
                  pl.BlockSpec((1, C_WIDTH), lambda i: (0, 0)),
                  pl.BlockSpec((C_WIDTH, C_WIDTH), lambda i: (0, 0))],
        out_specs=pl.BlockSpec((tb, C_WIDTH), lambda i: (i, 0)),
        out_shape=jax.ShapeDtypeStruct((T, C_WIDTH), jnp.float32),
        compiler_params=pltpu.CompilerParams(dimension_semantics=("arbitrary",)),
        name="s5_glu",
    )(y, u, d_skip.reshape(1, C_WIDTH).astype(jnp.float32), w_glu.astype(jnp.bfloat16))


def _s5_mixer(u, uc, a_re, a_im, log_dt, b_re, b_im, c_re, c_im, d_skip, w_glu, with_ctx):
    B, L, _ = u.shape
    Lc = uc.shape[1]
    p_mat, w_mat, a_chunk = _s5_operators(a_re, a_im, log_dt, b_re, b_im, c_re, c_im)
    u_seq = jnp.concatenate([uc, u], axis=1).astype(jnp.float32)
    y = _s5_scan(u_seq, Lc // S5_TC, p_mat, w_mat, a_chunk)
    if with_ctx:
        out = _s5_glu(y.reshape(B * (Lc + L), C_WIDTH), u_seq.reshape(B * (Lc + L), C_WIDTH), d_skip, w_glu)
        out = out.reshape(B, Lc + L, C_WIDTH)
        return out[:, Lc:], out[:, :Lc]
    out = _s5_glu(y[:, Lc:].reshape(B * L, C_WIDTH), u.astype(jnp.float32).reshape(B * L, C_WIDTH), d_skip, w_glu)
    return out.reshape(B, L, C_WIDTH), None


def _token_mixers(h, hc, row, col, layer_idx, w_in, qn_a, kn_a, qn_b, kn_b,
                  lam_q1, lam_k1, lam_q2, lam_k2, subln_g,
                  a_re, a_im, log_dt, b_re, b_im, c_re, c_im, d_skip, w_glu, with_ctx):
    B, L, _ = h.shape
    Lc = hc.shape[1]
    cuts = [A_Q, A_Q + A_KV, A_Q + 2 * A_KV, A_Q + 2 * A_KV + B_QK,
            A_Q + 2 * A_KV + 2 * B_QK, A_Q + 2 * A_KV + 2 * B_QK + B_V]
    qa, ka, va, qb, kb, vb, us = jnp.split(h @ w_in, cuts, axis=-1)
    qac, kac, vac, qbc, kbc, vbc, usc = jnp.split(hc @ w_in, cuts, axis=-1)

    qa = _rope_2d(_rms(qa.reshape(B, L, A_HEADS, HEAD_DIM), qn_a), row, col)
    ka = _rope_2d(_rms(ka.reshape(B, L, A_KV_HEADS, HEAD_DIM), kn_a), row, col)
    va = va.reshape(B, L, A_KV_HEADS, HEAD_DIM)
    kac = _rms(kac.reshape(B, Lc, A_KV_HEADS, HEAD_DIM), kn_a)
    vac = vac.reshape(B, Lc, A_KV_HEADS, HEAD_DIM)
    ka_all = jnp.concatenate([kac, ka], axis=1)
    va_all = jnp.concatenate([vac, va], axis=1)
    qa_s = (qa * (HEAD_DIM ** -0.5 * LOG2E)).reshape(B, L, A_Q).astype(jnp.bfloat16)
    ka_t = jnp.transpose(ka_all, (0, 2, 3, 1)).astype(jnp.bfloat16)
    o_a = _gqa_attention(qa_s, ka_t, _value_ext(va_all))

    lambda_init = 0.8 - 0.6 * math.exp(-0.3 * layer_idx)
    lam = (jnp.exp(jnp.sum(lam_q1.astype(jnp.float32) * lam_k1.astype(jnp.float32)))
           - jnp.exp(jnp.sum(lam_q2.astype(jnp.float32) * lam_k2.astype(jnp.float32))) + lambda_init)
    qb = _rope_2d(_rms(qb.reshape(B, L, 2 * B_HEADS, B_SUB), qn_b), row, col)
    kb = _rope_2d(_rms(kb.reshape(B, L, 2 * B_HEADS, B_SUB), kn_b), row, col)
    vb = vb.reshape(B, L, B_HEADS, B_VDIM)
    kbc = _rms(kbc.reshape(B, Lc, 2 * B_HEADS, B_SUB), kn_b)
    vbc = vbc.reshape(B, Lc, B_HEADS, B_VDIM)
    kb_all = jnp.concatenate([kbc, kb], axis=1)
    vb_all = jnp.concatenate([vbc, vb], axis=1)
    qb_s = (qb * (B_SUB ** -0.5 * LOG2E)).reshape(B, L, B_QK).astype(jnp.bfloat16)
    kb_t = jnp.transpose(kb_all, (0, 2, 3, 1)).astype(jnp.bfloat16)
    o_b = _diff_attention(qb_s, kb_t, _value_ext(vb_all), lam, subln_g, 1 - lambda_init)

    o_c, o_cc = _s5_mixer(us, usc, a_re, a_im, log_dt, b_re, b_im, c_re, c_im, d_skip, w_glu, with_ctx)

    mix = jnp.concatenate([o_a.astype(h.dtype), o_b.astype(h.dtype), o_c], axis=-1)
    if not with_ctx:
        return mix, None
    qac = _rms(qac.reshape(B, Lc, A_HEADS, HEAD_DIM), qn_a).reshape(B, Lc, A_KV_HEADS, A_GROUP, HEAD_DIM)
    o_ac = _gqa_block(qac, kac, vac).reshape(B, Lc, A_Q)
    kbc5 = kbc.reshape(B, Lc, B_HEADS, 2, B_SUB)
    qbc = _rms(qbc.reshape(B, Lc, 2 * B_HEADS, B_SUB), qn_b).reshape(B, Lc, B_HEADS, 2, B_SUB)
    o_bc = _diff_block(qbc, kbc5, vbc, lam)
    o_bc = (_rms(o_bc, subln_g) * (1 - lambda_init)).reshape(B, Lc, B_V)
    mixc = jnp.concatenate([o_ac, o_bc, o_cc], axis=-1)
    return mix, mixc


PEER_TB = 256
PEER_EB = 8
N_SLOTS = PEER_HEADS * PEER_TOPK
HALF_D = D_MODEL // 2
CAND_ROWS = 80
INVALID_FLAT = 1.0e9
SC_CORES = 2
SC_SUBCORES = 16
SC_CHUNK = 32


def _cand_flat_table():
    f = [float(b) for b in range(16)]
    for a in range(1, 8):
        f += [float(a * 16 + b) if (a + 1) * (b + 1) <= 16 else INVALID_FLAT for b in range(8)]
    f += [float(a * 16) for a in range(8, 16)]
    return jnp.broadcast_to(jnp.asarray(f, jnp.float32)[:, None], (CAND_ROWS, PEER_TB))


def _pair_rows(first, second):
    blocks = [first[0:1] + second]
    blocks += [first[a:a + 1] + second[0:8] for a in range(1, 8)]
    blocks += [first[8:16] + second[0:1]]
    return jnp.concatenate(blocks, axis=0)


def _peer_route_kernel(x_ref, g_ref, sh_ref, sc_ref, wq_ref, keys_ref, ftab_ref,
                       idx_ref, gate_ref, h_ref, val_scr, id_scr, sc_scr, e_scr):
    f32 = jnp.float32
    x = x_ref[...]
    y = x * lax.rsqrt(jnp.mean(x * x, axis=-1, keepdims=True) + EPS) * g_ref[...]
    h2 = (y * (1.0 + sc_ref[0]) + sh_ref[0]).astype(jnp.bfloat16)
    h_ref[...] = h2
    q = jnp.dot(h2, wq_ref[...], preferred_element_type=f32)
    ftab = ftab_ref[...]
    key_id = lax.broadcasted_iota(jnp.int32, (N_KEYS, PEER_TB), 0).astype(f32)
    neg_inf = -jnp.inf
    for head in range(PEER_HEADS):
        for t in range(2):
            ht = head * 2 + t
            qh = q[:, ht * PEER_HALF:(ht + 1) * PEER_HALF].astype(jnp.bfloat16)
            s = lax.dot_general(keys_ref[ht], qh, (((1,), (1,)), ((), ())),
                                preferred_element_type=f32)
            for r in range(PEER_TOPK):
                m = jnp.max(s, axis=0, keepdims=True)
                pick = jnp.min(jnp.where(s == m, key_id, float(N_KEYS)), axis=0, keepdims=True)
                s = jnp.where(key_id == pick, neg_inf, s)
                val_scr[t, r:r + 1, :] = m
                id_scr[t, r:r + 1, :] = pick
        cand = _pair_rows(val_scr[0], val_scr[1])
        cand = jnp.where(ftab < float(PEER_TOPK * PEER_TOPK), cand, neg_inf)
        expert = _pair_rows(id_scr[0] * float(N_KEYS), id_scr[1])
        for r in range(PEER_TOPK):
            m = jnp.max(cand, axis=0, keepdims=True)
            pick = jnp.min(jnp.where(cand == m, ftab, INVALID_FLAT), axis=0, keepdims=True)
            hit = ftab == pick
            sc_scr[r:r + 1, :] = m
            e_scr[r:r + 1, :] = jnp.sum(jnp.where(hit, expert, 0.0), axis=0, keepdims=True)
            cand = jnp.where(hit, neg_inf, cand)
        sc = sc_scr[...]
        p = jnp.exp(sc - sc[0:1])
        gate = p / jnp.sum(p, axis=0, keepdims=True)
        idx_ref[head * PEER_TOPK:(head + 1) * PEER_TOPK, :] = e_scr[...].astype(jnp.int32)
        gate_ref[head * PEER_TOPK:(head + 1) * PEER_TOPK, :] = gate


def _peer_route(x, norm_g, shift, scale, wq_bf, keys_bf, tokens_per_mod):
    T, D = x.shape
    nq = wq_bf.shape[1]
    blocks_per_mod = tokens_per_mod // PEER_TB
    return pl.pallas_call(
        _peer_route_kernel,
        grid=(T // PEER_TB,),
        in_specs=[
            pl.BlockSpec((PEER_TB, D), lambda i: (i, 0)),
            pl.BlockSpec((1, D), lambda i: (0, 0)),
            pl.BlockSpec((1, 1, D), lambda i: (i // blocks_per_mod, 0, 0)),
            pl.BlockSpec((1, 1, D), lambda i: (i // blocks_per_mod, 0, 0)),
            pl.BlockSpec((D, nq), lambda i: (0, 0)),
            pl.BlockSpec((2 * PEER_HEADS, N_KEYS, PEER_HALF), lambda i: (0, 0, 0)),
            pl.BlockSpec((CAND_ROWS, PEER_TB), lambda i: (0, 0)),
        ],
        out_specs=[
            pl.BlockSpec((N_SLOTS, PEER_TB), lambda i: (0, i)),
            pl.BlockSpec((N_SLOTS, PEER_TB), lambda i: (0, i)),
            pl.BlockSpec((PEER_TB, D), lambda i: (i, 0)),
        ],
        out_shape=[
            jax.ShapeDtypeStruct((N_SLOTS, T), jnp.int32),
            jax.ShapeDtypeStruct((N_SLOTS, T), jnp.float32),
            jax.ShapeDtypeStruct((T, D), jnp.bfloat16),
        ],
        scratch_shapes=[pltpu.VMEM((2, PEER_TOPK, PEER_TB), jnp.float32),
                        pltpu.VMEM((2, PEER_TOPK, PEER_TB), jnp.float32),
                        pltpu.VMEM((PEER_TOPK, PEER_TB), jnp.float32),
                        pltpu.VMEM((PEER_TOPK, PEER_TB), jnp.float32)],
        compiler_params=pltpu.CompilerParams(dimension_semantics=("arbitrary",),
                                             vmem_limit_bytes=VMEM_LIMIT),
        name="peer_route",
    )(x, norm_g.reshape(1, D).astype(jnp.float32), shift, scale, wq_bf, keys_bf, _cand_flat_table())


def _gather_rows(table, idx2d):
    n_chunks_total, chunk = idx2d.shape
    width = table.shape[1]
    n_workers = SC_CORES * SC_SUBCORES
    n_chunks = n_chunks_total // n_workers
    mesh = plsc.VectorSubcoreMesh(core_axis_name="c", subcore_axis_name="s")

    def body(table_hbm, idx_hbm, out_hbm, idx_v, rows_v, gsem, wsem):
        wid = lax.axis_index("s") * SC_CORES + lax.axis_index("c")
        cbase = wid * n_chunks

        def gather_copy(slot):
            return pltpu.make_async_copy(table_hbm.at[idx_v.at[slot]], rows_v.at[slot], gsem.at[slot])

        def writeback_copy(r, slot):
            return pltpu.make_async_copy(rows_v.at[slot], out_hbm.at[pl.ds(r * chunk, chunk)], wsem.at[slot])

        pltpu.sync_copy(idx_hbm.at[cbase], idx_v.at[0])
        gather_copy(0).start()

        def step(i2, carry):
            for slot in range(2):
                i = i2 * 2 + slot
                r = cbase + i
                other = 1 - slot
                gather_copy(slot).wait()

                @pl.when(i >= 1)
                def _():
                    writeback_copy(r - 1, other).wait()

                @pl.when(i + 1 < n_chunks)
                def _():
                    pltpu.sync_copy(idx_hbm.at[r + 1], idx_v.at[other])
                    gather_copy(other).start()

                writeback_copy(r, slot).start()
            return carry

        lax.fori_loop(0, n_chunks // 2, step, 0)
        writeback_copy(cbase + n_chunks - 1, 1).wait()

    return pl.kernel(
        body, mesh=mesh,
        out_type=jax.ShapeDtypeStruct((n_chunks_total * chunk, width), jnp.int32),
        scratch_types=[pltpu.VMEM((2, chunk), jnp.int32),
                       pltpu.VMEM((2, chunk, width), jnp.int32),
                       pltpu.SemaphoreType.DMA((2,)),
                       pltpu.SemaphoreType.DMA((2,))],
        name="peer_gather",
    )(table, idx2d)


def _unpack_pair(words):
    lo = lax.bitcast_convert_type(words << 16, jnp.float32)
    hi = lax.bitcast_convert_type(words & jnp.int32(-65536), jnp.float32)
    return lo, hi


def _peer_expert_kernel(rows_ref, h_ref, gate_ref, x_ref, g2_ref, o_ref):
    f32 = jnp.float32
    gate = gate_ref[0]
    outs = []
    for t in range(PEER_EB):
        words = rows_ref[t]
        u_lo, u_hi = _unpack_pair(words[:, :HALF_D])
        h = h_ref[t:t + 1, :].astype(f32)
        act = jnp.sum(u_lo * h[:, :HALF_D] + u_hi * h[:, HALF_D:], axis=-1, keepdims=True)
        wgt = jax.nn.gelu(act) * gate[:, t:t + 1]
        v_lo, v_hi = _unpack_pair(words[:, HALF_D:])
        outs.append(jnp.concatenate([jnp.sum(wgt * v_lo, axis=0, keepdims=True),
                                     jnp.sum(wgt * v_hi, axis=0, keepdims=True)], axis=-1))
    o_ref[...] = x_ref[...] + g2_ref[0] * jnp.concatenate(outs, axis=0)


def _peer_expert(rows, h2, gate_t, x, g2, tokens_per_mod):
    T, D = x.shape
    nblk = T // PEER_EB
    blocks_per_mod = tokens_per_mod // PEER_EB
    gate_blk = jnp.transpose(gate_t.reshape(N_SLOTS, nblk, PEER_EB), (1, 0, 2))
    return pl.pallas_call(
        _peer_expert_kernel,
        grid=(nblk,),
        in_specs=[
            pl.BlockSpec((PEER_EB, N_SLOTS, D), lambda i: (i, 0, 0)),
            pl.BlockSpec((PEER_EB, D), lambda i: (i, 0)),
            pl.BlockSpec((1, N_SLOTS, PEER_EB), lambda i: (i, 0, 0)),
            pl.BlockSpec((PEER_EB, D), lambda i: (i, 0)),
            pl.BlockSpec((1, 1, D), lambda i: (i // blocks_per_mod, 0, 0)),
        ],
        out_specs=pl.BlockSpec((PEER_EB, D), lambda i: (i, 0)),
        out_shape=jax.ShapeDtypeStruct((T, D), jnp.float32),
        compiler_params=pltpu.CompilerParams(dimension_semantics=("arbitrary",),
                                             vmem_limit_bytes=VMEM_LIMIT),
        name="peer_expert",
    )(rows.reshape(T, N_SLOTS, D), h2, gate_blk, x, g2)


def _pack_tables(u_emb, v_emb):
    def pack(w):
        bits = lax.bitcast_convert_type(w.astype(jnp.bfloat16), jnp.uint16).astype(jnp.uint32)
        return lax.bitcast_convert_type(bits[:, :HALF_D] | (bits[:, HALF_D:] << 16), jnp.int32)
    return jnp.concatenate([pack(u_emb), pack(v_emb)], axis=1)


def _peer_residual(x, norm_g, shift, scale, gate2, wq_bf, keys_bf, table, tokens_per_mod, n_splits):
    T, D = x.shape
    idx_t, gate_t, h2 = _peer_route(x, norm_g, shift, scale, wq_bf, keys_bf, tokens_per_mod)
    idx_tok = jnp.transpose(idx_t)
    ts = T // n_splits
    mods_per_split = max(ts // tokens_per_mod, 1)
    outs = []
    for s in range(n_splits):
        rows = _gather_rows(table, idx_tok[s * ts:(s + 1) * ts].reshape(ts * N_SLOTS // SC_CHUNK, SC_CHUNK))
        m0 = (s * ts) // tokens_per_mod
        outs.append(_peer_expert(rows, h2[s * ts:(s + 1) * ts], gate_t[:, s * ts:(s + 1) * ts],
                                 x[s * ts:(s + 1) * ts], gate2[m0:m0 + mods_per_split], tokens_per_mod))
    return outs[0] if n_splits == 1 else jnp.concatenate(outs, axis=0)


def kernel(x, c, ctx, c_ctx, w_ada, b_ada, norm1_g, norm2_g, w_in, w_out,
           qn_a, kn_a, qn_b, kn_b, lam_q1, lam_k1, lam_q2, lam_k2, subln_g,
           s5_a_re, s5_a_im, s5_log_dt, s5_b_re, s5_b_im, s5_c_re, s5_c_im,
           s5_d, s5_w_glu, peer_wq, peer_keys, peer_u, peer_v):
    B, L, D = x.shape
    Lc = ctx.shape[1]
    depth = w_in.shape[0]
    rows = L // GRID_W
    row = jnp.repeat(jnp.arange(rows, dtype=jnp.int32), GRID_W)
    col = jnp.tile(jnp.arange(GRID_W, dtype=jnp.int32), rows)
    s_c = jax.nn.silu(c)
    s_cc = jax.nn.silu(c_ctx)
    for l in range(depth):
        with_ctx = l < depth - 1
        sh1, sc1, g1, sh2, sc2, g2 = jnp.split((s_c @ w_ada[l] + b_ada[l])[:, None, :], 6, axis=-1)
        csh1, csc1, cg1, csh2, csc2, cg2 = jnp.split(s_cc @ w_ada[l] + b_ada[l], 6, axis=-1)

        h = _modulate(_rms(x, norm1_g[l]), sh1, sc1)
        hc = _modulate(_rms(ctx, norm1_g[l]), csh1, csc1)
        mix, mixc = _token_mixers(h, hc, row, col, l, w_in[l], qn_a[l], kn_a[l], qn_b[l], kn_b[l],
                                  lam_q1[l], lam_k1[l], lam_q2[l], lam_k2[l], subln_g[l],
                                  s5_a_re[l], s5_a_im[l], s5_log_dt[l], s5_b_re[l], s5_b_im[l],
                                  s5_c_re[l], s5_c_im[l], s5_d[l], s5_w_glu[l], with_ctx)
        x = x + g1 * (mix @ w_out[l])

        wq_bf = peer_wq[l].astype(jnp.bfloat16)
        keys_bf = peer_keys[l].reshape(2 * PEER_HEADS, N_KEYS, PEER_HALF).astype(jnp.bfloat16)
        table = _pack_tables(peer_u[l], peer_v[l])
        x = _peer_residual(x.reshape(B * L, D), norm2_g[l], sh2, sc2, g2, wq_bf, keys_bf, table,
                           tokens_per_mod=L, n_splits=B).reshape(B, L, D)

        if with_ctx:
            ctx = ctx + cg1 * (mixc @ w_out[l])
            ctx = _peer_residual(ctx.reshape(B * Lc, D), norm2_g[l], csh2.reshape(1, 1, D), csc2.reshape(1, 1, D),
                                 cg2.reshape(1, 1, D), wq_bf, keys_bf, table,
                                 tokens_per_mod=B * Lc, n_splits=1).reshape(B, Lc, D)
    return x
```

```python
import functools
import math

import jax
import jax.numpy as jnp
from jax import lax
from jax.experimental import pallas as pl
from jax.experimental.pallas import tpu as pltpu
from jax.experimental.pallas import tpu_sc as plsc

D_MODEL = 1024
GRID_W = 64
EPS = 1e-6
ROPE_THETA = 10000.0
HEAD_DIM = 64
A_HEADS = 6
A_KV_HEADS = 2
A_GROUP = A_HEADS // A_KV_HEADS
A_Q = A_HEADS * HEAD_DIM
A_KV = A_KV_HEADS * HEAD_DIM
B_HEADS = 6
B_SUB = 32
B_VDIM = 2 * B_SUB
B_QK = B_HEADS * 2 * B_SUB
B_V = B_HEADS * B_VDIM
C_GROUPS = 16
C_GROUP_CH = 16
C_WIDTH = C_GROUPS * C_GROUP_CH
C_STATE = 64
PEER_HEADS = 8
N_KEYS = 128
PEER_TOPK = 16
PEER_HALF = 128
TOKEN_CHUNK = 128

LANES = 128
V_EXT = LANES
LOG2E = math.log2(math.e)
NEG_BIG = -1e30
VMEM_LIMIT = 48 * 1024 * 1024


TQ = LANES
ATTN_TK = 1408
K_PACK = LANES


def _flash_unit(k, q_t, v_t, m_ref, acc_ref, unit):
    s = jnp.dot(k, q_t, preferred_element_type=jnp.float32)
    m_old = m_ref[unit]
    m_new = jnp.maximum(m_old, jnp.max(s, axis=0, keepdims=True))
    alpha = jnp.exp2(m_old - m_new)
    p = jnp.exp2(s - m_new).astype(jnp.bfloat16)
    acc_ref[unit] = alpha * acc_ref[unit] + jnp.dot(v_t, p, preferred_element_type=jnp.float32)
    m_ref[unit] = m_new


def _flash_all_units(q_ref, k_ref, vt_ref, m_ref, acc_ref, *, n_units, units_per_group, tk, n_keys):
    m_ref[...] = jnp.full(m_ref.shape, NEG_BIG, jnp.float32)
    acc_ref[...] = jnp.zeros(acc_ref.shape, jnp.float32)

    def body(j, carry):
        off = pl.multiple_of(j * tk, tk)
        k = k_ref[0, pl.ds(off, tk), :]
        for u in range(n_units):
            _flash_unit(k, q_ref[0, u], vt_ref[0, u // units_per_group, :, pl.ds(off, tk)], m_ref, acc_ref, u)
        return carry

    lax.fori_loop(0, n_keys // tk, body, 0)


def _gqa_kernel(q_ref, k_ref, vt_ref, o_ref, m_ref, acc_ref, *, tk, n_keys):
    _flash_all_units(q_ref, k_ref, vt_ref, m_ref, acc_ref,
                     n_units=A_HEADS, units_per_group=A_GROUP, tk=tk, n_keys=n_keys)
    for h in range(A_HEADS):
        acc = acc_ref[h]
        o_ref[0, h * HEAD_DIM:(h + 1) * HEAD_DIM, :] = (
            acc[:HEAD_DIM] / acc[HEAD_DIM:HEAD_DIM + 1]).astype(o_ref.dtype)


def _diff_kernel(lam_ref, q_ref, k_ref, vt_ref, g_ref, o_ref, m_ref, acc_ref, *, tk, n_keys, out_scale):
    _flash_all_units(q_ref, k_ref, vt_ref, m_ref, acc_ref,
                     n_units=4, units_per_group=2, tk=tk, n_keys=n_keys)
    lam = lam_ref[0]
    for h in range(2):
        a1 = acc_ref[2 * h]
        a2 = acc_ref[2 * h + 1]
        o = a1[:B_VDIM] / a1[B_VDIM:B_VDIM + 1] - lam * (a2[:B_VDIM] / a2[B_VDIM:B_VDIM + 1])
        o = o * lax.rsqrt(jnp.mean(o * o, axis=0, keepdims=True) + EPS)
        o_ref[0, h * B_VDIM:(h + 1) * B_VDIM, :] = (o * g_ref[...] * out_scale).astype(o_ref.dtype)


def _gqa_attention(q_t, k, v_t, *, tk=ATTN_TK):
    B, _, _, L = q_t.shape
    S = k.shape[1]
    return pl.pallas_call(
        functools.partial(_gqa_kernel, tk=tk, n_keys=S),
        grid=(B, L // TQ),
        in_specs=[
            pl.BlockSpec((1, A_HEADS, K_PACK, TQ), lambda b, i: (b, 0, 0, i)),
            pl.BlockSpec((1, S, K_PACK), lambda b, i: (b, 0, 0)),
            pl.BlockSpec((1, A_KV_HEADS, V_EXT, S), lambda b, i: (b, 0, 0, 0)),
        ],
        out_specs=pl.BlockSpec((1, A_Q, TQ), lambda b, i: (b, 0, i)),
        out_shape=jax.ShapeDtypeStruct((B, A_Q, L), jnp.bfloat16),
        scratch_shapes=[pltpu.VMEM((A_HEADS, 1, TQ), jnp.float32),
                        pltpu.VMEM((A_HEADS, V_EXT, TQ), jnp.float32)],
        compiler_params=pltpu.CompilerParams(
            dimension_semantics=("arbitrary", "arbitrary"), vmem_limit_bytes=VMEM_LIMIT),
        name="gqa_attention",
    )(q_t, k, v_t)


def _diff_attention(q_t, k, v_t, lam, subln_g, out_scale, *, tk=ATTN_TK):
    B, _, _, L = q_t.shape
    S = k.shape[1]
    return pl.pallas_call(
        functools.partial(_diff_kernel, tk=tk, n_keys=S, out_scale=out_scale),
        grid=(B, B_HEADS // 2, L // TQ),
        in_specs=[
            pl.BlockSpec(memory_space=pltpu.SMEM),
            pl.BlockSpec((1, 4, K_PACK, TQ), lambda b, h, i: (b, h, 0, i)),
            pl.BlockSpec((1, S, K_PACK), lambda b, h, i: (b, 0, h)),
            pl.BlockSpec((1, 2, V_EXT, S), lambda b, h, i: (b, h, 0, 0)),
            pl.BlockSpec((B_VDIM, 1), lambda b, h, i: (0, 0)),
        ],
        out_specs=pl.BlockSpec((1, 2 * B_VDIM, TQ), lambda b, h, i: (b, h, i)),
        out_shape=jax.ShapeDtypeStruct((B, B_V, L), jnp.bfloat16),
        scratch_shapes=[pltpu.VMEM((4, 1, TQ), jnp.float32),
                        pltpu.VMEM((4, V_EXT, TQ), jnp.float32)],
        compiler_params=pltpu.CompilerParams(
            dimension_semantics=("arbitrary", "arbitrary", "arbitrary"), vmem_limit_bytes=VMEM_LIMIT),
        name="diff_attention",
    )(lam.reshape(1).astype(jnp.float32), q_t, k, v_t, subln_g.reshape(B_VDIM, 1).astype(jnp.float32))


def _value_ext_t(v):
    B, S, H, d = v.shape
    ones = jnp.ones((B, S, H, 1), v.dtype)
    pad = jnp.zeros((B, S, H, V_EXT - d - 1), v.dtype)
    return jnp.transpose(jnp.concatenate([v, ones, pad], axis=-1), (0, 2, 3, 1)).astype(jnp.bfloat16)


def _query_ext_t(q, offsets):
    B, L, U, d = q.shape
    q_t = jnp.transpose(q, (0, 2, 3, 1)).astype(jnp.bfloat16)
    units = [jnp.pad(q_t[:, u], ((0, 0), (off, K_PACK - d - off), (0, 0))) for u, off in enumerate(offsets)]
    return jnp.stack(units, axis=1)


def _rms(x, g):
    xf = x.astype(jnp.float32)
    y = xf * lax.rsqrt(jnp.mean(xf * xf, axis=-1, keepdims=True) + EPS)
    return (y * g.astype(jnp.float32)).astype(x.dtype)


def _modulate(h, shift, scale):
    return h * (1 + scale) + shift


def _rope_2d(x, row, col):
    d = x.shape[-1]
    half = d // 2
    inv = 1.0 / (ROPE_THETA ** (jnp.arange(0, half, 2, dtype=jnp.float32) / half))
    ang = jnp.concatenate([row.astype(jnp.float32)[:, None] * inv[None, :],
                           col.astype(jnp.float32)[:, None] * inv[None, :]], axis=-1)
    cos = jnp.cos(ang)[None, :, None, :]
    sin = jnp.sin(ang)[None, :, None, :]
    xr = x.astype(jnp.float32).reshape(x.shape[:-1] + (half, 2))
    x0, x1 = xr[..., 0], xr[..., 1]
    out = jnp.stack([x0 * cos - x1 * sin, x0 * sin + x1 * cos], axis=-1)
    return out.reshape(x.shape).astype(x.dtype)


def _gqa_block(q, k, v):
    s = jnp.einsum('bqkgd,bskd->bkgqs', q, k).astype(jnp.float32) * (HEAD_DIM ** -0.5)
    p = jax.nn.softmax(s, axis=-1).astype(v.dtype)
    return jnp.einsum('bkgqs,bskd->bqkgd', p, v)


def _diff_block(q, k, v, lam):
    s = jnp.einsum('bqhtd,bshtd->bthqs', q, k).astype(jnp.float32) * (B_SUB ** -0.5)
    p = jax.nn.softmax(s, axis=-1)
    a = (p[:, 0] - lam * p[:, 1]).astype(v.dtype)
    return jnp.einsum('bhqs,bshd->bqhd', a, v)


S5_TC = 64
S5_ROWS = 8
S5_CW = S5_TC * C_GROUP_CH
S5_SW = 2 * C_STATE


def _s5_operators(a_re, a_im, log_dt, b_re, b_im, c_re, c_im):
    f32 = jnp.float32
    tc, P, N, G = S5_TC, C_GROUP_CH, C_STATE, C_GROUPS
    j = jnp.arange(tc + 1, dtype=f32)
    toep, p_parts, q_parts, a_parts = [], [], [], []
    s_idx = jnp.arange(tc)[:, None]
    t_idx = jnp.arange(tc)[None, :]
    for dirn in range(2):
        A = lax.complex(a_re[dirn].astype(f32), a_im[dirn].astype(f32))
        adt = A * jnp.exp(log_dt[dirn].astype(f32))[:, None]
        abar = jnp.exp(adt)
        bbar = ((abar - 1) / A)[..., None] * lax.complex(b_re[dirn].astype(f32), b_im[dirn].astype(f32))
        cmat = lax.complex(c_re[dirn].astype(f32), c_im[dirn].astype(f32))
        pw = jnp.exp(adt[None] * j[:, None, None])
        kern = jnp.einsum('gpn,jgn,gnq->jgpq', cmat, pw[:tc], bbar).real
        lag = (s_idx - t_idx) if dirn == 1 else (t_idx - s_idx)
        tm = jnp.where((lag >= 0)[:, :, None, None, None], kern[jnp.clip(lag, 0, tc - 1)], 0.0)
        toep.append(jnp.transpose(tm, (2, 0, 4, 1, 3)).reshape(G, S5_CW, S5_CW))
        inj_pw = pw[:tc] if dirn == 1 else pw[:tc][::-1]
        inj = inj_pw[:, :, :, None] * bbar[None]
        p_parts.append(jnp.transpose(inj, (1, 0, 3, 2)).reshape(G, S5_CW, N))
        out_pw = pw[1:][::-1] if dirn == 1 else pw[1:]
        outm = cmat[None] * out_pw[:, :, None, :]
        q_parts.append(jnp.transpose(outm, (1, 3, 0, 2)).reshape(G, N, S5_CW))
        a_parts.append(pw[tc])
    zeros = jnp.zeros((G, N, S5_CW), f32)
    w_mat = jnp.concatenate([
        toep[0] + toep[1],
        q_parts[0].real, zeros, -q_parts[0].imag, zeros,
        zeros, q_parts[1].real, zeros, -q_parts[1].imag,
    ], axis=1)
    p_mat = jnp.concatenate([p_parts[0].real, p_parts[1].real, p_parts[0].imag, p_parts[1].imag], axis=-1)
    a_chunk = jnp.stack([jnp.concatenate([a_parts[0].real, a_parts[1].real], axis=-1),
                         jnp.concatenate([a_parts[0].imag, a_parts[1].imag], axis=-1)], axis=1)
    return p_mat.astype(jnp.bfloat16), w_mat.astype(jnp.bfloat16), a_chunk.astype(f32)


def _s5_scan_kernel(u_ref, p_ref, w_ref, a_ref, y_ref, s_scr, hs_scr, *, order_f, order_r):
    f32 = jnp.float32
    u = u_ref[0]
    s_scr[...] = jnp.dot(u, p_ref[0], preferred_element_type=f32)
    a_re = a_ref[0, 0:1, :]
    a_im = a_ref[0, 1:2, :]
    is_fwd = lax.broadcasted_iota(jnp.int32, (S5_ROWS, S5_SW), 1) < C_STATE
    h_re = jnp.zeros((S5_ROWS, S5_SW), f32)
    h_im = jnp.zeros((S5_ROWS, S5_SW), f32)
    for cf, cr in zip(order_f, order_r):
        rf = slice(cf * S5_ROWS, (cf + 1) * S5_ROWS)
        rr = slice(cr * S5_ROWS, (cr + 1) * S5_ROWS)
        hs_scr[rf, 0:S5_SW] = h_re
        hs_scr[rf, S5_SW:2 * S5_SW] = h_im
        hs_scr[rr, 2 * S5_SW:3 * S5_SW] = h_re
        hs_scr[rr, 3 * S5_SW:4 * S5_SW] = h_im
        s_re = jnp.where(is_fwd, s_scr[rf, 0:S5_SW], s_scr[rr, 0:S5_SW])
        s_im = jnp.where(is_fwd, s_scr[rf, S5_SW:2 * S5_SW], s_scr[rr, S5_SW:2 * S5_SW])
        h_re, h_im = a_re * h_re - a_im * h_im + s_re, a_re * h_im + a_im * h_re + s_im
    y = jnp.dot(u, w_ref[0, 0:S5_CW, :], preferred_element_type=f32)
    y += jnp.dot(hs_scr[...].astype(jnp.bfloat16), w_ref[0, S5_CW:, :], preferred_element_type=f32)
    y_ref[0] = y


def _s5_scan(u_seq, n_ctx_chunks, p_mat, w_mat, a_chunk):
    B, S, _ = u_seq.shape
    G, P = C_GROUPS, C_GROUP_CH
    nc = S // S5_TC
    R = nc * S5_ROWS
    ug = jnp.transpose(u_seq.reshape(B, nc, S5_TC, G, P), (3, 1, 0, 2, 4))
    ug = jnp.pad(ug, ((0, 0), (0, 0), (0, S5_ROWS - B), (0, 0), (0, 0)))
    ug = ug.reshape(G, R, S5_CW).astype(jnp.bfloat16)
    order_f = tuple(range(nc))
    order_r = tuple(range(n_ctx_chunks - 1, -1, -1)) + tuple(range(nc - 1, n_ctx_chunks - 1, -1))
    y = pl.pallas_call(
        functools.partial(_s5_scan_kernel, order_f=order_f, order_r=order_r),
        grid=(G,),
        in_specs=[
            pl.BlockSpec((1, R, S5_CW), lambda g: (g, 0, 0)),
            pl.BlockSpec((1, S5_CW, 2 * S5_SW), lambda g: (g, 0, 0)),
            pl.BlockSpec((1, S5_CW + 4 * S5_SW, S5_CW), lambda g: (g, 0, 0)),
            pl.BlockSpec((1, 2, S5_SW), lambda g: (g, 0, 0)),
        ],
        out_specs=pl.BlockSpec((1, R, S5_CW), lambda g: (g, 0, 0)),
        out_shape=jax.ShapeDtypeStruct((G, R, S5_CW), jnp.float32),
        scratch_shapes=[pltpu.VMEM((R, 2 * S5_SW), jnp.float32),
                        pltpu.VMEM((R, 4 * S5_SW), jnp.float32)],
        compiler_params=pltpu.CompilerParams(dimension_semantics=("arbitrary",),
                                             vmem_limit_bytes=VMEM_LIMIT),
        name="s5_scan",
    )(ug, p_mat, w_mat, a_chunk)
    y = y.reshape(G, nc, S5_ROWS, S5_TC, P)[:, :, :B]
    return jnp.transpose(y, (2, 1, 3, 0, 4)).reshape(B, S, C_WIDTH)


def _s5_glu_kernel(y_ref, u_ref, d_ref, w_ref, o_ref):
    y = y_ref[...] + d_ref[...] * u_ref[...]
    g = jax.nn.gelu(y)
    z = jnp.dot(g.astype(jnp.bfloat16), w_ref[...], preferred_element_type=jnp.float32)
    o_ref[...] = (g * jax.nn.sigmoid(z)).astype(o_ref.dtype)


def _s5_glu(y, u, d_skip, w_glu):
    T = y.shape[0]
    tb = math.gcd(T, 1024)
    return pl.pallas_call(
        _s5_glu_kernel,
        grid=(T // tb,),
        in_specs=[pl.BlockSpec((tb, C_WIDTH), lambda i: (i, 0)),
                  pl.BlockSpec((tb, C_WIDTH), lambda i: (i, 0)),
                  pl.BlockSpec((1, C_WIDTH), lambda i: (0, 0)),
                  pl.BlockSpec((C_WIDTH, C_WIDTH), lambda i: (0, 0))],
        out_specs=pl.BlockSpec((tb, C_WIDTH), lambda i: (i, 0)),
        out_shape=jax.ShapeDtypeStruct((T, C_WIDTH), jnp.float32),
        compiler_params=pltpu.CompilerParams(dimension_semantics=("arbitrary",)),
        name="s5_glu",
    )(y, u, d_skip.reshape(1, C_WIDTH).astype(jnp.float32), w_glu.astype(jnp.bfloat16))


def _s5_mixer(u, uc, a_re, a_im, log_dt, b_re, b_im, c_re, c_im, d_skip, w_glu, with_ctx):
    B, L, _ = u.shape
    Lc = uc.shape[1]
    p_mat, w_mat, a_chunk = _s5_operators(a_re, a_im, log_dt, b_re, b_im, c_re, c_im)
    u_seq = jnp.concatenate([uc, u], axis=1).astype(jnp.float32)
    y = _s5_scan(u_seq, Lc // S5_TC, p_mat, w_mat, a_chunk)
    if with_ctx:
        out = _s5_glu(y.reshape(B * (Lc + L), C_WIDTH), u_seq.reshape(B * (Lc + L), C_WIDTH), d_skip, w_glu)
        out = out.reshape(B, Lc + L, C_WIDTH)
        return out[:, Lc:], out[:, :Lc]
    out = _s5_glu(y[:, Lc:].reshape(B * L, C_WIDTH), u.astype(jnp.float32).reshape(B * L, C_WIDTH), d_skip, w_glu)
    return out.reshape(B, L, C_WIDTH), None


def _token_mixers(h, hc, row, col, layer_idx, w_in, qn_a, kn_a, qn_b, kn_b,
                  lam_q1, lam_k1, lam_q2, lam_k2, subln_g,
                  a_re, a_im, log_dt, b_re, b_im, c_re, c_im, d_skip, w_glu, with_ctx):
    B, L, _ = h.shape
    Lc = hc.shape[1]
    cuts = [A_Q, A_Q + A_KV, A_Q + 2 * A_KV, A_Q + 2 * A_KV + B_QK,
            A_Q + 2 * A_KV + 2 * B_QK, A_Q + 2 * A_KV + 2 * B_QK + B_V]
    qa, ka, va, qb, kb, vb, us = jnp.split(h @ w_in, cuts, axis=-1)
    qac, kac, vac, qbc, kbc, vbc, usc = jnp.split(hc @ w_in, cuts, axis=-1)

    qa = _rope_2d(_rms(qa.reshape(B, L, A_HEADS, HEAD_DIM), qn_a), row, col)
    ka = _rope_2d(_rms(ka.reshape(B, L, A_KV_HEADS, HEAD_DIM), kn_a), row, col)
    va = va.reshape(B, L, A_KV_HEADS, HEAD_DIM)
    kac = _rms(kac.reshape(B, Lc, A_KV_HEADS, HEAD_DIM), kn_a)
    vac = vac.reshape(B, Lc, A_KV_HEADS, HEAD_DIM)
    ka_all = jnp.concatenate([kac, ka], axis=1)
    va_all = jnp.concatenate([vac, va], axis=1)
    qa_t = _query_ext_t(qa * (HEAD_DIM ** -0.5 * LOG2E), [(hq // A_GROUP) * HEAD_DIM for hq in range(A_HEADS)])
    ka_pack = ka_all.reshape(B, Lc + L, A_KV).astype(jnp.bfloat16)
    o_a = jnp.transpose(_gqa_attention(qa_t, ka_pack, _value_ext_t(va_all)), (0, 2, 1))

    lambda_init = 0.8 - 0.6 * math.exp(-0.3 * layer_idx)
    lam = (jnp.exp(jnp.sum(lam_q1.astype(jnp.float32) * lam_k1.astype(jnp.float32)))
           - jnp.exp(jnp.sum(lam_q2.astype(jnp.float32) * lam_k2.astype(jnp.float32))) + lambda_init)
    qb = _rope_2d(_rms(qb.reshape(B, L, 2 * B_HEADS, B_SUB), qn_b), row, col)
    kb = _rope_2d(_rms(kb.reshape(B, L, 2 * B_HEADS, B_SUB), kn_b), row, col)
    vb = vb.reshape(B, L, B_HEADS, B_VDIM)
    kbc = _rms(kbc.reshape(B, Lc, 2 * B_HEADS, B_SUB), kn_b)
    vbc = vbc.reshape(B, Lc, B_HEADS, B_VDIM)
    kb_all = jnp.concatenate([kbc, kb], axis=1)
    vb_all = jnp.concatenate([vbc, vb], axis=1)
    qb_t = _query_ext_t(qb * (B_SUB ** -0.5 * LOG2E), [(j % 4) * B_SUB for j in range(2 * B_HEADS)])
    kb_pack = kb_all.reshape(B, Lc + L, B_QK).astype(jnp.bfloat16)
    o_b = jnp.transpose(_diff_attention(qb_t, kb_pack, _value_ext_t(vb_all), lam, subln_g, 1 - lambda_init),
                        (0, 2, 1))

    o_c, o_cc = _s5_mixer(us, usc, a_re, a_im, log_dt, b_re, b_im, c_re, c_im, d_skip, w_glu, with_ctx)

    mix = jnp.concatenate([o_a.astype(h.dtype), o_b.astype(h.dtype), o_c], axis=-1)
    if not with_ctx:
        return mix, None
    qac = _rms(qac.reshape(B, Lc, A_HEADS, HEAD_DIM), qn_a).reshape(B, Lc, A_KV_HEADS, A_GROUP, HEAD_DIM)
    o_ac = _gqa_block(qac, kac, vac).reshape(B, Lc, A_Q)
    kbc5 = kbc.reshape(B, Lc, B_HEADS, 2, B_SUB)
    qbc = _rms(qbc.reshape(B, Lc, 2 * B_HEADS, B_SUB), qn_b).reshape(B, Lc, B_HEADS, 2, B_SUB)
    o_bc = _diff_block(qbc, kbc5, vbc, lam)
    o_bc = (_rms(o_bc, subln_g) * (1 - lambda_init)).reshape(B, Lc, B_V)
    mixc = jnp.concatenate([o_ac, o_bc, o_cc], axis=-1)
    return mix, mixc


PEER_TB = 256
PEER_EB = 8
N_SLOTS = PEER_HEADS * PEER_TOPK
HALF_D = D_MODEL // 2
CAND_ROWS = 80
INVALID_FLAT = 1.0e9
SC_CORES = 2
SC_SUBCORES = 16
SC_CHUNK = 32


def _cand_flat_table():
    f = [float(b) for b in range(16)]
    for a in range(1, 8):
        f += [float(a * 16 + b) if (a + 1) * (b + 1) <= 16 else INVALID_FLAT for b in range(8)]
    f += [float(a * 16) for a in range(8, 16)]
    return jnp.broadcast_to(jnp.asarray(f, jnp.float32)[:, None], (CAND_ROWS, PEER_TB))


def _pair_rows(first, second):
    blocks = [first[0:1] + second]
    blocks += [first[a:a + 1] + second[0:8] for a in range(1, 8)]
    blocks += [first[8:16] + second[0:1]]
    return jnp.concatenate(blocks, axis=0)


def _peer_route_kernel(x_ref, g_ref, sh_ref, sc_ref, wq_ref, keys_ref, ftab_ref,
                       idx_ref, gate_ref, h_ref, val_scr, id_scr, sc_scr, e_scr):
    f32 = jnp.float32
    x = x_ref[...]
    y = x * lax.rsqrt(jnp.mean(x * x, axis=-1, keepdims=True) + EPS) * g_ref[...]
    h2 = (y * (1.0 + sc_ref[0]) + sh_ref[0]).astype(jnp.bfloat16)
    h_ref[...] = h2
    q = jnp.dot(h2, wq_ref[...], preferred_element_type=f32)
    ftab = ftab_ref[...]
    key_id = lax.broadcasted_iota(jnp.int32, (N_KEYS, PEER_TB), 0).astype(f32)
    neg_inf = -jnp.inf
    for head in range(PEER_HEADS):
        for t in range(2):
            ht = head * 2 + t
            qh = q[:, ht * PEER_HALF:(ht + 1) * PEER_HALF].astype(jnp.bfloat16)
            s = lax.dot_general(keys_ref[ht], qh, (((1,), (1,)), ((), ())),
                                preferred_element_type=f32)
            for r in range(PEER_TOPK):
                m = jnp.max(s, axis=0, keepdims=True)
                pick = jnp.min(jnp.where(s == m, key_id, float(N_KEYS)), axis=0, keepdims=True)
                s = jnp.where(key_id == pick, neg_inf, s)
                val_scr[t, r:r + 1, :] = m
                id_scr[t, r:r + 1, :] = pick
        cand = _pair_rows(val_scr[0], val_scr[1])
        cand = jnp.where(ftab < float(PEER_TOPK * PEER_TOPK), cand, neg_inf)
        expert = _pair_rows(id_scr[0] * float(N_KEYS), id_scr[1])
        for r in range(PEER_TOPK):
            m = jnp.max(cand, axis=0, keepdims=True)
            pick = jnp.min(jnp.where(cand == m, ftab, INVALID_FLAT), axis=0, keepdims=True)
            hit = ftab == pick
            sc_scr[r:r + 1, :] = m
            e_scr[r:r + 1, :] = jnp.sum(jnp.where(hit, expert, 0.0), axis=0, keepdims=True)
            cand = jnp.where(hit, neg_inf, cand)
        sc = sc_scr[...]
        p = jnp.exp(sc - sc[0:1])
        gate = p / jnp.sum(p, axis=0, keepdims=True)
        idx_ref[head * PEER_TOPK:(head + 1) * PEER_TOPK, :] = e_scr[...].astype(jnp.int32)
        gate_ref[head * PEER_TOPK:(head + 1) * PEER_TOPK, :] = gate


def _peer_route(x, norm_g, shift, scale, wq_bf, keys_bf, tokens_per_mod):
    T, D = x.shape
    nq = wq_bf.shape[1]
    blocks_per_mod = tokens_per_mod // PEER_TB
    return pl.pallas_call(
        _peer_route_kernel,
        grid=(T // PEER_TB,),
        in_specs=[
            pl.BlockSpec((PEER_TB, D), lambda i: (i, 0)),
            pl.BlockSpec((1, D), lambda i: (0, 0)),
            pl.BlockSpec((1, 1, D), lambda i: (i // blocks_per_mod, 0, 0)),
            pl.BlockSpec((1, 1, D), lambda i: (i // blocks_per_mod, 0, 0)),
            pl.BlockSpec((D, nq), lambda i: (0, 0)),
            pl.BlockSpec((2 * PEER_HEADS, N_KEYS, PEER_HALF), lambda i: (0, 0, 0)),
            pl.BlockSpec((CAND_ROWS, PEER_TB), lambda i: (0, 0)),
        ],
        out_specs=[
            pl.BlockSpec((N_SLOTS, PEER_TB), lambda i: (0, i)),
            pl.BlockSpec((N_SLOTS, PEER_TB), lambda i: (0, i)),
            pl.BlockSpec((PEER_TB, D), lambda i: (i, 0)),
        ],
        out_shape=[
            jax.ShapeDtypeStruct((N_SLOTS, T), jnp.int32),
            jax.ShapeDtypeStruct((N_SLOTS, T), jnp.float32),
            jax.ShapeDtypeStruct((T, D), jnp.bfloat16),
        ],
        scratch_shapes=[pltpu.VMEM((2, PEER_TOPK, PEER_TB), jnp.float32),
                        pltpu.VMEM((2, PEER_TOPK, PEER_TB), jnp.float32),
                        pltpu.VMEM((PEER_TOPK, PEER_TB), jnp.float32),
                        pltpu.VMEM((PEER_TOPK, PEER_TB), jnp.float32)],
        compiler_params=pltpu.CompilerParams(dimension_semantics=("arbitrary",),
                                             vmem_limit_bytes=VMEM_LIMIT),
        name="peer_route",
    )(x, norm_g.reshape(1, D).astype(jnp.float32), shift, scale, wq_bf, keys_bf, _cand_flat_table())


def _gather_rows(table, idx2d):
    n_chunks_total, chunk = idx2d.shape
    width = table.shape[1]
    n_workers = SC_CORES * SC_SUBCORES
    n_chunks = n_chunks_total // n_workers
    mesh = plsc.VectorSubcoreMesh(core_axis_name="c", subcore_axis_name="s")

    def body(table_hbm, idx_hbm, out_hbm, idx_v, rows_v, gsem, wsem):
        wid = lax.axis_index("s") * SC_CORES + lax.axis_index("c")
        cbase = wid * n_chunks

        def gather_copy(slot):
            return pltpu.make_async_copy(table_hbm.at[idx_v.at[slot]], rows_v.at[slot], gsem.at[slot])

        def writeback_copy(r, slot):
            return pltpu.make_async_copy(rows_v.at[slot], out_hbm.at[pl.ds(r * chunk, chunk)], wsem.at[slot])

        pltpu.sync_copy(idx_hbm.at[cbase], idx_v.at[0])
        gather_copy(0).start()

        def step(i2, carry):
            for slot in range(2):
                i = i2 * 2 + slot
                r = cbase + i
                other = 1 - slot
                gather_copy(slot).wait()

                @pl.when(i >= 1)
                def _():
                    writeback_copy(r - 1, other).wait()

                @pl.when(i + 1 < n_chunks)
                def _():
                    pltpu.sync_copy(idx_hbm.at[r + 1], idx_v.at[other])
                    gather_copy(other).start()

                writeback_copy(r, slot).start()
            return carry

        lax.fori_loop(0, n_chunks // 2, step, 0)
        writeback_copy(cbase + n_chunks - 1, 1).wait()

    return pl.kernel(
        body, mesh=mesh,
        out_type=jax.ShapeDtypeStruct((n_chunks_total * chunk, width), jnp.int32),
        scratch_types=[pltpu.VMEM((2, chunk), jnp.int32),
                       pltpu.VMEM((2, chunk, width), jnp.int32),
                       pltpu.SemaphoreType.DMA((2,)),
                       pltpu.SemaphoreType.DMA((2,))],
        name="peer_gather",
    )(table, idx2d)


def _unpack_pair(words):
    lo = lax.bitcast_convert_type(words << 16, jnp.float32)
    hi = lax.bitcast_convert_type(words & jnp.int32(-65536), jnp.float32)
    return lo, hi


def _peer_expert_kernel(rows_ref, h_ref, gate_ref, x_ref, g2_ref, o_ref):
    f32 = jnp.float32
    gate = gate_ref[0]
    outs = []
    for t in range(PEER_EB):
        words = rows_ref[t]
        u_lo, u_hi = _unpack_pair(words[:, :HALF_D])
        h = h_ref[t:t + 1, :].astype(f32)
        act = jnp.sum(u_lo * h[:, :HALF_D] + u_hi * h[:, HALF_D:], axis=-1, keepdims=True)
        wgt = jax.nn.gelu(act) * gate[:, t:t + 1]
        v_lo, v_hi = _unpack_pair(words[:, HALF_D:])
        outs.append(jnp.concatenate([jnp.sum(wgt * v_lo, axis=0, keepdims=True),
                                     jnp.sum(wgt * v_hi, axis=0, keepdims=True)], axis=-1))
    o_ref[...] = x_ref[...] + g2_ref[0] * jnp.concatenate(outs, axis=0)


def _peer_expert(rows, h2, gate_t, x, g2, tokens_per_mod):
    T, D = x.shape
    nblk = T // PEER_EB
    blocks_per_mod = tokens_per_mod // PEER_EB
    gate_blk = jnp.transpose(gate_t.reshape(N_SLOTS, nblk, PEER_EB), (1, 0, 2))
    return pl.pallas_call(
        _peer_expert_kernel,
        grid=(nblk,),
        in_specs=[
            pl.BlockSpec((PEER_EB, N_SLOTS, D), lambda i: (i, 0, 0)),
            pl.BlockSpec((PEER_EB, D), lambda i: (i, 0)),
            pl.BlockSpec((1, N_SLOTS, PEER_EB), lambda i: (i, 0, 0)),
            pl.BlockSpec((PEER_EB, D), lambda i: (i, 0)),
            pl.BlockSpec((1, 1, D), lambda i: (i // blocks_per_mod, 0, 0)),
        ],
        out_specs=pl.BlockSpec((PEER_EB, D), lambda i: (i, 0)),
        out_shape=jax.ShapeDtypeStruct((T, D), jnp.float32),
        compiler_params=pltpu.CompilerParams(dimension_semantics=("arbitrary",),
                                             vmem_limit_bytes=VMEM_LIMIT),
        name="peer_expert",
    )(rows.reshape(T, N_SLOTS, D), h2, gate_blk, x, g2)


def _pack_tables(u_emb, v_emb):
    def pack(w):
        bits = lax.bitcast_convert_type(w.astype(jnp.bfloat16), jnp.uint16).astype(jnp.uint32)
        return lax.bitcast_convert_type(bits[:, :HALF_D] | (bits[:, HALF_D:] << 16), jnp.int32)
    return jnp.concatenate([pack(u_emb), pack(v_emb)], axis=1)


def _peer_residual(x, norm_g, shift, scale, gate2, wq_bf, keys_bf, table, tokens_per_mod, n_splits):
    T, D = x.shape
    ts = T // n_splits
    mods_per_split = max(ts // tokens_per_mod, 1)
    outs = []
    for s in range(n_splits):
        m0 = (s * ts) // tokens_per_mod
        mods = slice(m0, m0 + mods_per_split)
        xs = x[s * ts:(s + 1) * ts]
        idx_t, gate_t, h2 = _peer_route(xs, norm_g, shift[mods], scale[mods], wq_bf, keys_bf, tokens_per_mod)
        rows = _gather_rows(table, jnp.transpose(idx_t).reshape(ts * N_SLOTS // SC_CHUNK, SC_CHUNK))
        outs.append(_peer_expert(rows, h2, gate_t, xs, gate2[mods], tokens_per_mod))
    return outs[0] if n_splits == 1 else jnp.concatenate(outs, axis=0)


def kernel(x, c, ctx, c_ctx, w_ada, b_ada, norm1_g, norm2_g, w_in, w_out,
           qn_a, kn_a, qn_b, kn_b, lam_q1, lam_k1, lam_q2, lam_k2, subln_g,
           s5_a_re, s5_a_im, s5_log_dt, s5_b_re, s5_b_im, s5_c_re, s5_c_im,
           s5_d, s5_w_glu, peer_wq, peer_keys, peer_u, peer_v):
    B, L, D = x.shape
    Lc = ctx.shape[1]
    depth = w_in.shape[0]
    rows = L // GRID_W
    row = jnp.repeat(jnp.arange(rows, dtype=jnp.int32), GRID_W)
    col = jnp.tile(jnp.arange(GRID_W, dtype=jnp.int32), rows)
    s_c = jax.nn.silu(c)
    s_cc = jax.nn.silu(c_ctx)
    for l in range(depth):
        with_ctx = l < depth - 1
        sh1, sc1, g1, sh2, sc2, g2 = jnp.split((s_c @ w_ada[l] + b_ada[l])[:, None, :], 6, axis=-1)
        csh1, csc1, cg1, csh2, csc2, cg2 = jnp.split(s_cc @ w_ada[l] + b_ada[l], 6, axis=-1)

        h = _modulate(_rms(x, norm1_g[l]), sh1, sc1)
        hc = _modulate(_rms(ctx, norm1_g[l]), csh1, csc1)
        mix, mixc = _token_mixers(h, hc, row, col, l, w_in[l], qn_a[l], kn_a[l], qn_b[l], kn_b[l],
                                  lam_q1[l], lam_k1[l], lam_q2[l], lam_k2[l], subln_g[l],
                                  s5_a_re[l], s5_a_im[l], s5_log_dt[l], s5_b_re[l], s5_b_im[l],
                                  s5_c_re[l], s5_c_im[l], s5_d[l], s5_w_glu[l], with_ctx)
        x = x + g1 * (mix @ w_out[l])

        wq_bf = peer_wq[l].astype(jnp.bfloat16)
        keys_bf = peer_keys[l].reshape(2 * PEER_HEADS, N_KEYS, PEER_HALF).astype(jnp.bfloat16)
        table = _pack_tables(peer_u[l], peer_v[l])
        x = _peer_residual(x.reshape(B * L, D), norm2_g[l], sh2, sc2, g2, wq_bf, keys_bf, table,
                           tokens_per_mod=L, n_splits=B).reshape(B, L, D)

        if with_ctx:
            ctx = ctx + cg1 * (mixc @ w_out[l])
            ctx = _peer_residual(ctx.reshape(B * Lc, D), norm2_g[l], csh2.reshape(1, 1, D), csc2.reshape(1, 1, D),
                                 cg2.reshape(1, 1, D), wq_bf, keys_bf, table,
                                 tokens_per_mod=B * Lc, n_splits=1).reshape(B, Lc, D)
    return x
```

```python
import functools
import math

import jax
import jax.numpy as jnp
from jax import lax
from jax.experimental import pallas as pl
from jax.experimental.pallas import tpu as pltpu
from jax.experimental.pallas import tpu_sc as plsc

D_MODEL = 1024
GRID_W = 64
EPS = 1e-6
ROPE_THETA = 10000.0
HEAD_DIM = 64
A_HEADS = 6
A_KV_HEADS = 2
A_GROUP = A_HEADS // A_KV_HEADS
A_Q = A_HEADS * HEAD_DIM
A_KV = A_KV_HEADS * HEAD_DIM
B_HEADS = 6
B_SUB = 32
B_VDIM = 2 * B_SUB
B_QK = B_HEADS * 2 * B_SUB
B_V = B_HEADS * B_VDIM
C_GROUPS = 16
C_GROUP_CH = 16
C_WIDTH = C_GROUPS * C_GROUP_CH
C_STATE = 64
PEER_HEADS = 8
N_KEYS = 128
PEER_TOPK = 16
PEER_HALF = 128
TOKEN_CHUNK = 128

LANES = 128
V_EXT = LANES
LOG2E = math.log2(math.e)
NEG_BIG = -1e30
VMEM_LIMIT = 48 * 1024 * 1024


TQ = LANES
ATTN_TK = 1408
K_PACK = LANES


def _flash_unit(k, q_t, v_t, m_ref, acc_ref, unit):
    s = jnp.dot(k, q_t, preferred_element_type=jnp.float32)
    m_old = m_ref[unit]
    m_new = jnp.maximum(m_old, jnp.max(s, axis=0, keepdims=True))
    alpha = jnp.exp2(m_old - m_new)
    p = jnp.exp2(s - m_new).astype(jnp.bfloat16)
    acc_ref[unit] = alpha * acc_ref[unit] + jnp.dot(v_t, p, preferred_element_type=jnp.float32)
    m_ref[unit] = m_new


def _flash_all_units(q_ref, k_ref, vt_ref, m_ref, acc_ref, *, n_units, units_per_group, tk, n_keys):
    m_ref[...] = jnp.full(m_ref.shape, NEG_BIG, jnp.float32)
    acc_ref[...] = jnp.zeros(acc_ref.shape, jnp.float32)

    def body(j, carry):
        off = pl.multiple_of(j * tk, tk)
        k = k_ref[0, pl.ds(off, tk), :]
        for u in range(n_units):
            _flash_unit(k, q_ref[0, u], vt_ref[0, u // units_per_group, :, pl.ds(off, tk)], m_ref, acc_ref, u)
        return carry

    lax.fori_loop(0, n_keys // tk, body, 0)


def _gqa_kernel(q_ref, k_ref, vt_ref, o_ref, m_ref, acc_ref, *, tk, n_keys):
    _flash_all_units(q_ref, k_ref, vt_ref, m_ref, acc_ref,
                     n_units=A_HEADS, units_per_group=A_GROUP, tk=tk, n_keys=n_keys)
    for h in range(A_HEADS):
        acc = acc_ref[h]
        o_ref[0, h * HEAD_DIM:(h + 1) * HEAD_DIM, :] = (
            acc[:HEAD_DIM] / acc[HEAD_DIM:HEAD_DIM + 1]).astype(o_ref.dtype)


def _diff_kernel(lam_ref, q_ref, k_ref, vt_ref, g_ref, o_ref, m_ref, acc_ref, *, tk, n_keys, out_scale):
    _flash_all_units(q_ref, k_ref, vt_ref, m_ref, acc_ref,
                     n_units=4, units_per_group=2, tk=tk, n_keys=n_keys)
    lam = lam_ref[0]
    for h in range(2):
        a1 = acc_ref[2 * h]
        a2 = acc_ref[2 * h + 1]
        o = a1[:B_VDIM] / a1[B_VDIM:B_VDIM + 1] - lam * (a2[:B_VDIM] / a2[B_VDIM:B_VDIM + 1])
        o = o * lax.rsqrt(jnp.mean(o * o, axis=0, keepdims=True) + EPS)
        o_ref[0, h * B_VDIM:(h + 1) * B_VDIM, :] = (o * g_ref[...] * out_scale).astype(o_ref.dtype)


def _gqa_attention(q_t, k, v_t, *, tk=ATTN_TK):
    B, _, _, L = q_t.shape
    S = k.shape[1]
    return pl.pallas_call(
        functools.partial(_gqa_kernel, tk=tk, n_keys=S),
        grid=(B, L // TQ),
        in_specs=[
            pl.BlockSpec((1, A_HEADS, K_PACK, TQ), lambda b, i: (b, 0, 0, i)),
            pl.BlockSpec((1, S, K_PACK), lambda b, i: (b, 0, 0)),
            pl.BlockSpec((1, A_KV_HEADS, V_EXT, S), lambda b, i: (b, 0, 0, 0)),
        ],
        out_specs=pl.BlockSpec((1, A_Q, TQ), lambda b, i: (b, 0, i)),
        out_shape=jax.ShapeDtypeStruct((B, A_Q, L), jnp.bfloat16),
        scratch_shapes=[pltpu.VMEM((A_HEADS, 1, TQ), jnp.float32),
                        pltpu.VMEM((A_HEADS, V_EXT, TQ), jnp.float32)],
        compiler_params=pltpu.CompilerParams(
            dimension_semantics=("arbitrary", "arbitrary"), vmem_limit_bytes=VMEM_LIMIT),
        name="gqa_attention",
    )(q_t, k, v_t)


def _diff_attention(q_t, k, v_t, lam, subln_g, out_scale, *, tk=ATTN_TK):
    B, _, _, L = q_t.shape
    S = k.shape[1]
    return pl.pallas_call(
        functools.partial(_diff_kernel, tk=tk, n_keys=S, out_scale=out_scale),
        grid=(B, B_HEADS // 2, L // TQ),
        in_specs=[
            pl.BlockSpec(memory_space=pltpu.SMEM),
            pl.BlockSpec((1, 4, K_PACK, TQ), lambda b, h, i: (b, h, 0, i)),
            pl.BlockSpec((1, S, K_PACK), lambda b, h, i: (b, 0, h)),
            pl.BlockSpec((1, 2, V_EXT, S), lambda b, h, i: (b, h, 0, 0)),
            pl.BlockSpec((B_VDIM, 1), lambda b, h, i: (0, 0)),
        ],
        out_specs=pl.BlockSpec((1, 2 * B_VDIM, TQ), lambda b, h, i: (b, h, i)),
        out_shape=jax.ShapeDtypeStruct((B, B_V, L), jnp.bfloat16),
        scratch_shapes=[pltpu.VMEM((4, 1, TQ), jnp.float32),
                        pltpu.VMEM((4, V_EXT, TQ), jnp.float32)],
        compiler_params=pltpu.CompilerParams(
            dimension_semantics=("arbitrary", "arbitrary", "arbitrary"), vmem_limit_bytes=VMEM_LIMIT),
        name="diff_attention",
    )(lam.reshape(1).astype(jnp.float32), q_t, k, v_t, subln_g.reshape(B_VDIM, 1).astype(jnp.float32))


def _value_ext_t(v):
    B, S, H, d = v.shape
    ones = jnp.ones((B, S, H, 1), v.dtype)
    pad = jnp.zeros((B, S, H, V_EXT - d - 1), v.dtype)
    return jnp.transpose(jnp.concatenate([v, ones, pad], axis=-1), (0, 2, 3, 1)).astype(jnp.bfloat16)


def _query_ext_t(q, offsets):
    B, L, U, d = q.shape
    q_t = jnp.transpose(q, (0, 2, 3, 1)).astype(jnp.bfloat16)
    units = [jnp.pad(q_t[:, u], ((0, 0), (off, K_PACK - d - off), (0, 0))) for u, off in enumerate(offsets)]
    return jnp.stack(units, axis=1)


def _rms(x, g):
    xf = x.astype(jnp.float32)
    y = xf * lax.rsqrt(jnp.mean(xf * xf, axis=-1, keepdims=True) + EPS)
    return (y * g.astype(jnp.float32)).astype(x.dtype)


def _modulate(h, shift, scale):
    return h * (1 + scale) + shift


def _rope_2d(x, row, col):
    d = x.shape[-1]
    half = d // 2
    inv = 1.0 / (ROPE_THETA ** (jnp.arange(0, half, 2, dtype=jnp.float32) / half))
    ang = jnp.concatenate([row.astype(jnp.float32)[:, None] * inv[None, :],
                           col.astype(jnp.float32)[:, None] * inv[None, :]], axis=-1)
    cos = jnp.cos(ang)[None, :, None, :]
    sin = jnp.sin(ang)[None, :, None, :]
    xr = x.astype(jnp.float32).reshape(x.shape[:-1] + (half, 2))
    x0, x1 = xr[..., 0], xr[..., 1]
    out = jnp.stack([x0 * cos - x1 * sin, x0 * sin + x1 * cos], axis=-1)
    return out.reshape(x.shape).astype(x.dtype)


def _gqa_block(q, k, v):
    s = jnp.einsum('bqkgd,bskd->bkgqs', q, k).astype(jnp.float32) * (HEAD_DIM ** -0.5)
    p = jax.nn.softmax(s, axis=-1).astype(v.dtype)
    return jnp.einsum('bkgqs,bskd->bqkgd', p, v)


def _diff_block(q, k, v, lam):
    s = jnp.einsum('bqhtd,bshtd->bthqs', q, k).astype(jnp.float32) * (B_SUB ** -0.5)
    p = jax.nn.softmax(s, axis=-1)
    a = (p[:, 0] - lam * p[:, 1]).astype(v.dtype)
    return jnp.einsum('bhqs,bshd->bqhd', a, v)


S5_TC = 64
S5_ROWS = 8
S5_CW = S5_TC * C_GROUP_CH
S5_SW = 2 * C_STATE


def _s5_operators(a_re, a_im, log_dt, b_re, b_im, c_re, c_im):
    f32 = jnp.float32
    tc, P, N, G = S5_TC, C_GROUP_CH, C_STATE, C_GROUPS
    j = jnp.arange(tc + 1, dtype=f32)
    toep, p_parts, q_parts, a_parts = [], [], [], []
    s_idx = jnp.arange(tc)[:, None]
    t_idx = jnp.arange(tc)[None, :]
    for dirn in range(2):
        A = lax.complex(a_re[dirn].astype(f32), a_im[dirn].astype(f32))
        adt = A * jnp.exp(log_dt[dirn].astype(f32))[:, None]
        abar = jnp.exp(adt)
        bbar = ((abar - 1) / A)[..., None] * lax.complex(b_re[dirn].astype(f32), b_im[dirn].astype(f32))
        cmat = lax.complex(c_re[dirn].astype(f32), c_im[dirn].astype(f32))
        pw = jnp.exp(adt[None] * j[:, None, None])
        kern = jnp.einsum('gpn,jgn,gnq->jgpq', cmat, pw[:tc], bbar).real
        lag = (s_idx - t_idx) if dirn == 1 else (t_idx - s_idx)
        tm = jnp.where((lag >= 0)[:, :, None, None, None], kern[jnp.clip(lag, 0, tc - 1)], 0.0)
        toep.append(jnp.transpose(tm, (2, 0, 4, 1, 3)).reshape(G, S5_CW, S5_CW))
        inj_pw = pw[:tc] if dirn == 1 else pw[:tc][::-1]
        inj = inj_pw[:, :, :, None] * bbar[None]
        p_parts.append(jnp.transpose(inj, (1, 0, 3, 2)).reshape(G, S5_CW, N))
        out_pw = pw[1:][::-1] if dirn == 1 else pw[1:]
        outm = cmat[None] * out_pw[:, :, None, :]
        q_parts.append(jnp.transpose(outm, (1, 3, 0, 2)).reshape(G, N, S5_CW))
        a_parts.append(pw[tc])
    zeros = jnp.zeros((G, N, S5_CW), f32)
    w_mat = jnp.concatenate([
        toep[0] + toep[1],
        q_parts[0].real, zeros, -q_parts[0].imag, zeros,
        zeros, q_parts[1].real, zeros, -q_parts[1].imag,
    ], axis=1)
    p_mat = jnp.concatenate([p_parts[0].real, p_parts[1].real, p_parts[0].imag, p_parts[1].imag], axis=-1)
    a_chunk = jnp.stack([jnp.concatenate([a_parts[0].real, a_parts[1].real], axis=-1),
                         jnp.concatenate([a_parts[0].imag, a_parts[1].imag], axis=-1)], axis=1)
    return p_mat.astype(jnp.bfloat16), w_mat.astype(jnp.bfloat16), a_chunk.astype(f32)


def _s5_scan_kernel(u_ref, p_ref, w_ref, a_ref, y_ref, s_scr, hs_scr, *, order_f, order_r):
    f32 = jnp.float32
    u = u_ref[0]
    s_scr[...] = jnp.dot(u, p_ref[0], preferred_element_type=f32)
    a_re = a_ref[0, 0:1, :]
    a_im = a_ref[0, 1:2, :]
    is_fwd = lax.broadcasted_iota(jnp.int32, (S5_ROWS, S5_SW), 1) < C_STATE
    h_re = jnp.zeros((S5_ROWS, S5_SW), f32)
    h_im = jnp.zeros((S5_ROWS, S5_SW), f32)
    for cf, cr in zip(order_f, order_r):
        rf = slice(cf * S5_ROWS, (cf + 1) * S5_ROWS)
        rr = slice(cr * S5_ROWS, (cr + 1) * S5_ROWS)
        hs_scr[rf, 0:S5_SW] = h_re
        hs_scr[rf, S5_SW:2 * S5_SW] = h_im
        hs_scr[rr, 2 * S5_SW:3 * S5_SW] = h_re
        hs_scr[rr, 3 * S5_SW:4 * S5_SW] = h_im
        s_re = jnp.where(is_fwd, s_scr[rf, 0:S5_SW], s_scr[rr, 0:S5_SW])
        s_im = jnp.where(is_fwd, s_scr[rf, S5_SW:2 * S5_SW], s_scr[rr, S5_SW:2 * S5_SW])
        h_re, h_im = a_re * h_re - a_im * h_im + s_re, a_re * h_im + a_im * h_re + s_im
    y = jnp.dot(u, w_ref[0, 0:S5_CW, :], preferred_element_type=f32)
    y += jnp.dot(hs_scr[...].astype(jnp.bfloat16), w_ref[0, S5_CW:, :], preferred_element_type=f32)
    y_ref[0] = y


def _s5_scan(u_seq, n_ctx_chunks, p_mat, w_mat, a_chunk):
    B, S, _ = u_seq.shape
    G, P = C_GROUPS, C_GROUP_CH
    nc = S // S5_TC
    R = nc * S5_ROWS
    ug = jnp.transpose(u_seq.reshape(B, nc, S5_TC, G, P), (3, 1, 0, 2, 4))
    ug = jnp.pad(ug, ((0, 0), (0, 0), (0, S5_ROWS - B), (0, 0), (0, 0)))
    ug = ug.reshape(G, R, S5_CW).astype(jnp.bfloat16)
    order_f = tuple(range(nc))
    order_r = tuple(range(n_ctx_chunks - 1, -1, -1)) + tuple(range(nc - 1, n_ctx_chunks - 1, -1))
    y = pl.pallas_call(
        functools.partial(_s5_scan_kernel, order_f=order_f, order_r=order_r),
        grid=(G,),
        in_specs=[
            pl.BlockSpec((1, R, S5_CW), lambda g: (g, 0, 0)),
            pl.BlockSpec((1, S5_CW, 2 * S5_SW), lambda g: (g, 0, 0)),
            pl.BlockSpec((1, S5_CW + 4 * S5_SW, S5_CW), lambda g: (g, 0, 0)),
            pl.BlockSpec((1, 2, S5_SW), lambda g: (g, 0, 0)),
        ],
        out_specs=pl.BlockSpec((1, R, S5_CW), lambda g: (g, 0, 0)),
        out_shape=jax.ShapeDtypeStruct((G, R, S5_CW), jnp.float32),
        scratch_shapes=[pltpu.VMEM((R, 2 * S5_SW), jnp.float32),
                        pltpu.VMEM((R, 4 * S5_SW), jnp.float32)],
        compiler_params=pltpu.CompilerParams(dimension_semantics=("arbitrary",),
                                             vmem_limit_bytes=VMEM_LIMIT),
        name="s5_scan",
    )(ug, p_mat, w_mat, a_chunk)
    y = y.reshape(G, nc, S5_ROWS, S5_TC, P)[:, :, :B]
    return jnp.transpose(y, (2, 1, 3, 0, 4)).reshape(B, S, C_WIDTH)


def _s5_glu_kernel(y_ref, u_ref, d_ref, w_ref, o_ref):
    y = y_ref[...] + d_ref[...] * u_ref[...]
    g = jax.nn.gelu(y)
    z = jnp.dot(g.astype(jnp.bfloat16), w_ref[...], preferred_element_type=jnp.float32)
    o_ref[...] = (g * jax.nn.sigmoid(z)).astype(o_ref.dtype)


def _s5_glu(y, u, d_skip, w_glu):
    T = y.shape[0]
    tb = math.gcd(T, 1024)
    return pl.pallas_call(
        _s5_glu_kernel,
        grid=(T // tb,),
        in_specs=[pl.BlockSpec((tb, C_WIDTH), lambda i: (i, 0)),
                  pl.BlockSpec((tb, C_WIDTH), lambda i: (i, 0)),
                  pl.BlockSpec((1, C_WIDTH), lambda i: (0, 0)),
                  pl.BlockSpec((C_WIDTH, C_WIDTH), lambda i: (0, 0))],
        out_specs=pl.BlockSpec((tb, C_WIDTH), lambda i: (i, 0)),
        out_shape=jax.ShapeDtypeStruct((T, C_WIDTH), jnp.float32),
        compiler_params=pltpu.CompilerParams(dimension_semantics=("arbitrary",)),
        name="s5_glu",
    )(y, u, d_skip.reshape(1, C_WIDTH).astype(jnp.float32), w_glu.astype(jnp.bfloat16))


def _s5_mixer(u, uc, s5_ops, d_skip, w_glu, with_ctx):
    B, L, _ = u.shape
    Lc = uc.shape[1]
    p_mat, w_mat, a_chunk = s5_ops
    u_seq = jnp.concatenate([uc, u], axis=1).astype(jnp.float32)
    y = _s5_scan(u_seq, Lc // S5_TC, p_mat, w_mat, a_chunk)
    if with_ctx:
        out = _s5_glu(y.reshape(B * (Lc + L), C_WIDTH), u_seq.reshape(B * (Lc + L), C_WIDTH), d_skip, w_glu)
        out = out.reshape(B, Lc + L, C_WIDTH)
        return out[:, Lc:], out[:, :Lc]
    out = _s5_glu(y[:, Lc:].reshape(B * L, C_WIDTH), u.astype(jnp.float32).reshape(B * L, C_WIDTH), d_skip, w_glu)
    return out.reshape(B, L, C_WIDTH), None


def _token_mixers(h, hc, row, col, layer_idx, w_in, qn_a, kn_a, qn_b, kn_b,
                  lam_q1, lam_k1, lam_q2, lam_k2, subln_g, s5_ops, d_skip, w_glu, with_ctx):
    B, L, _ = h.shape
    Lc = hc.shape[1]
    cuts = [A_Q, A_Q + A_KV, A_Q + 2 * A_KV, A_Q + 2 * A_KV + B_QK,
            A_Q + 2 * A_KV + 2 * B_QK, A_Q + 2 * A_KV + 2 * B_QK + B_V]
    qa, ka, va, qb, kb, vb, us = jnp.split(h @ w_in, cuts, axis=-1)
    qac, kac, vac, qbc, kbc, vbc, usc = jnp.split(hc @ w_in, cuts, axis=-1)

    qa = _rope_2d(_rms(qa.reshape(B, L, A_HEADS, HEAD_DIM), qn_a), row, col)
    ka = _rope_2d(_rms(ka.reshape(B, L, A_KV_HEADS, HEAD_DIM), kn_a), row, col)
    va = va.reshape(B, L, A_KV_HEADS, HEAD_DIM)
    kac = _rms(kac.reshape(B, Lc, A_KV_HEADS, HEAD_DIM), kn_a)
    vac = vac.reshape(B, Lc, A_KV_HEADS, HEAD_DIM)
    ka_all = jnp.concatenate([kac, ka], axis=1)
    va_all = jnp.concatenate([vac, va], axis=1)
    qa_t = _query_ext_t(qa * (HEAD_DIM ** -0.5 * LOG2E), [(hq // A_GROUP) * HEAD_DIM for hq in range(A_HEADS)])
    ka_pack = ka_all.reshape(B, Lc + L, A_KV).astype(jnp.bfloat16)
    o_a = jnp.transpose(_gqa_attention(qa_t, ka_pack, _value_ext_t(va_all)), (0, 2, 1))

    lambda_init = 0.8 - 0.6 * math.exp(-0.3 * layer_idx)
    lam = (jnp.exp(jnp.sum(lam_q1.astype(jnp.float32) * lam_k1.astype(jnp.float32)))
           - jnp.exp(jnp.sum(lam_q2.astype(jnp.float32) * lam_k2.astype(jnp.float32))) + lambda_init)
    qb = _rope_2d(_rms(qb.reshape(B, L, 2 * B_HEADS, B_SUB), qn_b), row, col)
    kb = _rope_2d(_rms(kb.reshape(B, L, 2 * B_HEADS, B_SUB), kn_b), row, col)
    vb = vb.reshape(B, L, B_HEADS, B_VDIM)
    kbc = _rms(kbc.reshape(B, Lc, 2 * B_HEADS, B_SUB), kn_b)
    vbc = vbc.reshape(B, Lc, B_HEADS, B_VDIM)
    kb_all = jnp.concatenate([kbc, kb], axis=1)
    vb_all = jnp.concatenate([vbc, vb], axis=1)
    qb_t = _query_ext_t(qb * (B_SUB ** -0.5 * LOG2E), [(j % 4) * B_SUB for j in range(2 * B_HEADS)])
    kb_pack = kb_all.reshape(B, Lc + L, B_QK).astype(jnp.bfloat16)
    o_b = jnp.transpose(_diff_attention(qb_t, kb_pack, _value_ext_t(vb_all), lam, subln_g, 1 - lambda_init),
                        (0, 2, 1))

    o_c, o_cc = _s5_mixer(us, usc, s5_ops, d_skip, w_glu, with_ctx)

    mix = jnp.concatenate([o_a.astype(h.dtype), o_b.astype(h.dtype), o_c], axis=-1)
    if not with_ctx:
        return mix, None
    qac = _rms(qac.reshape(B, Lc, A_HEADS, HEAD_DIM), qn_a).reshape(B, Lc, A_KV_HEADS, A_GROUP, HEAD_DIM)
    o_ac = _gqa_block(qac, kac, vac).reshape(B, Lc, A_Q)
    kbc5 = kbc.reshape(B, Lc, B_HEADS, 2, B_SUB)
    qbc = _rms(qbc.reshape(B, Lc, 2 * B_HEADS, B_SUB), qn_b).reshape(B, Lc, B_HEADS, 2, B_SUB)
    o_bc = _diff_block(qbc, kbc5, vbc, lam)
    o_bc = (_rms(o_bc, subln_g) * (1 - lambda_init)).reshape(B, Lc, B_V)
    mixc = jnp.concatenate([o_ac, o_bc, o_cc], axis=-1)
    return mix, mixc


PEER_TB = 256
PEER_EB = 8
N_SLOTS = PEER_HEADS * PEER_TOPK
HALF_D = D_MODEL // 2
CAND_ROWS = 80
INVALID_FLAT = 1.0e9
SC_CORES = 2
SC_SUBCORES = 16
SC_CHUNK = 32


def _cand_flat_table():
    f = [float(b) for b in range(16)]
    for a in range(1, 8):
        f += [float(a * 16 + b) if (a + 1) * (b + 1) <= 16 else INVALID_FLAT for b in range(8)]
    f += [float(a * 16) for a in range(8, 16)]
    return jnp.broadcast_to(jnp.asarray(f, jnp.float32)[:, None], (CAND_ROWS, PEER_TB))


def _pair_rows(first, second):
    blocks = [first[0:1] + second]
    blocks += [first[a:a + 1] + second[0:8] for a in range(1, 8)]
    blocks += [first[8:16] + second[0:1]]
    return jnp.concatenate(blocks, axis=0)


def _peer_route_kernel(x_ref, g_ref, sh_ref, sc_ref, wq_ref, keys_ref, ftab_ref,
                       idx_ref, gate_ref, h_ref, val_scr, id_scr, sc_scr, e_scr):
    f32 = jnp.float32
    x = x_ref[...]
    y = x * lax.rsqrt(jnp.mean(x * x, axis=-1, keepdims=True) + EPS) * g_ref[...]
    h2 = (y * (1.0 + sc_ref[0]) + sh_ref[0]).astype(jnp.bfloat16)
    h_ref[...] = h2
    q = jnp.dot(h2, wq_ref[...], preferred_element_type=f32)
    ftab = ftab_ref[...]
    key_id = lax.broadcasted_iota(jnp.int32, (N_KEYS, PEER_TB), 0).astype(f32)
    neg_inf = -jnp.inf
    for head in range(PEER_HEADS):
        for t in range(2):
            ht = head * 2 + t
            qh = q[:, ht * PEER_HALF:(ht + 1) * PEER_HALF].astype(jnp.bfloat16)
            s = lax.dot_general(keys_ref[ht], qh, (((1,), (1,)), ((), ())),
                                preferred_element_type=f32)
            for r in range(PEER_TOPK):
                m = jnp.max(s, axis=0, keepdims=True)
                pick = jnp.min(jnp.where(s == m, key_id, float(N_KEYS)), axis=0, keepdims=True)
                s = jnp.where(key_id == pick, neg_inf, s)
                val_scr[t, r:r + 1, :] = m
                id_scr[t, r:r + 1, :] = pick
        cand = _pair_rows(val_scr[0], val_scr[1])
        cand = jnp.where(ftab < float(PEER_TOPK * PEER_TOPK), cand, neg_inf)
        expert = _pair_rows(id_scr[0] * float(N_KEYS), id_scr[1])
        for r in range(PEER_TOPK):
            m = jnp.max(cand, axis=0, keepdims=True)
            pick = jnp.min(jnp.where(cand == m, ftab, INVALID_FLAT), axis=0, keepdims=True)
            hit = ftab == pick
            sc_scr[r:r + 1, :] = m
            e_scr[r:r + 1, :] = jnp.sum(jnp.where(hit, expert, 0.0), axis=0, keepdims=True)
            cand = jnp.where(hit, neg_inf, cand)
        sc = sc_scr[...]
        p = jnp.exp(sc - sc[0:1])
        gate = p / jnp.sum(p, axis=0, keepdims=True)
        idx_ref[head * PEER_TOPK:(head + 1) * PEER_TOPK, :] = e_scr[...].astype(jnp.int32)
        gate_ref[head * PEER_TOPK:(head + 1) * PEER_TOPK, :] = gate


def _peer_route(x, norm_g, shift, scale, wq_bf, keys_bf, tokens_per_mod):
    T, D = x.shape
    nq = wq_bf.shape[1]
    blocks_per_mod = tokens_per_mod // PEER_TB
    return pl.pallas_call(
        _peer_route_kernel,
        grid=(T // PEER_TB,),
        in_specs=[
            pl.BlockSpec((PEER_TB, D), lambda i: (i, 0)),
            pl.BlockSpec((1, D), lambda i: (0, 0)),
            pl.BlockSpec((1, 1, D), lambda i: (i // blocks_per_mod, 0, 0)),
            pl.BlockSpec((1, 1, D), lambda i: (i // blocks_per_mod, 0, 0)),
            pl.BlockSpec((D, nq), lambda i: (0, 0)),
            pl.BlockSpec((2 * PEER_HEADS, N_KEYS, PEER_HALF), lambda i: (0, 0, 0)),
            pl.BlockSpec((CAND_ROWS, PEER_TB), lambda i: (0, 0)),
        ],
        out_specs=[
            pl.BlockSpec((N_SLOTS, PEER_TB), lambda i: (0, i)),
            pl.BlockSpec((N_SLOTS, PEER_TB), lambda i: (0, i)),
            pl.BlockSpec((PEER_TB, D), lambda i: (i, 0)),
        ],
        out_shape=[
            jax.ShapeDtypeStruct((N_SLOTS, T), jnp.int32),
            jax.ShapeDtypeStruct((N_SLOTS, T), jnp.float32),
            jax.ShapeDtypeStruct((T, D), jnp.bfloat16),
        ],
        scratch_shapes=[pltpu.VMEM((2, PEER_TOPK, PEER_TB), jnp.float32),
                        pltpu.VMEM((2, PEER_TOPK, PEER_TB), jnp.float32),
                        pltpu.VMEM((PEER_TOPK, PEER_TB), jnp.float32),
                        pltpu.VMEM((PEER_TOPK, PEER_TB), jnp.float32)],
        compiler_params=pltpu.CompilerParams(dimension_semantics=("arbitrary",),
                                             vmem_limit_bytes=VMEM_LIMIT),
        name="peer_route",
    )(x, norm_g.reshape(1, D).astype(jnp.float32), shift, scale, wq_bf, keys_bf, _cand_flat_table())


def _gather_rows(table, idx2d):
    n_chunks_total, chunk = idx2d.shape
    width = table.shape[1]
    n_workers = SC_CORES * SC_SUBCORES
    n_chunks = n_chunks_total // n_workers
    mesh = plsc.VectorSubcoreMesh(core_axis_name="c", subcore_axis_name="s")

    def body(table_hbm, idx_hbm, out_hbm, idx_v, rows_v, gsem, wsem):
        wid = lax.axis_index("s") * SC_CORES + lax.axis_index("c")
        cbase = wid * n_chunks

        def gather_copy(slot):
            return pltpu.make_async_copy(table_hbm.at[idx_v.at[slot]], rows_v.at[slot], gsem.at[slot])

        def writeback_copy(r, slot):
            return pltpu.make_async_copy(rows_v.at[slot], out_hbm.at[pl.ds(r * chunk, chunk)], wsem.at[slot])

        pltpu.sync_copy(idx_hbm.at[cbase], idx_v.at[0])
        gather_copy(0).start()

        def step(i2, carry):
            for slot in range(2):
                i = i2 * 2 + slot
                r = cbase + i
                other = 1 - slot
                gather_copy(slot).wait()

                @pl.when(i >= 1)
                def _():
                    writeback_copy(r - 1, other).wait()

                @pl.when(i + 1 < n_chunks)
                def _():
                    pltpu.sync_copy(idx_hbm.at[r + 1], idx_v.at[other])
                    gather_copy(other).start()

                writeback_copy(r, slot).start()
            return carry

        lax.fori_loop(0, n_chunks // 2, step, 0)
        writeback_copy(cbase + n_chunks - 1, 1).wait()

    return pl.kernel(
        body, mesh=mesh,
        out_type=jax.ShapeDtypeStruct((n_chunks_total * chunk, width), jnp.int32),
        scratch_types=[pltpu.VMEM((2, chunk), jnp.int32),
                       pltpu.VMEM((2, chunk, width), jnp.int32),
                       pltpu.SemaphoreType.DMA((2,)),
                       pltpu.SemaphoreType.DMA((2,))],
        name="peer_gather",
    )(table, idx2d)


def _unpack_pair(words):
    lo = lax.bitcast_convert_type(words << 16, jnp.float32)
    hi = lax.bitcast_convert_type(words & jnp.int32(-65536), jnp.float32)
    return lo, hi


def _as_bf16_rows(words):
    return pltpu.bitcast(words, jnp.bfloat16)


def _peer_expert_kernel(rows_ref, h_ref, gate_ref, x_ref, g2_ref, o_ref):
    f32 = jnp.float32
    acts = []
    for t in range(PEER_EB):
        hb = _as_bf16_rows(jnp.broadcast_to(h_ref[t:t + 1, :], (N_SLOTS, HALF_D)))
        prod = _as_bf16_rows(rows_ref[t, :, :HALF_D]) * hb
        part = ((prod[:, 0:LANES] + prod[:, LANES:2 * LANES])
                + (prod[:, 2 * LANES:3 * LANES] + prod[:, 3 * LANES:4 * LANES]))
        p_lo, p_hi = _unpack_pair(pltpu.bitcast(part, jnp.int32))
        acts.append(jnp.sum(p_lo + p_hi, axis=-1, keepdims=True))
    wgt = jax.nn.gelu(jnp.concatenate(acts, axis=1)) * gate_ref[0]
    w_bits = lax.bitcast_convert_type(wgt.astype(jnp.bfloat16).astype(f32), jnp.int32)
    w_pair = w_bits | lax.shift_right_logical(w_bits, 16)
    outs = []
    for t in range(PEER_EB):
        wb = _as_bf16_rows(jnp.broadcast_to(w_pair[:, t:t + 1], (N_SLOTS, HALF_D)))
        tree = _as_bf16_rows(rows_ref[t, :, HALF_D:]) * wb
        rows = 2 * N_SLOTS
        while rows > 16:
            rows //= 2
            tree = tree[:rows] + tree[rows:]
        o_lo, o_hi = _unpack_pair(pltpu.bitcast(tree, jnp.int32))
        outs.append(jnp.concatenate([jnp.sum(o_lo, axis=0, keepdims=True),
                                     jnp.sum(o_hi, axis=0, keepdims=True)], axis=-1))
    o_ref[...] = x_ref[...] + g2_ref[0] * jnp.concatenate(outs, axis=0)


def _peer_expert(rows, h2, gate_t, x, g2, tokens_per_mod):
    T, D = x.shape
    nblk = T // PEER_EB
    blocks_per_mod = tokens_per_mod // PEER_EB
    gate_blk = jnp.transpose(gate_t.reshape(N_SLOTS, nblk, PEER_EB), (1, 0, 2))
    return pl.pallas_call(
        _peer_expert_kernel,
        grid=(nblk,),
        in_specs=[
            pl.BlockSpec((PEER_EB, N_SLOTS, D), lambda i: (i, 0, 0)),
            pl.BlockSpec((PEER_EB, HALF_D), lambda i: (i, 0)),
            pl.BlockSpec((1, N_SLOTS, PEER_EB), lambda i: (i, 0, 0)),
            pl.BlockSpec((PEER_EB, D), lambda i: (i, 0)),
            pl.BlockSpec((1, 1, D), lambda i: (i // blocks_per_mod, 0, 0)),
        ],
        out_specs=pl.BlockSpec((PEER_EB, D), lambda i: (i, 0)),
        out_shape=jax.ShapeDtypeStruct((T, D), jnp.float32),
        compiler_params=pltpu.CompilerParams(dimension_semantics=("arbitrary",),
                                             vmem_limit_bytes=VMEM_LIMIT),
        name="peer_expert",
    )(rows.reshape(T, N_SLOTS, D), _pack_halves(h2), gate_blk, x, g2)


def _pack_halves(w):
    bits = lax.bitcast_convert_type(w.astype(jnp.bfloat16), jnp.uint16).astype(jnp.uint32)
    return lax.bitcast_convert_type(bits[:, :HALF_D] | (bits[:, HALF_D:] << 16), jnp.int32)


def _pack_tables(u_emb, v_emb):
    return jnp.concatenate([_pack_halves(u_emb), _pack_halves(v_emb)], axis=1)


def _peer_residual(x, norm_g, shift, scale, gate2, wq_bf, keys_bf, table, tokens_per_mod, n_splits):
    T, D = x.shape
    ts = T // n_splits
    mods_per_split = max(ts // tokens_per_mod, 1)
    outs = []
    for s in range(n_splits):
        m0 = (s * ts) // tokens_per_mod
        mods = slice(m0, m0 + mods_per_split)
        xs = x[s * ts:(s + 1) * ts]
        idx_t, gate_t, h2 = _peer_route(xs, norm_g, shift[mods], scale[mods], wq_bf, keys_bf, tokens_per_mod)
        rows = _gather_rows(table, jnp.transpose(idx_t).reshape(ts * N_SLOTS // SC_CHUNK, SC_CHUNK))
        outs.append(_peer_expert(rows, h2, gate_t, xs, gate2[mods], tokens_per_mod))
    return outs[0] if n_splits == 1 else jnp.concatenate(outs, axis=0)


def kernel(x, c, ctx, c_ctx, w_ada, b_ada, norm1_g, norm2_g, w_in, w_out,
           qn_a, kn_a, qn_b, kn_b, lam_q1, lam_k1, lam_q2, lam_k2, subln_g,
           s5_a_re, s5_a_im, s5_log_dt, s5_b_re, s5_b_im, s5_c_re, s5_c_im,
           s5_d, s5_w_glu, peer_wq, peer_keys, peer_u, peer_v):
    B, L, D = x.shape
    Lc = ctx.shape[1]
    depth = w_in.shape[0]
    rows = L // GRID_W
    row = jnp.repeat(jnp.arange(rows, dtype=jnp.int32), GRID_W)
    col = jnp.tile(jnp.arange(GRID_W, dtype=jnp.int32), rows)
    s_c = jax.nn.silu(c)
    s_cc = jax.nn.silu(c_ctx)
    mods, cmods, s5_ops, wq_bf, keys_bf, tables = [], [], [], [], [], []
    for l in range(depth):
        mods.append(jnp.split((s_c @ w_ada[l] + b_ada[l])[:, None, :], 6, axis=-1))
        cmods.append([m.reshape(1, 1, D) for m in jnp.split(s_cc @ w_ada[l] + b_ada[l], 6, axis=-1)])
        s5_ops.append(_s5_operators(s5_a_re[l], s5_a_im[l], s5_log_dt[l], s5_b_re[l], s5_b_im[l],
                                    s5_c_re[l], s5_c_im[l]))
        wq_bf.append(peer_wq[l].astype(jnp.bfloat16))
        keys_bf.append(peer_keys[l].reshape(2 * PEER_HEADS, N_KEYS, PEER_HALF).astype(jnp.bfloat16))
        tables.append(_pack_tables(peer_u[l], peer_v[l]))

    outs = []
    for b in range(B):
        xb = x[b:b + 1]
        cb = ctx[b:b + 1]
        for l in range(depth):
            with_ctx = l < depth - 1
            sh1, sc1, g1, sh2, sc2, g2 = [m[b:b + 1] for m in mods[l]]
            csh1, csc1, cg1, csh2, csc2, cg2 = cmods[l]
            h = _modulate(_rms(xb, norm1_g[l]), sh1, sc1)
            hc = _modulate(_rms(cb, norm1_g[l]), csh1, csc1)
            mix, mixc = _token_mixers(h, hc, row, col, l, w_in[l], qn_a[l], kn_a[l], qn_b[l], kn_b[l],
                                      lam_q1[l], lam_k1[l], lam_q2[l], lam_k2[l], subln_g[l],
                                      s5_ops[l], s5_d[l], s5_w_glu[l], with_ctx)
            xb = xb + g1 * (mix @ w_out[l])
            xb = _peer_residual(xb.reshape(L, D), norm2_g[l], sh2, sc2, g2, wq_bf[l], keys_bf[l], tables[l],
                                tokens_per_mod=L, n_splits=1).reshape(1, L, D)
            if with_ctx:
                cb = cb + cg1 * (mixc @ w_out[l])
                cb = _peer_residual(cb.reshape(Lc, D), norm2_g[l], csh2, csc2, cg2, wq_bf[l], keys_bf[l],
                                    tables[l], tokens_per_mod=Lc, n_splits=1).reshape(1, Lc, D)
        outs.append(xb)
    return jnp.concatenate(outs, axis=0)
```

```python
import functools
import math

import jax
import jax.numpy as jnp
from jax import lax
from jax.experimental import pallas as pl
from jax.experimental.pallas import tpu as pltpu
from jax.experimental.pallas import tpu_sc as plsc

D_MODEL = 1024
GRID_W = 64
EPS = 1e-6
ROPE_THETA = 10000.0
HEAD_DIM = 64
A_HEADS = 6
A_KV_HEADS = 2
A_GROUP = A_HEADS // A_KV_HEADS
A_Q = A_HEADS * HEAD_DIM
A_KV = A_KV_HEADS * HEAD_DIM
B_HEADS = 6
B_SUB = 32
B_VDIM = 2 * B_SUB
B_QK = B_HEADS * 2 * B_SUB
B_V = B_HEADS * B_VDIM
C_GROUPS = 16
C_GROUP_CH = 16
C_WIDTH = C_GROUPS * C_GROUP_CH
C_STATE = 64
PEER_HEADS = 8
N_KEYS = 128
PEER_TOPK = 16
PEER_HALF = 128
TOKEN_CHUNK = 128

LANES = 128
V_EXT = LANES
LOG2E = math.log2(math.e)
NEG_BIG = -1e30
VMEM_LIMIT = 48 * 1024 * 1024


TQ = LANES
ATTN_TK = 1408
K_PACK = LANES


def _flash_unit(k, q_t, v_t, m_ref, acc_ref, unit):
    s = jnp.dot(k, q_t, preferred_element_type=jnp.float32)
    m_old = m_ref[unit]
    m_new = jnp.maximum(m_old, jnp.max(s, axis=0, keepdims=True))
    alpha = jnp.exp2(m_old - m_new)
    p = jnp.exp2(s - m_new).astype(jnp.bfloat16)
    acc_ref[unit] = alpha * acc_ref[unit] + jnp.dot(v_t, p, preferred_element_type=jnp.float32)
    m_ref[unit] = m_new


def _flash_all_units(q_ref, k_ref, vt_ref, m_ref, acc_ref, *, n_units, units_per_group, tk, n_keys):
    m_ref[...] = jnp.full(m_ref.shape, NEG_BIG, jnp.float32)
    acc_ref[...] = jnp.zeros(acc_ref.shape, jnp.float32)

    def body(j, carry):
        off = pl.multiple_of(j * tk, tk)
        k = k_ref[0, pl.ds(off, tk), :]
        for u in range(n_units):
            _flash_unit(k, q_ref[0, u], vt_ref[0, u // units_per_group, :, pl.ds(off, tk)], m_ref, acc_ref, u)
        return carry

    lax.fori_loop(0, n_keys // tk, body, 0)


def _gqa_kernel(q_ref, k_ref, vt_ref, o_ref, m_ref, acc_ref, *, tk, n_keys):
    _flash_all_units(q_ref, k_ref, vt_ref, m_ref, acc_ref,
                     n_units=A_HEADS, units_per_group=A_GROUP, tk=tk, n_keys=n_keys)
    for h in range(A_HEADS):
        acc = acc_ref[h]
        o_ref[0, h * HEAD_DIM:(h + 1) * HEAD_DIM, :] = (
            acc[:HEAD_DIM] / acc[HEAD_DIM:HEAD_DIM + 1]).astype(o_ref.dtype)


def _diff_kernel(lam_ref, q_ref, k_ref, vt_ref, g_ref, o_ref, m_ref, acc_ref, *, tk, n_keys, out_scale):
    _flash_all_units(q_ref, k_ref, vt_ref, m_ref, acc_ref,
                     n_units=4, units_per_group=2, tk=tk, n_keys=n_keys)
    lam = lam_ref[0]
    for h in range(2):
        a1 = acc_ref[2 * h]
        a2 = acc_ref[2 * h + 1]
        o = a1[:B_VDIM] / a1[B_VDIM:B_VDIM + 1] - lam * (a2[:B_VDIM] / a2[B_VDIM:B_VDIM + 1])
        o = o * lax.rsqrt(jnp.mean(o * o, axis=0, keepdims=True) + EPS)
        o_ref[0, h * B_VDIM:(h + 1) * B_VDIM, :] = (o * g_ref[...] * out_scale).astype(o_ref.dtype)


def _gqa_attention(q_t, k, v_t, *, tk=ATTN_TK):
    B, _, _, L = q_t.shape
    S = k.shape[1]
    return pl.pallas_call(
        functools.partial(_gqa_kernel, tk=tk, n_keys=S),
        grid=(B, L // TQ),
        in_specs=[
            pl.BlockSpec((1, A_HEADS, K_PACK, TQ), lambda b, i: (b, 0, 0, i)),
            pl.BlockSpec((1, S, K_PACK), lambda b, i: (b, 0, 0)),
            pl.BlockSpec((1, A_KV_HEADS, V_EXT, S), lambda b, i: (b, 0, 0, 0)),
        ],
        out_specs=pl.BlockSpec((1, A_Q, TQ), lambda b, i: (b, 0, i)),
        out_shape=jax.ShapeDtypeStruct((B, A_Q, L), jnp.bfloat16),
        scratch_shapes=[pltpu.VMEM((A_HEADS, 1, TQ), jnp.float32),
                        pltpu.VMEM((A_HEADS, V_EXT, TQ), jnp.float32)],
        compiler_params=pltpu.CompilerParams(
            dimension_semantics=("arbitrary", "arbitrary"), vmem_limit_bytes=VMEM_LIMIT),
        name="gqa_attention",
    )(q_t, k, v_t)


def _diff_attention(q_t, k, v_t, lam, subln_g, out_scale, *, tk=ATTN_TK):
    B, _, _, L = q_t.shape
    S = k.shape[1]
    return pl.pallas_call(
        functools.partial(_diff_kernel, tk=tk, n_keys=S, out_scale=out_scale),
        grid=(B, B_HEADS // 2, L // TQ),
        in_specs=[
            pl.BlockSpec(memory_space=pltpu.SMEM),
            pl.BlockSpec((1, 4, K_PACK, TQ), lambda b, h, i: (b, h, 0, i)),
            pl.BlockSpec((1, S, K_PACK), lambda b, h, i: (b, 0, h)),
            pl.BlockSpec((1, 2, V_EXT, S), lambda b, h, i: (b, h, 0, 0)),
            pl.BlockSpec((B_VDIM, 1), lambda b, h, i: (0, 0)),
        ],
        out_specs=pl.BlockSpec((1, 2 * B_VDIM, TQ), lambda b, h, i: (b, h, i)),
        out_shape=jax.ShapeDtypeStruct((B, B_V, L), jnp.bfloat16),
        scratch_shapes=[pltpu.VMEM((4, 1, TQ), jnp.float32),
                        pltpu.VMEM((4, V_EXT, TQ), jnp.float32)],
        compiler_params=pltpu.CompilerParams(
            dimension_semantics=("arbitrary", "arbitrary", "arbitrary"), vmem_limit_bytes=VMEM_LIMIT),
        name="diff_attention",
    )(lam.reshape(1).astype(jnp.float32), q_t, k, v_t, subln_g.reshape(B_VDIM, 1).astype(jnp.float32))


def _value_ext_t(v):
    B, S, H, d = v.shape
    ones = jnp.ones((B, S, H, 1), v.dtype)
    pad = jnp.zeros((B, S, H, V_EXT - d - 1), v.dtype)
    return jnp.transpose(jnp.concatenate([v, ones, pad], axis=-1), (0, 2, 3, 1)).astype(jnp.bfloat16)


def _query_ext_t(q, offsets):
    B, L, U, d = q.shape
    q_t = jnp.transpose(q, (0, 2, 3, 1)).astype(jnp.bfloat16)
    units = [jnp.pad(q_t[:, u], ((0, 0), (off, K_PACK - d - off), (0, 0))) for u, off in enumerate(offsets)]
    return jnp.stack(units, axis=1)


def _rms(x, g):
    xf = x.astype(jnp.float32)
    y = xf * lax.rsqrt(jnp.mean(xf * xf, axis=-1, keepdims=True) + EPS)
    return (y * g.astype(jnp.float32)).astype(x.dtype)


def _modulate(h, shift, scale):
    return h * (1 + scale) + shift


def _rope_2d(x, row, col):
    d = x.shape[-1]
    half = d // 2
    inv = 1.0 / (ROPE_THETA ** (jnp.arange(0, half, 2, dtype=jnp.float32) / half))
    ang = jnp.concatenate([row.astype(jnp.float32)[:, None] * inv[None, :],
                           col.astype(jnp.float32)[:, None] * inv[None, :]], axis=-1)
    cos = jnp.cos(ang)[None, :, None, :]
    sin = jnp.sin(ang)[None, :, None, :]
    xr = x.astype(jnp.float32).reshape(x.shape[:-1] + (half, 2))
    x0, x1 = xr[..., 0], xr[..., 1]
    out = jnp.stack([x0 * cos - x1 * sin, x0 * sin + x1 * cos], axis=-1)
    return out.reshape(x.shape).astype(x.dtype)


def _gqa_block(q, k, v):
    s = jnp.einsum('bqkgd,bskd->bkgqs', q, k).astype(jnp.float32) * (HEAD_DIM ** -0.5)
    p = jax.nn.softmax(s, axis=-1).astype(v.dtype)
    return jnp.einsum('bkgqs,bskd->bqkgd', p, v)


def _diff_block(q, k, v, lam):
    s = jnp.einsum('bqhtd,bshtd->bthqs', q, k).astype(jnp.float32) * (B_SUB ** -0.5)
    p = jax.nn.softmax(s, axis=-1)
    a = (p[:, 0] - lam * p[:, 1]).astype(v.dtype)
    return jnp.einsum('bhqs,bshd->bqhd', a, v)


S5_TC = 64
S5_ROW_TILE = 16
S5_CW = S5_TC * C_GROUP_CH
S5_SW = 2 * C_STATE


def _s5_operators(a_re, a_im, log_dt, b_re, b_im, c_re, c_im):
    f32 = jnp.float32
    tc, P, N, G = S5_TC, C_GROUP_CH, C_STATE, C_GROUPS
    j = jnp.arange(tc + 1, dtype=f32)
    toep, p_parts, q_parts, a_parts = [], [], [], []
    s_idx = jnp.arange(tc)[:, None]
    t_idx = jnp.arange(tc)[None, :]
    for dirn in range(2):
        A = lax.complex(a_re[dirn].astype(f32), a_im[dirn].astype(f32))
        adt = A * jnp.exp(log_dt[dirn].astype(f32))[:, None]
        abar = jnp.exp(adt)
        bbar = ((abar - 1) / A)[..., None] * lax.complex(b_re[dirn].astype(f32), b_im[dirn].astype(f32))
        cmat = lax.complex(c_re[dirn].astype(f32), c_im[dirn].astype(f32))
        pw = jnp.exp(adt[None] * j[:, None, None])
        kern = jnp.einsum('gpn,jgn,gnq->jgpq', cmat, pw[:tc], bbar).real
        lag = (s_idx - t_idx) if dirn == 1 else (t_idx - s_idx)
        tm = jnp.where((lag >= 0)[:, :, None, None, None], kern[jnp.clip(lag, 0, tc - 1)], 0.0)
        toep.append(jnp.transpose(tm, (2, 0, 4, 1, 3)).reshape(G, S5_CW, S5_CW))
        inj_pw = pw[:tc] if dirn == 1 else pw[:tc][::-1]
        inj = inj_pw[:, :, :, None] * bbar[None]
        p_parts.append(jnp.transpose(inj, (1, 0, 3, 2)).reshape(G, S5_CW, N))
        out_pw = pw[1:][::-1] if dirn == 1 else pw[1:]
        outm = cmat[None] * out_pw[:, :, None, :]
        q_parts.append(jnp.transpose(outm, (1, 3, 0, 2)).reshape(G, N, S5_CW))
        a_parts.append(pw[tc])
    zeros = jnp.zeros((G, N, S5_CW), f32)
    w_mat = jnp.concatenate([
        toep[0] + toep[1],
        q_parts[0].real, zeros, -q_parts[0].imag, zeros,
        zeros, q_parts[1].real, zeros, -q_parts[1].imag,
    ], axis=1)
    p_mat = jnp.concatenate([p_parts[0].real, p_parts[1].real, p_parts[0].imag, p_parts[1].imag], axis=-1)
    a_chunk = jnp.stack([jnp.concatenate([a_parts[0].real, a_parts[1].real], axis=-1),
                         jnp.concatenate([a_parts[0].imag, a_parts[1].imag], axis=-1)], axis=1)
    return p_mat.astype(jnp.bfloat16), w_mat.astype(jnp.bfloat16), a_chunk.astype(f32)


def _s5_scan_kernel(u_ref, p_ref, w_ref, a_ref, y_ref, s_scr, hs_scr, *, rows, order_f, order_r):
    f32 = jnp.float32
    u = u_ref[0]
    s_scr[...] = jnp.dot(u, p_ref[0], preferred_element_type=f32)
    hs_scr[...] = jnp.zeros(hs_scr.shape, f32)
    a_re = a_ref[0, 0:1, :]
    a_im = a_ref[0, 1:2, :]
    is_fwd = lax.broadcasted_iota(jnp.int32, (rows, S5_SW), 1) < C_STATE
    h_re = jnp.zeros((rows, S5_SW), f32)
    h_im = jnp.zeros((rows, S5_SW), f32)
    for cf, cr in zip(order_f, order_r):
        rf = slice(cf * rows, (cf + 1) * rows)
        rr = slice(cr * rows, (cr + 1) * rows)
        hs_scr[rf, 0:S5_SW] = h_re
        hs_scr[rf, S5_SW:2 * S5_SW] = h_im
        hs_scr[rr, 2 * S5_SW:3 * S5_SW] = h_re
        hs_scr[rr, 3 * S5_SW:4 * S5_SW] = h_im
        s_re = jnp.where(is_fwd, s_scr[rf, 0:S5_SW], s_scr[rr, 0:S5_SW])
        s_im = jnp.where(is_fwd, s_scr[rf, S5_SW:2 * S5_SW], s_scr[rr, S5_SW:2 * S5_SW])
        h_re, h_im = a_re * h_re - a_im * h_im + s_re, a_re * h_im + a_im * h_re + s_im
    y = jnp.dot(u, w_ref[0, 0:S5_CW, :], preferred_element_type=f32)
    y += jnp.dot(hs_scr[...].astype(jnp.bfloat16), w_ref[0, S5_CW:, :], preferred_element_type=f32)
    y_ref[0] = y


def _s5_scan(u_seq, n_ctx_chunks, p_mat, w_mat, a_chunk):
    B, S, _ = u_seq.shape
    G, P = C_GROUPS, C_GROUP_CH
    nc = S // S5_TC
    R = -(-(nc * B) // S5_ROW_TILE) * S5_ROW_TILE
    ug = jnp.transpose(u_seq.reshape(B, nc, S5_TC, G, P), (3, 1, 0, 2, 4))
    ug = jnp.pad(ug.reshape(G, nc * B, S5_CW), ((0, 0), (0, R - nc * B), (0, 0))).astype(jnp.bfloat16)
    order_f = tuple(range(nc))
    order_r = tuple(range(n_ctx_chunks - 1, -1, -1)) + tuple(range(nc - 1, n_ctx_chunks - 1, -1))
    y = pl.pallas_call(
        functools.partial(_s5_scan_kernel, rows=B, order_f=order_f, order_r=order_r),
        grid=(G,),
        in_specs=[
            pl.BlockSpec((1, R, S5_CW), lambda g: (g, 0, 0)),
            pl.BlockSpec((1, S5_CW, 2 * S5_SW), lambda g: (g, 0, 0)),
            pl.BlockSpec((1, S5_CW + 4 * S5_SW, S5_CW), lambda g: (g, 0, 0)),
            pl.BlockSpec((1, 2, S5_SW), lambda g: (g, 0, 0)),
        ],
        out_specs=pl.BlockSpec((1, R, S5_CW), lambda g: (g, 0, 0)),
        out_shape=jax.ShapeDtypeStruct((G, R, S5_CW), jnp.float32),
        scratch_shapes=[pltpu.VMEM((R, 2 * S5_SW), jnp.float32),
                        pltpu.VMEM((R, 4 * S5_SW), jnp.float32)],
        compiler_params=pltpu.CompilerParams(dimension_semantics=("arbitrary",),
                                             vmem_limit_bytes=VMEM_LIMIT),
        name="s5_scan",
    )(ug, p_mat, w_mat, a_chunk)
    y = y[:, :nc * B].reshape(G, nc, B, S5_TC, P)
    return jnp.transpose(y, (2, 1, 3, 0, 4)).reshape(B, S, C_WIDTH)


def _s5_glu_kernel(y_ref, u_ref, d_ref, w_ref, o_ref):
    y = y_ref[...] + d_ref[...] * u_ref[...]
    g = jax.nn.gelu(y)
    z = jnp.dot(g.astype(jnp.bfloat16), w_ref[...], preferred_element_type=jnp.float32)
    o_ref[...] = (g * jax.nn.sigmoid(z)).astype(o_ref.dtype)


def _s5_glu(y, u, d_skip, w_glu):
    T = y.shape[0]
    tb = math.gcd(T, 1024)
    return pl.pallas_call(
        _s5_glu_kernel,
        grid=(T // tb,),
        in_specs=[pl.BlockSpec((tb, C_WIDTH), lambda i: (i, 0)),
                  pl.BlockSpec((tb, C_WIDTH), lambda i: (i, 0)),
                  pl.BlockSpec((1, C_WIDTH), lambda i: (0, 0)),
                  pl.BlockSpec((C_WIDTH, C_WIDTH), lambda i: (0, 0))],
        out_specs=pl.BlockSpec((tb, C_WIDTH), lambda i: (i, 0)),
        out_shape=jax.ShapeDtypeStruct((T, C_WIDTH), jnp.float32),
        compiler_params=pltpu.CompilerParams(dimension_semantics=("arbitrary",)),
        name="s5_glu",
    )(y, u, d_skip.reshape(1, C_WIDTH).astype(jnp.float32), w_glu.astype(jnp.bfloat16))


def _s5_mixer(u, uc, s5_ops, d_skip, w_glu, with_ctx):
    B, L, _ = u.shape
    Lc = uc.shape[1]
    p_mat, w_mat, a_chunk = s5_ops
    u_seq = jnp.concatenate([uc, u], axis=1).astype(jnp.float32)
    y = _s5_scan(u_seq, Lc // S5_TC, p_mat, w_mat, a_chunk)
    if with_ctx:
        out = _s5_glu(y.reshape(B * (Lc + L), C_WIDTH), u_seq.reshape(B * (Lc + L), C_WIDTH), d_skip, w_glu)
        out = out.reshape(B, Lc + L, C_WIDTH)
        return out[:, Lc:], out[:, :Lc]
    out = _s5_glu(y[:, Lc:].reshape(B * L, C_WIDTH), u.astype(jnp.float32).reshape(B * L, C_WIDTH), d_skip, w_glu)
    return out.reshape(B, L, C_WIDTH), None


def _token_mixers(h, hc, row, col, layer_idx, w_in, qn_a, kn_a, qn_b, kn_b,
                  lam_q1, lam_k1, lam_q2, lam_k2, subln_g, s5_ops, d_skip, w_glu, with_ctx):
    B, L, _ = h.shape
    Lc = hc.shape[1]
    cuts = [A_Q, A_Q + A_KV, A_Q + 2 * A_KV, A_Q + 2 * A_KV + B_QK,
            A_Q + 2 * A_KV + 2 * B_QK, A_Q + 2 * A_KV + 2 * B_QK + B_V]
    qa, ka, va, qb, kb, vb, us = jnp.split(h @ w_in, cuts, axis=-1)
    qac, kac, vac, qbc, kbc, vbc, usc = jnp.split(hc @ w_in, cuts, axis=-1)

    qa = _rope_2d(_rms(qa.reshape(B, L, A_HEADS, HEAD_DIM), qn_a), row, col)
    ka = _rope_2d(_rms(ka.reshape(B, L, A_KV_HEADS, HEAD_DIM), kn_a), row, col)
    va = va.reshape(B, L, A_KV_HEADS, HEAD_DIM)
    kac = _rms(kac.reshape(B, Lc, A_KV_HEADS, HEAD_DIM), kn_a)
    vac = vac.reshape(B, Lc, A_KV_HEADS, HEAD_DIM)
    ka_all = jnp.concatenate([kac, ka], axis=1)
    va_all = jnp.concatenate([vac, va], axis=1)
    qa_t = _query_ext_t(qa * (HEAD_DIM ** -0.5 * LOG2E), [(hq // A_GROUP) * HEAD_DIM for hq in range(A_HEADS)])
    ka_pack = ka_all.reshape(B, Lc + L, A_KV).astype(jnp.bfloat16)
    o_a = jnp.transpose(_gqa_attention(qa_t, ka_pack, _value_ext_t(va_all)), (0, 2, 1))

    lambda_init = 0.8 - 0.6 * math.exp(-0.3 * layer_idx)
    lam = (jnp.exp(jnp.sum(lam_q1.astype(jnp.float32) * lam_k1.astype(jnp.float32)))
           - jnp.exp(jnp.sum(lam_q2.astype(jnp.float32) * lam_k2.astype(jnp.float32))) + lambda_init)
    qb = _rope_2d(_rms(qb.reshape(B, L, 2 * B_HEADS, B_SUB), qn_b), row, col)
    kb = _rope_2d(_rms(kb.reshape(B, L, 2 * B_HEADS, B_SUB), kn_b), row, col)
    vb = vb.reshape(B, L, B_HEADS, B_VDIM)
    kbc = _rms(kbc.reshape(B, Lc, 2 * B_HEADS, B_SUB), kn_b)
    vbc = vbc.reshape(B, Lc, B_HEADS, B_VDIM)
    kb_all = jnp.concatenate([kbc, kb], axis=1)
    vb_all = jnp.concatenate([vbc, vb], axis=1)
    qb_t = _query_ext_t(qb * (B_SUB ** -0.5 * LOG2E), [(j % 4) * B_SUB for j in range(2 * B_HEADS)])
    kb_pack = kb_all.reshape(B, Lc + L, B_QK).astype(jnp.bfloat16)
    o_b = jnp.transpose(_diff_attention(qb_t, kb_pack, _value_ext_t(vb_all), lam, subln_g, 1 - lambda_init),
                        (0, 2, 1))

    o_c, o_cc = _s5_mixer(us, usc, s5_ops, d_skip, w_glu, with_ctx)

    mix = jnp.concatenate([o_a.astype(h.dtype), o_b.astype(h.dtype), o_c], axis=-1)
    if not with_ctx:
        return mix, None
    qac = _rms(qac.reshape(B, Lc, A_HEADS, HEAD_DIM), qn_a).reshape(B, Lc, A_KV_HEADS, A_GROUP, HEAD_DIM)
    o_ac = _gqa_block(qac, kac, vac).reshape(B, Lc, A_Q)
    kbc5 = kbc.reshape(B, Lc, B_HEADS, 2, B_SUB)
    qbc = _rms(qbc.reshape(B, Lc, 2 * B_HEADS, B_SUB), qn_b).reshape(B, Lc, B_HEADS, 2, B_SUB)
    o_bc = _diff_block(qbc, kbc5, vbc, lam)
    o_bc = (_rms(o_bc, subln_g) * (1 - lambda_init)).reshape(B, Lc, B_V)
    mixc = jnp.concatenate([o_ac, o_bc, o_cc], axis=-1)
    return mix, mixc


PEER_TB = 256
PEER_EB = 8
N_SLOTS = PEER_HEADS * PEER_TOPK
HALF_D = D_MODEL // 2
CAND_ROWS = 80
INVALID_FLAT = 1.0e9
SC_CORES = 2
SC_SUBCORES = 16
SC_CHUNK = 32


def _cand_flat_table():
    f = [float(b) for b in range(16)]
    for a in range(1, 8):
        f += [float(a * 16 + b) if (a + 1) * (b + 1) <= 16 else INVALID_FLAT for b in range(8)]
    f += [float(a * 16) for a in range(8, 16)]
    return jnp.broadcast_to(jnp.asarray(f, jnp.float32)[:, None], (CAND_ROWS, PEER_TB))


def _pair_rows(first, second):
    blocks = [first[0:1] + second]
    blocks += [first[a:a + 1] + second[0:8] for a in range(1, 8)]
    blocks += [first[8:16] + second[0:1]]
    return jnp.concatenate(blocks, axis=0)


def _peer_route_kernel(x_ref, g_ref, sh_ref, sc_ref, wq_ref, keys_ref, ftab_ref,
                       idx_ref, gate_ref, h_ref, val_scr, id_scr, sc_scr, e_scr):
    f32 = jnp.float32
    x = x_ref[...]
    y = x * lax.rsqrt(jnp.mean(x * x, axis=-1, keepdims=True) + EPS) * g_ref[...]
    h2 = (y * (1.0 + sc_ref[0]) + sh_ref[0]).astype(jnp.bfloat16)
    h_ref[...] = h2
    q = jnp.dot(h2, wq_ref[...], preferred_element_type=f32)
    ftab = ftab_ref[...]
    key_id = lax.broadcasted_iota(jnp.int32, (N_KEYS, PEER_TB), 0).astype(f32)
    neg_inf = -jnp.inf
    for head in range(PEER_HEADS):
        for t in range(2):
            ht = head * 2 + t
            qh = q[:, ht * PEER_HALF:(ht + 1) * PEER_HALF].astype(jnp.bfloat16)
            s = lax.dot_general(keys_ref[ht], qh, (((1,), (1,)), ((), ())),
                                preferred_element_type=f32)
            for r in range(PEER_TOPK):
                m = jnp.max(s, axis=0, keepdims=True)
                pick = jnp.min(jnp.where(s == m, key_id, float(N_KEYS)), axis=0, keepdims=True)
                s = jnp.where(key_id == pick, neg_inf, s)
                val_scr[t, r:r + 1, :] = m
                id_scr[t, r:r + 1, :] = pick
        cand = _pair_rows(val_scr[0], val_scr[1])
        cand = jnp.where(ftab < float(PEER_TOPK * PEER_TOPK), cand, neg_inf)
        expert = _pair_rows(id_scr[0] * float(N_KEYS), id_scr[1])
        for r in range(PEER_TOPK):
            m = jnp.max(cand, axis=0, keepdims=True)
            pick = jnp.min(jnp.where(cand == m, ftab, INVALID_FLAT), axis=0, keepdims=True)
            hit = ftab == pick
            sc_scr[r:r + 1, :] = m
            e_scr[r:r + 1, :] = jnp.sum(jnp.where(hit, expert, 0.0), axis=0, keepdims=True)
            cand = jnp.where(hit, neg_inf, cand)
        sc = sc_scr[...]
        p = jnp.exp(sc - sc[0:1])
        gate = p / jnp.sum(p, axis=0, keepdims=True)
        idx_ref[head * PEER_TOPK:(head + 1) * PEER_TOPK, :] = e_scr[...].astype(jnp.int32)
        gate_ref[head * PEER_TOPK:(head + 1) * PEER_TOPK, :] = gate


def _peer_route(x, norm_g, shift, scale, wq_bf, keys_bf, tokens_per_mod):
    T, D = x.shape
    nq = wq_bf.shape[1]
    blocks_per_mod = tokens_per_mod // PEER_TB
    return pl.pallas_call(
        _peer_route_kernel,
        grid=(T // PEER_TB,),
        in_specs=[
            pl.BlockSpec((PEER_TB, D), lambda i: (i, 0)),
            pl.BlockSpec((1, D), lambda i: (0, 0)),
            pl.BlockSpec((1, 1, D), lambda i: (i // blocks_per_mod, 0, 0)),
            pl.BlockSpec((1, 1, D), lambda i: (i // blocks_per_mod, 0, 0)),
            pl.BlockSpec((D, nq), lambda i: (0, 0)),
            pl.BlockSpec((2 * PEER_HEADS, N_KEYS, PEER_HALF), lambda i: (0, 0, 0)),
            pl.BlockSpec((CAND_ROWS, PEER_TB), lambda i: (0, 0)),
        ],
        out_specs=[
            pl.BlockSpec((N_SLOTS, PEER_TB), lambda i: (0, i)),
            pl.BlockSpec((N_SLOTS, PEER_TB), lambda i: (0, i)),
            pl.BlockSpec((PEER_TB, D), lambda i: (i, 0)),
        ],
        out_shape=[
            jax.ShapeDtypeStruct((N_SLOTS, T), jnp.int32),
            jax.ShapeDtypeStruct((N_SLOTS, T), jnp.float32),
            jax.ShapeDtypeStruct((T, D), jnp.bfloat16),
        ],
        scratch_shapes=[pltpu.VMEM((2, PEER_TOPK, PEER_TB), jnp.float32),
                        pltpu.VMEM((2, PEER_TOPK, PEER_TB), jnp.float32),
                        pltpu.VMEM((PEER_TOPK, PEER_TB), jnp.float32),
                        pltpu.VMEM((PEER_TOPK, PEER_TB), jnp.float32)],
        compiler_params=pltpu.CompilerParams(dimension_semantics=("arbitrary",),
                                             vmem_limit_bytes=VMEM_LIMIT),
        name="peer_route",
    )(x, norm_g.reshape(1, D).astype(jnp.float32), shift, scale, wq_bf, keys_bf, _cand_flat_table())


def _gather_rows(table, idx2d):
    n_chunks_total, chunk = idx2d.shape
    width = table.shape[1]
    n_workers = SC_CORES * SC_SUBCORES
    n_chunks = n_chunks_total // n_workers
    mesh = plsc.VectorSubcoreMesh(core_axis_name="c", subcore_axis_name="s")

    def body(table_hbm, idx_hbm, out_hbm, idx_v, rows_v, gsem, wsem):
        wid = lax.axis_index("s") * SC_CORES + lax.axis_index("c")
        cbase = wid * n_chunks

        def gather_copy(slot):
            return pltpu.make_async_copy(table_hbm.at[idx_v.at[slot]], rows_v.at[slot], gsem.at[slot])

        def writeback_copy(r, slot):
            return pltpu.make_async_copy(rows_v.at[slot], out_hbm.at[pl.ds(r * chunk, chunk)], wsem.at[slot])

        pltpu.sync_copy(idx_hbm.at[cbase], idx_v.at[0])
        gather_copy(0).start()

        def step(i2, carry):
            for slot in range(2):
                i = i2 * 2 + slot
                r = cbase + i
                other = 1 - slot
                gather_copy(slot).wait()

                @pl.when(i >= 1)
                def _():
                    writeback_copy(r - 1, other).wait()

                @pl.when(i + 1 < n_chunks)
                def _():
                    pltpu.sync_copy(idx_hbm.at[r + 1], idx_v.at[other])
                    gather_copy(other).start()

                writeback_copy(r, slot).start()
            return carry

        lax.fori_loop(0, n_chunks // 2, step, 0)
        writeback_copy(cbase + n_chunks - 1, 1).wait()

    return pl.kernel(
        body, mesh=mesh,
        out_type=jax.ShapeDtypeStruct((n_chunks_total * chunk, width), jnp.int32),
        scratch_types=[pltpu.VMEM((2, chunk), jnp.int32),
                       pltpu.VMEM((2, chunk, width), jnp.int32),
                       pltpu.SemaphoreType.DMA((2,)),
                       pltpu.SemaphoreType.DMA((2,))],
        name="peer_gather",
    )(table, idx2d)


def _unpack_pair(words):
    lo = lax.bitcast_convert_type(words << 16, jnp.float32)
    hi = lax.bitcast_convert_type(words & jnp.int32(-65536), jnp.float32)
    return lo, hi


def _as_bf16_rows(words):
    return pltpu.bitcast(words, jnp.bfloat16)


def _peer_expert_kernel(rows_ref, h_ref, gate_ref, x_ref, g2_ref, o_ref):
    f32 = jnp.float32
    acts = []
    for t in range(PEER_EB):
        hb = _as_bf16_rows(jnp.broadcast_to(h_ref[t:t + 1, :], (N_SLOTS, HALF_D)))
        prod = _as_bf16_rows(rows_ref[t, :, :HALF_D]) * hb
        part = ((prod[:, 0:LANES] + prod[:, LANES:2 * LANES])
                + (prod[:, 2 * LANES:3 * LANES] + prod[:, 3 * LANES:4 * LANES]))
        p_lo, p_hi = _unpack_pair(pltpu.bitcast(part, jnp.int32))
        acts.append(jnp.sum(p_lo + p_hi, axis=-1, keepdims=True))
    wgt = jax.nn.gelu(jnp.concatenate(acts, axis=1)) * gate_ref[0]
    w_bits = lax.bitcast_convert_type(wgt.astype(jnp.bfloat16).astype(f32), jnp.int32)
    w_pair = w_bits | lax.shift_right_logical(w_bits, 16)
    outs = []
    for t in range(PEER_EB):
        wb = _as_bf16_rows(jnp.broadcast_to(w_pair[:, t:t + 1], (N_SLOTS, HALF_D)))
        tree = _as_bf16_rows(rows_ref[t, :, HALF_D:]) * wb
        rows = 2 * N_SLOTS
        while rows > 16:
            rows //= 2
            tree = tree[:rows] + tree[rows:]
        o_lo, o_hi = _unpack_pair(pltpu.bitcast(tree, jnp.int32))
        outs.append(jnp.concatenate([jnp.sum(o_lo, axis=0, keepdims=True),
                                     jnp.sum(o_hi, axis=0, keepdims=True)], axis=-1))
    o_ref[...] = x_ref[...] + g2_ref[0] * jnp.concatenate(outs, axis=0)


def _peer_expert(rows, h2, gate_t, x, g2, tokens_per_mod):
    T, D = x.shape
    nblk = T // PEER_EB
    blocks_per_mod = tokens_per_mod // PEER_EB
    gate_blk = jnp.transpose(gate_t.reshape(N_SLOTS, nblk, PEER_EB), (1, 0, 2))
    return pl.pallas_call(
        _peer_expert_kernel,
        grid=(nblk,),
        in_specs=[
            pl.BlockSpec((PEER_EB, N_SLOTS, D), lambda i: (i, 0, 0)),
            pl.BlockSpec((PEER_EB, HALF_D), lambda i: (i, 0)),
            pl.BlockSpec((1, N_SLOTS, PEER_EB), lambda i: (i, 0, 0)),
            pl.BlockSpec((PEER_EB, D), lambda i: (i, 0)),
            pl.BlockSpec((1, 1, D), lambda i: (i // blocks_per_mod, 0, 0)),
        ],
        out_specs=pl.BlockSpec((PEER_EB, D), lambda i: (i, 0)),
        out_shape=jax.ShapeDtypeStruct((T, D), jnp.float32),
        compiler_params=pltpu.CompilerParams(dimension_semantics=("arbitrary",),
                                             vmem_limit_bytes=VMEM_LIMIT),
        name="peer_expert",
    )(rows.reshape(T, N_SLOTS, D), _pack_halves(h2), gate_blk, x, g2)


def _pack_halves(w):
    bits = lax.bitcast_convert_type(w.astype(jnp.bfloat16), jnp.uint16).astype(jnp.uint32)
    return lax.bitcast_convert_type(bits[:, :HALF_D] | (bits[:, HALF_D:] << 16), jnp.int32)


def _pack_tables(u_emb, v_emb):
    return jnp.concatenate([_pack_halves(u_emb), _pack_halves(v_emb)], axis=1)


def _peer_residual(x, norm_g, shift, scale, gate2, wq_bf, keys_bf, table, tokens_per_mod, n_splits):
    T, D = x.shape
    ts = T // n_splits
    mods_per_split = max(ts // tokens_per_mod, 1)
    outs = []
    for s in range(n_splits):
        m0 = (s * ts) // tokens_per_mod
        mods = slice(m0, m0 + mods_per_split)
        xs = x[s * ts:(s + 1) * ts]
        idx_t, gate_t, h2 = _peer_route(xs, norm_g, shift[mods], scale[mods], wq_bf, keys_bf, tokens_per_mod)
        rows = _gather_rows(table, jnp.transpose(idx_t).reshape(ts * N_SLOTS // SC_CHUNK, SC_CHUNK))
        outs.append(_peer_expert(rows, h2, gate_t, xs, gate2[mods], tokens_per_mod))
    return outs[0] if n_splits == 1 else jnp.concatenate(outs, axis=0)


def kernel(x, c, ctx, c_ctx, w_ada, b_ada, norm1_g, norm2_g, w_in, w_out,
           qn_a, kn_a, qn_b, kn_b, lam_q1, lam_k1, lam_q2, lam_k2, subln_g,
           s5_a_re, s5_a_im, s5_log_dt, s5_b_re, s5_b_im, s5_c_re, s5_c_im,
           s5_d, s5_w_glu, peer_wq, peer_keys, peer_u, peer_v):
    B, L, D = x.shape
    Lc = ctx.shape[1]
    depth = w_in.shape[0]
    rows = L // GRID_W
    row = jnp.repeat(jnp.arange(rows, dtype=jnp.int32), GRID_W)
    col = jnp.tile(jnp.arange(GRID_W, dtype=jnp.int32), rows)
    s_c = jax.nn.silu(c)
    s_cc = jax.nn.silu(c_ctx)
    mods, cmods, s5_ops, wq_bf, keys_bf, tables = [], [], [], [], [], []
    for l in range(depth):
        mods.append(jnp.split((s_c @ w_ada[l] + b_ada[l])[:, None, :], 6, axis=-1))
        cmods.append([m.reshape(1, 1, D) for m in jnp.split(s_cc @ w_ada[l] + b_ada[l], 6, axis=-1)])
        s5_ops.append(_s5_operators(s5_a_re[l], s5_a_im[l], s5_log_dt[l], s5_b_re[l], s5_b_im[l],
                                    s5_c_re[l], s5_c_im[l]))
        wq_bf.append(peer_wq[l].astype(jnp.bfloat16))
        keys_bf.append(peer_keys[l].reshape(2 * PEER_HEADS, N_KEYS, PEER_HALF).astype(jnp.bfloat16))
        tables.append(_pack_tables(peer_u[l], peer_v[l]))

    outs = []
    for b in range(B):
        xb = x[b:b + 1]
        cb = ctx[b:b + 1]
        for l in range(depth):
            with_ctx = l < depth - 1
            sh1, sc1, g1, sh2, sc2, g2 = [m[b:b + 1] for m in mods[l]]
            csh1, csc1, cg1, csh2, csc2, cg2 = cmods[l]
            h = _modulate(_rms(xb, norm1_g[l]), sh1, sc1)
            hc = _modulate(_rms(cb, norm1_g[l]), csh1, csc1)
            mix, mixc = _token_mixers(h, hc, row, col, l, w_in[l], qn_a[l], kn_a[l], qn_b[l], kn_b[l],
                                      lam_q1[l], lam_k1[l], lam_q2[l], lam_k2[l], subln_g[l],
                                      s5_ops[l], s5_d[l], s5_w_glu[l], with_ctx)
            xb = xb + g1 * (mix @ w_out[l])
            xb = _peer_residual(xb.reshape(L, D), norm2_g[l], sh2, sc2, g2, wq_bf[l], keys_bf[l], tables[l],
                                tokens_per_mod=L, n_splits=1).reshape(1, L, D)
            if with_ctx:
                cb = cb + cg1 * (mixc @ w_out[l])
                cb = _peer_residual(cb.reshape(Lc, D), norm2_g[l], csh2, csc2, cg2, wq_bf[l], keys_bf[l],
                                    tables[l], tokens_per_mod=Lc, n_splits=1).reshape(1, Lc, D)
        outs.append(xb)
    return jnp.concatenate(outs, axis=0)
```

```python
import functools
import math

import jax
import jax.numpy as jnp
from jax import lax
from jax.experimental import pallas as pl
from jax.experimental.pallas import tpu as pltpu
from jax.experimental.pallas import tpu_sc as plsc

D_MODEL = 1024
GRID_W = 64
EPS = 1e-6
ROPE_THETA = 10000.0
HEAD_DIM = 64
A_HEADS = 6
A_KV_HEADS = 2
A_GROUP = A_HEADS // A_KV_HEADS
A_Q = A_HEADS * HEAD_DIM
A_KV = A_KV_HEADS * HEAD_DIM
B_HEADS = 6
B_SUB = 32
B_VDIM = 2 * B_SUB
B_QK = B_HEADS * 2 * B_SUB
B_V = B_HEADS * B_VDIM
C_GROUPS = 16
C_GROUP_CH = 16
C_WIDTH = C_GROUPS * C_GROUP_CH
C_STATE = 64
PEER_HEADS = 8
N_KEYS = 128
PEER_TOPK = 16
PEER_HALF = 128
TOKEN_CHUNK = 128

LANES = 128
V_EXT = LANES
LOG2E = math.log2(math.e)
NEG_BIG = -1e30
VMEM_LIMIT = 48 * 1024 * 1024


TQ = LANES
ATTN_TK = 1408
K_PACK = LANES


def _flash_unit(k, q_t, v_t, m_ref, acc_ref, unit):
    s = jnp.dot(k, q_t, preferred_element_type=jnp.float32)
    m_old = m_ref[unit]
    m_new = jnp.maximum(m_old, jnp.max(s, axis=0, keepdims=True))
    alpha = jnp.exp2(m_old - m_new)
    p = jnp.exp2(s - m_new).astype(jnp.bfloat16)
    acc_ref[unit] = alpha * acc_ref[unit] + jnp.dot(v_t, p, preferred_element_type=jnp.float32)
    m_ref[unit] = m_new


def _flash_all_units(q_ref, k_ref, vt_ref, m_ref, acc_ref, *, n_units, units_per_group, tk, n_keys):
    m_ref[...] = jnp.full(m_ref.shape, NEG_BIG, jnp.float32)
    acc_ref[...] = jnp.zeros(acc_ref.shape, jnp.float32)

    def body(j, carry):
        off = pl.multiple_of(j * tk, tk)
        k = k_ref[0, pl.ds(off, tk), :]
        for u in range(n_units):
            _flash_unit(k, q_ref[0, u], vt_ref[0, u // units_per_group, :, pl.ds(off, tk)], m_ref, acc_ref, u)
        return carry

    lax.fori_loop(0, n_keys // tk, body, 0)


def _gqa_kernel(q_ref, k_ref, vt_ref, o_ref, m_ref, acc_ref, *, tk, n_keys):
    _flash_all_units(q_ref, k_ref, vt_ref, m_ref, acc_ref,
                     n_units=A_HEADS, units_per_group=A_GROUP, tk=tk, n_keys=n_keys)
    for h in range(A_HEADS):
        acc = acc_ref[h]
        o_ref[0, h * HEAD_DIM:(h + 1) * HEAD_DIM, :] = (
            acc[:HEAD_DIM] / acc[HEAD_DIM:HEAD_DIM + 1]).astype(o_ref.dtype)


def _diff_kernel(lam_ref, q_ref, k_ref, vt_ref, g_ref, o_ref, m_ref, acc_ref, *, tk, n_keys, out_scale):
    _flash_all_units(q_ref, k_ref, vt_ref, m_ref, acc_ref,
                     n_units=4, units_per_group=2, tk=tk, n_keys=n_keys)
    lam = lam_ref[0]
    for h in range(2):
        a1 = acc_ref[2 * h]
        a2 = acc_ref[2 * h + 1]
        o = a1[:B_VDIM] / a1[B_VDIM:B_VDIM + 1] - lam * (a2[:B_VDIM] / a2[B_VDIM:B_VDIM + 1])
        o = o * lax.rsqrt(jnp.mean(o * o, axis=0, keepdims=True) + EPS)
        o_ref[0, h * B_VDIM:(h + 1) * B_VDIM, :] = (o * g_ref[...] * out_scale).astype(o_ref.dtype)


def _gqa_attention(q_t, k, v_t, *, tk=ATTN_TK):
    B, _, _, L = q_t.shape
    S = k.shape[1]
    return pl.pallas_call(
        functools.partial(_gqa_kernel, tk=tk, n_keys=S),
        grid=(B, L // TQ),
        in_specs=[
            pl.BlockSpec((1, A_HEADS, K_PACK, TQ), lambda b, i: (b, 0, 0, i)),
            pl.BlockSpec((1, S, K_PACK), lambda b, i: (b, 0, 0)),
            pl.BlockSpec((1, A_KV_HEADS, V_EXT, S), lambda b, i: (b, 0, 0, 0)),
        ],
        out_specs=pl.BlockSpec((1, A_Q, TQ), lambda b, i: (b, 0, i)),
        out_shape=jax.ShapeDtypeStruct((B, A_Q, L), jnp.bfloat16),
        scratch_shapes=[pltpu.VMEM((A_HEADS, 1, TQ), jnp.float32),
                        pltpu.VMEM((A_HEADS, V_EXT, TQ), jnp.float32)],
        compiler_params=pltpu.CompilerParams(
            dimension_semantics=("arbitrary", "arbitrary"), vmem_limit_bytes=VMEM_LIMIT),
        name="gqa_attention",
    )(q_t, k, v_t)


def _diff_attention(q_t, k, v_t, lam, subln_g, out_scale, *, tk=ATTN_TK):
    B, _, _, L = q_t.shape
    S = k.shape[1]
    return pl.pallas_call(
        functools.partial(_diff_kernel, tk=tk, n_keys=S, out_scale=out_scale),
        grid=(B, B_HEADS // 2, L // TQ),
        in_specs=[
            pl.BlockSpec(memory_space=pltpu.SMEM),
            pl.BlockSpec((1, 4, K_PACK, TQ), lambda b, h, i: (b, h, 0, i)),
            pl.BlockSpec((1, S, K_PACK), lambda b, h, i: (b, 0, h)),
            pl.BlockSpec((1, 2, V_EXT, S), lambda b, h, i: (b, h, 0, 0)),
            pl.BlockSpec((B_VDIM, 1), lambda b, h, i: (0, 0)),
        ],
        out_specs=pl.BlockSpec((1, 2 * B_VDIM, TQ), lambda b, h, i: (b, h, i)),
        out_shape=jax.ShapeDtypeStruct((B, B_V, L), jnp.bfloat16),
        scratch_shapes=[pltpu.VMEM((4, 1, TQ), jnp.float32),
                        pltpu.VMEM((4, V_EXT, TQ), jnp.float32)],
        compiler_params=pltpu.CompilerParams(
            dimension_semantics=("arbitrary", "arbitrary", "arbitrary"), vmem_limit_bytes=VMEM_LIMIT),
        name="diff_attention",
    )(lam.reshape(1).astype(jnp.float32), q_t, k, v_t, subln_g.reshape(B_VDIM, 1).astype(jnp.float32))


def _value_ext_t(v):
    B, S, H, d = v.shape
    ones = jnp.ones((B, S, H, 1), v.dtype)
    pad = jnp.zeros((B, S, H, V_EXT - d - 1), v.dtype)
    return jnp.transpose(jnp.concatenate([v, ones, pad], axis=-1), (0, 2, 3, 1)).astype(jnp.bfloat16)


def _query_ext_t(q, offsets):
    B, L, U, d = q.shape
    q_t = jnp.transpose(q, (0, 2, 3, 1)).astype(jnp.bfloat16)
    units = [jnp.pad(q_t[:, u], ((0, 0), (off, K_PACK - d - off), (0, 0))) for u, off in enumerate(offsets)]
    return jnp.stack(units, axis=1)


def _rms(x, g):
    xf = x.astype(jnp.float32)
    y = xf * lax.rsqrt(jnp.mean(xf * xf, axis=-1, keepdims=True) + EPS)
    return (y * g.astype(jnp.float32)).astype(x.dtype)


def _modulate(h, shift, scale):
    return h * (1 + scale) + shift


def _rope_2d(x, row, col):
    d = x.shape[-1]
    half = d // 2
    inv = 1.0 / (ROPE_THETA ** (jnp.arange(0, half, 2, dtype=jnp.float32) / half))
    ang = jnp.concatenate([row.astype(jnp.float32)[:, None] * inv[None, :],
                           col.astype(jnp.float32)[:, None] * inv[None, :]], axis=-1)
    cos = jnp.cos(ang)[None, :, None, :]
    sin = jnp.sin(ang)[None, :, None, :]
    xr = x.astype(jnp.float32).reshape(x.shape[:-1] + (half, 2))
    x0, x1 = xr[..., 0], xr[..., 1]
    out = jnp.stack([x0 * cos - x1 * sin, x0 * sin + x1 * cos], axis=-1)
    return out.reshape(x.shape).astype(x.dtype)


def _gqa_block(q, k, v):
    s = jnp.einsum('bqkgd,bskd->bkgqs', q, k).astype(jnp.float32) * (HEAD_DIM ** -0.5)
    p = jax.nn.softmax(s, axis=-1).astype(v.dtype)
    return jnp.einsum('bkgqs,bskd->bqkgd', p, v)


def _diff_block(q, k, v, lam):
    s = jnp.einsum('bqhtd,bshtd->bthqs', q, k).astype(jnp.float32) * (B_SUB ** -0.5)
    p = jax.nn.softmax(s, axis=-1)
    a = (p[:, 0] - lam * p[:, 1]).astype(v.dtype)
    return jnp.einsum('bhqs,bshd->bqhd', a, v)


S5_TC = 64
S5_ROW_TILE = 16
S5_CW = S5_TC * C_GROUP_CH
S5_SW = 2 * C_STATE


def _s5_operators(a_re, a_im, log_dt, b_re, b_im, c_re, c_im):
    f32 = jnp.float32
    tc, P, N, G = S5_TC, C_GROUP_CH, C_STATE, C_GROUPS
    j = jnp.arange(tc + 1, dtype=f32)
    toep, p_parts, q_parts, a_parts = [], [], [], []
    s_idx = jnp.arange(tc)[:, None]
    t_idx = jnp.arange(tc)[None, :]
    for dirn in range(2):
        A = lax.complex(a_re[dirn].astype(f32), a_im[dirn].astype(f32))
        adt = A * jnp.exp(log_dt[dirn].astype(f32))[:, None]
        abar = jnp.exp(adt)
        bbar = ((abar - 1) / A)[..., None] * lax.complex(b_re[dirn].astype(f32), b_im[dirn].astype(f32))
        cmat = lax.complex(c_re[dirn].astype(f32), c_im[dirn].astype(f32))
        pw = jnp.exp(adt[None] * j[:, None, None])
        kern = jnp.einsum('gpn,jgn,gnq->jgpq', cmat, pw[:tc], bbar).real
        lag = (s_idx - t_idx) if dirn == 1 else (t_idx - s_idx)
        tm = jnp.where((lag >= 0)[:, :, None, None, None], kern[jnp.clip(lag, 0, tc - 1)], 0.0)
        toep.append(jnp.transpose(tm, (2, 0, 4, 1, 3)).reshape(G, S5_CW, S5_CW))
        inj_pw = pw[:tc] if dirn == 1 else pw[:tc][::-1]
        inj = inj_pw[:, :, :, None] * bbar[None]
        p_parts.append(jnp.transpose(inj, (1, 0, 3, 2)).reshape(G, S5_CW, N))
        out_pw = pw[1:][::-1] if dirn == 1 else pw[1:]
        outm = cmat[None] * out_pw[:, :, None, :]
        q_parts.append(jnp.transpose(outm, (1, 3, 0, 2)).reshape(G, N, S5_CW))
        a_parts.append(pw[tc])
    zeros = jnp.zeros((G, N, S5_CW), f32)
    w_mat = jnp.concatenate([
        toep[0] + toep[1],
        q_parts[0].real, zeros, -q_parts[0].imag, zeros,
        zeros, q_parts[1].real, zeros, -q_parts[1].imag,
    ], axis=1)
    p_mat = jnp.concatenate([p_parts[0].real, p_parts[1].real, p_parts[0].imag, p_parts[1].imag], axis=-1)
    a_chunk = jnp.stack([jnp.concatenate([a_parts[0].real, a_parts[1].real], axis=-1),
                         jnp.concatenate([a_parts[0].imag, a_parts[1].imag], axis=-1)], axis=1)
    return p_mat.astype(jnp.bfloat16), w_mat.astype(jnp.bfloat16), a_chunk.astype(f32)


def _s5_scan_kernel(u_ref, p_ref, w_ref, a_ref, y_ref, s_scr, hs_scr, *, rows, order_f, order_r):
    f32 = jnp.float32
    u = u_ref[0]
    s_scr[...] = jnp.dot(u, p_ref[0], preferred_element_type=f32)
    hs_scr[...] = jnp.zeros(hs_scr.shape, f32)
    a_re = a_ref[0, 0:1, :]
    a_im = a_ref[0, 1:2, :]
    is_fwd = lax.broadcasted_iota(jnp.int32, (rows, S5_SW), 1) < C_STATE
    h_re = jnp.zeros((rows, S5_SW), f32)
    h_im = jnp.zeros((rows, S5_SW), f32)
    for cf, cr in zip(order_f, order_r):
        rf = slice(cf * rows, (cf + 1) * rows)
        rr = slice(cr * rows, (cr + 1) * rows)
        hs_scr[rf, 0:S5_SW] = h_re
        hs_scr[rf, S5_SW:2 * S5_SW] = h_im
        hs_scr[rr, 2 * S5_SW:3 * S5_SW] = h_re
        hs_scr[rr, 3 * S5_SW:4 * S5_SW] = h_im
        s_re = jnp.where(is_fwd, s_scr[rf, 0:S5_SW], s_scr[rr, 0:S5_SW])
        s_im = jnp.where(is_fwd, s_scr[rf, S5_SW:2 * S5_SW], s_scr[rr, S5_SW:2 * S5_SW])
        h_re, h_im = a_re * h_re - a_im * h_im + s_re, a_re * h_im + a_im * h_re + s_im
    y = jnp.dot(u, w_ref[0, 0:S5_CW, :], preferred_element_type=f32)
    y += jnp.dot(hs_scr[...].astype(jnp.bfloat16), w_ref[0, S5_CW:, :], preferred_element_type=f32)
    y_ref[0] = y


def _s5_scan(u_seq, n_ctx_chunks, p_mat, w_mat, a_chunk):
    B, S, _ = u_seq.shape
    G, P = C_GROUPS, C_GROUP_CH
    nc = S // S5_TC
    R = -(-(nc * B) // S5_ROW_TILE) * S5_ROW_TILE
    ug = jnp.transpose(u_seq.reshape(B, nc, S5_TC, G, P), (3, 1, 0, 2, 4))
    ug = jnp.pad(ug.reshape(G, nc * B, S5_CW), ((0, 0), (0, R - nc * B), (0, 0))).astype(jnp.bfloat16)
    order_f = tuple(range(nc))
    order_r = tuple(range(n_ctx_chunks - 1, -1, -1)) + tuple(range(nc - 1, n_ctx_chunks - 1, -1))
    y = pl.pallas_call(
        functools.partial(_s5_scan_kernel, rows=B, order_f=order_f, order_r=order_r),
        grid=(G,),
        in_specs=[
            pl.BlockSpec((1, R, S5_CW), lambda g: (g, 0, 0)),
            pl.BlockSpec((1, S5_CW, 2 * S5_SW), lambda g: (g, 0, 0)),
            pl.BlockSpec((1, S5_CW + 4 * S5_SW, S5_CW), lambda g: (g, 0, 0)),
            pl.BlockSpec((1, 2, S5_SW), lambda g: (g, 0, 0)),
        ],
        out_specs=pl.BlockSpec((1, R, S5_CW), lambda g: (g, 0, 0)),
        out_shape=jax.ShapeDtypeStruct((G, R, S5_CW), jnp.float32),
        scratch_shapes=[pltpu.VMEM((R, 2 * S5_SW), jnp.float32),
                        pltpu.VMEM((R, 4 * S5_SW), jnp.float32)],
        compiler_params=pltpu.CompilerParams(dimension_semantics=("arbitrary",),
                                             vmem_limit_bytes=VMEM_LIMIT),
        name="s5_scan",
    )(ug, p_mat, w_mat, a_chunk)
    y = y[:, :nc * B].reshape(G, nc, B, S5_TC, P)
    return jnp.transpose(y, (2, 1, 3, 0, 4)).reshape(B, S, C_WIDTH)


def _s5_glu_kernel(y_ref, u_ref, d_ref, w_ref, o_ref):
    y = y_ref[...] + d_ref[...] * u_ref[...]
    g = jax.nn.gelu(y)
    z = jnp.dot(g.astype(jnp.bfloat16), w_ref[...], preferred_element_type=jnp.float32)
    o_ref[...] = (g * jax.nn.sigmoid(z)).astype(o_ref.dtype)


def _s5_glu(y, u, d_skip, w_glu):
    T = y.shape[0]
    tb = math.gcd(T, 1024)
    return pl.pallas_call(
        _s5_glu_kernel,
        grid=(T // tb,),
        in_specs=[pl.BlockSpec((tb, C_WIDTH), lambda i: (i, 0)),
                  pl.BlockSpec((tb, C_WIDTH), lambda i: (i, 0)),
                  pl.BlockSpec((1, C_WIDTH), lambda i: (0, 0)),
                  pl.BlockSpec((C_WIDTH, C_WIDTH), lambda i: (0, 0))],
        out_specs=pl.BlockSpec((tb, C_WIDTH), lambda i: (i, 0)),
        out_shape=jax.ShapeDtypeStruct((T, C_WIDTH), jnp.float32),
        compiler_params=pltpu.CompilerParams(dimension_semantics=("arbitrary",)),
        name="s5_glu",
    )(y, u, d_skip.reshape(1, C_WIDTH).astype(jnp.float32), w_glu.astype(jnp.bfloat16))


def _s5_mixer(u, uc, s5_ops, d_skip, w_glu, with_ctx):
    B, L, _ = u.shape
    Lc = uc.shape[1]
    p_mat, w_mat, a_chunk = s5_ops
    u_seq = jnp.concatenate([uc, u], axis=1).astype(jnp.float32)
    y = _s5_scan(u_seq, Lc // S5_TC, p_mat, w_mat, a_chunk)
    if with_ctx:
        out = _s5_glu(y.reshape(B * (Lc + L), C_WIDTH), u_seq.reshape(B * (Lc + L), C_WIDTH), d_skip, w_glu)
        out = out.reshape(B, Lc + L, C_WIDTH)
        return out[:, Lc:], out[:, :Lc]
    out = _s5_glu(y[:, Lc:].reshape(B * L, C_WIDTH), u.astype(jnp.float32).reshape(B * L, C_WIDTH), d_skip, w_glu)
    return out.reshape(B, L, C_WIDTH), None


def _token_mixers(h, hc, row, col, layer_idx, w_in, qn_a, kn_a, qn_b, kn_b,
                  lam_q1, lam_k1, lam_q2, lam_k2, subln_g, s5_ops, d_skip, w_glu, with_ctx):
    B, L, _ = h.shape
    Lc = hc.shape[1]
    cuts = [A_Q, A_Q + A_KV, A_Q + 2 * A_KV, A_Q + 2 * A_KV + B_QK,
            A_Q + 2 * A_KV + 2 * B_QK, A_Q + 2 * A_KV + 2 * B_QK + B_V]
    qa, ka, va, qb, kb, vb, us = jnp.split(h @ w_in, cuts, axis=-1)
    qac, kac, vac, qbc, kbc, vbc, usc = jnp.split(hc @ w_in, cuts, axis=-1)

    qa = _rope_2d(_rms(qa.reshape(B, L, A_HEADS, HEAD_DIM), qn_a), row, col)
    ka = _rope_2d(_rms(ka.reshape(B, L, A_KV_HEADS, HEAD_DIM), kn_a), row, col)
    va = va.reshape(B, L, A_KV_HEADS, HEAD_DIM)
    kac = _rms(kac.reshape(B, Lc, A_KV_HEADS, HEAD_DIM), kn_a)
    vac = vac.reshape(B, Lc, A_KV_HEADS, HEAD_DIM)
    ka_all = jnp.concatenate([kac, ka], axis=1)
    va_all = jnp.concatenate([vac, va], axis=1)
    qa_t = _query_ext_t(qa * (HEAD_DIM ** -0.5 * LOG2E), [(hq // A_GROUP) * HEAD_DIM for hq in range(A_HEADS)])
    ka_pack = ka_all.reshape(B, Lc + L, A_KV).astype(jnp.bfloat16)
    o_a = jnp.transpose(_gqa_attention(qa_t, ka_pack, _value_ext_t(va_all)), (0, 2, 1))

    lambda_init = 0.8 - 0.6 * math.exp(-0.3 * layer_idx)
    lam = (jnp.exp(jnp.sum(lam_q1.astype(jnp.float32) * lam_k1.astype(jnp.float32)))
           - jnp.exp(jnp.sum(lam_q2.astype(jnp.float32) * lam_k2.astype(jnp.float32))) + lambda_init)
    qb = _rope_2d(_rms(qb.reshape(B, L, 2 * B_HEADS, B_SUB), qn_b), row, col)
    kb = _rope_2d(_rms(kb.reshape(B, L, 2 * B_HEADS, B_SUB), kn_b), row, col)
    vb = vb.reshape(B, L, B_HEADS, B_VDIM)
    kbc = _rms(kbc.reshape(B, Lc, 2 * B_HEADS, B_SUB), kn_b)
    vbc = vbc.reshape(B, Lc, B_HEADS, B_VDIM)
    kb_all = jnp.concatenate([kbc, kb], axis=1)
    vb_all = jnp.concatenate([vbc, vb], axis=1)
    qb_t = _query_ext_t(qb * (B_SUB ** -0.5 * LOG2E), [(j % 4) * B_SUB for j in range(2 * B_HEADS)])
    kb_pack = kb_all.reshape(B, Lc + L, B_QK).astype(jnp.bfloat16)
    o_b = jnp.transpose(_diff_attention(qb_t, kb_pack, _value_ext_t(vb_all), lam, subln_g, 1 - lambda_init),
                        (0, 2, 1))

    o_c, o_cc = _s5_mixer(us, usc, s5_ops, d_skip, w_glu, with_ctx)

    mix = jnp.concatenate([o_a.astype(h.dtype), o_b.astype(h.dtype), o_c], axis=-1)
    if not with_ctx:
        return mix, None
    qac = _rms(qac.reshape(B, Lc, A_HEADS, HEAD_DIM), qn_a).reshape(B, Lc, A_KV_HEADS, A_GROUP, HEAD_DIM)
    o_ac = _gqa_block(qac, kac, vac).reshape(B, Lc, A_Q)
    kbc5 = kbc.reshape(B, Lc, B_HEADS, 2, B_SUB)
    qbc = _rms(qbc.reshape(B, Lc, 2 * B_HEADS, B_SUB), qn_b).reshape(B, Lc, B_HEADS, 2, B_SUB)
    o_bc = _diff_block(qbc, kbc5, vbc, lam)
    o_bc = (_rms(o_bc, subln_g) * (1 - lambda_init)).reshape(B, Lc, B_V)
    mixc = jnp.concatenate([o_ac, o_bc, o_cc], axis=-1)
    return mix, mixc


PEER_TB = 256
PEER_EB = 8
N_SLOTS = PEER_HEADS * PEER_TOPK
HALF_D = D_MODEL // 2
CAND_ROWS = 80
INVALID_FLAT = 1.0e9
SC_CORES = 2
SC_SUBCORES = 16
SC_CHUNK = 32


def _cand_flat_table():
    f = [float(b) for b in range(16)]
    for a in range(1, 8):
        f += [float(a * 16 + b) if (a + 1) * (b + 1) <= 16 else INVALID_FLAT for b in range(8)]
    f += [float(a * 16) for a in range(8, 16)]
    return jnp.broadcast_to(jnp.asarray(f, jnp.float32)[:, None], (CAND_ROWS, PEER_TB))


def _pair_rows(first, second):
    blocks = [first[0:1] + second]
    blocks += [first[a:a + 1] + second[0:8] for a in range(1, 8)]
    blocks += [first[8:16] + second[0:1]]
    return jnp.concatenate(blocks, axis=0)


def _peer_route_kernel(x_ref, g_ref, sh_ref, sc_ref, wq_ref, keys_ref, ftab_ref,
                       idx_ref, gate_ref, h_ref, val_scr, id_scr, sc_scr, e_scr):
    f32 = jnp.float32
    x = x_ref[...]
    y = x * lax.rsqrt(jnp.mean(x * x, axis=-1, keepdims=True) + EPS) * g_ref[...]
    h2 = (y * (1.0 + sc_ref[0]) + sh_ref[0]).astype(jnp.bfloat16)
    h_ref[...] = h2
    q = jnp.dot(h2, wq_ref[...], preferred_element_type=f32)
    ftab = ftab_ref[...]
    key_id = lax.broadcasted_iota(jnp.int32, (N_KEYS, PEER_TB), 0).astype(f32)
    neg_inf = -jnp.inf
    for head in range(PEER_HEADS):
        for t in range(2):
            ht = head * 2 + t
            qh = q[:, ht * PEER_HALF:(ht + 1) * PEER_HALF].astype(jnp.bfloat16)
            s = lax.dot_general(keys_ref[ht], qh, (((1,), (1,)), ((), ())),
                                preferred_element_type=f32)
            for r in range(PEER_TOPK):
                m = jnp.max(s, axis=0, keepdims=True)
                pick = jnp.min(jnp.where(s == m, key_id, float(N_KEYS)), axis=0, keepdims=True)
                s = jnp.where(key_id == pick, neg_inf, s)
                val_scr[t, r:r + 1, :] = m
                id_scr[t, r:r + 1, :] = pick
        cand = _pair_rows(val_scr[0], val_scr[1])
        cand = jnp.where(ftab < float(PEER_TOPK * PEER_TOPK), cand, neg_inf)
        expert = _pair_rows(id_scr[0] * float(N_KEYS), id_scr[1])
        for r in range(PEER_TOPK):
            m = jnp.max(cand, axis=0, keepdims=True)
            pick = jnp.min(jnp.where(cand == m, ftab, INVALID_FLAT), axis=0, keepdims=True)
            hit = ftab == pick
            sc_scr[r:r + 1, :] = m
            e_scr[r:r + 1, :] = jnp.sum(jnp.where(hit, expert, 0.0), axis=0, keepdims=True)
            cand = jnp.where(hit, neg_inf, cand)
        sc = sc_scr[...]
        p = jnp.exp(sc - sc[0:1])
        gate = p / jnp.sum(p, axis=0, keepdims=True)
        idx_ref[head * PEER_TOPK:(head + 1) * PEER_TOPK, :] = e_scr[...].astype(jnp.int32)
        gate_ref[head * PEER_TOPK:(head + 1) * PEER_TOPK, :] = gate


def _peer_route(x, norm_g, shift, scale, wq_bf, keys_bf, tokens_per_mod):
    T, D = x.shape
    nq = wq_bf.shape[1]
    blocks_per_mod = tokens_per_mod // PEER_TB
    return pl.pallas_call(
        _peer_route_kernel,
        grid=(T // PEER_TB,),
        in_specs=[
            pl.BlockSpec((PEER_TB, D), lambda i: (i, 0)),
            pl.BlockSpec((1, D), lambda i: (0, 0)),
            pl.BlockSpec((1, 1, D), lambda i: (i // blocks_per_mod, 0, 0)),
            pl.BlockSpec((1, 1, D), lambda i: (i // blocks_per_mod, 0, 0)),
            pl.BlockSpec((D, nq), lambda i: (0, 0)),
            pl.BlockSpec((2 * PEER_HEADS, N_KEYS, PEER_HALF), lambda i: (0, 0, 0)),
            pl.BlockSpec((CAND_ROWS, PEER_TB), lambda i: (0, 0)),
        ],
        out_specs=[
            pl.BlockSpec((N_SLOTS, PEER_TB), lambda i: (0, i)),
            pl.BlockSpec((N_SLOTS, PEER_TB), lambda i: (0, i)),
            pl.BlockSpec((PEER_TB, D), lambda i: (i, 0)),
        ],
        out_shape=[
            jax.ShapeDtypeStruct((N_SLOTS, T), jnp.int32),
            jax.ShapeDtypeStruct((N_SLOTS, T), jnp.float32),
            jax.ShapeDtypeStruct((T, D), jnp.bfloat16),
        ],
        scratch_shapes=[pltpu.VMEM((2, PEER_TOPK, PEER_TB), jnp.float32),
                        pltpu.VMEM((2, PEER_TOPK, PEER_TB), jnp.float32),
                        pltpu.VMEM((PEER_TOPK, PEER_TB), jnp.float32),
                        pltpu.VMEM((PEER_TOPK, PEER_TB), jnp.float32)],
        compiler_params=pltpu.CompilerParams(dimension_semantics=("arbitrary",),
                                             vmem_limit_bytes=VMEM_LIMIT),
        name="peer_route",
    )(x, norm_g.reshape(1, D).astype(jnp.float32), shift, scale, wq_bf, keys_bf, _cand_flat_table())


def _gather_rows(table, idx2d):
    n_chunks_total, chunk = idx2d.shape
    width = table.shape[1]
    n_workers = SC_CORES * SC_SUBCORES
    n_chunks = n_chunks_total // n_workers
    mesh = plsc.VectorSubcoreMesh(core_axis_name="c", subcore_axis_name="s")

    def body(table_hbm, idx_hbm, out_hbm, idx_v, rows_v, gsem, wsem):
        wid = lax.axis_index("s") * SC_CORES + lax.axis_index("c")
        cbase = wid * n_chunks

        def gather_copy(slot):
            return pltpu.make_async_copy(table_hbm.at[idx_v.at[slot]], rows_v.at[slot], gsem.at[slot])

        def writeback_copy(r, slot):
            return pltpu.make_async_copy(rows_v.at[slot], out_hbm.at[pl.ds(r * chunk, chunk)], wsem.at[slot])

        pltpu.sync_copy(idx_hbm.at[cbase], idx_v.at[0])
        gather_copy(0).start()

        def step(i2, carry):
            for slot in range(2):
                i = i2 * 2 + slot
                r = cbase + i
                other = 1 - slot
                gather_copy(slot).wait()

                @pl.when(i >= 1)
                def _():
                    writeback_copy(r - 1, other).wait()

                @pl.when(i + 1 < n_chunks)
                def _():
                    pltpu.sync_copy(idx_hbm.at[r + 1], idx_v.at[other])
                    gather_copy(other).start()

                writeback_copy(r, slot).start()
            return carry

        lax.fori_loop(0, n_chunks // 2, step, 0)
        writeback_copy(cbase + n_chunks - 1, 1).wait()

    return pl.kernel(
        body, mesh=mesh,
        out_type=jax.ShapeDtypeStruct((n_chunks_total * chunk, width), jnp.int32),
        scratch_types=[pltpu.VMEM((2, chunk), jnp.int32),
                       pltpu.VMEM((2, chunk, width), jnp.int32),
                       pltpu.SemaphoreType.DMA((2,)),
                       pltpu.SemaphoreType.DMA((2,))],
        name="peer_gather",
    )(table, idx2d)


def _unpack_pair(words):
    lo = lax.bitcast_convert_type(words << 16, jnp.float32)
    hi = lax.bitcast_convert_type(words & jnp.int32(-65536), jnp.float32)
    return lo, hi


def _as_bf16_rows(words):
    return pltpu.bitcast(words, jnp.bfloat16)


def _peer_expert_kernel(rows_ref, h_ref, gate_ref, x_ref, g2_ref, o_ref):
    f32 = jnp.float32
    acts = []
    for t in range(PEER_EB):
        hb = _as_bf16_rows(jnp.broadcast_to(h_ref[t:t + 1, :], (N_SLOTS, HALF_D)))
        prod = _as_bf16_rows(rows_ref[t, :, :HALF_D]) * hb
        part = ((prod[:, 0:LANES] + prod[:, LANES:2 * LANES])
                + (prod[:, 2 * LANES:3 * LANES] + prod[:, 3 * LANES:4 * LANES]))
        p_lo, p_hi = _unpack_pair(pltpu.bitcast(part, jnp.int32))
        acts.append(jnp.sum(p_lo + p_hi, axis=-1, keepdims=True))
    wgt = jax.nn.gelu(jnp.concatenate(acts, axis=1)) * gate_ref[0]
    w_bits = lax.bitcast_convert_type(wgt.astype(jnp.bfloat16).astype(f32), jnp.int32)
    w_pair = w_bits | lax.shift_right_logical(w_bits, 16)
    outs = []
    for t in range(PEER_EB):
        wb = _as_bf16_rows(jnp.broadcast_to(w_pair[:, t:t + 1], (N_SLOTS, HALF_D)))
        tree = _as_bf16_rows(rows_ref[t, :, HALF_D:]) * wb
        rows = 2 * N_SLOTS
        while rows > 16:
            rows //= 2
            tree = tree[:rows] + tree[rows:]
        o_lo, o_hi = _unpack_pair(pltpu.bitcast(tree, jnp.int32))
        outs.append(jnp.concatenate([jnp.sum(o_lo, axis=0, keepdims=True),
                                     jnp.sum(o_hi, axis=0, keepdims=True)], axis=-1))
    o_ref[...] = x_ref[...] + g2_ref[0] * jnp.concatenate(outs, axis=0)


def _peer_expert(rows, h2, gate_t, x, g2, tokens_per_mod):
    T, D = x.shape
    nblk = T // PEER_EB
    blocks_per_mod = tokens_per_mod // PEER_EB
    gate_blk = jnp.transpose(gate_t.reshape(N_SLOTS, nblk, PEER_EB), (1, 0, 2))
    return pl.pallas_call(
        _peer_expert_kernel,
        grid=(nblk,),
        in_specs=[
            pl.BlockSpec((PEER_EB, N_SLOTS, D), lambda i: (i, 0, 0)),
            pl.BlockSpec((PEER_EB, HALF_D), lambda i: (i, 0)),
            pl.BlockSpec((1, N_SLOTS, PEER_EB), lambda i: (i, 0, 0)),
            pl.BlockSpec((PEER_EB, D), lambda i: (i, 0)),
            pl.BlockSpec((1, 1, D), lambda i: (i // blocks_per_mod, 0, 0)),
        ],
        out_specs=pl.BlockSpec((PEER_EB, D), lambda i: (i, 0)),
        out_shape=jax.ShapeDtypeStruct((T, D), jnp.float32),
        compiler_params=pltpu.CompilerParams(dimension_semantics=("arbitrary",),
                                             vmem_limit_bytes=VMEM_LIMIT),
        name="peer_expert",
    )(rows.reshape(T, N_SLOTS, D), _pack_halves(h2), gate_blk, x, g2)


def _pack_halves(w):
    bits = lax.bitcast_convert_type(w.astype(jnp.bfloat16), jnp.uint16).astype(jnp.uint32)
    return lax.bitcast_convert_type(bits[:, :HALF_D] | (bits[:, HALF_D:] << 16), jnp.int32)


def _pack_tables(u_emb, v_emb):
    return jnp.concatenate([_pack_halves(u_emb), _pack_halves(v_emb)], axis=1)


N_EXPERTS = N_KEYS * N_KEYS
SC_LANES = 16
SC_TOKENS = 32
DENSE_TM = 512
DENSE_TE = 2048


def _peer_scores_kernel(h_ref, ut_ref, s_ref):
    s_ref[...] = jnp.dot(h_ref[...], ut_ref[...], preferred_element_type=jnp.float32)


def _peer_scores(h2, u_t):
    T, D = h2.shape
    tm = min(DENSE_TM, T)
    return pl.pallas_call(
        _peer_scores_kernel,
        grid=(N_EXPERTS // DENSE_TE, T // tm),
        in_specs=[pl.BlockSpec((tm, D), lambda j, i: (i, 0)),
                  pl.BlockSpec((D, DENSE_TE), lambda j, i: (0, j))],
        out_specs=pl.BlockSpec((tm, DENSE_TE), lambda j, i: (i, j)),
        out_shape=jax.ShapeDtypeStruct((T, N_EXPERTS), jnp.float32),
        compiler_params=pltpu.CompilerParams(dimension_semantics=("arbitrary", "arbitrary"),
                                             vmem_limit_bytes=VMEM_LIMIT),
        name="peer_scores",
    )(h2, u_t)


def _sc_worker_tokens(n_tokens):
    per_worker = n_tokens // (SC_CORES * SC_SUBCORES)
    blk = min(SC_TOKENS, per_worker)
    return per_worker, blk


def _sc_extract(scores, idx_flat):
    T = scores.shape[0]
    per_worker, blk = _sc_worker_tokens(T)
    mesh = plsc.VectorSubcoreMesh(core_axis_name="c", subcore_axis_name="s")

    def body(s_hbm, idx_hbm, out_hbm, row0_v, row1_v, idx_v, out_v, rsem):
        rows = (row0_v, row1_v)
        wid = lax.axis_index("s") * SC_CORES + lax.axis_index("c")
        t0 = wid * per_worker

        def row_copy(tok, slot):
            return pltpu.make_async_copy(s_hbm.at[tok], rows[slot], rsem.at[slot])

        def block(b, carry):
            tb = t0 + b * blk
            pltpu.sync_copy(idx_hbm.at[pl.ds(tb * N_SLOTS, blk * N_SLOTS)], idx_v)
            row_copy(tb, 0).start()

            def pair(i2, c):
                for slot in range(2):
                    i = i2 * 2 + slot

                    @pl.when(i + 1 < blk)
                    def _():
                        row_copy(tb + i + 1, 1 - slot).start()

                    row_copy(tb + i, slot).wait()
                    for j in range(N_SLOTS // SC_LANES):
                        at = pl.ds(i * N_SLOTS + j * SC_LANES, SC_LANES)
                        out_v[at] = plsc.load_gather(rows[slot], [idx_v[at]])
                return c

            lax.fori_loop(0, blk // 2, pair, 0)
            pltpu.sync_copy(out_v, out_hbm.at[pl.ds(tb * N_SLOTS, blk * N_SLOTS)])
            return carry

        lax.fori_loop(0, per_worker // blk, block, 0)

    return pl.kernel(
        body, mesh=mesh,
        out_type=jax.ShapeDtypeStruct((T * N_SLOTS,), jnp.float32),
        scratch_types=[pltpu.VMEM((N_EXPERTS,), jnp.float32), pltpu.VMEM((N_EXPERTS,), jnp.float32),
                       pltpu.VMEM((blk * N_SLOTS,), jnp.int32), pltpu.VMEM((blk * N_SLOTS,), jnp.float32),
                       pltpu.SemaphoreType.DMA((2,))],
        compiler_params=pltpu.CompilerParams(needs_layout_passes=False),
        name="peer_extract",
    )(scores, idx_flat)


def _sc_scatter(w_flat, idx_flat, n_tokens):
    T = n_tokens
    per_worker, blk = _sc_worker_tokens(T)
    n_vec = N_SLOTS // SC_LANES
    mesh = plsc.VectorSubcoreMesh(core_axis_name="c", subcore_axis_name="s")

    def body(w_hbm, idx_hbm, out_hbm, row0_v, row1_v, idx_v, w_v, wsem):
        rows = (row0_v, row1_v)
        wid = lax.axis_index("s") * SC_CORES + lax.axis_index("c")
        t0 = wid * per_worker
        zero = jnp.zeros((SC_LANES,), jnp.float32)

        def zero_fill(k, c):
            for slot in range(2):
                rows[slot][pl.ds(k * SC_LANES, SC_LANES)] = zero
            return c

        lax.fori_loop(0, N_EXPERTS // SC_LANES, zero_fill, 0)

        def out_copy(tok, slot):
            return pltpu.make_async_copy(rows[slot], out_hbm.at[tok], wsem.at[slot])

        def retire(tok, i, slot):
            out_copy(tok, slot).wait()
            for j in range(n_vec):
                plsc.store_scatter(rows[slot], [idx_v[pl.ds(i * N_SLOTS + j * SC_LANES, SC_LANES)]], zero)

        def block(b, carry):
            tb = t0 + b * blk
            pltpu.sync_copy(idx_hbm.at[pl.ds(tb * N_SLOTS, blk * N_SLOTS)], idx_v)
            pltpu.sync_copy(w_hbm.at[pl.ds(tb * N_SLOTS, blk * N_SLOTS)], w_v)

            def pair(i2, c):
                for slot in range(2):
                    i = i2 * 2 + slot

                    @pl.when(i >= 2)
                    def _():
                        retire(tb + i - 2, i - 2, slot)

                    for j in range(n_vec):
                        at = pl.ds(i * N_SLOTS + j * SC_LANES, SC_LANES)
                        plsc.addupdate_scatter(rows[slot], [idx_v[at]], w_v[at])
                    out_copy(tb + i, slot).start()
                return c

            lax.fori_loop(0, blk // 2, pair, 0)
            for slot in range(2):
                retire(tb + blk - 2 + slot, blk - 2 + slot, slot)
            return carry

        lax.fori_loop(0, per_worker // blk, block, 0)

    return pl.kernel(
        body, mesh=mesh,
        out_type=jax.ShapeDtypeStruct((T, N_EXPERTS), jnp.float32),
        scratch_types=[pltpu.VMEM((N_EXPERTS,), jnp.float32), pltpu.VMEM((N_EXPERTS,), jnp.float32),
                       pltpu.VMEM((blk * N_SLOTS,), jnp.int32), pltpu.VMEM((blk * N_SLOTS,), jnp.float32),
                       pltpu.SemaphoreType.DMA((2,))],
        compiler_params=pltpu.CompilerParams(needs_layout_passes=False),
        name="peer_scatter",
    )(w_flat, idx_flat)


def _peer_gate_kernel(a_ref, g_ref, w_ref):
    w_ref[...] = jax.nn.gelu(a_ref[...]) * g_ref[...]


def _peer_gate(act, gate):
    T = act.shape[0]
    tb = min(2048, T)
    spec = pl.BlockSpec((tb, N_SLOTS), lambda i: (i, 0))
    return pl.pallas_call(
        _peer_gate_kernel, grid=(T // tb,), in_specs=[spec, spec], out_specs=spec,
        out_shape=jax.ShapeDtypeStruct((T, N_SLOTS), jnp.float32),
        compiler_params=pltpu.CompilerParams(dimension_semantics=("arbitrary",)),
        name="peer_gate",
    )(act, gate)


def _peer_combine_kernel(w_ref, v_ref, x_ref, g2_ref, o_ref, acc_ref):
    k = pl.program_id(1)

    @pl.when(k == 0)
    def _():
        acc_ref[...] = jnp.zeros(acc_ref.shape, jnp.float32)

    acc_ref[...] += jnp.dot(w_ref[...].astype(jnp.bfloat16), v_ref[...], preferred_element_type=jnp.float32)

    @pl.when(k == pl.num_programs(1) - 1)
    def _():
        o_ref[...] = x_ref[...] + g2_ref[0] * acc_ref[...]


def _peer_combine(w_dense, v_bf, x, g2, tokens_per_mod):
    T, D = x.shape
    tm = min(DENSE_TM, T)
    blocks_per_mod = tokens_per_mod // tm
    return pl.pallas_call(
        _peer_combine_kernel,
        grid=(T // tm, N_EXPERTS // DENSE_TE),
        in_specs=[pl.BlockSpec((tm, DENSE_TE), lambda i, k: (i, k)),
                  pl.BlockSpec((DENSE_TE, D), lambda i, k: (k, 0)),
                  pl.BlockSpec((tm, D), lambda i, k: (i, 0)),
                  pl.BlockSpec((1, 1, D), lambda i, k: (i // blocks_per_mod, 0, 0))],
        out_specs=pl.BlockSpec((tm, D), lambda i, k: (i, 0)),
        out_shape=jax.ShapeDtypeStruct((T, D), jnp.float32),
        scratch_shapes=[pltpu.VMEM((tm, D), jnp.float32)],
        compiler_params=pltpu.CompilerParams(dimension_semantics=("arbitrary", "arbitrary"),
                                             vmem_limit_bytes=VMEM_LIMIT),
        name="peer_combine",
    )(w_dense, v_bf, x, g2)


def _peer_residual(x, norm_g, shift, scale, gate2, wq_bf, keys_bf, u_t, v_bf, tokens_per_mod):
    T, D = x.shape
    idx_t, gate_t, h2 = _peer_route(x, norm_g, shift, scale, wq_bf, keys_bf, tokens_per_mod)
    idx_flat = jnp.transpose(idx_t).reshape(T * N_SLOTS)
    act = _sc_extract(_peer_scores(h2, u_t), idx_flat).reshape(T, N_SLOTS)
    w = _peer_gate(act, jnp.transpose(gate_t))
    w_dense = _sc_scatter(w.reshape(T * N_SLOTS), idx_flat, T)
    return _peer_combine(w_dense, v_bf, x, gate2, tokens_per_mod)


def kernel(x, c, ctx, c_ctx, w_ada, b_ada, norm1_g, norm2_g, w_in, w_out,
           qn_a, kn_a, qn_b, kn_b, lam_q1, lam_k1, lam_q2, lam_k2, subln_g,
           s5_a_re, s5_a_im, s5_log_dt, s5_b_re, s5_b_im, s5_c_re, s5_c_im,
           s5_d, s5_w_glu, peer_wq, peer_keys, peer_u, peer_v):
    B, L, D = x.shape
    Lc = ctx.shape[1]
    depth = w_in.shape[0]
    rows = L // GRID_W
    row = jnp.repeat(jnp.arange(rows, dtype=jnp.int32), GRID_W)
    col = jnp.tile(jnp.arange(GRID_W, dtype=jnp.int32), rows)
    s_c = jax.nn.silu(c)
    s_cc = jax.nn.silu(c_ctx)
    mods, cmods, s5_ops, wq_bf, keys_bf, u_t, v_bf = [], [], [], [], [], [], []
    for l in range(depth):
        mods.append(jnp.split((s_c @ w_ada[l] + b_ada[l])[:, None, :], 6, axis=-1))
        cmods.append([m.reshape(1, 1, D) for m in jnp.split(s_cc @ w_ada[l] + b_ada[l], 6, axis=-1)])
        s5_ops.append(_s5_operators(s5_a_re[l], s5_a_im[l], s5_log_dt[l], s5_b_re[l], s5_b_im[l],
                                    s5_c_re[l], s5_c_im[l]))
        wq_bf.append(peer_wq[l].astype(jnp.bfloat16))
        keys_bf.append(peer_keys[l].reshape(2 * PEER_HEADS, N_KEYS, PEER_HALF).astype(jnp.bfloat16))
        u_t.append(jnp.transpose(peer_u[l].astype(jnp.bfloat16)))
        v_bf.append(peer_v[l].astype(jnp.bfloat16))

    outs = []
    for b in range(B):
        xb = x[b:b + 1]
        cb = ctx[b:b + 1]
        for l in range(depth):
            with_ctx = l < depth - 1
            sh1, sc1, g1, sh2, sc2, g2 = [m[b:b + 1] for m in mods[l]]
            csh1, csc1, cg1, csh2, csc2, cg2 = cmods[l]
            h = _modulate(_rms(xb, norm1_g[l]), sh1, sc1)
            hc = _modulate(_rms(cb, norm1_g[l]), csh1, csc1)
            mix, mixc = _token_mixers(h, hc, row, col, l, w_in[l], qn_a[l], kn_a[l], qn_b[l], kn_b[l],
                                      lam_q1[l], lam_k1[l], lam_q2[l], lam_k2[l], subln_g[l],
                                      s5_ops[l], s5_d[l], s5_w_glu[l], with_ctx)
            xb = xb + g1 * (mix @ w_out[l])
            xb = _peer_residual(xb.reshape(L, D), norm2_g[l], sh2, sc2, g2, wq_bf[l], keys_bf[l], u_t[l], v_bf[l],
                                tokens_per_mod=L).reshape(1, L, D)
            if with_ctx:
                cb = cb + cg1 * (mixc @ w_out[l])
                cb = _peer_residual(cb.reshape(Lc, D), norm2_g[l], csh2, csc2, cg2, wq_bf[l], keys_bf[l],
                                    u_t[l], v_bf[l], tokens_per_mod=Lc).reshape(1, Lc, D)
        outs.append(xb)
    return jnp.concatenate(outs, axis=0)
```

```python
import functools
import math

import jax
import jax.numpy as jnp
from jax import lax
from jax.experimental import pallas as pl
from jax.experimental.pallas import tpu as pltpu
from jax.experimental.pallas import tpu_sc as plsc

D_MODEL = 1024
GRID_W = 64
EPS = 1e-6
ROPE_THETA = 10000.0
HEAD_DIM = 64
A_HEADS = 6
A_KV_HEADS = 2
A_GROUP = A_HEADS // A_KV_HEADS
A_Q = A_HEADS * HEAD_DIM
A_KV = A_KV_HEADS * HEAD_DIM
B_HEADS = 6
B_SUB = 32
B_VDIM = 2 * B_SUB
B_QK = B_HEADS * 2 * B_SUB
B_V = B_HEADS * B_VDIM
C_GROUPS = 16
C_GROUP_CH = 16
C_WIDTH = C_GROUPS * C_GROUP_CH
C_STATE = 64
PEER_HEADS = 8
N_KEYS = 128
PEER_TOPK = 16
PEER_HALF = 128
TOKEN_CHUNK = 128

LANES = 128
V_EXT = LANES
LOG2E = math.log2(math.e)
NEG_BIG = -1e30
VMEM_LIMIT = 48 * 1024 * 1024


TQ = 1024
ATTN_TK = 2816
K_PACK = LANES


def _flash_unit(s, v_t, m_ref, acc_ref, unit):
    m_old = m_ref[unit]
    m_new = jnp.maximum(m_old, jnp.max(s, axis=0, keepdims=True))
    alpha = jnp.exp2(m_old - m_new)
    p = jnp.exp2(s - m_new).astype(jnp.bfloat16)
    acc_ref[unit] = alpha * acc_ref[unit] + jnp.dot(v_t, p, preferred_element_type=jnp.float32)
    m_ref[unit] = m_new


def _flash_all_units(q_ref, k_ref, vt_ref, m_ref, acc_ref, *, n_units, units_per_group, tk, n_keys):
    m_ref[...] = jnp.full(m_ref.shape, NEG_BIG, jnp.float32)
    acc_ref[...] = jnp.zeros(acc_ref.shape, jnp.float32)

    def body(j, carry):
        off = pl.multiple_of(j * tk, tk)
        k = k_ref[0, pl.ds(off, tk), :]

        def scores(u):
            return jnp.dot(k, q_ref[0, u], preferred_element_type=jnp.float32)

        s = scores(0)
        for u in range(n_units):
            s_next = scores(u + 1) if u + 1 < n_units else None
            _flash_unit(s, vt_ref[0, u // units_per_group, :, pl.ds(off, tk)], m_ref, acc_ref, u)
            s = s_next
        return carry

    lax.fori_loop(0, n_keys // tk, body, 0)


def _gqa_kernel(q_ref, k_ref, vt_ref, o_ref, m_ref, acc_ref, *, tk, n_keys):
    _flash_all_units(q_ref, k_ref, vt_ref, m_ref, acc_ref,
                     n_units=A_HEADS, units_per_group=A_GROUP, tk=tk, n_keys=n_keys)
    for h in range(A_HEADS):
        acc = acc_ref[h]
        o_ref[0, h * HEAD_DIM:(h + 1) * HEAD_DIM, :] = (
            acc[:HEAD_DIM] / acc[HEAD_DIM:HEAD_DIM + 1]).astype(o_ref.dtype)


def _diff_kernel(lam_ref, q_ref, k_ref, vt_ref, g_ref, o_ref, m_ref, acc_ref, *, tk, n_keys, out_scale):
    _flash_all_units(q_ref, k_ref, vt_ref, m_ref, acc_ref,
                     n_units=4, units_per_group=2, tk=tk, n_keys=n_keys)
    lam = lam_ref[0]
    for h in range(2):
        a1 = acc_ref[2 * h]
        a2 = acc_ref[2 * h + 1]
        o = a1[:B_VDIM] / a1[B_VDIM:B_VDIM + 1] - lam * (a2[:B_VDIM] / a2[B_VDIM:B_VDIM + 1])
        o = o * lax.rsqrt(jnp.mean(o * o, axis=0, keepdims=True) + EPS)
        o_ref[0, h * B_VDIM:(h + 1) * B_VDIM, :] = (o * g_ref[...] * out_scale).astype(o_ref.dtype)


def _gqa_attention(q_t, k, v_t, *, tk=ATTN_TK):
    B, _, _, L = q_t.shape
    S = k.shape[1]
    return pl.pallas_call(
        functools.partial(_gqa_kernel, tk=tk, n_keys=S),
        grid=(B, L // TQ),
        in_specs=[
            pl.BlockSpec((1, A_HEADS, K_PACK, TQ), lambda b, i: (b, 0, 0, i)),
            pl.BlockSpec((1, S, K_PACK), lambda b, i: (b, 0, 0)),
            pl.BlockSpec((1, A_KV_HEADS, V_EXT, S), lambda b, i: (b, 0, 0, 0)),
        ],
        out_specs=pl.BlockSpec((1, A_Q, TQ), lambda b, i: (b, 0, i)),
        out_shape=jax.ShapeDtypeStruct((B, A_Q, L), jnp.bfloat16),
        scratch_shapes=[pltpu.VMEM((A_HEADS, 1, TQ), jnp.float32),
                        pltpu.VMEM((A_HEADS, V_EXT, TQ), jnp.float32)],
        compiler_params=pltpu.CompilerParams(
            dimension_semantics=("arbitrary", "arbitrary"), vmem_limit_bytes=VMEM_LIMIT),
        name="gqa_attention",
    )(q_t, k, v_t)


def _diff_attention(q_t, k, v_t, lam, subln_g, out_scale, *, tk=ATTN_TK):
    B, _, _, L = q_t.shape
    S = k.shape[1]
    return pl.pallas_call(
        functools.partial(_diff_kernel, tk=tk, n_keys=S, out_scale=out_scale),
        grid=(B, B_HEADS // 2, L // TQ),
        in_specs=[
            pl.BlockSpec(memory_space=pltpu.SMEM),
            pl.BlockSpec((1, 4, K_PACK, TQ), lambda b, h, i: (b, h, 0, i)),
            pl.BlockSpec((1, S, K_PACK), lambda b, h, i: (b, 0, h)),
            pl.BlockSpec((1, 2, V_EXT, S), lambda b, h, i: (b, h, 0, 0)),
            pl.BlockSpec((B_VDIM, 1), lambda b, h, i: (0, 0)),
        ],
        out_specs=pl.BlockSpec((1, 2 * B_VDIM, TQ), lambda b, h, i: (b, h, i)),
        out_shape=jax.ShapeDtypeStruct((B, B_V, L), jnp.bfloat16),
        scratch_shapes=[pltpu.VMEM((4, 1, TQ), jnp.float32),
                        pltpu.VMEM((4, V_EXT, TQ), jnp.float32)],
        compiler_params=pltpu.CompilerParams(
            dimension_semantics=("arbitrary", "arbitrary", "arbitrary"), vmem_limit_bytes=VMEM_LIMIT),
        name="diff_attention",
    )(lam.reshape(1).astype(jnp.float32), q_t, k, v_t, subln_g.reshape(B_VDIM, 1).astype(jnp.float32))


def _value_ext_t(v):
    B, S, H, d = v.shape
    ones = jnp.ones((B, S, H, 1), v.dtype)
    pad = jnp.zeros((B, S, H, V_EXT - d - 1), v.dtype)
    return jnp.transpose(jnp.concatenate([v, ones, pad], axis=-1), (0, 2, 3, 1)).astype(jnp.bfloat16)


def _query_ext_t(q, offsets):
    B, L, U, d = q.shape
    q_t = jnp.transpose(q, (0, 2, 3, 1)).astype(jnp.bfloat16)
    units = [jnp.pad(q_t[:, u], ((0, 0), (off, K_PACK - d - off), (0, 0))) for u, off in enumerate(offsets)]
    return jnp.stack(units, axis=1)


def _rms(x, g):
    xf = x.astype(jnp.float32)
    y = xf * lax.rsqrt(jnp.mean(xf * xf, axis=-1, keepdims=True) + EPS)
    return (y * g.astype(jnp.float32)).astype(x.dtype)


def _modulate(h, shift, scale):
    return h * (1 + scale) + shift


def _rope_2d(x, row, col):
    d = x.shape[-1]
    half = d // 2
    inv = 1.0 / (ROPE_THETA ** (jnp.arange(0, half, 2, dtype=jnp.float32) / half))
    ang = jnp.concatenate([row.astype(jnp.float32)[:, None] * inv[None, :],
                           col.astype(jnp.float32)[:, None] * inv[None, :]], axis=-1)
    cos = jnp.cos(ang)[None, :, None, :]
    sin = jnp.sin(ang)[None, :, None, :]
    xr = x.astype(jnp.float32).reshape(x.shape[:-1] + (half, 2))
    x0, x1 = xr[..., 0], xr[..., 1]
    out = jnp.stack([x0 * cos - x1 * sin, x0 * sin + x1 * cos], axis=-1)
    return out.reshape(x.shape).astype(x.dtype)


def _gqa_block(q, k, v):
    s = jnp.einsum('bqkgd,bskd->bkgqs', q, k).astype(jnp.float32) * (HEAD_DIM ** -0.5)
    p = jax.nn.softmax(s, axis=-1).astype(v.dtype)
    return jnp.einsum('bkgqs,bskd->bqkgd', p, v)


def _diff_block(q, k, v, lam):
    s = jnp.einsum('bqhtd,bshtd->bthqs', q, k).astype(jnp.float32) * (B_SUB ** -0.5)
    p = jax.nn.softmax(s, axis=-1)
    a = (p[:, 0] - lam * p[:, 1]).astype(v.dtype)
    return jnp.einsum('bhqs,bshd->bqhd', a, v)


S5_TC = 64
S5_ROW_TILE = 16
S5_CW = S5_TC * C_GROUP_CH
S5_SW = 2 * C_STATE


def _s5_operators(a_re, a_im, log_dt, b_re, b_im, c_re, c_im):
    f32 = jnp.float32
    tc, P, N, G = S5_TC, C_GROUP_CH, C_STATE, C_GROUPS
    j = jnp.arange(tc + 1, dtype=f32)
    toep, p_parts, q_parts, a_parts = [], [], [], []
    s_idx = jnp.arange(tc)[:, None]
    t_idx = jnp.arange(tc)[None, :]
    for dirn in range(2):
        A = lax.complex(a_re[dirn].astype(f32), a_im[dirn].astype(f32))
        adt = A * jnp.exp(log_dt[dirn].astype(f32))[:, None]
        abar = jnp.exp(adt)
        bbar = ((abar - 1) / A)[..., None] * lax.complex(b_re[dirn].astype(f32), b_im[dirn].astype(f32))
        cmat = lax.complex(c_re[dirn].astype(f32), c_im[dirn].astype(f32))
        pw = jnp.exp(adt[None] * j[:, None, None])
        kern = jnp.einsum('gpn,jgn,gnq->jgpq', cmat, pw[:tc], bbar).real
        lag = (s_idx - t_idx) if dirn == 1 else (t_idx - s_idx)
        tm = jnp.where((lag >= 0)[:, :, None, None, None], kern[jnp.clip(lag, 0, tc - 1)], 0.0)
        toep.append(jnp.transpose(tm, (2, 0, 4, 1, 3)).reshape(G, S5_CW, S5_CW))
        inj_pw = pw[:tc] if dirn == 1 else pw[:tc][::-1]
        inj = inj_pw[:, :, :, None] * bbar[None]
        p_parts.append(jnp.transpose(inj, (1, 0, 3, 2)).reshape(G, S5_CW, N))
        out_pw = pw[1:][::-1] if dirn == 1 else pw[1:]
        outm = cmat[None] * out_pw[:, :, None, :]
        q_parts.append(jnp.transpose(outm, (1, 3, 0, 2)).reshape(G, N, S5_CW))
        a_parts.append(pw[tc])
    zeros = jnp.zeros((G, N, S5_CW), f32)
    w_mat = jnp.concatenate([
        toep[0] + toep[1],
        q_parts[0].real, zeros, -q_parts[0].imag, zeros,
        zeros, q_parts[1].real, zeros, -q_parts[1].imag,
    ], axis=1)
    p_mat = jnp.concatenate([p_parts[0].real, p_parts[1].real, p_parts[0].imag, p_parts[1].imag], axis=-1)
    a_chunk = jnp.stack([jnp.concatenate([a_parts[0].real, a_parts[1].real], axis=-1),
                         jnp.concatenate([a_parts[0].imag, a_parts[1].imag], axis=-1)], axis=1)
    return p_mat.astype(jnp.bfloat16), w_mat.astype(jnp.bfloat16), a_chunk.astype(f32)


def _s5_scan_kernel(u_ref, p_ref, w_ref, a_ref, y_ref, s_scr, hs_scr, *, rows, order_f, order_r):
    f32 = jnp.float32
    u = u_ref[0]
    s_scr[...] = jnp.dot(u, p_ref[0], preferred_element_type=f32)
    hs_scr[...] = jnp.zeros(hs_scr.shape, f32)
    a_re = a_ref[0, 0:1, :]
    a_im = a_ref[0, 1:2, :]
    is_fwd = lax.broadcasted_iota(jnp.int32, (rows, S5_SW), 1) < C_STATE
    h_re = jnp.zeros((rows, S5_SW), f32)
    h_im = jnp.zeros((rows, S5_SW), f32)
    for cf, cr in zip(order_f, order_r):
        rf = slice(cf * rows, (cf + 1) * rows)
        rr = slice(cr * rows, (cr + 1) * rows)
        hs_scr[rf, 0:S5_SW] = h_re
        hs_scr[rf, S5_SW:2 * S5_SW] = h_im
        hs_scr[rr, 2 * S5_SW:3 * S5_SW] = h_re
        hs_scr[rr, 3 * S5_SW:4 * S5_SW] = h_im
        s_re = jnp.where(is_fwd, s_scr[rf, 0:S5_SW], s_scr[rr, 0:S5_SW])
        s_im = jnp.where(is_fwd, s_scr[rf, S5_SW:2 * S5_SW], s_scr[rr, S5_SW:2 * S5_SW])
        h_re, h_im = a_re * h_re - a_im * h_im + s_re, a_re * h_im + a_im * h_re + s_im
    y = jnp.dot(u, w_ref[0, 0:S5_CW, :], preferred_element_type=f32)
    y += jnp.dot(hs_scr[...].astype(jnp.bfloat16), w_ref[0, S5_CW:, :], preferred_element_type=f32)
    y_ref[0] = y


def _s5_scan(u_seq, n_ctx_chunks, p_mat, w_mat, a_chunk):
    B, S, _ = u_seq.shape
    G, P = C_GROUPS, C_GROUP_CH
    nc = S // S5_TC
    R = -(-(nc * B) // S5_ROW_TILE) * S5_ROW_TILE
    ug = jnp.transpose(u_seq.reshape(B, nc, S5_TC, G, P), (3, 1, 0, 2, 4))
    ug = jnp.pad(ug.reshape(G, nc * B, S5_CW), ((0, 0), (0, R - nc * B), (0, 0))).astype(jnp.bfloat16)
    order_f = tuple(range(nc))
    order_r = tuple(range(n_ctx_chunks - 1, -1, -1)) + tuple(range(nc - 1, n_ctx_chunks - 1, -1))
    y = pl.pallas_call(
        functools.partial(_s5_scan_kernel, rows=B, order_f=order_f, order_r=order_r),
        grid=(G,),
        in_specs=[
            pl.BlockSpec((1, R, S5_CW), lambda g: (g, 0, 0)),
            pl.BlockSpec((1, S5_CW, 2 * S5_SW), lambda g: (g, 0, 0)),
            pl.BlockSpec((1, S5_CW + 4 * S5_SW, S5_CW), lambda g: (g, 0, 0)),
            pl.BlockSpec((1, 2, S5_SW), lambda g: (g, 0, 0)),
        ],
        out_specs=pl.BlockSpec((1, R, S5_CW), lambda g: (g, 0, 0)),
        out_shape=jax.ShapeDtypeStruct((G, R, S5_CW), jnp.float32),
        scratch_shapes=[pltpu.VMEM((R, 2 * S5_SW), jnp.float32),
                        pltpu.VMEM((R, 4 * S5_SW), jnp.float32)],
        compiler_params=pltpu.CompilerParams(dimension_semantics=("arbitrary",),
                                             vmem_limit_bytes=VMEM_LIMIT),
        name="s5_scan",
    )(ug, p_mat, w_mat, a_chunk)
    y = y[:, :nc * B].reshape(G, nc, B, S5_TC, P)
    return jnp.transpose(y, (2, 1, 3, 0, 4)).reshape(B, S, C_WIDTH)


def _s5_glu_kernel(y_ref, u_ref, d_ref, w_ref, o_ref):
    y = y_ref[...] + d_ref[...] * u_ref[...]
    g = jax.nn.gelu(y)
    z = jnp.dot(g.astype(jnp.bfloat16), w_ref[...], preferred_element_type=jnp.float32)
    o_ref[...] = (g * jax.nn.sigmoid(z)).astype(o_ref.dtype)


def _s5_glu(y, u, d_skip, w_glu):
    T = y.shape[0]
    tb = math.gcd(T, 1024)
    return pl.pallas_call(
        _s5_glu_kernel,
        grid=(T // tb,),
        in_specs=[pl.BlockSpec((tb, C_WIDTH), lambda i: (i, 0)),
                  pl.BlockSpec((tb, C_WIDTH), lambda i: (i, 0)),
                  pl.BlockSpec((1, C_WIDTH), lambda i: (0, 0)),
                  pl.BlockSpec((C_WIDTH, C_WIDTH), lambda i: (0, 0))],
        out_specs=pl.BlockSpec((tb, C_WIDTH), lambda i: (i, 0)),
        out_shape=jax.ShapeDtypeStruct((T, C_WIDTH), jnp.float32),
        compiler_params=pltpu.CompilerParams(dimension_semantics=("arbitrary",)),
        name="s5_glu",
    )(y, u, d_skip.reshape(1, C_WIDTH).astype(jnp.float32), w_glu.astype(jnp.bfloat16))


def _s5_mixer(u, uc, s5_ops, d_skip, w_glu, with_ctx):
    B, L, _ = u.shape
    Lc = uc.shape[1]
    p_mat, w_mat, a_chunk = s5_ops
    u_seq = jnp.concatenate([uc, u], axis=1).astype(jnp.float32)
    y = _s5_scan(u_seq, Lc // S5_TC, p_mat, w_mat, a_chunk)
    if with_ctx:
        out = _s5_glu(y.reshape(B * (Lc + L), C_WIDTH), u_seq.reshape(B * (Lc + L), C_WIDTH), d_skip, w_glu)
        out = out.reshape(B, Lc + L, C_WIDTH)
        return out[:, Lc:], out[:, :Lc]
    out = _s5_glu(y[:, Lc:].reshape(B * L, C_WIDTH), u.astype(jnp.float32).reshape(B * L, C_WIDTH), d_skip, w_glu)
    return out.reshape(B, L, C_WIDTH), None


def _token_mixers(h, hc, row, col, layer_idx, w_in, qn_a, kn_a, qn_b, kn_b,
                  lam_q1, lam_k1, lam_q2, lam_k2, subln_g, s5_ops, d_skip, w_glu, with_ctx):
    B, L, _ = h.shape
    Lc = hc.shape[1]
    cuts = [A_Q, A_Q + A_KV, A_Q + 2 * A_KV, A_Q + 2 * A_KV + B_QK,
            A_Q + 2 * A_KV + 2 * B_QK, A_Q + 2 * A_KV + 2 * B_QK + B_V]
    qa, ka, va, qb, kb, vb, us = jnp.split(h @ w_in, cuts, axis=-1)
    qac, kac, vac, qbc, kbc, vbc, usc = jnp.split(hc @ w_in, cuts, axis=-1)

    qa = _rope_2d(_rms(qa.reshape(B, L, A_HEADS, HEAD_DIM), qn_a), row, col)
    ka = _rope_2d(_rms(ka.reshape(B, L, A_KV_HEADS, HEAD_DIM), kn_a), row, col)
    va = va.reshape(B, L, A_KV_HEADS, HEAD_DIM)
    kac = _rms(kac.reshape(B, Lc, A_KV_HEADS, HEAD_DIM), kn_a)
    vac = vac.reshape(B, Lc, A_KV_HEADS, HEAD_DIM)
    ka_all = jnp.concatenate([kac, ka], axis=1)
    va_all = jnp.concatenate([vac, va], axis=1)
    qa_t = _query_ext_t(qa * (HEAD_DIM ** -0.5 * LOG2E), [(hq // A_GROUP) * HEAD_DIM for hq in range(A_HEADS)])
    ka_pack = ka_all.reshape(B, Lc + L, A_KV).astype(jnp.bfloat16)
    o_a = jnp.transpose(_gqa_attention(qa_t, ka_pack, _value_ext_t(va_all)), (0, 2, 1))

    lambda_init = 0.8 - 0.6 * math.exp(-0.3 * layer_idx)
    lam = (jnp.exp(jnp.sum(lam_q1.astype(jnp.float32) * lam_k1.astype(jnp.float32)))
           - jnp.exp(jnp.sum(lam_q2.astype(jnp.float32) * lam_k2.astype(jnp.float32))) + lambda_init)
    qb = _rope_2d(_rms(qb.reshape(B, L, 2 * B_HEADS, B_SUB), qn_b), row, col)
    kb = _rope_2d(_rms(kb.reshape(B, L, 2 * B_HEADS, B_SUB), kn_b), row, col)
    vb = vb.reshape(B, L, B_HEADS, B_VDIM)
    kbc = _rms(kbc.reshape(B, Lc, 2 * B_HEADS, B_SUB), kn_b)
    vbc = vbc.reshape(B, Lc, B_HEADS, B_VDIM)
    kb_all = jnp.concatenate([kbc, kb], axis=1)
    vb_all = jnp.concatenate([vbc, vb], axis=1)
    qb_t = _query_ext_t(qb * (B_SUB ** -0.5 * LOG2E), [(j % 4) * B_SUB for j in range(2 * B_HEADS)])
    kb_pack = kb_all.reshape(B, Lc + L, B_QK).astype(jnp.bfloat16)
    o_b = jnp.transpose(_diff_attention(qb_t, kb_pack, _value_ext_t(vb_all), lam, subln_g, 1 - lambda_init),
                        (0, 2, 1))

    o_c, o_cc = _s5_mixer(us, usc, s5_ops, d_skip, w_glu, with_ctx)

    mix = jnp.concatenate([o_a.astype(h.dtype), o_b.astype(h.dtype), o_c], axis=-1)
    if not with_ctx:
        return mix, None
    qac = _rms(qac.reshape(B, Lc, A_HEADS, HEAD_DIM), qn_a).reshape(B, Lc, A_KV_HEADS, A_GROUP, HEAD_DIM)
    o_ac = _gqa_block(qac, kac, vac).reshape(B, Lc, A_Q)
    kbc5 = kbc.reshape(B, Lc, B_HEADS, 2, B_SUB)
    qbc = _rms(qbc.reshape(B, Lc, 2 * B_HEADS, B_SUB), qn_b).reshape(B, Lc, B_HEADS, 2, B_SUB)
    o_bc = _diff_block(qbc, kbc5, vbc, lam)
    o_bc = (_rms(o_bc, subln_g) * (1 - lambda_init)).reshape(B, Lc, B_V)
    mixc = jnp.concatenate([o_ac, o_bc, o_cc], axis=-1)
    return mix, mixc


PEER_TB = 256
N_SLOTS = PEER_HEADS * PEER_TOPK
CAND_ROWS = 80
INVALID_FLAT = 1.0e9
SC_CORES = 2
SC_SUBCORES = 16


def _cand_flat_table():
    f = [float(b) for b in range(16)]
    for a in range(1, 8):
        f += [float(a * 16 + b) if (a + 1) * (b + 1) <= 16 else INVALID_FLAT for b in range(8)]
    f += [float(a * 16) for a in range(8, 16)]
    return jnp.broadcast_to(jnp.asarray(f, jnp.float32)[:, None], (CAND_ROWS, PEER_TB))


def _pair_rows(first, second):
    blocks = [first[0:1] + second]
    blocks += [first[a:a + 1] + second[0:8] for a in range(1, 8)]
    blocks += [first[8:16] + second[0:1]]
    return jnp.concatenate(blocks, axis=0)


def _peer_route_kernel(x_ref, g_ref, sh_ref, sc_ref, wq_ref, keys_ref, ftab_ref,
                       idx_ref, gate_ref, h_ref, val_scr, id_scr, sc_scr, e_scr):
    f32 = jnp.float32
    x = x_ref[...]
    y = x * lax.rsqrt(jnp.mean(x * x, axis=-1, keepdims=True) + EPS) * g_ref[...]
    h2 = (y * (1.0 + sc_ref[0]) + sh_ref[0]).astype(jnp.bfloat16)
    h_ref[...] = h2
    q = jnp.dot(h2, wq_ref[...], preferred_element_type=f32)
    ftab = ftab_ref[...]
    key_id = lax.broadcasted_iota(jnp.int32, (N_KEYS, PEER_TB), 0).astype(f32)
    neg_inf = -jnp.inf
    for head in range(PEER_HEADS):
        for t in range(2):
            ht = head * 2 + t
            qh = q[:, ht * PEER_HALF:(ht + 1) * PEER_HALF].astype(jnp.bfloat16)
            s = lax.dot_general(keys_ref[ht], qh, (((1,), (1,)), ((), ())),
                                preferred_element_type=f32)
            for r in range(PEER_TOPK):
                m = jnp.max(s, axis=0, keepdims=True)
                pick = jnp.min(jnp.where(s == m, key_id, float(N_KEYS)), axis=0, keepdims=True)
                s = jnp.where(key_id == pick, neg_inf, s)
                val_scr[t, r:r + 1, :] = m
                id_scr[t, r:r + 1, :] = pick
        cand = _pair_rows(val_scr[0], val_scr[1])
        cand = jnp.where(ftab < float(PEER_TOPK * PEER_TOPK), cand, neg_inf)
        expert = _pair_rows(id_scr[0] * float(N_KEYS), id_scr[1])
        for r in range(PEER_TOPK):
            m = jnp.max(cand, axis=0, keepdims=True)
            pick = jnp.min(jnp.where(cand == m, ftab, INVALID_FLAT), axis=0, keepdims=True)
            hit = ftab == pick
            sc_scr[r:r + 1, :] = m
            e_scr[r:r + 1, :] = jnp.sum(jnp.where(hit, expert, 0.0), axis=0, keepdims=True)
            cand = jnp.where(hit, neg_inf, cand)
        sc = sc_scr[...]
        p = jnp.exp(sc - sc[0:1])
        gate = p / jnp.sum(p, axis=0, keepdims=True)
        idx_ref[head * PEER_TOPK:(head + 1) * PEER_TOPK, :] = e_scr[...].astype(jnp.int32)
        gate_ref[head * PEER_TOPK:(head + 1) * PEER_TOPK, :] = gate


def _peer_route(x, norm_g, shift, scale, wq_bf, keys_bf, tokens_per_mod):
    T, D = x.shape
    nq = wq_bf.shape[1]
    blocks_per_mod = tokens_per_mod // PEER_TB
    return pl.pallas_call(
        _peer_route_kernel,
        grid=(T // PEER_TB,),
        in_specs=[
            pl.BlockSpec((PEER_TB, D), lambda i: (i, 0)),
            pl.BlockSpec((1, D), lambda i: (0, 0)),
            pl.BlockSpec((1, 1, D), lambda i: (i // blocks_per_mod, 0, 0)),
            pl.BlockSpec((1, 1, D), lambda i: (i // blocks_per_mod, 0, 0)),
            pl.BlockSpec((D, nq), lambda i: (0, 0)),
            pl.BlockSpec((2 * PEER_HEADS, N_KEYS, PEER_HALF), lambda i: (0, 0, 0)),
            pl.BlockSpec((CAND_ROWS, PEER_TB), lambda i: (0, 0)),
        ],
        out_specs=[
            pl.BlockSpec((N_SLOTS, PEER_TB), lambda i: (0, i)),
            pl.BlockSpec((N_SLOTS, PEER_TB), lambda i: (0, i)),
            pl.BlockSpec((PEER_TB, D), lambda i: (i, 0)),
        ],
        out_shape=[
            jax.ShapeDtypeStruct((N_SLOTS, T), jnp.int32),
            jax.ShapeDtypeStruct((N_SLOTS, T), jnp.float32),
            jax.ShapeDtypeStruct((T, D), jnp.bfloat16),
        ],
        scratch_shapes=[pltpu.VMEM((2, PEER_TOPK, PEER_TB), jnp.float32),
                        pltpu.VMEM((2, PEER_TOPK, PEER_TB), jnp.float32),
                        pltpu.VMEM((PEER_TOPK, PEER_TB), jnp.float32),
                        pltpu.VMEM((PEER_TOPK, PEER_TB), jnp.float32)],
        compiler_params=pltpu.CompilerParams(dimension_semantics=("arbitrary",),
                                             vmem_limit_bytes=VMEM_LIMIT),
        name="peer_route",
    )(x, norm_g.reshape(1, D).astype(jnp.float32), shift, scale, wq_bf, keys_bf, _cand_flat_table())


N_EXPERTS = N_KEYS * N_KEYS
SC_LANES = 16
SC_TOKENS = 32
DENSE_TM = 512
DENSE_TE = 2048


def _peer_scores_kernel(h_ref, ut_ref, s_ref):
    s_ref[...] = jnp.dot(h_ref[...], ut_ref[...], preferred_element_type=jnp.float32)


def _peer_scores(h2, u_t):
    T, D = h2.shape
    tm = min(DENSE_TM, T)
    return pl.pallas_call(
        _peer_scores_kernel,
        grid=(N_EXPERTS // DENSE_TE, T // tm),
        in_specs=[pl.BlockSpec((tm, D), lambda j, i: (i, 0)),
                  pl.BlockSpec((D, DENSE_TE), lambda j, i: (0, j))],
        out_specs=pl.BlockSpec((tm, DENSE_TE), lambda j, i: (i, j)),
        out_shape=jax.ShapeDtypeStruct((T, N_EXPERTS), jnp.float32),
        compiler_params=pltpu.CompilerParams(dimension_semantics=("arbitrary", "arbitrary"),
                                             vmem_limit_bytes=VMEM_LIMIT),
        name="peer_scores",
    )(h2, u_t)


def _sc_worker_tokens(n_tokens):
    per_worker = n_tokens // (SC_CORES * SC_SUBCORES)
    blk = min(SC_TOKENS, per_worker)
    return per_worker, blk


def _sc_extract(scores, idx_flat):
    T = scores.shape[0]
    per_worker, blk = _sc_worker_tokens(T)
    mesh = plsc.VectorSubcoreMesh(core_axis_name="c", subcore_axis_name="s")

    def body(s_hbm, idx_hbm, out_hbm, row0_v, row1_v, idx_v, out_v, rsem):
        rows = (row0_v, row1_v)
        wid = lax.axis_index("s") * SC_CORES + lax.axis_index("c")
        t0 = wid * per_worker

        def row_copy(tok, slot):
            return pltpu.make_async_copy(s_hbm.at[tok], rows[slot], rsem.at[slot])

        def block(b, carry):
            tb = t0 + b * blk
            pltpu.sync_copy(idx_hbm.at[pl.ds(tb * N_SLOTS, blk * N_SLOTS)], idx_v)
            row_copy(tb, 0).start()

            def pair(i2, c):
                for slot in range(2):
                    i = i2 * 2 + slot

                    @pl.when(i + 1 < blk)
                    def _():
                        row_copy(tb + i + 1, 1 - slot).start()

                    row_copy(tb + i, slot).wait()
                    for j in range(N_SLOTS // SC_LANES):
                        at = pl.ds(i * N_SLOTS + j * SC_LANES, SC_LANES)
                        out_v[at] = plsc.load_gather(rows[slot], [idx_v[at]])
                return c

            lax.fori_loop(0, blk // 2, pair, 0)
            pltpu.sync_copy(out_v, out_hbm.at[pl.ds(tb * N_SLOTS, blk * N_SLOTS)])
            return carry

        lax.fori_loop(0, per_worker // blk, block, 0)

    return pl.kernel(
        body, mesh=mesh,
        out_type=jax.ShapeDtypeStruct((T * N_SLOTS,), jnp.float32),
        scratch_types=[pltpu.VMEM((N_EXPERTS,), jnp.float32), pltpu.VMEM((N_EXPERTS,), jnp.float32),
                       pltpu.VMEM((blk * N_SLOTS,), jnp.int32), pltpu.VMEM((blk * N_SLOTS,), jnp.float32),
                       pltpu.SemaphoreType.DMA((2,))],
        compiler_params=pltpu.CompilerParams(needs_layout_passes=False),
        name="peer_extract",
    )(scores, idx_flat)


def _sc_scatter(w_flat, idx_flat, n_tokens):
    T = n_tokens
    per_worker, blk = _sc_worker_tokens(T)
    n_vec = N_SLOTS // SC_LANES
    mesh = plsc.VectorSubcoreMesh(core_axis_name="c", subcore_axis_name="s")

    def body(w_hbm, idx_hbm, out_hbm, row0_v, row1_v, idx_v, w_v, wsem):
        rows = (row0_v, row1_v)
        wid = lax.axis_index("s") * SC_CORES + lax.axis_index("c")
        t0 = wid * per_worker
        zero = jnp.zeros((SC_LANES,), jnp.float32)

        def zero_fill(k, c):
            for slot in range(2):
                rows[slot][pl.ds(k * SC_LANES, SC_LANES)] = zero
            return c

        lax.fori_loop(0, N_EXPERTS // SC_LANES, zero_fill, 0)

        def out_copy(tok, slot):
            return pltpu.make_async_copy(rows[slot], out_hbm.at[tok], wsem.at[slot])

        def retire(tok, i, slot):
            out_copy(tok, slot).wait()
            for j in range(n_vec):
                plsc.store_scatter(rows[slot], [idx_v[pl.ds(i * N_SLOTS + j * SC_LANES, SC_LANES)]], zero)

        def block(b, carry):
            tb = t0 + b * blk
            pltpu.sync_copy(idx_hbm.at[pl.ds(tb * N_SLOTS, blk * N_SLOTS)], idx_v)
            pltpu.sync_copy(w_hbm.at[pl.ds(tb * N_SLOTS, blk * N_SLOTS)], w_v)

            def pair(i2, c):
                for slot in range(2):
                    i = i2 * 2 + slot

                    @pl.when(i >= 2)
                    def _():
                        retire(tb + i - 2, i - 2, slot)

                    for j in range(n_vec):
                        at = pl.ds(i * N_SLOTS + j * SC_LANES, SC_LANES)
                        plsc.addupdate_scatter(rows[slot], [idx_v[at]], w_v[at])
                    out_copy(tb + i, slot).start()
                return c

            lax.fori_loop(0, blk // 2, pair, 0)
            for slot in range(2):
                retire(tb + blk - 2 + slot, blk - 2 + slot, slot)
            return carry

        lax.fori_loop(0, per_worker // blk, block, 0)

    return pl.kernel(
        body, mesh=mesh,
        out_type=jax.ShapeDtypeStruct((T, N_EXPERTS), jnp.float32),
        scratch_types=[pltpu.VMEM((N_EXPERTS,), jnp.float32), pltpu.VMEM((N_EXPERTS,), jnp.float32),
                       pltpu.VMEM((blk * N_SLOTS,), jnp.int32), pltpu.VMEM((blk * N_SLOTS,), jnp.float32),
                       pltpu.SemaphoreType.DMA((2,))],
        compiler_params=pltpu.CompilerParams(needs_layout_passes=False),
        name="peer_scatter",
    )(w_flat, idx_flat)


def _peer_gate_kernel(a_ref, g_ref, w_ref):
    w_ref[...] = jax.nn.gelu(a_ref[...]) * g_ref[...]


def _peer_gate(act, gate):
    T = act.shape[0]
    tb = min(2048, T)
    spec = pl.BlockSpec((tb, N_SLOTS), lambda i: (i, 0))
    return pl.pallas_call(
        _peer_gate_kernel, grid=(T // tb,), in_specs=[spec, spec], out_specs=spec,
        out_shape=jax.ShapeDtypeStruct((T, N_SLOTS), jnp.float32),
        compiler_params=pltpu.CompilerParams(dimension_semantics=("arbitrary",)),
        name="peer_gate",
    )(act, gate)


def _peer_combine_kernel(w_ref, v_ref, x_ref, g2_ref, o_ref, acc_ref):
    k = pl.program_id(1)

    @pl.when(k == 0)
    def _():
        acc_ref[...] = jnp.zeros(acc_ref.shape, jnp.float32)

    acc_ref[...] += jnp.dot(w_ref[...].astype(jnp.bfloat16), v_ref[...], preferred_element_type=jnp.float32)

    @pl.when(k == pl.num_programs(1) - 1)
    def _():
        o_ref[...] = x_ref[...] + g2_ref[0] * acc_ref[...]


def _peer_combine(w_dense, v_bf, x, g2, tokens_per_mod):
    T, D = x.shape
    tm = min(DENSE_TM, T)
    blocks_per_mod = tokens_per_mod // tm
    return pl.pallas_call(
        _peer_combine_kernel,
        grid=(T // tm, N_EXPERTS // DENSE_TE),
        in_specs=[pl.BlockSpec((tm, DENSE_TE), lambda i, k: (i, k)),
                  pl.BlockSpec((DENSE_TE, D), lambda i, k: (k, 0)),
                  pl.BlockSpec((tm, D), lambda i, k: (i, 0)),
                  pl.BlockSpec((1, 1, D), lambda i, k: (i // blocks_per_mod, 0, 0))],
        out_specs=pl.BlockSpec((tm, D), lambda i, k: (i, 0)),
        out_shape=jax.ShapeDtypeStruct((T, D), jnp.float32),
        scratch_shapes=[pltpu.VMEM((tm, D), jnp.float32)],
        compiler_params=pltpu.CompilerParams(dimension_semantics=("arbitrary", "arbitrary"),
                                             vmem_limit_bytes=VMEM_LIMIT),
        name="peer_combine",
    )(w_dense, v_bf, x, g2)


def _peer_residual(x, norm_g, shift, scale, gate2, wq_bf, keys_bf, u_t, v_bf, tokens_per_mod):
    T, D = x.shape
    idx_t, gate_t, h2 = _peer_route(x, norm_g, shift, scale, wq_bf, keys_bf, tokens_per_mod)
    idx_flat = jnp.transpose(idx_t).reshape(T * N_SLOTS)
    act = _sc_extract(_peer_scores(h2, u_t), idx_flat).reshape(T, N_SLOTS)
    w = _peer_gate(act, jnp.transpose(gate_t))
    w_dense = _sc_scatter(w.reshape(T * N_SLOTS), idx_flat, T)
    return _peer_combine(w_dense, v_bf, x, gate2, tokens_per_mod)


def kernel(x, c, ctx, c_ctx, w_ada, b_ada, norm1_g, norm2_g, w_in, w_out,
           qn_a, kn_a, qn_b, kn_b, lam_q1, lam_k1, lam_q2, lam_k2, subln_g,
           s5_a_re, s5_a_im, s5_log_dt, s5_b_re, s5_b_im, s5_c_re, s5_c_im,
           s5_d, s5_w_glu, peer_wq, peer_keys, peer_u, peer_v):
    B, L, D = x.shape
    Lc = ctx.shape[1]
    depth = w_in.shape[0]
    rows = L // GRID_W
    row = jnp.repeat(jnp.arange(rows, dtype=jnp.int32), GRID_W)
    col = jnp.tile(jnp.arange(GRID_W, dtype=jnp.int32), rows)
    s_c = jax.nn.silu(c)
    s_cc = jax.nn.silu(c_ctx)
    mods, cmods, s5_ops, wq_bf, keys_bf, u_t, v_bf = [], [], [], [], [], [], []
    for l in range(depth):
        mods.append(jnp.split((s_c @ w_ada[l] + b_ada[l])[:, None, :], 6, axis=-1))
        cmods.append([m.reshape(1, 1, D) for m in jnp.split(s_cc @ w_ada[l] + b_ada[l], 6, axis=-1)])
        s5_ops.append(_s5_operators(s5_a_re[l], s5_a_im[l], s5_log_dt[l], s5_b_re[l], s5_b_im[l],
                                    s5_c_re[l], s5_c_im[l]))
        wq_bf.append(peer_wq[l].astype(jnp.bfloat16))
        keys_bf.append(peer_keys[l].reshape(2 * PEER_HEADS, N_KEYS, PEER_HALF).astype(jnp.bfloat16))
        u_t.append(jnp.transpose(peer_u[l].astype(jnp.bfloat16)))
        v_bf.append(peer_v[l].astype(jnp.bfloat16))

    outs = []
    for b in range(B):
        xb = x[b:b + 1]
        cb = ctx[b:b + 1]
        for l in range(depth):
            with_ctx = l < depth - 1
            sh1, sc1, g1, sh2, sc2, g2 = [m[b:b + 1] for m in mods[l]]
            csh1, csc1, cg1, csh2, csc2, cg2 = cmods[l]
            h = _modulate(_rms(xb, norm1_g[l]), sh1, sc1)
            hc = _modulate(_rms(cb, norm1_g[l]), csh1, csc1)
            mix, mixc = _token_mixers(h, hc, row, col, l, w_in[l], qn_a[l], kn_a[l], qn_b[l], kn_b[l],
                                      lam_q1[l], lam_k1[l], lam_q2[l], lam_k2[l], subln_g[l],
                                      s5_ops[l], s5_d[l], s5_w_glu[l], with_ctx)
            xb = xb + g1 * (mix @ w_out[l])
            xb = _peer_residual(xb.reshape(L, D), norm2_g[l], sh2, sc2, g2, wq_bf[l], keys_bf[l], u_t[l], v_bf[l],
                                tokens_per_mod=L).reshape(1, L, D)
            if with_ctx:
                cb = cb + cg1 * (mixc @ w_out[l])
                cb = _peer_residual(cb.reshape(Lc, D), norm2_g[l], csh2, csc2, cg2, wq_bf[l], keys_bf[l],
                                    u_t[l], v_bf[l], tokens_per_mod=Lc).reshape(1, Lc, D)
        outs.append(xb)
    return jnp.concatenate(outs, axis=0)
```

```python
import functools
import math

import jax
import jax.numpy as jnp
from jax import lax
from jax.experimental import pallas as pl
from jax.experimental.pallas import tpu as pltpu
from jax.experimental.pallas import tpu_sc as plsc

D_MODEL = 1024
GRID_W = 64
EPS = 1e-6
ROPE_THETA = 10000.0
HEAD_DIM = 64
A_HEADS = 6
A_KV_HEADS = 2
A_GROUP = A_HEADS // A_KV_HEADS
A_Q = A_HEADS * HEAD_DIM
A_KV = A_KV_HEADS * HEAD_DIM
B_HEADS = 6
B_SUB = 32
B_VDIM = 2 * B_SUB
B_QK = B_HEADS * 2 * B_SUB
B_V = B_HEADS * B_VDIM
C_GROUPS = 16
C_GROUP_CH = 16
C_WIDTH = C_GROUPS * C_GROUP_CH
C_STATE = 64
PEER_HEADS = 8
N_KEYS = 128
PEER_TOPK = 16
PEER_HALF = 128
TOKEN_CHUNK = 128

LANES = 128
V_EXT = LANES
LOG2E = math.log2(math.e)
NEG_BIG = -1e30
VMEM_LIMIT = 48 * 1024 * 1024


TQ = 1024
ATTN_TK = 2816
K_PACK = LANES


def _flash_unit(s, v_t, m_ref, acc_ref, unit):
    m_old = m_ref[unit]
    m_new = jnp.maximum(m_old, jnp.max(s, axis=0, keepdims=True))
    alpha = jnp.exp2(m_old - m_new)
    p = jnp.exp2(s - m_new).astype(jnp.bfloat16)
    acc_ref[unit] = alpha * acc_ref[unit] + jnp.dot(v_t, p, preferred_element_type=jnp.float32)
    m_ref[unit] = m_new


def _flash_all_units(q_ref, k_ref, vt_ref, m_ref, acc_ref, *, n_units, units_per_group, tk, n_keys):
    m_ref[...] = jnp.full(m_ref.shape, NEG_BIG, jnp.float32)
    acc_ref[...] = jnp.zeros(acc_ref.shape, jnp.float32)

    def body(j, carry):
        off = pl.multiple_of(j * tk, tk)
        k = k_ref[0, pl.ds(off, tk), :]

        def scores(u):
            return jnp.dot(k, q_ref[0, u], preferred_element_type=jnp.float32)

        s = scores(0)
        for u in range(n_units):
            s_next = scores(u + 1) if u + 1 < n_units else None
            _flash_unit(s, vt_ref[0, u // units_per_group, :, pl.ds(off, tk)], m_ref, acc_ref, u)
            s = s_next
        return carry

    lax.fori_loop(0, n_keys // tk, body, 0)


def _gqa_kernel(q_ref, k_ref, vt_ref, o_ref, m_ref, acc_ref, *, tk, n_keys):
    _flash_all_units(q_ref, k_ref, vt_ref, m_ref, acc_ref,
                     n_units=A_HEADS, units_per_group=A_GROUP, tk=tk, n_keys=n_keys)
    for h in range(A_HEADS):
        acc = acc_ref[h]
        o_ref[0, h * HEAD_DIM:(h + 1) * HEAD_DIM, :] = (
            acc[:HEAD_DIM] / acc[HEAD_DIM:HEAD_DIM + 1]).astype(o_ref.dtype)


def _diff_kernel(lam_ref, q_ref, k_ref, vt_ref, g_ref, o_ref, m_ref, acc_ref, *, tk, n_keys, out_scale):
    _flash_all_units(q_ref, k_ref, vt_ref, m_ref, acc_ref,
                     n_units=4, units_per_group=2, tk=tk, n_keys=n_keys)
    lam = lam_ref[0]
    for h in range(2):
        a1 = acc_ref[2 * h]
        a2 = acc_ref[2 * h + 1]
        o = a1[:B_VDIM] / a1[B_VDIM:B_VDIM + 1] - lam * (a2[:B_VDIM] / a2[B_VDIM:B_VDIM + 1])
        o = o * lax.rsqrt(jnp.mean(o * o, axis=0, keepdims=True) + EPS)
        o_ref[0, h * B_VDIM:(h + 1) * B_VDIM, :] = (o * g_ref[...] * out_scale).astype(o_ref.dtype)


def _gqa_attention(q_t, k, v_t, *, tk=ATTN_TK):
    B, _, _, L = q_t.shape
    S = k.shape[1]
    return pl.pallas_call(
        functools.partial(_gqa_kernel, tk=tk, n_keys=S),
        grid=(B, L // TQ),
        in_specs=[
            pl.BlockSpec((1, A_HEADS, K_PACK, TQ), lambda b, i: (b, 0, 0, i)),
            pl.BlockSpec((1, S, K_PACK), lambda b, i: (b, 0, 0)),
            pl.BlockSpec((1, A_KV_HEADS, V_EXT, S), lambda b, i: (b, 0, 0, 0)),
        ],
        out_specs=pl.BlockSpec((1, A_Q, TQ), lambda b, i: (b, 0, i)),
        out_shape=jax.ShapeDtypeStruct((B, A_Q, L), jnp.bfloat16),
        scratch_shapes=[pltpu.VMEM((A_HEADS, 1, TQ), jnp.float32),
                        pltpu.VMEM((A_HEADS, V_EXT, TQ), jnp.float32)],
        compiler_params=pltpu.CompilerParams(
            dimension_semantics=("arbitrary", "arbitrary"), vmem_limit_bytes=VMEM_LIMIT),
        name="gqa_attention",
    )(q_t, k, v_t)


def _diff_attention(q_t, k, v_t, lam, subln_g, out_scale, *, tk=ATTN_TK):
    B, _, _, L = q_t.shape
    S = k.shape[1]
    return pl.pallas_call(
        functools.partial(_diff_kernel, tk=tk, n_keys=S, out_scale=out_scale),
        grid=(B, B_HEADS // 2, L // TQ),
        in_specs=[
            pl.BlockSpec(memory_space=pltpu.SMEM),
            pl.BlockSpec((1, 4, K_PACK, TQ), lambda b, h, i: (b, h, 0, i)),
            pl.BlockSpec((1, S, K_PACK), lambda b, h, i: (b, 0, h)),
            pl.BlockSpec((1, 2, V_EXT, S), lambda b, h, i: (b, h, 0, 0)),
            pl.BlockSpec((B_VDIM, 1), lambda b, h, i: (0, 0)),
        ],
        out_specs=pl.BlockSpec((1, 2 * B_VDIM, TQ), lambda b, h, i: (b, h, i)),
        out_shape=jax.ShapeDtypeStruct((B, B_V, L), jnp.bfloat16),
        scratch_shapes=[pltpu.VMEM((4, 1, TQ), jnp.float32),
                        pltpu.VMEM((4, V_EXT, TQ), jnp.float32)],
        compiler_params=pltpu.CompilerParams(
            dimension_semantics=("arbitrary", "arbitrary", "arbitrary"), vmem_limit_bytes=VMEM_LIMIT),
        name="diff_attention",
    )(lam.reshape(1).astype(jnp.float32), q_t, k, v_t, subln_g.reshape(B_VDIM, 1).astype(jnp.float32))


def _value_ext_t(v):
    B, S, H, d = v.shape
    ones = jnp.ones((B, S, H, 1), v.dtype)
    pad = jnp.zeros((B, S, H, V_EXT - d - 1), v.dtype)
    return jnp.transpose(jnp.concatenate([v, ones, pad], axis=-1), (0, 2, 3, 1)).astype(jnp.bfloat16)


def _query_ext_t(q, offsets):
    B, L, U, d = q.shape
    q_t = jnp.transpose(q, (0, 2, 3, 1)).astype(jnp.bfloat16)
    units = [jnp.pad(q_t[:, u], ((0, 0), (off, K_PACK - d - off), (0, 0))) for u, off in enumerate(offsets)]
    return jnp.stack(units, axis=1)


def _rms(x, g):
    xf = x.astype(jnp.float32)
    y = xf * lax.rsqrt(jnp.mean(xf * xf, axis=-1, keepdims=True) + EPS)
    return (y * g.astype(jnp.float32)).astype(x.dtype)


def _modulate(h, shift, scale):
    return h * (1 + scale) + shift


def _rope_2d(x, row, col):
    d = x.shape[-1]
    half = d // 2
    inv = 1.0 / (ROPE_THETA ** (jnp.arange(0, half, 2, dtype=jnp.float32) / half))
    ang = jnp.concatenate([row.astype(jnp.float32)[:, None] * inv[None, :],
                           col.astype(jnp.float32)[:, None] * inv[None, :]], axis=-1)
    cos = jnp.cos(ang)[None, :, None, :]
    sin = jnp.sin(ang)[None, :, None, :]
    xr = x.astype(jnp.float32).reshape(x.shape[:-1] + (half, 2))
    x0, x1 = xr[..., 0], xr[..., 1]
    out = jnp.stack([x0 * cos - x1 * sin, x0 * sin + x1 * cos], axis=-1)
    return out.reshape(x.shape).astype(x.dtype)


def _gqa_block(q, k, v):
    s = jnp.einsum('bqkgd,bskd->bkgqs', q, k).astype(jnp.float32) * (HEAD_DIM ** -0.5)
    p = jax.nn.softmax(s, axis=-1).astype(v.dtype)
    return jnp.einsum('bkgqs,bskd->bqkgd', p, v)


def _diff_block(q, k, v, lam):
    s = jnp.einsum('bqhtd,bshtd->bthqs', q, k).astype(jnp.float32) * (B_SUB ** -0.5)
    p = jax.nn.softmax(s, axis=-1)
    a = (p[:, 0] - lam * p[:, 1]).astype(v.dtype)
    return jnp.einsum('bhqs,bshd->bqhd', a, v)


S5_TC = 64
S5_ROW_TILE = 16
S5_CW = S5_TC * C_GROUP_CH
S5_SW = 2 * C_STATE


def _s5_operators(a_re, a_im, log_dt, b_re, b_im, c_re, c_im):
    f32 = jnp.float32
    tc, P, N, G = S5_TC, C_GROUP_CH, C_STATE, C_GROUPS
    j = jnp.arange(tc + 1, dtype=f32)
    kerns, p_parts, q_parts, a_parts = [], [], [], []
    for dirn in range(2):
        A = lax.complex(a_re[dirn].astype(f32), a_im[dirn].astype(f32))
        adt = A * jnp.exp(log_dt[dirn].astype(f32))[:, None]
        abar = jnp.exp(adt)
        bbar = ((abar - 1) / A)[..., None] * lax.complex(b_re[dirn].astype(f32), b_im[dirn].astype(f32))
        cmat = lax.complex(c_re[dirn].astype(f32), c_im[dirn].astype(f32))
        pw = jnp.exp(adt[None] * j[:, None, None])
        kerns.append(jnp.einsum('gpn,jgn,gnq->gqjp', cmat, pw[:tc], bbar).real)
        inj_pw = pw[:tc] if dirn == 1 else pw[:tc][::-1]
        inj = inj_pw[:, :, :, None] * bbar[None]
        p_parts.append(jnp.transpose(inj, (1, 0, 3, 2)).reshape(G, S5_CW, N))
        out_pw = pw[1:][::-1] if dirn == 1 else pw[1:]
        outm = cmat[None] * out_pw[:, :, None, :]
        q_parts.append(jnp.transpose(outm, (1, 3, 0, 2)).reshape(G, N, S5_CW))
        a_parts.append(pw[tc])
    f_seq = jnp.concatenate([jnp.flip(kerns[1][:, :, 1:], axis=2), kerns[0][:, :, :1] + kerns[1][:, :, :1],
                             kerns[0][:, :, 1:], jnp.zeros((G, P, 1, P), f32)], axis=2)
    rolled = jnp.tile(f_seq.astype(jnp.bfloat16).reshape(G, P, 2 * tc * P), (1, 1, tc))
    rolled = rolled[:, :, :tc * (2 * tc - 1) * P].reshape(G, P, tc, 2 * tc - 1, P)
    toep = jnp.transpose(rolled[:, :, :, tc - 1:, :], (0, 2, 1, 3, 4)).reshape(G, S5_CW, S5_CW)
    zeros = jnp.zeros((G, N, S5_CW), jnp.bfloat16)
    bf = lambda m: m.astype(jnp.bfloat16)
    w_mat = jnp.concatenate([
        toep,
        bf(q_parts[0].real), zeros, bf(-q_parts[0].imag), zeros,
        zeros, bf(q_parts[1].real), zeros, bf(-q_parts[1].imag),
    ], axis=1)
    p_mat = jnp.concatenate([p_parts[0].real, p_parts[1].real, p_parts[0].imag, p_parts[1].imag], axis=-1)
    a_chunk = jnp.stack([jnp.concatenate([a_parts[0].real, a_parts[1].real], axis=-1),
                         jnp.concatenate([a_parts[0].imag, a_parts[1].imag], axis=-1)], axis=1)
    return p_mat.astype(jnp.bfloat16), w_mat, a_chunk.astype(f32)


def _s5_scan_kernel(u_ref, p_ref, w_ref, a_ref, y_ref, s_scr, hs_scr, *, rows, order_f, order_r):
    f32 = jnp.float32
    u = u_ref[0]
    s_scr[...] = jnp.dot(u, p_ref[0], preferred_element_type=f32)
    hs_scr[...] = jnp.zeros(hs_scr.shape, f32)
    a_re = a_ref[0, 0:1, :]
    a_im = a_ref[0, 1:2, :]
    is_fwd = lax.broadcasted_iota(jnp.int32, (rows, S5_SW), 1) < C_STATE
    h_re = jnp.zeros((rows, S5_SW), f32)
    h_im = jnp.zeros((rows, S5_SW), f32)
    for cf, cr in zip(order_f, order_r):
        rf = slice(cf * rows, (cf + 1) * rows)
        rr = slice(cr * rows, (cr + 1) * rows)
        hs_scr[rf, 0:S5_SW] = h_re
        hs_scr[rf, S5_SW:2 * S5_SW] = h_im
        hs_scr[rr, 2 * S5_SW:3 * S5_SW] = h_re
        hs_scr[rr, 3 * S5_SW:4 * S5_SW] = h_im
        s_re = jnp.where(is_fwd, s_scr[rf, 0:S5_SW], s_scr[rr, 0:S5_SW])
        s_im = jnp.where(is_fwd, s_scr[rf, S5_SW:2 * S5_SW], s_scr[rr, S5_SW:2 * S5_SW])
        h_re, h_im = a_re * h_re - a_im * h_im + s_re, a_re * h_im + a_im * h_re + s_im
    y = jnp.dot(u, w_ref[0, 0:S5_CW, :], preferred_element_type=f32)
    y += jnp.dot(hs_scr[...].astype(jnp.bfloat16), w_ref[0, S5_CW:, :], preferred_element_type=f32)
    y_ref[0] = y


def _s5_scan(u_seq, n_ctx_chunks, p_mat, w_mat, a_chunk):
    B, S, _ = u_seq.shape
    G, P = C_GROUPS, C_GROUP_CH
    nc = S // S5_TC
    R = -(-(nc * B) // S5_ROW_TILE) * S5_ROW_TILE
    ug = jnp.transpose(u_seq.reshape(B, nc, S5_TC, G, P), (3, 1, 0, 2, 4))
    ug = jnp.pad(ug.reshape(G, nc * B, S5_CW), ((0, 0), (0, R - nc * B), (0, 0))).astype(jnp.bfloat16)
    order_f = tuple(range(nc))
    order_r = tuple(range(n_ctx_chunks - 1, -1, -1)) + tuple(range(nc - 1, n_ctx_chunks - 1, -1))
    y = pl.pallas_call(
        functools.partial(_s5_scan_kernel, rows=B, order_f=order_f, order_r=order_r),
        grid=(G,),
        in_specs=[
            pl.BlockSpec((1, R, S5_CW), lambda g: (g, 0, 0)),
            pl.BlockSpec((1, S5_CW, 2 * S5_SW), lambda g: (g, 0, 0)),
            pl.BlockSpec((1, S5_CW + 4 * S5_SW, S5_CW), lambda g: (g, 0, 0)),
            pl.BlockSpec((1, 2, S5_SW), lambda g: (g, 0, 0)),
        ],
        out_specs=pl.BlockSpec((1, R, S5_CW), lambda g: (g, 0, 0)),
        out_shape=jax.ShapeDtypeStruct((G, R, S5_CW), jnp.float32),
        scratch_shapes=[pltpu.VMEM((R, 2 * S5_SW), jnp.float32),
                        pltpu.VMEM((R, 4 * S5_SW), jnp.float32)],
        compiler_params=pltpu.CompilerParams(dimension_semantics=("arbitrary",),
                                             vmem_limit_bytes=VMEM_LIMIT),
        name="s5_scan",
    )(ug, p_mat, w_mat, a_chunk)
    y = y[:, :nc * B].reshape(G, nc, B, S5_TC, P)
    return jnp.transpose(y, (2, 1, 3, 0, 4)).reshape(B, S, C_WIDTH)


def _s5_glu_kernel(y_ref, u_ref, d_ref, w_ref, o_ref):
    y = y_ref[...] + d_ref[...] * u_ref[...]
    g = jax.nn.gelu(y)
    z = jnp.dot(g.astype(jnp.bfloat16), w_ref[...], preferred_element_type=jnp.float32)
    o_ref[...] = (g * jax.nn.sigmoid(z)).astype(o_ref.dtype)


def _s5_glu(y, u, d_skip, w_glu):
    T = y.shape[0]
    tb = math.gcd(T, 1024)
    return pl.pallas_call(
        _s5_glu_kernel,
        grid=(T // tb,),
        in_specs=[pl.BlockSpec((tb, C_WIDTH), lambda i: (i, 0)),
                  pl.BlockSpec((tb, C_WIDTH), lambda i: (i, 0)),
                  pl.BlockSpec((1, C_WIDTH), lambda i: (0, 0)),
                  pl.BlockSpec((C_WIDTH, C_WIDTH), lambda i: (0, 0))],
        out_specs=pl.BlockSpec((tb, C_WIDTH), lambda i: (i, 0)),
        out_shape=jax.ShapeDtypeStruct((T, C_WIDTH), jnp.float32),
        compiler_params=pltpu.CompilerParams(dimension_semantics=("arbitrary",)),
        name="s5_glu",
    )(y, u, d_skip.reshape(1, C_WIDTH).astype(jnp.float32), w_glu.astype(jnp.bfloat16))


def _s5_mixer(u, uc, s5_ops, d_skip, w_glu, with_ctx):
    B, L, _ = u.shape
    Lc = uc.shape[1]
    p_mat, w_mat, a_chunk = s5_ops
    u_seq = jnp.concatenate([uc, u], axis=1).astype(jnp.float32)
    y = _s5_scan(u_seq, Lc // S5_TC, p_mat, w_mat, a_chunk)
    if with_ctx:
        out = _s5_glu(y.reshape(B * (Lc + L), C_WIDTH), u_seq.reshape(B * (Lc + L), C_WIDTH), d_skip, w_glu)
        out = out.reshape(B, Lc + L, C_WIDTH)
        return out[:, Lc:], out[:, :Lc]
    out = _s5_glu(y[:, Lc:].reshape(B * L, C_WIDTH), u.astype(jnp.float32).reshape(B * L, C_WIDTH), d_skip, w_glu)
    return out.reshape(B, L, C_WIDTH), None


def _token_mixers(h, hc, row, col, layer_idx, w_in, qn_a, kn_a, qn_b, kn_b,
                  lam_q1, lam_k1, lam_q2, lam_k2, subln_g, s5_ops, d_skip, w_glu, with_ctx):
    B, L, _ = h.shape
    Lc = hc.shape[1]
    cuts = [A_Q, A_Q + A_KV, A_Q + 2 * A_KV, A_Q + 2 * A_KV + B_QK,
            A_Q + 2 * A_KV + 2 * B_QK, A_Q + 2 * A_KV + 2 * B_QK + B_V]
    qa, ka, va, qb, kb, vb, us = jnp.split(h @ w_in, cuts, axis=-1)
    qac, kac, vac, qbc, kbc, vbc, usc = jnp.split(hc @ w_in, cuts, axis=-1)

    qa = _rope_2d(_rms(qa.reshape(B, L, A_HEADS, HEAD_DIM), qn_a), row, col)
    ka = _rope_2d(_rms(ka.reshape(B, L, A_KV_HEADS, HEAD_DIM), kn_a), row, col)
    va = va.reshape(B, L, A_KV_HEADS, HEAD_DIM)
    kac = _rms(kac.reshape(B, Lc, A_KV_HEADS, HEAD_DIM), kn_a)
    vac = vac.reshape(B, Lc, A_KV_HEADS, HEAD_DIM)
    ka_all = jnp.concatenate([kac, ka], axis=1)
    va_all = jnp.concatenate([vac, va], axis=1)
    qa_t = _query_ext_t(qa * (HEAD_DIM ** -0.5 * LOG2E), [(hq // A_GROUP) * HEAD_DIM for hq in range(A_HEADS)])
    ka_pack = ka_all.reshape(B, Lc + L, A_KV).astype(jnp.bfloat16)
    o_a = jnp.transpose(_gqa_attention(qa_t, ka_pack, _value_ext_t(va_all)), (0, 2, 1))

    lambda_init = 0.8 - 0.6 * math.exp(-0.3 * layer_idx)
    lam = (jnp.exp(jnp.sum(lam_q1.astype(jnp.float32) * lam_k1.astype(jnp.float32)))
           - jnp.exp(jnp.sum(lam_q2.astype(jnp.float32) * lam_k2.astype(jnp.float32))) + lambda_init)
    qb = _rope_2d(_rms(qb.reshape(B, L, 2 * B_HEADS, B_SUB), qn_b), row, col)
    kb = _rope_2d(_rms(kb.reshape(B, L, 2 * B_HEADS, B_SUB), kn_b), row, col)
    vb = vb.reshape(B, L, B_HEADS, B_VDIM)
    kbc = _rms(kbc.reshape(B, Lc, 2 * B_HEADS, B_SUB), kn_b)
    vbc = vbc.reshape(B, Lc, B_HEADS, B_VDIM)
    kb_all = jnp.concatenate([kbc, kb], axis=1)
    vb_all = jnp.concatenate([vbc, vb], axis=1)
    qb_t = _query_ext_t(qb * (B_SUB ** -0.5 * LOG2E), [(j % 4) * B_SUB for j in range(2 * B_HEADS)])
    kb_pack = kb_all.reshape(B, Lc + L, B_QK).astype(jnp.bfloat16)
    o_b = jnp.transpose(_diff_attention(qb_t, kb_pack, _value_ext_t(vb_all), lam, subln_g, 1 - lambda_init),
                        (0, 2, 1))

    o_c, o_cc = _s5_mixer(us, usc, s5_ops, d_skip, w_glu, with_ctx)

    mix = jnp.concatenate([o_a.astype(h.dtype), o_b.astype(h.dtype), o_c], axis=-1)
    if not with_ctx:
        return mix, None
    qac = _rms(qac.reshape(B, Lc, A_HEADS, HEAD_DIM), qn_a).reshape(B, Lc, A_KV_HEADS, A_GROUP, HEAD_DIM)
    o_ac = _gqa_block(qac, kac, vac).reshape(B, Lc, A_Q)
    kbc5 = kbc.reshape(B, Lc, B_HEADS, 2, B_SUB)
    qbc = _rms(qbc.reshape(B, Lc, 2 * B_HEADS, B_SUB), qn_b).reshape(B, Lc, B_HEADS, 2, B_SUB)
    o_bc = _diff_block(qbc, kbc5, vbc, lam)
    o_bc = (_rms(o_bc, subln_g) * (1 - lambda_init)).reshape(B, Lc, B_V)
    mixc = jnp.concatenate([o_ac, o_bc, o_cc], axis=-1)
    return mix, mixc


PEER_TB = 256
N_SLOTS = PEER_HEADS * PEER_TOPK
CAND_ROWS = 80
INVALID_FLAT = 1.0e9
SC_CORES = 2
SC_SUBCORES = 16


def _cand_flat_table():
    f = [float(b) for b in range(16)]
    for a in range(1, 8):
        f += [float(a * 16 + b) if (a + 1) * (b + 1) <= 16 else INVALID_FLAT for b in range(8)]
    f += [float(a * 16) for a in range(8, 16)]
    return jnp.broadcast_to(jnp.asarray(f, jnp.float32)[:, None], (CAND_ROWS, PEER_TB))


def _pair_rows(first, second):
    blocks = [first[0:1] + second]
    blocks += [first[a:a + 1] + second[0:8] for a in range(1, 8)]
    blocks += [first[8:16] + second[0:1]]
    return jnp.concatenate(blocks, axis=0)


def _peer_route_kernel(x_ref, g_ref, sh_ref, sc_ref, wq_ref, keys_ref, ftab_ref,
                       idx_ref, gate_ref, h_ref, val_scr, id_scr, sc_scr, e_scr):
    f32 = jnp.float32
    x = x_ref[...]
    y = x * lax.rsqrt(jnp.mean(x * x, axis=-1, keepdims=True) + EPS) * g_ref[...]
    h2 = (y * (1.0 + sc_ref[0]) + sh_ref[0]).astype(jnp.bfloat16)
    h_ref[...] = h2
    q = jnp.dot(h2, wq_ref[...], preferred_element_type=f32)
    ftab = ftab_ref[...]
    key_id = lax.broadcasted_iota(jnp.int32, (N_KEYS, PEER_TB), 0).astype(f32)
    neg_inf = -jnp.inf
    for head in range(PEER_HEADS):
        for t in range(2):
            ht = head * 2 + t
            qh = q[:, ht * PEER_HALF:(ht + 1) * PEER_HALF].astype(jnp.bfloat16)
            s = lax.dot_general(keys_ref[ht], qh, (((1,), (1,)), ((), ())),
                                preferred_element_type=f32)
            for r in range(PEER_TOPK):
                m = jnp.max(s, axis=0, keepdims=True)
                pick = jnp.min(jnp.where(s == m, key_id, float(N_KEYS)), axis=0, keepdims=True)
                s = jnp.where(key_id == pick, neg_inf, s)
                val_scr[t, r:r + 1, :] = m
                id_scr[t, r:r + 1, :] = pick
        cand = _pair_rows(val_scr[0], val_scr[1])
        cand = jnp.where(ftab < float(PEER_TOPK * PEER_TOPK), cand, neg_inf)
        expert = _pair_rows(id_scr[0] * float(N_KEYS), id_scr[1])
        for r in range(PEER_TOPK):
            m = jnp.max(cand, axis=0, keepdims=True)
            pick = jnp.min(jnp.where(cand == m, ftab, INVALID_FLAT), axis=0, keepdims=True)
            hit = ftab == pick
            sc_scr[r:r + 1, :] = m
            e_scr[r:r + 1, :] = jnp.sum(jnp.where(hit, expert, 0.0), axis=0, keepdims=True)
            cand = jnp.where(hit, neg_inf, cand)
        sc = sc_scr[...]
        p = jnp.exp(sc - sc[0:1])
        gate = p / jnp.sum(p, axis=0, keepdims=True)
        idx_ref[head * PEER_TOPK:(head + 1) * PEER_TOPK, :] = e_scr[...].astype(jnp.int32)
        gate_ref[head * PEER_TOPK:(head + 1) * PEER_TOPK, :] = gate


def _peer_route(x, norm_g, shift, scale, wq_bf, keys_bf, tokens_per_mod):
    T, D = x.shape
    nq = wq_bf.shape[1]
    blocks_per_mod = tokens_per_mod // PEER_TB
    return pl.pallas_call(
        _peer_route_kernel,
        grid=(T // PEER_TB,),
        in_specs=[
            pl.BlockSpec((PEER_TB, D), lambda i: (i, 0)),
            pl.BlockSpec((1, D), lambda i: (0, 0)),
            pl.BlockSpec((1, 1, D), lambda i: (i // blocks_per_mod, 0, 0)),
            pl.BlockSpec((1, 1, D), lambda i: (i // blocks_per_mod, 0, 0)),
            pl.BlockSpec((D, nq), lambda i: (0, 0)),
            pl.BlockSpec((2 * PEER_HEADS, N_KEYS, PEER_HALF), lambda i: (0, 0, 0)),
            pl.BlockSpec((CAND_ROWS, PEER_TB), lambda i: (0, 0)),
        ],
        out_specs=[
            pl.BlockSpec((N_SLOTS, PEER_TB), lambda i: (0, i)),
            pl.BlockSpec((N_SLOTS, PEER_TB), lambda i: (0, i)),
            pl.BlockSpec((PEER_TB, D), lambda i: (i, 0)),
        ],
        out_shape=[
            jax.ShapeDtypeStruct((N_SLOTS, T), jnp.int32),
            jax.ShapeDtypeStruct((N_SLOTS, T), jnp.float32),
            jax.ShapeDtypeStruct((T, D), jnp.bfloat16),
        ],
        scratch_shapes=[pltpu.VMEM((2, PEER_TOPK, PEER_TB), jnp.float32),
                        pltpu.VMEM((2, PEER_TOPK, PEER_TB), jnp.float32),
                        pltpu.VMEM((PEER_TOPK, PEER_TB), jnp.float32),
                        pltpu.VMEM((PEER_TOPK, PEER_TB), jnp.float32)],
        compiler_params=pltpu.CompilerParams(dimension_semantics=("arbitrary",),
                                             vmem_limit_bytes=VMEM_LIMIT),
        name="peer_route",
    )(x, norm_g.reshape(1, D).astype(jnp.float32), shift, scale, wq_bf, keys_bf, _cand_flat_table())


N_EXPERTS = N_KEYS * N_KEYS
SC_LANES = 16
SC_TOKENS = 32
DENSE_TM = 512
DENSE_TE = 2048


def _peer_scores_kernel(h_ref, ut_ref, s_ref):
    s_ref[...] = jnp.dot(h_ref[...], ut_ref[...], preferred_element_type=jnp.float32)


def _peer_scores(h2, u_t):
    T, D = h2.shape
    tm = min(DENSE_TM, T)
    return pl.pallas_call(
        _peer_scores_kernel,
        grid=(N_EXPERTS // DENSE_TE, T // tm),
        in_specs=[pl.BlockSpec((tm, D), lambda j, i: (i, 0)),
                  pl.BlockSpec((D, DENSE_TE), lambda j, i: (0, j))],
        out_specs=pl.BlockSpec((tm, DENSE_TE), lambda j, i: (i, j)),
        out_shape=jax.ShapeDtypeStruct((T, N_EXPERTS), jnp.float32),
        compiler_params=pltpu.CompilerParams(dimension_semantics=("arbitrary", "arbitrary"),
                                             vmem_limit_bytes=VMEM_LIMIT),
        name="peer_scores",
    )(h2, u_t)


def _sc_worker_tokens(n_tokens):
    per_worker = n_tokens // (SC_CORES * SC_SUBCORES)
    blk = min(SC_TOKENS, per_worker)
    return per_worker, blk


def _sc_extract(scores, idx_flat):
    T = scores.shape[0]
    per_worker, blk = _sc_worker_tokens(T)
    mesh = plsc.VectorSubcoreMesh(core_axis_name="c", subcore_axis_name="s")

    def body(s_hbm, idx_hbm, out_hbm, row0_v, row1_v, idx_v, out_v, rsem):
        rows = (row0_v, row1_v)
        wid = lax.axis_index("s") * SC_CORES + lax.axis_index("c")
        t0 = wid * per_worker

        def row_copy(tok, slot):
            return pltpu.make_async_copy(s_hbm.at[tok], rows[slot], rsem.at[slot])

        def block(b, carry):
            tb = t0 + b * blk
            pltpu.sync_copy(idx_hbm.at[pl.ds(tb * N_SLOTS, blk * N_SLOTS)], idx_v)
            row_copy(tb, 0).start()

            def pair(i2, c):
                for slot in range(2):
                    i = i2 * 2 + slot

                    @pl.when(i + 1 < blk)
                    def _():
                        row_copy(tb + i + 1, 1 - slot).start()

                    row_copy(tb + i, slot).wait()
                    for j in range(N_SLOTS // SC_LANES):
                        at = pl.ds(i * N_SLOTS + j * SC_LANES, SC_LANES)
                        out_v[at] = plsc.load_gather(rows[slot], [idx_v[at]])
                return c

            lax.fori_loop(0, blk // 2, pair, 0)
            pltpu.sync_copy(out_v, out_hbm.at[pl.ds(tb * N_SLOTS, blk * N_SLOTS)])
            return carry

        lax.fori_loop(0, per_worker // blk, block, 0)

    return pl.kernel(
        body, mesh=mesh,
        out_type=jax.ShapeDtypeStruct((T * N_SLOTS,), jnp.float32),
        scratch_types=[pltpu.VMEM((N_EXPERTS,), jnp.float32), pltpu.VMEM((N_EXPERTS,), jnp.float32),
                       pltpu.VMEM((blk * N_SLOTS,), jnp.int32), pltpu.VMEM((blk * N_SLOTS,), jnp.float32),
                       pltpu.SemaphoreType.DMA((2,))],
        compiler_params=pltpu.CompilerParams(needs_layout_passes=False),
        name="peer_extract",
    )(scores, idx_flat)


def _sc_scatter(w_flat, idx_flat, n_tokens):
    T = n_tokens
    per_worker, blk = _sc_worker_tokens(T)
    n_vec = N_SLOTS // SC_LANES
    mesh = plsc.VectorSubcoreMesh(core_axis_name="c", subcore_axis_name="s")

    def body(w_hbm, idx_hbm, out_hbm, row0_v, row1_v, idx_v, w_v, wsem):
        rows = (row0_v, row1_v)
        wid = lax.axis_index("s") * SC_CORES + lax.axis_index("c")
        t0 = wid * per_worker
        zero = jnp.zeros((SC_LANES,), jnp.float32)

        def zero_fill(k, c):
            for slot in range(2):
                rows[slot][pl.ds(k * SC_LANES, SC_LANES)] = zero
            return c

        lax.fori_loop(0, N_EXPERTS // SC_LANES, zero_fill, 0)

        def out_copy(tok, slot):
            return pltpu.make_async_copy(rows[slot], out_hbm.at[tok], wsem.at[slot])

        def retire(tok, i, slot):
            out_copy(tok, slot).wait()
            for j in range(n_vec):
                plsc.store_scatter(rows[slot], [idx_v[pl.ds(i * N_SLOTS + j * SC_LANES, SC_LANES)]], zero)

        def block(b, carry):
            tb = t0 + b * blk
            pltpu.sync_copy(idx_hbm.at[pl.ds(tb * N_SLOTS, blk * N_SLOTS)], idx_v)
            pltpu.sync_copy(w_hbm.at[pl.ds(tb * N_SLOTS, blk * N_SLOTS)], w_v)

            def pair(i2, c):
                for slot in range(2):
                    i = i2 * 2 + slot

                    @pl.when(i >= 2)
                    def _():
                        retire(tb + i - 2, i - 2, slot)

                    for j in range(n_vec):
                        at = pl.ds(i * N_SLOTS + j * SC_LANES, SC_LANES)
                        plsc.addupdate_scatter(rows[slot], [idx_v[at]], w_v[at])
                    out_copy(tb + i, slot).start()
                return c

            lax.fori_loop(0, blk // 2, pair, 0)
            for slot in range(2):
                retire(tb + blk - 2 + slot, blk - 2 + slot, slot)
            return carry

        lax.fori_loop(0, per_worker // blk, block, 0)

    return pl.kernel(
        body, mesh=mesh,
        out_type=jax.ShapeDtypeStruct((T, N_EXPERTS), jnp.float32),
        scratch_types=[pltpu.VMEM((N_EXPERTS,), jnp.float32), pltpu.VMEM((N_EXPERTS,), jnp.float32),
                       pltpu.VMEM((blk * N_SLOTS,), jnp.int32), pltpu.VMEM((blk * N_SLOTS,), jnp.float32),
                       pltpu.SemaphoreType.DMA((2,))],
        compiler_params=pltpu.CompilerParams(needs_layout_passes=False),
        name="peer_scatter",
    )(w_flat, idx_flat)


def _peer_gate_kernel(a_ref, g_ref, w_ref):
    w_ref[...] = jax.nn.gelu(a_ref[...]) * g_ref[...]


def _peer_gate(act, gate):
    T = act.shape[0]
    tb = min(2048, T)
    spec = pl.BlockSpec((tb, N_SLOTS), lambda i: (i, 0))
    return pl.pallas_call(
        _peer_gate_kernel, grid=(T // tb,), in_specs=[spec, spec], out_specs=spec,
        out_shape=jax.ShapeDtypeStruct((T, N_SLOTS), jnp.float32),
        compiler_params=pltpu.CompilerParams(dimension_semantics=("arbitrary",)),
        name="peer_gate",
    )(act, gate)


def _peer_combine_kernel(w_ref, v_ref, x_ref, g2_ref, o_ref, acc_ref):
    k = pl.program_id(1)

    @pl.when(k == 0)
    def _():
        acc_ref[...] = jnp.zeros(acc_ref.shape, jnp.float32)

    acc_ref[...] += jnp.dot(w_ref[...].astype(jnp.bfloat16), v_ref[...], preferred_element_type=jnp.float32)

    @pl.when(k == pl.num_programs(1) - 1)
    def _():
        o_ref[...] = x_ref[...] + g2_ref[0] * acc_ref[...]


def _peer_combine(w_dense, v_bf, x, g2, tokens_per_mod):
    T, D = x.shape
    tm = min(DENSE_TM, T)
    blocks_per_mod = tokens_per_mod // tm
    return pl.pallas_call(
        _peer_combine_kernel,
        grid=(T // tm, N_EXPERTS // DENSE_TE),
        in_specs=[pl.BlockSpec((tm, DENSE_TE), lambda i, k: (i, k)),
                  pl.BlockSpec((DENSE_TE, D), lambda i, k: (k, 0)),
                  pl.BlockSpec((tm, D), lambda i, k: (i, 0)),
                  pl.BlockSpec((1, 1, D), lambda i, k: (i // blocks_per_mod, 0, 0))],
        out_specs=pl.BlockSpec((tm, D), lambda i, k: (i, 0)),
        out_shape=jax.ShapeDtypeStruct((T, D), jnp.float32),
        scratch_shapes=[pltpu.VMEM((tm, D), jnp.float32)],
        compiler_params=pltpu.CompilerParams(dimension_semantics=("arbitrary", "arbitrary"),
                                             vmem_limit_bytes=VMEM_LIMIT),
        name="peer_combine",
    )(w_dense, v_bf, x, g2)


def _peer_residual(x, norm_g, shift, scale, gate2, wq_bf, keys_bf, u_t, v_bf, tokens_per_mod):
    T, D = x.shape
    idx_t, gate_t, h2 = _peer_route(x, norm_g, shift, scale, wq_bf, keys_bf, tokens_per_mod)
    idx_flat = jnp.transpose(idx_t).reshape(T * N_SLOTS)
    act = _sc_extract(_peer_scores(h2, u_t), idx_flat).reshape(T, N_SLOTS)
    w = _peer_gate(act, jnp.transpose(gate_t))
    w_dense = _sc_scatter(w.reshape(T * N_SLOTS), idx_flat, T)
    return _peer_combine(w_dense, v_bf, x, gate2, tokens_per_mod)


def kernel(x, c, ctx, c_ctx, w_ada, b_ada, norm1_g, norm2_g, w_in, w_out,
           qn_a, kn_a, qn_b, kn_b, lam_q1, lam_k1, lam_q2, lam_k2, subln_g,
           s5_a_re, s5_a_im, s5_log_dt, s5_b_re, s5_b_im, s5_c_re, s5_c_im,
           s5_d, s5_w_glu, peer_wq, peer_keys, peer_u, peer_v):
    B, L, D = x.shape
    Lc = ctx.shape[1]
    depth = w_in.shape[0]
    rows = L // GRID_W
    row = jnp.repeat(jnp.arange(rows, dtype=jnp.int32), GRID_W)
    col = jnp.tile(jnp.arange(GRID_W, dtype=jnp.int32), rows)
    s_c = jax.nn.silu(c)
    s_cc = jax.nn.silu(c_ctx)
    mods, cmods, s5_ops, wq_bf, keys_bf, u_t, v_bf = [], [], [], [], [], [], []
    for l in range(depth):
        mods.append(jnp.split((s_c @ w_ada[l] + b_ada[l])[:, None, :], 6, axis=-1))
        cmods.append([m.reshape(1, 1, D) for m in jnp.split(s_cc @ w_ada[l] + b_ada[l], 6, axis=-1)])
        s5_ops.append(_s5_operators(s5_a_re[l], s5_a_im[l], s5_log_dt[l], s5_b_re[l], s5_b_im[l],
                                    s5_c_re[l], s5_c_im[l]))
        wq_bf.append(peer_wq[l].astype(jnp.bfloat16))
        keys_bf.append(peer_keys[l].reshape(2 * PEER_HEADS, N_KEYS, PEER_HALF).astype(jnp.bfloat16))
        u_t.append(jnp.transpose(peer_u[l].astype(jnp.bfloat16)))
        v_bf.append(peer_v[l].astype(jnp.bfloat16))

    outs = []
    for b in range(B):
        xb = x[b:b + 1]
        cb = ctx[b:b + 1]
        for l in range(depth):
            with_ctx = l < depth - 1
            sh1, sc1, g1, sh2, sc2, g2 = [m[b:b + 1] for m in mods[l]]
            csh1, csc1, cg1, csh2, csc2, cg2 = cmods[l]
            h = _modulate(_rms(xb, norm1_g[l]), sh1, sc1)
            hc = _modulate(_rms(cb, norm1_g[l]), csh1, csc1)
            mix, mixc = _token_mixers(h, hc, row, col, l, w_in[l], qn_a[l], kn_a[l], qn_b[l], kn_b[l],
                                      lam_q1[l], lam_k1[l], lam_q2[l], lam_k2[l], subln_g[l],
                                      s5_ops[l], s5_d[l], s5_w_glu[l], with_ctx)
            xb = xb + g1 * (mix @ w_out[l])
            xb = _peer_residual(xb.reshape(L, D), norm2_g[l], sh2, sc2, g2, wq_bf[l], keys_bf[l], u_t[l], v_bf[l],
                                tokens_per_mod=L).reshape(1, L, D)
            if with_ctx:
                cb = cb + cg1 * (mixc @ w_out[l])
                cb = _peer_residual(cb.reshape(Lc, D), norm2_g[l], csh2, csc2, cg2, wq_bf[l], keys_bf[l],
                                    u_t[l], v_bf[l], tokens_per_mod=Lc).reshape(1, Lc, D)
        outs.append(xb)
    return jnp.concatenate(outs, axis=0)
```

```python
import functools
import math

import jax
import jax.numpy as jnp
from jax import lax
from jax.experimental import pallas as pl
from jax.experimental.pallas import tpu as pltpu
from jax.experimental.pallas import tpu_sc as plsc

D_MODEL = 1024
GRID_W = 64
EPS = 1e-6
ROPE_THETA = 10000.0
HEAD_DIM = 64
A_HEADS = 6
A_KV_HEADS = 2
A_GROUP = A_HEADS // A_KV_HEADS
A_Q = A_HEADS * HEAD_DIM
A_KV = A_KV_HEADS * HEAD_DIM
B_HEADS = 6
B_SUB = 32
B_VDIM = 2 * B_SUB
B_QK = B_HEADS * 2 * B_SUB
B_V = B_HEADS * B_VDIM
C_GROUPS = 16
C_GROUP_CH = 16
C_WIDTH = C_GROUPS * C_GROUP_CH
C_STATE = 64
PEER_HEADS = 8
N_KEYS = 128
PEER_TOPK = 16
PEER_HALF = 128
TOKEN_CHUNK = 128

LANES = 128
V_EXT = LANES
LOG2E = math.log2(math.e)
NEG_BIG = -1e30
VMEM_LIMIT = 48 * 1024 * 1024


TQ = 1024
ATTN_TK = 2816
K_PACK = LANES


def _flash_unit(s, v_t, m_ref, acc_ref, unit):
    m_old = m_ref[unit]
    m_new = jnp.maximum(m_old, jnp.max(s, axis=0, keepdims=True))
    alpha = jnp.exp2(m_old - m_new)
    p = jnp.exp2(s - m_new).astype(jnp.bfloat16)
    acc_ref[unit] = alpha * acc_ref[unit] + jnp.dot(v_t, p, preferred_element_type=jnp.float32)
    m_ref[unit] = m_new


def _flash_all_units(q_ref, k_ref, vt_ref, m_ref, acc_ref, *, n_units, units_per_group, tk, n_keys):
    m_ref[...] = jnp.full(m_ref.shape, NEG_BIG, jnp.float32)
    acc_ref[...] = jnp.zeros(acc_ref.shape, jnp.float32)

    def body(j, carry):
        off = pl.multiple_of(j * tk, tk)
        k = k_ref[0, pl.ds(off, tk), :]

        def scores(u):
            return jnp.dot(k, q_ref[0, u], preferred_element_type=jnp.float32)

        s = scores(0)
        for u in range(n_units):
            s_next = scores(u + 1) if u + 1 < n_units else None
            _flash_unit(s, vt_ref[0, u // units_per_group, :, pl.ds(off, tk)], m_ref, acc_ref, u)
            s = s_next
        return carry

    lax.fori_loop(0, n_keys // tk, body, 0)


def _gqa_kernel(q_ref, k_ref, vt_ref, o_ref, m_ref, acc_ref, *, tk, n_keys):
    _flash_all_units(q_ref, k_ref, vt_ref, m_ref, acc_ref,
                     n_units=A_HEADS, units_per_group=A_GROUP, tk=tk, n_keys=n_keys)
    for h in range(A_HEADS):
        acc = acc_ref[h]
        o_ref[0, h * HEAD_DIM:(h + 1) * HEAD_DIM, :] = (
            acc[:HEAD_DIM] / acc[HEAD_DIM:HEAD_DIM + 1]).astype(o_ref.dtype)


def _diff_kernel(lam_ref, q_ref, k_ref, vt_ref, g_ref, o_ref, m_ref, acc_ref, *, tk, n_keys, out_scale):
    _flash_all_units(q_ref, k_ref, vt_ref, m_ref, acc_ref,
                     n_units=4, units_per_group=2, tk=tk, n_keys=n_keys)
    lam = lam_ref[0]
    for h in range(2):
        a1 = acc_ref[2 * h]
        a2 = acc_ref[2 * h + 1]
        o = a1[:B_VDIM] / a1[B_VDIM:B_VDIM + 1] - lam * (a2[:B_VDIM] / a2[B_VDIM:B_VDIM + 1])
        o = o * lax.rsqrt(jnp.mean(o * o, axis=0, keepdims=True) + EPS)
        o_ref[0, h * B_VDIM:(h + 1) * B_VDIM, :] = (o * g_ref[...] * out_scale).astype(o_ref.dtype)


def _gqa_attention(q_t, k, v_t, *, tk=ATTN_TK):
    B, _, _, L = q_t.shape
    S = k.shape[1]
    return pl.pallas_call(
        functools.partial(_gqa_kernel, tk=tk, n_keys=S),
        grid=(B, L // TQ),
        in_specs=[
            pl.BlockSpec((1, A_HEADS, K_PACK, TQ), lambda b, i: (b, 0, 0, i)),
            pl.BlockSpec((1, S, K_PACK), lambda b, i: (b, 0, 0)),
            pl.BlockSpec((1, A_KV_HEADS, V_EXT, S), lambda b, i: (b, 0, 0, 0)),
        ],
        out_specs=pl.BlockSpec((1, A_Q, TQ), lambda b, i: (b, 0, i)),
        out_shape=jax.ShapeDtypeStruct((B, A_Q, L), jnp.bfloat16),
        scratch_shapes=[pltpu.VMEM((A_HEADS, 1, TQ), jnp.float32),
                        pltpu.VMEM((A_HEADS, V_EXT, TQ), jnp.float32)],
        compiler_params=pltpu.CompilerParams(
            dimension_semantics=("arbitrary", "arbitrary"), vmem_limit_bytes=VMEM_LIMIT),
        name="gqa_attention",
    )(q_t, k, v_t)


def _diff_attention(q_t, k, v_t, lam, subln_g, out_scale, *, tk=ATTN_TK):
    B, _, _, L = q_t.shape
    S = k.shape[1]
    return pl.pallas_call(
        functools.partial(_diff_kernel, tk=tk, n_keys=S, out_scale=out_scale),
        grid=(B, B_HEADS // 2, L // TQ),
        in_specs=[
            pl.BlockSpec(memory_space=pltpu.SMEM),
            pl.BlockSpec((1, 4, K_PACK, TQ), lambda b, h, i: (b, h, 0, i)),
            pl.BlockSpec((1, S, K_PACK), lambda b, h, i: (b, 0, h)),
            pl.BlockSpec((1, 2, V_EXT, S), lambda b, h, i: (b, h, 0, 0)),
            pl.BlockSpec((B_VDIM, 1), lambda b, h, i: (0, 0)),
        ],
        out_specs=pl.BlockSpec((1, 2 * B_VDIM, TQ), lambda b, h, i: (b, h, i)),
        out_shape=jax.ShapeDtypeStruct((B, B_V, L), jnp.bfloat16),
        scratch_shapes=[pltpu.VMEM((4, 1, TQ), jnp.float32),
                        pltpu.VMEM((4, V_EXT, TQ), jnp.float32)],
        compiler_params=pltpu.CompilerParams(
            dimension_semantics=("arbitrary", "arbitrary", "arbitrary"), vmem_limit_bytes=VMEM_LIMIT),
        name="diff_attention",
    )(lam.reshape(1).astype(jnp.float32), q_t, k, v_t, subln_g.reshape(B_VDIM, 1).astype(jnp.float32))


def _value_ext_t(v):
    B, S, H, d = v.shape
    ones = jnp.ones((B, S, H, 1), v.dtype)
    pad = jnp.zeros((B, S, H, V_EXT - d - 1), v.dtype)
    return jnp.transpose(jnp.concatenate([v, ones, pad], axis=-1), (0, 2, 3, 1)).astype(jnp.bfloat16)


def _query_ext_t(q, offsets):
    B, L, U, d = q.shape
    q_t = jnp.transpose(q, (0, 2, 3, 1)).astype(jnp.bfloat16)
    units = [jnp.pad(q_t[:, u], ((0, 0), (off, K_PACK - d - off), (0, 0))) for u, off in enumerate(offsets)]
    return jnp.stack(units, axis=1)


def _rms(x, g):
    xf = x.astype(jnp.float32)
    y = xf * lax.rsqrt(jnp.mean(xf * xf, axis=-1, keepdims=True) + EPS)
    return (y * g.astype(jnp.float32)).astype(x.dtype)


def _modulate(h, shift, scale):
    return h * (1 + scale) + shift


def _rope_2d(x, row, col):
    d = x.shape[-1]
    half = d // 2
    inv = 1.0 / (ROPE_THETA ** (jnp.arange(0, half, 2, dtype=jnp.float32) / half))
    ang = jnp.concatenate([row.astype(jnp.float32)[:, None] * inv[None, :],
                           col.astype(jnp.float32)[:, None] * inv[None, :]], axis=-1)
    cos = jnp.cos(ang)[None, :, None, :]
    sin = jnp.sin(ang)[None, :, None, :]
    xr = x.astype(jnp.float32).reshape(x.shape[:-1] + (half, 2))
    x0, x1 = xr[..., 0], xr[..., 1]
    out = jnp.stack([x0 * cos - x1 * sin, x0 * sin + x1 * cos], axis=-1)
    return out.reshape(x.shape).astype(x.dtype)


def _gqa_block(q, k, v):
    s = jnp.einsum('bqkgd,bskd->bkgqs', q, k).astype(jnp.float32) * (HEAD_DIM ** -0.5)
    p = jax.nn.softmax(s, axis=-1).astype(v.dtype)
    return jnp.einsum('bkgqs,bskd->bqkgd', p, v)


def _diff_block(q, k, v, lam):
    s = jnp.einsum('bqhtd,bshtd->bthqs', q, k).astype(jnp.float32) * (B_SUB ** -0.5)
    p = jax.nn.softmax(s, axis=-1)
    a = (p[:, 0] - lam * p[:, 1]).astype(v.dtype)
    return jnp.einsum('bhqs,bshd->bqhd', a, v)


S5_TC = 64
S5_ROW_TILE = 16
S5_CW = S5_TC * C_GROUP_CH
S5_SW = 2 * C_STATE


def _s5_operators(a_re, a_im, log_dt, b_re, b_im, c_re, c_im):
    f32 = jnp.float32
    tc, P, N, G = S5_TC, C_GROUP_CH, C_STATE, C_GROUPS
    j = jnp.arange(tc + 1, dtype=f32)
    kerns, p_parts, q_parts, a_parts = [], [], [], []
    for dirn in range(2):
        A = lax.complex(a_re[dirn].astype(f32), a_im[dirn].astype(f32))
        adt = A * jnp.exp(log_dt[dirn].astype(f32))[:, None]
        abar = jnp.exp(adt)
        bbar = ((abar - 1) / A)[..., None] * lax.complex(b_re[dirn].astype(f32), b_im[dirn].astype(f32))
        cmat = lax.complex(c_re[dirn].astype(f32), c_im[dirn].astype(f32))
        pw = jnp.exp(adt[None] * j[:, None, None])
        kerns.append(jnp.einsum('gpn,jgn,gnq->gqjp', cmat, pw[:tc], bbar).real)
        inj_pw = pw[:tc] if dirn == 1 else pw[:tc][::-1]
        inj = inj_pw[:, :, :, None] * bbar[None]
        p_parts.append(jnp.transpose(inj, (1, 0, 3, 2)).reshape(G, S5_CW, N))
        out_pw = pw[1:][::-1] if dirn == 1 else pw[1:]
        outm = cmat[None] * out_pw[:, :, None, :]
        q_parts.append(jnp.transpose(outm, (1, 3, 0, 2)).reshape(G, N, S5_CW))
        a_parts.append(pw[tc])
    f_seq = jnp.concatenate([jnp.flip(kerns[1][:, :, 1:], axis=2), kerns[0][:, :, :1] + kerns[1][:, :, :1],
                             kerns[0][:, :, 1:], jnp.zeros((G, P, 1, P), f32)], axis=2)
    rolled = jnp.tile(f_seq.astype(jnp.bfloat16).reshape(G, P, 2 * tc * P), (1, 1, tc))
    rolled = rolled[:, :, :tc * (2 * tc - 1) * P].reshape(G, P, tc, (2 * tc - 1) * P)
    toep = jnp.transpose(rolled[:, :, :, (tc - 1) * P:], (0, 2, 1, 3)).reshape(G, S5_CW, S5_CW)
    zeros = jnp.zeros((G, N, S5_CW), jnp.bfloat16)
    bf = lambda m: m.astype(jnp.bfloat16)
    w_mat = jnp.concatenate([
        toep,
        bf(q_parts[0].real), zeros, bf(-q_parts[0].imag), zeros,
        zeros, bf(q_parts[1].real), zeros, bf(-q_parts[1].imag),
    ], axis=1)
    p_mat = jnp.concatenate([p_parts[0].real, p_parts[1].real, p_parts[0].imag, p_parts[1].imag], axis=-1)
    a_chunk = jnp.stack([jnp.concatenate([a_parts[0].real, a_parts[1].real], axis=-1),
                         jnp.concatenate([a_parts[0].imag, a_parts[1].imag], axis=-1)], axis=1)
    return p_mat.astype(jnp.bfloat16), w_mat, a_chunk.astype(f32)


def _s5_scan_kernel(u_ref, p_ref, w_ref, a_ref, y_ref, s_scr, hs_scr, *, rows, order_f, order_r):
    f32 = jnp.float32
    u = u_ref[0]
    s_scr[...] = jnp.dot(u, p_ref[0], preferred_element_type=f32)
    hs_scr[...] = jnp.zeros(hs_scr.shape, f32)
    a_re = a_ref[0, 0:1, :]
    a_im = a_ref[0, 1:2, :]
    is_fwd = lax.broadcasted_iota(jnp.int32, (rows, S5_SW), 1) < C_STATE
    h_re = jnp.zeros((rows, S5_SW), f32)
    h_im = jnp.zeros((rows, S5_SW), f32)
    for cf, cr in zip(order_f, order_r):
        rf = slice(cf * rows, (cf + 1) * rows)
        rr = slice(cr * rows, (cr + 1) * rows)
        hs_scr[rf, 0:S5_SW] = h_re
        hs_scr[rf, S5_SW:2 * S5_SW] = h_im
        hs_scr[rr, 2 * S5_SW:3 * S5_SW] = h_re
        hs_scr[rr, 3 * S5_SW:4 * S5_SW] = h_im
        s_re = jnp.where(is_fwd, s_scr[rf, 0:S5_SW], s_scr[rr, 0:S5_SW])
        s_im = jnp.where(is_fwd, s_scr[rf, S5_SW:2 * S5_SW], s_scr[rr, S5_SW:2 * S5_SW])
        h_re, h_im = a_re * h_re - a_im * h_im + s_re, a_re * h_im + a_im * h_re + s_im
    y = jnp.dot(u, w_ref[0, 0:S5_CW, :], preferred_element_type=f32)
    y += jnp.dot(hs_scr[...].astype(jnp.bfloat16), w_ref[0, S5_CW:, :], preferred_element_type=f32)
    y_ref[0] = y


def _s5_scan(u_seq, n_ctx_chunks, p_mat, w_mat, a_chunk):
    B, S, _ = u_seq.shape
    G, P = C_GROUPS, C_GROUP_CH
    nc = S // S5_TC
    R = -(-(nc * B) // S5_ROW_TILE) * S5_ROW_TILE
    ug = jnp.transpose(u_seq.reshape(B, nc, S5_TC, G, P), (3, 1, 0, 2, 4))
    ug = jnp.pad(ug.reshape(G, nc * B, S5_CW), ((0, 0), (0, R - nc * B), (0, 0))).astype(jnp.bfloat16)
    order_f = tuple(range(nc))
    order_r = tuple(range(n_ctx_chunks - 1, -1, -1)) + tuple(range(nc - 1, n_ctx_chunks - 1, -1))
    y = pl.pallas_call(
        functools.partial(_s5_scan_kernel, rows=B, order_f=order_f, order_r=order_r),
        grid=(G,),
        in_specs=[
            pl.BlockSpec((1, R, S5_CW), lambda g: (g, 0, 0)),
            pl.BlockSpec((1, S5_CW, 2 * S5_SW), lambda g: (g, 0, 0)),
            pl.BlockSpec((1, S5_CW + 4 * S5_SW, S5_CW), lambda g: (g, 0, 0)),
            pl.BlockSpec((1, 2, S5_SW), lambda g: (g, 0, 0)),
        ],
        out_specs=pl.BlockSpec((1, R, S5_CW), lambda g: (g, 0, 0)),
        out_shape=jax.ShapeDtypeStruct((G, R, S5_CW), jnp.float32),
        scratch_shapes=[pltpu.VMEM((R, 2 * S5_SW), jnp.float32),
                        pltpu.VMEM((R, 4 * S5_SW), jnp.float32)],
        compiler_params=pltpu.CompilerParams(dimension_semantics=("arbitrary",),
                                             vmem_limit_bytes=VMEM_LIMIT),
        name="s5_scan",
    )(ug, p_mat, w_mat, a_chunk)
    y = y[:, :nc * B].reshape(G, nc, B, S5_TC, P)
    return jnp.transpose(y, (2, 1, 3, 0, 4)).reshape(B, S, C_WIDTH)


def _s5_glu_kernel(y_ref, u_ref, d_ref, w_ref, o_ref):
    y = y_ref[...] + d_ref[...] * u_ref[...]
    g = jax.nn.gelu(y)
    z = jnp.dot(g.astype(jnp.bfloat16), w_ref[...], preferred_element_type=jnp.float32)
    o_ref[...] = (g * jax.nn.sigmoid(z)).astype(o_ref.dtype)


def _s5_glu(y, u, d_skip, w_glu):
    T = y.shape[0]
    tb = math.gcd(T, 1024)
    return pl.pallas_call(
        _s5_glu_kernel,
        grid=(T // tb,),
        in_specs=[pl.BlockSpec((tb, C_WIDTH), lambda i: (i, 0)),
                  pl.BlockSpec((tb, C_WIDTH), lambda i: (i, 0)),
                  pl.BlockSpec((1, C_WIDTH), lambda i: (0, 0)),
                  pl.BlockSpec((C_WIDTH, C_WIDTH), lambda i: (0, 0))],
        out_specs=pl.BlockSpec((tb, C_WIDTH), lambda i: (i, 0)),
        out_shape=jax.ShapeDtypeStruct((T, C_WIDTH), jnp.float32),
        compiler_params=pltpu.CompilerParams(dimension_semantics=("arbitrary",)),
        name="s5_glu",
    )(y, u, d_skip.reshape(1, C_WIDTH).astype(jnp.float32), w_glu.astype(jnp.bfloat16))


def _s5_mixer(u, uc, s5_ops, d_skip, w_glu, with_ctx):
    B, L, _ = u.shape
    Lc = uc.shape[1]
    p_mat, w_mat, a_chunk = s5_ops
    u_seq = jnp.concatenate([uc, u], axis=1).astype(jnp.float32)
    y = _s5_scan(u_seq, Lc // S5_TC, p_mat, w_mat, a_chunk)
    if with_ctx:
        out = _s5_glu(y.reshape(B * (Lc + L), C_WIDTH), u_seq.reshape(B * (Lc + L), C_WIDTH), d_skip, w_glu)
        out = out.reshape(B, Lc + L, C_WIDTH)
        return out[:, Lc:], out[:, :Lc]
    out = _s5_glu(y[:, Lc:].reshape(B * L, C_WIDTH), u.astype(jnp.float32).reshape(B * L, C_WIDTH), d_skip, w_glu)
    return out.reshape(B, L, C_WIDTH), None


def _token_mixers(h, hc, row, col, layer_idx, w_in, qn_a, kn_a, qn_b, kn_b,
                  lam_q1, lam_k1, lam_q2, lam_k2, subln_g, s5_ops, d_skip, w_glu, with_ctx):
    B, L, _ = h.shape
    Lc = hc.shape[1]
    cuts = [A_Q, A_Q + A_KV, A_Q + 2 * A_KV, A_Q + 2 * A_KV + B_QK,
            A_Q + 2 * A_KV + 2 * B_QK, A_Q + 2 * A_KV + 2 * B_QK + B_V]
    qa, ka, va, qb, kb, vb, us = jnp.split(h @ w_in, cuts, axis=-1)
    qac, kac, vac, qbc, kbc, vbc, usc = jnp.split(hc @ w_in, cuts, axis=-1)

    qa = _rope_2d(_rms(qa.reshape(B, L, A_HEADS, HEAD_DIM), qn_a), row, col)
    ka = _rope_2d(_rms(ka.reshape(B, L, A_KV_HEADS, HEAD_DIM), kn_a), row, col)
    va = va.reshape(B, L, A_KV_HEADS, HEAD_DIM)
    kac = _rms(kac.reshape(B, Lc, A_KV_HEADS, HEAD_DIM), kn_a)
    vac = vac.reshape(B, Lc, A_KV_HEADS, HEAD_DIM)
    ka_all = jnp.concatenate([kac, ka], axis=1)
    va_all = jnp.concatenate([vac, va], axis=1)
    qa_t = _query_ext_t(qa * (HEAD_DIM ** -0.5 * LOG2E), [(hq // A_GROUP) * HEAD_DIM for hq in range(A_HEADS)])
    ka_pack = ka_all.reshape(B, Lc + L, A_KV).astype(jnp.bfloat16)
    o_a = jnp.transpose(_gqa_attention(qa_t, ka_pack, _value_ext_t(va_all)), (0, 2, 1))

    lambda_init = 0.8 - 0.6 * math.exp(-0.3 * layer_idx)
    lam = (jnp.exp(jnp.sum(lam_q1.astype(jnp.float32) * lam_k1.astype(jnp.float32)))
           - jnp.exp(jnp.sum(lam_q2.astype(jnp.float32) * lam_k2.astype(jnp.float32))) + lambda_init)
    qb = _rope_2d(_rms(qb.reshape(B, L, 2 * B_HEADS, B_SUB), qn_b), row, col)
    kb = _rope_2d(_rms(kb.reshape(B, L, 2 * B_HEADS, B_SUB), kn_b), row, col)
    vb = vb.reshape(B, L, B_HEADS, B_VDIM)
    kbc = _rms(kbc.reshape(B, Lc, 2 * B_HEADS, B_SUB), kn_b)
    vbc = vbc.reshape(B, Lc, B_HEADS, B_VDIM)
    kb_all = jnp.concatenate([kbc, kb], axis=1)
    vb_all = jnp.concatenate([vbc, vb], axis=1)
    qb_t = _query_ext_t(qb * (B_SUB ** -0.5 * LOG2E), [(j % 4) * B_SUB for j in range(2 * B_HEADS)])
    kb_pack = kb_all.reshape(B, Lc + L, B_QK).astype(jnp.bfloat16)
    o_b = jnp.transpose(_diff_attention(qb_t, kb_pack, _value_ext_t(vb_all), lam, subln_g, 1 - lambda_init),
                        (0, 2, 1))

    o_c, o_cc = _s5_mixer(us, usc, s5_ops, d_skip, w_glu, with_ctx)

    mix = jnp.concatenate([o_a.astype(h.dtype), o_b.astype(h.dtype), o_c], axis=-1)
    if not with_ctx:
        return mix, None
    qac = _rms(qac.reshape(B, Lc, A_HEADS, HEAD_DIM), qn_a).reshape(B, Lc, A_KV_HEADS, A_GROUP, HEAD_DIM)
    o_ac = _gqa_block(qac, kac, vac).reshape(B, Lc, A_Q)
    kbc5 = kbc.reshape(B, Lc, B_HEADS, 2, B_SUB)
    qbc = _rms(qbc.reshape(B, Lc, 2 * B_HEADS, B_SUB), qn_b).reshape(B, Lc, B_HEADS, 2, B_SUB)
    o_bc = _diff_block(qbc, kbc5, vbc, lam)
    o_bc = (_rms(o_bc, subln_g) * (1 - lambda_init)).reshape(B, Lc, B_V)
    mixc = jnp.concatenate([o_ac, o_bc, o_cc], axis=-1)
    return mix, mixc


PEER_TB = 256
N_SLOTS = PEER_HEADS * PEER_TOPK
CAND_ROWS = 80
INVALID_FLAT = 1.0e9
SC_CORES = 2
SC_SUBCORES = 16


def _cand_flat_table():
    f = [float(b) for b in range(16)]
    for a in range(1, 8):
        f += [float(a * 16 + b) if (a + 1) * (b + 1) <= 16 else INVALID_FLAT for b in range(8)]
    f += [float(a * 16) for a in range(8, 16)]
    return jnp.broadcast_to(jnp.asarray(f, jnp.float32)[:, None], (CAND_ROWS, PEER_TB))


def _pair_rows(first, second):
    blocks = [first[0:1] + second]
    blocks += [first[a:a + 1] + second[0:8] for a in range(1, 8)]
    blocks += [first[8:16] + second[0:1]]
    return jnp.concatenate(blocks, axis=0)


N_EXPERTS = N_KEYS * N_KEYS


def _peer_route_kernel(x_ref, g_ref, sh_ref, sc_ref, wq_ref, keys_ref, ftab_ref, ut_ref,
                       idx_ref, gate_ref, s_ref, h_scr, q_scr, val_scr, id_scr, sc_scr, e_scr):
    f32 = jnp.float32
    head = pl.program_id(1)

    @pl.when(head == 0)
    def _():
        x = x_ref[...]
        y = x * lax.rsqrt(jnp.mean(x * x, axis=-1, keepdims=True) + EPS) * g_ref[...]
        h2 = (y * (1.0 + sc_ref[0]) + sh_ref[0]).astype(jnp.bfloat16)
        h_scr[...] = h2
        q = jnp.dot(h2, wq_ref[...], preferred_element_type=f32)
        for ht in range(2 * PEER_HEADS):
            q_scr[ht] = q[:, ht * PEER_HALF:(ht + 1) * PEER_HALF].astype(jnp.bfloat16)

    sub_scores = [lax.dot_general(keys_ref[head * 2 + t], q_scr[head * 2 + t], (((1,), (1,)), ((), ())),
                                  preferred_element_type=f32) for t in range(2)]
    s_ref[...] = jnp.dot(h_scr[...], ut_ref[...], preferred_element_type=f32)

    ftab = ftab_ref[...]
    key_id = lax.broadcasted_iota(jnp.int32, (N_KEYS, PEER_TB), 0).astype(f32)
    neg_inf = -jnp.inf
    for t in range(2):
        s = sub_scores[t]
        for r in range(PEER_TOPK):
            m = jnp.max(s, axis=0, keepdims=True)
            pick = jnp.min(jnp.where(s == m, key_id, float(N_KEYS)), axis=0, keepdims=True)
            s = jnp.where(key_id == pick, neg_inf, s)
            val_scr[t, r:r + 1, :] = m
            id_scr[t, r:r + 1, :] = pick
    cand = _pair_rows(val_scr[0], val_scr[1])
    cand = jnp.where(ftab < float(PEER_TOPK * PEER_TOPK), cand, neg_inf)
    expert = _pair_rows(id_scr[0] * float(N_KEYS), id_scr[1])
    for r in range(PEER_TOPK):
        m = jnp.max(cand, axis=0, keepdims=True)
        pick = jnp.min(jnp.where(cand == m, ftab, INVALID_FLAT), axis=0, keepdims=True)
        hit = ftab == pick
        sc_scr[r:r + 1, :] = m
        e_scr[r:r + 1, :] = jnp.sum(jnp.where(hit, expert, 0.0), axis=0, keepdims=True)
        cand = jnp.where(hit, neg_inf, cand)
    sc = sc_scr[...]
    p = jnp.exp(sc - sc[0:1])
    rows = pl.ds(pl.multiple_of(head * PEER_TOPK, PEER_TOPK), PEER_TOPK)
    idx_ref[rows, :] = e_scr[...].astype(jnp.int32)
    gate_ref[rows, :] = p / jnp.sum(p, axis=0, keepdims=True)


def _peer_route(x, norm_g, shift, scale, wq_bf, keys_bf, u_t, tokens_per_mod):
    T, D = x.shape
    nq = wq_bf.shape[1]
    te = N_EXPERTS // PEER_HEADS
    blocks_per_mod = tokens_per_mod // PEER_TB
    return pl.pallas_call(
        _peer_route_kernel,
        grid=(T // PEER_TB, PEER_HEADS),
        in_specs=[
            pl.BlockSpec((PEER_TB, D), lambda i, j: (i, 0)),
            pl.BlockSpec((1, D), lambda i, j: (0, 0)),
            pl.BlockSpec((1, 1, D), lambda i, j: (i // blocks_per_mod, 0, 0)),
            pl.BlockSpec((1, 1, D), lambda i, j: (i // blocks_per_mod, 0, 0)),
            pl.BlockSpec((D, nq), lambda i, j: (0, 0)),
            pl.BlockSpec((2 * PEER_HEADS, N_KEYS, PEER_HALF), lambda i, j: (0, 0, 0)),
            pl.BlockSpec((CAND_ROWS, PEER_TB), lambda i, j: (0, 0)),
            pl.BlockSpec((D, te), lambda i, j: (0, j)),
        ],
        out_specs=[
            pl.BlockSpec((N_SLOTS, PEER_TB), lambda i, j: (0, i)),
            pl.BlockSpec((N_SLOTS, PEER_TB), lambda i, j: (0, i)),
            pl.BlockSpec((PEER_TB, te), lambda i, j: (i, j)),
        ],
        out_shape=[
            jax.ShapeDtypeStruct((N_SLOTS, T), jnp.int32),
            jax.ShapeDtypeStruct((N_SLOTS, T), jnp.float32),
            jax.ShapeDtypeStruct((T, N_EXPERTS), jnp.float32),
        ],
        scratch_shapes=[pltpu.VMEM((PEER_TB, D), jnp.bfloat16),
                        pltpu.VMEM((2 * PEER_HEADS, PEER_TB, PEER_HALF), jnp.bfloat16),
                        pltpu.VMEM((2, PEER_TOPK, PEER_TB), jnp.float32),
                        pltpu.VMEM((2, PEER_TOPK, PEER_TB), jnp.float32),
                        pltpu.VMEM((PEER_TOPK, PEER_TB), jnp.float32),
                        pltpu.VMEM((PEER_TOPK, PEER_TB), jnp.float32)],
        compiler_params=pltpu.CompilerParams(dimension_semantics=("arbitrary", "arbitrary"),
                                             vmem_limit_bytes=VMEM_LIMIT),
        name="peer_route",
    )(x, norm_g.reshape(1, D).astype(jnp.float32), shift, scale, wq_bf, keys_bf, _cand_flat_table(), u_t)
SC_LANES = 16
SC_TOKENS = 32
DENSE_TM = 512
DENSE_TE = 2048


def _sc_worker_tokens(n_tokens):
    per_worker = n_tokens // (SC_CORES * SC_SUBCORES)
    blk = min(SC_TOKENS, per_worker)
    return per_worker, blk


def _sc_extract(scores, idx_flat):
    T = scores.shape[0]
    per_worker, blk = _sc_worker_tokens(T)
    mesh = plsc.VectorSubcoreMesh(core_axis_name="c", subcore_axis_name="s")

    def body(s_hbm, idx_hbm, out_hbm, row0_v, row1_v, idx_v, out_v, rsem):
        rows = (row0_v, row1_v)
        wid = lax.axis_index("s") * SC_CORES + lax.axis_index("c")
        t0 = wid * per_worker

        def row_copy(tok, slot):
            return pltpu.make_async_copy(s_hbm.at[tok], rows[slot], rsem.at[slot])

        def block(b, carry):
            tb = t0 + b * blk
            pltpu.sync_copy(idx_hbm.at[pl.ds(tb * N_SLOTS, blk * N_SLOTS)], idx_v)
            row_copy(tb, 0).start()

            def pair(i2, c):
                for slot in range(2):
                    i = i2 * 2 + slot

                    @pl.when(i + 1 < blk)
                    def _():
                        row_copy(tb + i + 1, 1 - slot).start()

                    row_copy(tb + i, slot).wait()
                    for j in range(N_SLOTS // SC_LANES):
                        at = pl.ds(i * N_SLOTS + j * SC_LANES, SC_LANES)
                        out_v[at] = plsc.load_gather(rows[slot], [idx_v[at]])
                return c

            lax.fori_loop(0, blk // 2, pair, 0)
            pltpu.sync_copy(out_v, out_hbm.at[pl.ds(tb * N_SLOTS, blk * N_SLOTS)])
            return carry

        lax.fori_loop(0, per_worker // blk, block, 0)

    return pl.kernel(
        body, mesh=mesh,
        out_type=jax.ShapeDtypeStruct((T * N_SLOTS,), jnp.float32),
        scratch_types=[pltpu.VMEM((N_EXPERTS,), jnp.float32), pltpu.VMEM((N_EXPERTS,), jnp.float32),
                       pltpu.VMEM((blk * N_SLOTS,), jnp.int32), pltpu.VMEM((blk * N_SLOTS,), jnp.float32),
                       pltpu.SemaphoreType.DMA((2,))],
        compiler_params=pltpu.CompilerParams(needs_layout_passes=False),
        name="peer_extract",
    )(scores, idx_flat)


def _sc_scatter(w_flat, idx_flat, n_tokens):
    T = n_tokens
    per_worker, blk = _sc_worker_tokens(T)
    n_vec = N_SLOTS // SC_LANES
    mesh = plsc.VectorSubcoreMesh(core_axis_name="c", subcore_axis_name="s")

    def body(w_hbm, idx_hbm, out_hbm, row0_v, row1_v, idx_v, w_v, wsem):
        rows = (row0_v, row1_v)
        wid = lax.axis_index("s") * SC_CORES + lax.axis_index("c")
        t0 = wid * per_worker
        zero = jnp.zeros((SC_LANES,), jnp.float32)

        def zero_fill(k, c):
            for slot in range(2):
                rows[slot][pl.ds(k * SC_LANES, SC_LANES)] = zero
            return c

        lax.fori_loop(0, N_EXPERTS // SC_LANES, zero_fill, 0)

        def out_copy(tok, slot):
            return pltpu.make_async_copy(rows[slot], out_hbm.at[tok], wsem.at[slot])

        def retire(tok, i, slot):
            out_copy(tok, slot).wait()
            for j in range(n_vec):
                plsc.store_scatter(rows[slot], [idx_v[pl.ds(i * N_SLOTS + j * SC_LANES, SC_LANES)]], zero)

        def block(b, carry):
            tb = t0 + b * blk
            pltpu.sync_copy(idx_hbm.at[pl.ds(tb * N_SLOTS, blk * N_SLOTS)], idx_v)
            pltpu.sync_copy(w_hbm.at[pl.ds(tb * N_SLOTS, blk * N_SLOTS)], w_v)

            def pair(i2, c):
                for slot in range(2):
                    i = i2 * 2 + slot

                    @pl.when(i >= 2)
                    def _():
                        retire(tb + i - 2, i - 2, slot)

                    for j in range(n_vec):
                        at = pl.ds(i * N_SLOTS + j * SC_LANES, SC_LANES)
                        plsc.addupdate_scatter(rows[slot], [idx_v[at]], w_v[at])
                    out_copy(tb + i, slot).start()
                return c

            lax.fori_loop(0, blk // 2, pair, 0)
            for slot in range(2):
                retire(tb + blk - 2 + slot, blk - 2 + slot, slot)
            return carry

        lax.fori_loop(0, per_worker // blk, block, 0)

    return pl.kernel(
        body, mesh=mesh,
        out_type=jax.ShapeDtypeStruct((T, N_EXPERTS), jnp.float32),
        scratch_types=[pltpu.VMEM((N_EXPERTS,), jnp.float32), pltpu.VMEM((N_EXPERTS,), jnp.float32),
                       pltpu.VMEM((blk * N_SLOTS,), jnp.int32), pltpu.VMEM((blk * N_SLOTS,), jnp.float32),
                       pltpu.SemaphoreType.DMA((2,))],
        compiler_params=pltpu.CompilerParams(needs_layout_passes=False),
        name="peer_scatter",
    )(w_flat, idx_flat)


def _peer_gate_kernel(a_ref, g_ref, w_ref):
    w_ref[...] = jax.nn.gelu(a_ref[...]) * g_ref[...]


def _peer_gate(act, gate):
    T = act.shape[0]
    tb = min(2048, T)
    spec = pl.BlockSpec((tb, N_SLOTS), lambda i: (i, 0))
    return pl.pallas_call(
        _peer_gate_kernel, grid=(T // tb,), in_specs=[spec, spec], out_specs=spec,
        out_shape=jax.ShapeDtypeStruct((T, N_SLOTS), jnp.float32),
        compiler_params=pltpu.CompilerParams(dimension_semantics=("arbitrary",)),
        name="peer_gate",
    )(act, gate)


def _peer_combine_kernel(w_ref, v_ref, x_ref, g2_ref, o_ref, acc_ref):
    k = pl.program_id(1)

    @pl.when(k == 0)
    def _():
        acc_ref[...] = jnp.zeros(acc_ref.shape, jnp.float32)

    acc_ref[...] += jnp.dot(w_ref[...].astype(jnp.bfloat16), v_ref[...], preferred_element_type=jnp.float32)

    @pl.when(k == pl.num_programs(1) - 1)
    def _():
        o_ref[...] = x_ref[...] + g2_ref[0] * acc_ref[...]


def _peer_combine(w_dense, v_bf, x, g2, tokens_per_mod):
    T, D = x.shape
    tm = min(DENSE_TM, T)
    blocks_per_mod = tokens_per_mod // tm
    return pl.pallas_call(
        _peer_combine_kernel,
        grid=(T // tm, N_EXPERTS // DENSE_TE),
        in_specs=[pl.BlockSpec((tm, DENSE_TE), lambda i, k: (i, k)),
                  pl.BlockSpec((DENSE_TE, D), lambda i, k: (k, 0)),
                  pl.BlockSpec((tm, D), lambda i, k: (i, 0)),
                  pl.BlockSpec((1, 1, D), lambda i, k: (i // blocks_per_mod, 0, 0))],
        out_specs=pl.BlockSpec((tm, D), lambda i, k: (i, 0)),
        out_shape=jax.ShapeDtypeStruct((T, D), jnp.float32),
        scratch_shapes=[pltpu.VMEM((tm, D), jnp.float32)],
        compiler_params=pltpu.CompilerParams(dimension_semantics=("arbitrary", "arbitrary"),
                                             vmem_limit_bytes=VMEM_LIMIT),
        name="peer_combine",
    )(w_dense, v_bf, x, g2)


def _peer_residual(x, norm_g, shift, scale, gate2, wq_bf, keys_bf, u_t, v_bf, tokens_per_mod):
    T, D = x.shape
    idx_t, gate_t, scores = _peer_route(x, norm_g, shift, scale, wq_bf, keys_bf, u_t, tokens_per_mod)
    idx_flat = jnp.transpose(idx_t).reshape(T * N_SLOTS)
    act = _sc_extract(scores, idx_flat).reshape(T, N_SLOTS)
    w = _peer_gate(act, jnp.transpose(gate_t))
    w_dense = _sc_scatter(w.reshape(T * N_SLOTS), idx_flat, T)
    return _peer_combine(w_dense, v_bf, x, gate2, tokens_per_mod)


def kernel(x, c, ctx, c_ctx, w_ada, b_ada, norm1_g, norm2_g, w_in, w_out,
           qn_a, kn_a, qn_b, kn_b, lam_q1, lam_k1, lam_q2, lam_k2, subln_g,
           s5_a_re, s5_a_im, s5_log_dt, s5_b_re, s5_b_im, s5_c_re, s5_c_im,
           s5_d, s5_w_glu, peer_wq, peer_keys, peer_u, peer_v):
    B, L, D = x.shape
    Lc = ctx.shape[1]
    depth = w_in.shape[0]
    rows = L // GRID_W
    row = jnp.repeat(jnp.arange(rows, dtype=jnp.int32), GRID_W)
    col = jnp.tile(jnp.arange(GRID_W, dtype=jnp.int32), rows)
    s_c = jax.nn.silu(c)
    s_cc = jax.nn.silu(c_ctx)
    mods, cmods, s5_ops, wq_bf, keys_bf, u_t, v_bf = [], [], [], [], [], [], []
    for l in range(depth):
        mods.append(jnp.split((s_c @ w_ada[l] + b_ada[l])[:, None, :], 6, axis=-1))
        cmods.append([m.reshape(1, 1, D) for m in jnp.split(s_cc @ w_ada[l] + b_ada[l], 6, axis=-1)])
        s5_ops.append(_s5_operators(s5_a_re[l], s5_a_im[l], s5_log_dt[l], s5_b_re[l], s5_b_im[l],
                                    s5_c_re[l], s5_c_im[l]))
        wq_bf.append(peer_wq[l].astype(jnp.bfloat16))
        keys_bf.append(peer_keys[l].reshape(2 * PEER_HEADS, N_KEYS, PEER_HALF).astype(jnp.bfloat16))
        u_t.append(jnp.transpose(peer_u[l].astype(jnp.bfloat16)))
        v_bf.append(peer_v[l].astype(jnp.bfloat16))

    outs = []
    for b in range(B):
        xb = x[b:b + 1]
        cb = ctx[b:b + 1]
        for l in range(depth):
            with_ctx = l < depth - 1
            sh1, sc1, g1, sh2, sc2, g2 = [m[b:b + 1] for m in mods[l]]
            csh1, csc1, cg1, csh2, csc2, cg2 = cmods[l]
            h = _modulate(_rms(xb, norm1_g[l]), sh1, sc1)
            hc = _modulate(_rms(cb, norm1_g[l]), csh1, csc1)
            mix, mixc = _token_mixers(h, hc, row, col, l, w_in[l], qn_a[l], kn_a[l], qn_b[l], kn_b[l],
                                      lam_q1[l], lam_k1[l], lam_q2[l], lam_k2[l], subln_g[l],
                                      s5_ops[l], s5_d[l], s5_w_glu[l], with_ctx)
            xb = xb + g1 * (mix @ w_out[l])
            xb = _peer_residual(xb.reshape(L, D), norm2_g[l], sh2, sc2, g2, wq_bf[l], keys_bf[l], u_t[l], v_bf[l],
                                tokens_per_mod=L).reshape(1, L, D)
            if with_ctx:
                cb = cb + cg1 * (mixc @ w_out[l])
                cb = _peer_residual(cb.reshape(Lc, D), norm2_g[l], csh2, csc2, cg2, wq_bf[l], keys_bf[l],
                                    u_t[l], v_bf[l], tokens_per_mod=Lc).reshape(1, Lc, D)
        outs.append(xb)
    return jnp.concatenate(outs, axis=0)
```

```python
import functools
import math

import jax
import jax.numpy as jnp
from jax import lax
from jax.experimental import pallas as pl
from jax.experimental.pallas import tpu as pltpu
from jax.experimental.pallas import tpu_sc as plsc

D_MODEL = 1024
GRID_W = 64
EPS = 1e-6
ROPE_THETA = 10000.0
HEAD_DIM = 64
A_HEADS = 6
A_KV_HEADS = 2
A_GROUP = A_HEADS // A_KV_HEADS
A_Q = A_HEADS * HEAD_DIM
A_KV = A_KV_HEADS * HEAD_DIM
B_HEADS = 6
B_SUB = 32
B_VDIM = 2 * B_SUB
B_QK = B_HEADS * 2 * B_SUB
B_V = B_HEADS * B_VDIM
C_GROUPS = 16
C_GROUP_CH = 16
C_WIDTH = C_GROUPS * C_GROUP_CH
C_STATE = 64
PEER_HEADS = 8
N_KEYS = 128
PEER_TOPK = 16
PEER_HALF = 128
TOKEN_CHUNK = 128

LANES = 128
V_EXT = LANES
LOG2E = math.log2(math.e)
NEG_BIG = -1e30
VMEM_LIMIT = 48 * 1024 * 1024


TQ = 1024
ATTN_TK = 2816
K_PACK = LANES


def _flash_unit(s, v_t, m_ref, acc_ref, unit):
    m_old = m_ref[unit]
    m_new = jnp.maximum(m_old, jnp.max(s, axis=0, keepdims=True))
    alpha = jnp.exp2(m_old - m_new)
    p = jnp.exp2(s - m_new).astype(jnp.bfloat16)
    acc_ref[unit] = alpha * acc_ref[unit] + jnp.dot(v_t, p, preferred_element_type=jnp.float32)
    m_ref[unit] = m_new


def _flash_all_units(q_ref, k_ref, vt_ref, m_ref, acc_ref, *, n_units, units_per_group, tk, n_keys):
    m_ref[...] = jnp.full(m_ref.shape, NEG_BIG, jnp.float32)
    acc_ref[...] = jnp.zeros(acc_ref.shape, jnp.float32)

    def body(j, carry):
        off = pl.multiple_of(j * tk, tk)
        k = k_ref[0, pl.ds(off, tk), :]

        def scores(u):
            return jnp.dot(k, q_ref[0, u], preferred_element_type=jnp.float32)

        s = scores(0)
        for u in range(n_units):
            s_next = scores(u + 1) if u + 1 < n_units else None
            _flash_unit(s, vt_ref[0, u // units_per_group, :, pl.ds(off, tk)], m_ref, acc_ref, u)
            s = s_next
        return carry

    lax.fori_loop(0, n_keys // tk, body, 0)


def _gqa_kernel(q_ref, k_ref, vt_ref, o_ref, m_ref, acc_ref, *, tk, n_keys):
    _flash_all_units(q_ref, k_ref, vt_ref, m_ref, acc_ref,
                     n_units=A_HEADS, units_per_group=A_GROUP, tk=tk, n_keys=n_keys)
    for h in range(A_HEADS):
        acc = acc_ref[h]
        o_ref[0, h * HEAD_DIM:(h + 1) * HEAD_DIM, :] = (
            acc[:HEAD_DIM] / acc[HEAD_DIM:HEAD_DIM + 1]).astype(o_ref.dtype)


def _diff_kernel(lam_ref, q_ref, k_ref, vt_ref, g_ref, o_ref, m_ref, acc_ref, *, tk, n_keys, out_scale):
    _flash_all_units(q_ref, k_ref, vt_ref, m_ref, acc_ref,
                     n_units=4, units_per_group=2, tk=tk, n_keys=n_keys)
    lam = lam_ref[0]
    for h in range(2):
        a1 = acc_ref[2 * h]
        a2 = acc_ref[2 * h + 1]
        o = a1[:B_VDIM] / a1[B_VDIM:B_VDIM + 1] - lam * (a2[:B_VDIM] / a2[B_VDIM:B_VDIM + 1])
        o = o * lax.rsqrt(jnp.mean(o * o, axis=0, keepdims=True) + EPS)
        o_ref[0, h * B_VDIM:(h + 1) * B_VDIM, :] = (o * g_ref[...] * out_scale).astype(o_ref.dtype)


def _gqa_attention(q_t, k, v_t):
    B, _, _, L = q_t.shape
    S = k.shape[1]
    tq, tk = min(TQ, L), min(ATTN_TK, S)
    return pl.pallas_call(
        functools.partial(_gqa_kernel, tk=tk, n_keys=S),
        grid=(B, L // tq),
        in_specs=[
            pl.BlockSpec((1, A_HEADS, K_PACK, tq), lambda b, i: (b, 0, 0, i)),
            pl.BlockSpec((1, S, K_PACK), lambda b, i: (b, 0, 0)),
            pl.BlockSpec((1, A_KV_HEADS, V_EXT, S), lambda b, i: (b, 0, 0, 0)),
        ],
        out_specs=pl.BlockSpec((1, A_Q, tq), lambda b, i: (b, 0, i)),
        out_shape=jax.ShapeDtypeStruct((B, A_Q, L), jnp.bfloat16),
        scratch_shapes=[pltpu.VMEM((A_HEADS, 1, tq), jnp.float32),
                        pltpu.VMEM((A_HEADS, V_EXT, tq), jnp.float32)],
        compiler_params=pltpu.CompilerParams(
            dimension_semantics=("arbitrary", "arbitrary"), vmem_limit_bytes=VMEM_LIMIT),
        name="gqa_attention",
    )(q_t, k, v_t)


def _diff_attention(q_t, k, v_t, lam, subln_g, out_scale):
    B, _, _, L = q_t.shape
    S = k.shape[1]
    tq, tk = min(TQ, L), min(ATTN_TK, S)
    return pl.pallas_call(
        functools.partial(_diff_kernel, tk=tk, n_keys=S, out_scale=out_scale),
        grid=(B, B_HEADS // 2, L // tq),
        in_specs=[
            pl.BlockSpec(memory_space=pltpu.SMEM),
            pl.BlockSpec((1, 4, K_PACK, tq), lambda b, h, i: (b, h, 0, i)),
            pl.BlockSpec((1, S, K_PACK), lambda b, h, i: (b, 0, h)),
            pl.BlockSpec((1, 2, V_EXT, S), lambda b, h, i: (b, h, 0, 0)),
            pl.BlockSpec((B_VDIM, 1), lambda b, h, i: (0, 0)),
        ],
        out_specs=pl.BlockSpec((1, 2 * B_VDIM, tq), lambda b, h, i: (b, h, i)),
        out_shape=jax.ShapeDtypeStruct((B, B_V, L), jnp.bfloat16),
        scratch_shapes=[pltpu.VMEM((4, 1, tq), jnp.float32),
                        pltpu.VMEM((4, V_EXT, tq), jnp.float32)],
        compiler_params=pltpu.CompilerParams(
            dimension_semantics=("arbitrary", "arbitrary", "arbitrary"), vmem_limit_bytes=VMEM_LIMIT),
        name="diff_attention",
    )(lam.reshape(1).astype(jnp.float32), q_t, k, v_t, subln_g.reshape(B_VDIM, 1).astype(jnp.float32))


def _value_ext_t(v):
    B, S, H, d = v.shape
    ones = jnp.ones((B, S, H, 1), v.dtype)
    pad = jnp.zeros((B, S, H, V_EXT - d - 1), v.dtype)
    return jnp.transpose(jnp.concatenate([v, ones, pad], axis=-1), (0, 2, 3, 1)).astype(jnp.bfloat16)


def _query_ext_t(q, offsets):
    B, L, U, d = q.shape
    q_t = jnp.transpose(q, (0, 2, 3, 1)).astype(jnp.bfloat16)
    units = [jnp.pad(q_t[:, u], ((0, 0), (off, K_PACK - d - off), (0, 0))) for u, off in enumerate(offsets)]
    return jnp.stack(units, axis=1)


def _rms(x, g):
    xf = x.astype(jnp.float32)
    y = xf * lax.rsqrt(jnp.mean(xf * xf, axis=-1, keepdims=True) + EPS)
    return (y * g.astype(jnp.float32)).astype(x.dtype)


def _rope_2d(x, row, col):
    d = x.shape[-1]
    half = d // 2
    inv = 1.0 / (ROPE_THETA ** (jnp.arange(0, half, 2, dtype=jnp.float32) / half))
    ang = jnp.concatenate([row.astype(jnp.float32)[:, None] * inv[None, :],
                           col.astype(jnp.float32)[:, None] * inv[None, :]], axis=-1)
    cos = jnp.cos(ang)[None, :, None, :]
    sin = jnp.sin(ang)[None, :, None, :]
    xr = x.astype(jnp.float32).reshape(x.shape[:-1] + (half, 2))
    x0, x1 = xr[..., 0], xr[..., 1]
    out = jnp.stack([x0 * cos - x1 * sin, x0 * sin + x1 * cos], axis=-1)
    return out.reshape(x.shape).astype(x.dtype)


PROJ_TM = 512


S5_TC = 64
S5_ROW_TILE = 16
S5_CW = S5_TC * C_GROUP_CH
S5_SW = 2 * C_STATE


def _s5_operators(a_re, a_im, log_dt, b_re, b_im, c_re, c_im):
    f32 = jnp.float32
    tc, P, N, G = S5_TC, C_GROUP_CH, C_STATE, C_GROUPS
    j = jnp.arange(tc + 1, dtype=f32)
    kerns, p_parts, q_parts, a_parts = [], [], [], []
    for dirn in range(2):
        A = lax.complex(a_re[dirn].astype(f32), a_im[dirn].astype(f32))
        adt = A * jnp.exp(log_dt[dirn].astype(f32))[:, None]
        abar = jnp.exp(adt)
        bbar = ((abar - 1) / A)[..., None] * lax.complex(b_re[dirn].astype(f32), b_im[dirn].astype(f32))
        cmat = lax.complex(c_re[dirn].astype(f32), c_im[dirn].astype(f32))
        pw = jnp.exp(adt[None] * j[:, None, None])
        kerns.append(jnp.einsum('gpn,jgn,gnq->gqjp', cmat, pw[:tc], bbar).real)
        inj_pw = pw[:tc] if dirn == 1 else pw[:tc][::-1]
        inj = inj_pw[:, :, :, None] * bbar[None]
        p_parts.append(jnp.transpose(inj, (1, 0, 3, 2)).reshape(G, S5_CW, N))
        out_pw = pw[1:][::-1] if dirn == 1 else pw[1:]
        outm = cmat[None] * out_pw[:, :, None, :]
        q_parts.append(jnp.transpose(outm, (1, 3, 0, 2)).reshape(G, N, S5_CW))
        a_parts.append(pw[tc])
    f_seq = jnp.concatenate([jnp.flip(kerns[1][:, :, 1:], axis=2), kerns[0][:, :, :1] + kerns[1][:, :, :1],
                             kerns[0][:, :, 1:], jnp.zeros((G, P, 1, P), f32)], axis=2)
    rolled = jnp.tile(f_seq.astype(jnp.bfloat16).reshape(G, P, 2 * tc * P), (1, 1, tc))
    rolled = rolled[:, :, :tc * (2 * tc - 1) * P].reshape(G, P, tc, (2 * tc - 1) * P)
    toep = jnp.transpose(rolled[:, :, :, (tc - 1) * P:], (0, 2, 1, 3)).reshape(G, S5_CW, S5_CW)
    zeros = jnp.zeros((G, N, S5_CW), jnp.bfloat16)
    bf = lambda m: m.astype(jnp.bfloat16)
    w_mat = jnp.concatenate([
        toep,
        bf(q_parts[0].real), zeros, bf(-q_parts[0].imag), zeros,
        zeros, bf(q_parts[1].real), zeros, bf(-q_parts[1].imag),
    ], axis=1)
    p_mat = jnp.concatenate([p_parts[0].real, p_parts[1].real, p_parts[0].imag, p_parts[1].imag], axis=-1)
    a_chunk = jnp.stack([jnp.concatenate([a_parts[0].real, a_parts[1].real], axis=-1),
                         jnp.concatenate([a_parts[0].imag, a_parts[1].imag], axis=-1)], axis=1)
    return p_mat.astype(jnp.bfloat16), w_mat, a_chunk.astype(f32)


def _s5_scan_kernel(u_ref, p_ref, w_ref, a_ref, y_ref, s_scr, hs_scr, *, rows, order_f, order_r):
    f32 = jnp.float32
    u = u_ref[0]
    s_scr[...] = jnp.dot(u, p_ref[0], preferred_element_type=f32)
    hs_scr[...] = jnp.zeros(hs_scr.shape, f32)
    a_re = a_ref[0, 0:1, :]
    a_im = a_ref[0, 1:2, :]
    is_fwd = lax.broadcasted_iota(jnp.int32, (rows, S5_SW), 1) < C_STATE
    h_re = jnp.zeros((rows, S5_SW), f32)
    h_im = jnp.zeros((rows, S5_SW), f32)
    for cf, cr in zip(order_f, order_r):
        rf = slice(cf * rows, (cf + 1) * rows)
        rr = slice(cr * rows, (cr + 1) * rows)
        hs_scr[rf, 0:S5_SW] = h_re
        hs_scr[rf, S5_SW:2 * S5_SW] = h_im
        hs_scr[rr, 2 * S5_SW:3 * S5_SW] = h_re
        hs_scr[rr, 3 * S5_SW:4 * S5_SW] = h_im
        s_re = jnp.where(is_fwd, s_scr[rf, 0:S5_SW], s_scr[rr, 0:S5_SW])
        s_im = jnp.where(is_fwd, s_scr[rf, S5_SW:2 * S5_SW], s_scr[rr, S5_SW:2 * S5_SW])
        h_re, h_im = a_re * h_re - a_im * h_im + s_re, a_re * h_im + a_im * h_re + s_im
    y = jnp.dot(u, w_ref[0, 0:S5_CW, :], preferred_element_type=f32)
    y += jnp.dot(hs_scr[...].astype(jnp.bfloat16), w_ref[0, S5_CW:, :], preferred_element_type=f32)
    y_ref[0] = y


def _s5_scan(u_seq, n_ctx_chunks, p_mat, w_mat, a_chunk):
    B, S, _ = u_seq.shape
    G, P = C_GROUPS, C_GROUP_CH
    nc = S // S5_TC
    R = -(-(nc * B) // S5_ROW_TILE) * S5_ROW_TILE
    ug = jnp.transpose(u_seq.reshape(B, nc, S5_TC, G, P), (3, 1, 0, 2, 4))
    ug = jnp.pad(ug.reshape(G, nc * B, S5_CW), ((0, 0), (0, R - nc * B), (0, 0))).astype(jnp.bfloat16)
    order_f = tuple(range(nc))
    order_r = tuple(range(n_ctx_chunks - 1, -1, -1)) + tuple(range(nc - 1, n_ctx_chunks - 1, -1))
    y = pl.pallas_call(
        functools.partial(_s5_scan_kernel, rows=B, order_f=order_f, order_r=order_r),
        grid=(G,),
        in_specs=[
            pl.BlockSpec((1, R, S5_CW), lambda g: (g, 0, 0)),
            pl.BlockSpec((1, S5_CW, 2 * S5_SW), lambda g: (g, 0, 0)),
            pl.BlockSpec((1, S5_CW + 4 * S5_SW, S5_CW), lambda g: (g, 0, 0)),
            pl.BlockSpec((1, 2, S5_SW), lambda g: (g, 0, 0)),
        ],
        out_specs=pl.BlockSpec((1, R, S5_CW), lambda g: (g, 0, 0)),
        out_shape=jax.ShapeDtypeStruct((G, R, S5_CW), jnp.float32),
        scratch_shapes=[pltpu.VMEM((R, 2 * S5_SW), jnp.float32),
                        pltpu.VMEM((R, 4 * S5_SW), jnp.float32)],
        compiler_params=pltpu.CompilerParams(dimension_semantics=("arbitrary",),
                                             vmem_limit_bytes=VMEM_LIMIT),
        name="s5_scan",
    )(ug, p_mat, w_mat, a_chunk)
    y = y[:, :nc * B].reshape(G, nc, B, S5_TC, P)
    return jnp.transpose(y, (2, 1, 3, 0, 4)).reshape(B, S, C_WIDTH)


def _s5_glu_kernel(y_ref, u_ref, d_ref, w_ref, o_ref):
    y = y_ref[...] + d_ref[...] * u_ref[...]
    g = jax.nn.gelu(y)
    z = jnp.dot(g.astype(jnp.bfloat16), w_ref[...], preferred_element_type=jnp.float32)
    o_ref[...] = (g * jax.nn.sigmoid(z)).astype(o_ref.dtype)


def _s5_glu(y, u, d_skip, w_glu):
    T = y.shape[0]
    tb = math.gcd(T, 1024)
    return pl.pallas_call(
        _s5_glu_kernel,
        grid=(T // tb,),
        in_specs=[pl.BlockSpec((tb, C_WIDTH), lambda i: (i, 0)),
                  pl.BlockSpec((tb, C_WIDTH), lambda i: (i, 0)),
                  pl.BlockSpec((1, C_WIDTH), lambda i: (0, 0)),
                  pl.BlockSpec((C_WIDTH, C_WIDTH), lambda i: (0, 0))],
        out_specs=pl.BlockSpec((tb, C_WIDTH), lambda i: (i, 0)),
        out_shape=jax.ShapeDtypeStruct((T, C_WIDTH), jnp.float32),
        compiler_params=pltpu.CompilerParams(dimension_semantics=("arbitrary",)),
        name="s5_glu",
    )(y, u, d_skip.reshape(1, C_WIDTH).astype(jnp.float32), w_glu.astype(jnp.bfloat16))


def _s5_mixer(u, uc, s5_ops, d_skip, w_glu, with_ctx):
    B, L, _ = u.shape
    Lc = uc.shape[1]
    p_mat, w_mat, a_chunk = s5_ops
    u_seq = jnp.concatenate([uc, u], axis=1).astype(jnp.float32)
    y = _s5_scan(u_seq, Lc // S5_TC, p_mat, w_mat, a_chunk)
    if with_ctx:
        out = _s5_glu(y.reshape(B * (Lc + L), C_WIDTH), u_seq.reshape(B * (Lc + L), C_WIDTH), d_skip, w_glu)
        out = out.reshape(B, Lc + L, C_WIDTH)
        return out[:, Lc:], out[:, :Lc]
    out = _s5_glu(y[:, Lc:].reshape(B * L, C_WIDTH), u.astype(jnp.float32).reshape(B * L, C_WIDTH), d_skip, w_glu)
    return out.reshape(B, L, C_WIDTH), None


def _in_proj_kernel(x_ref, g_ref, sh_ref, sc_ref, w_ref, o_ref):
    x = x_ref[...]
    y = x * lax.rsqrt(jnp.mean(x * x, axis=-1, keepdims=True) + EPS) * g_ref[...]
    h = (y * (1.0 + sc_ref[0]) + sh_ref[0]).astype(jnp.bfloat16)
    o_ref[...] = jnp.dot(h, w_ref[...], preferred_element_type=jnp.float32)


def _in_proj(x, norm_g, shift, scale, w_bf):
    T, D = x.shape
    N = w_bf.shape[1]
    tm = min(PROJ_TM, T)
    return pl.pallas_call(
        _in_proj_kernel,
        grid=(T // tm,),
        in_specs=[pl.BlockSpec((tm, D), lambda i: (i, 0)),
                  pl.BlockSpec((1, D), lambda i: (0, 0)),
                  pl.BlockSpec((1, 1, D), lambda i: (0, 0, 0)),
                  pl.BlockSpec((1, 1, D), lambda i: (0, 0, 0)),
                  pl.BlockSpec((D, N), lambda i: (0, 0))],
        out_specs=pl.BlockSpec((tm, N), lambda i: (i, 0)),
        out_shape=jax.ShapeDtypeStruct((T, N), jnp.float32),
        compiler_params=pltpu.CompilerParams(dimension_semantics=("arbitrary",), vmem_limit_bytes=VMEM_LIMIT),
        name="in_proj",
    )(x, norm_g.reshape(1, D).astype(jnp.float32), shift, scale, w_bf)


def _out_proj_kernel(m_ref, w_ref, x_ref, g_ref, o_ref):
    o_ref[...] = x_ref[...] + g_ref[0] * jnp.dot(m_ref[...], w_ref[...], preferred_element_type=jnp.float32)


def _out_proj(mix_bf, w_bf, x, gate):
    T, D = x.shape
    K = mix_bf.shape[1]
    tm = min(PROJ_TM, T)
    return pl.pallas_call(
        _out_proj_kernel,
        grid=(T // tm,),
        in_specs=[pl.BlockSpec((tm, K), lambda i: (i, 0)),
                  pl.BlockSpec((K, D), lambda i: (0, 0)),
                  pl.BlockSpec((tm, D), lambda i: (i, 0)),
                  pl.BlockSpec((1, 1, D), lambda i: (0, 0, 0))],
        out_specs=pl.BlockSpec((tm, D), lambda i: (i, 0)),
        out_shape=jax.ShapeDtypeStruct((T, D), jnp.float32),
        compiler_params=pltpu.CompilerParams(dimension_semantics=("arbitrary",), vmem_limit_bytes=VMEM_LIMIT),
        name="out_proj",
    )(mix_bf, w_bf, x, gate)


def _token_mixers(proj, projc, row, col, layer_idx, qn_a, kn_a, qn_b, kn_b,
                  lam_q1, lam_k1, lam_q2, lam_k2, subln_g, s5_ops, d_skip, w_glu, with_ctx):
    B, L, _ = proj.shape
    Lc = projc.shape[1]
    cuts = [A_Q, A_Q + A_KV, A_Q + 2 * A_KV, A_Q + 2 * A_KV + B_QK,
            A_Q + 2 * A_KV + 2 * B_QK, A_Q + 2 * A_KV + 2 * B_QK + B_V]
    qa, ka, va, qb, kb, vb, us = jnp.split(proj, cuts, axis=-1)
    qac, kac, vac, qbc, kbc, vbc, usc = jnp.split(projc, cuts, axis=-1)

    qa = _rope_2d(_rms(qa.reshape(B, L, A_HEADS, HEAD_DIM), qn_a), row, col)
    ka = _rope_2d(_rms(ka.reshape(B, L, A_KV_HEADS, HEAD_DIM), kn_a), row, col)
    va = va.reshape(B, L, A_KV_HEADS, HEAD_DIM)
    kac = _rms(kac.reshape(B, Lc, A_KV_HEADS, HEAD_DIM), kn_a)
    vac = vac.reshape(B, Lc, A_KV_HEADS, HEAD_DIM)
    ka_all = jnp.concatenate([kac, ka], axis=1)
    va_all = jnp.concatenate([vac, va], axis=1)
    qa_t = _query_ext_t(qa * (HEAD_DIM ** -0.5 * LOG2E), [(hq // A_GROUP) * HEAD_DIM for hq in range(A_HEADS)])
    ka_pack = ka_all.reshape(B, Lc + L, A_KV).astype(jnp.bfloat16)
    o_a = jnp.transpose(_gqa_attention(qa_t, ka_pack, _value_ext_t(va_all)), (0, 2, 1))

    lambda_init = 0.8 - 0.6 * math.exp(-0.3 * layer_idx)
    lam = (jnp.exp(jnp.sum(lam_q1.astype(jnp.float32) * lam_k1.astype(jnp.float32)))
           - jnp.exp(jnp.sum(lam_q2.astype(jnp.float32) * lam_k2.astype(jnp.float32))) + lambda_init)
    qb = _rope_2d(_rms(qb.reshape(B, L, 2 * B_HEADS, B_SUB), qn_b), row, col)
    kb = _rope_2d(_rms(kb.reshape(B, L, 2 * B_HEADS, B_SUB), kn_b), row, col)
    vb = vb.reshape(B, L, B_HEADS, B_VDIM)
    kbc = _rms(kbc.reshape(B, Lc, 2 * B_HEADS, B_SUB), kn_b)
    vbc = vbc.reshape(B, Lc, B_HEADS, B_VDIM)
    kb_all = jnp.concatenate([kbc, kb], axis=1)
    vb_all = jnp.concatenate([vbc, vb], axis=1)
    qb_t = _query_ext_t(qb * (B_SUB ** -0.5 * LOG2E), [(j % 4) * B_SUB for j in range(2 * B_HEADS)])
    kb_pack = kb_all.reshape(B, Lc + L, B_QK).astype(jnp.bfloat16)
    o_b = jnp.transpose(_diff_attention(qb_t, kb_pack, _value_ext_t(vb_all), lam, subln_g, 1 - lambda_init),
                        (0, 2, 1))

    o_c, o_cc = _s5_mixer(us, usc, s5_ops, d_skip, w_glu, with_ctx)

    mix = jnp.concatenate([o_a, o_b, o_c.astype(jnp.bfloat16)], axis=-1)
    if not with_ctx:
        return mix, None
    qac = _rms(qac.reshape(B, Lc, A_HEADS, HEAD_DIM), qn_a)
    qac_t = _query_ext_t(qac * (HEAD_DIM ** -0.5 * LOG2E), [(hq // A_GROUP) * HEAD_DIM for hq in range(A_HEADS)])
    o_ac = jnp.transpose(_gqa_attention(qac_t, kac.reshape(B, Lc, A_KV).astype(jnp.bfloat16), _value_ext_t(vac)),
                         (0, 2, 1))
    qbc = _rms(qbc.reshape(B, Lc, 2 * B_HEADS, B_SUB), qn_b)
    qbc_t = _query_ext_t(qbc * (B_SUB ** -0.5 * LOG2E), [(j % 4) * B_SUB for j in range(2 * B_HEADS)])
    o_bc = jnp.transpose(_diff_attention(qbc_t, kbc.reshape(B, Lc, B_QK).astype(jnp.bfloat16), _value_ext_t(vbc),
                                         lam, subln_g, 1 - lambda_init), (0, 2, 1))
    mixc = jnp.concatenate([o_ac, o_bc, o_cc.astype(jnp.bfloat16)], axis=-1)
    return mix, mixc


PEER_TB = 256
N_SLOTS = PEER_HEADS * PEER_TOPK
CAND_ROWS = 80
INVALID_FLAT = 1.0e9
SC_CORES = 2
SC_SUBCORES = 16


def _cand_flat_table():
    f = [float(b) for b in range(16)]
    for a in range(1, 8):
        f += [float(a * 16 + b) if (a + 1) * (b + 1) <= 16 else INVALID_FLAT for b in range(8)]
    f += [float(a * 16) for a in range(8, 16)]
    return jnp.broadcast_to(jnp.asarray(f, jnp.float32)[:, None], (CAND_ROWS, PEER_TB))


def _pair_rows(first, second):
    blocks = [first[0:1] + second]
    blocks += [first[a:a + 1] + second[0:8] for a in range(1, 8)]
    blocks += [first[8:16] + second[0:1]]
    return jnp.concatenate(blocks, axis=0)


N_EXPERTS = N_KEYS * N_KEYS


def _peer_route_kernel(x_ref, g_ref, sh_ref, sc_ref, wq_ref, keys_ref, ftab_ref, ut_ref,
                       idx_ref, gate_ref, s_ref, h_scr, q_scr, val_scr, id_scr, sc_scr, e_scr):
    f32 = jnp.float32
    head = pl.program_id(1)

    @pl.when(head == 0)
    def _():
        x = x_ref[...]
        y = x * lax.rsqrt(jnp.mean(x * x, axis=-1, keepdims=True) + EPS) * g_ref[...]
        h2 = (y * (1.0 + sc_ref[0]) + sh_ref[0]).astype(jnp.bfloat16)
        h_scr[...] = h2
        q = jnp.dot(h2, wq_ref[...], preferred_element_type=f32)
        for ht in range(2 * PEER_HEADS):
            q_scr[ht] = q[:, ht * PEER_HALF:(ht + 1) * PEER_HALF].astype(jnp.bfloat16)

    sub_scores = [lax.dot_general(keys_ref[head * 2 + t], q_scr[head * 2 + t], (((1,), (1,)), ((), ())),
                                  preferred_element_type=f32) for t in range(2)]
    s_ref[...] = jnp.dot(h_scr[...], ut_ref[...], preferred_element_type=f32)

    ftab = ftab_ref[...]
    key_id = lax.broadcasted_iota(jnp.int32, (N_KEYS, PEER_TB), 0).astype(f32)
    neg_inf = -jnp.inf
    for t in range(2):
        s = sub_scores[t]
        for r in range(PEER_TOPK):
            m = jnp.max(s, axis=0, keepdims=True)
            pick = jnp.min(jnp.where(s == m, key_id, float(N_KEYS)), axis=0, keepdims=True)
            s = jnp.where(key_id == pick, neg_inf, s)
            val_scr[t, r:r + 1, :] = m
            id_scr[t, r:r + 1, :] = pick
    cand = _pair_rows(val_scr[0], val_scr[1])
    cand = jnp.where(ftab < float(PEER_TOPK * PEER_TOPK), cand, neg_inf)
    expert = _pair_rows(id_scr[0] * float(N_KEYS), id_scr[1])
    for r in range(PEER_TOPK):
        m = jnp.max(cand, axis=0, keepdims=True)
        pick = jnp.min(jnp.where(cand == m, ftab, INVALID_FLAT), axis=0, keepdims=True)
        hit = ftab == pick
        sc_scr[r:r + 1, :] = m
        e_scr[r:r + 1, :] = jnp.sum(jnp.where(hit, expert, 0.0), axis=0, keepdims=True)
        cand = jnp.where(hit, neg_inf, cand)
    sc = sc_scr[...]
    p = jnp.exp(sc - sc[0:1])
    rows = pl.ds(pl.multiple_of(head * PEER_TOPK, PEER_TOPK), PEER_TOPK)
    idx_ref[rows, :] = e_scr[...].astype(jnp.int32)
    gate_ref[rows, :] = p / jnp.sum(p, axis=0, keepdims=True)


def _peer_route(x, norm_g, shift, scale, wq_bf, keys_bf, u_t, tokens_per_mod):
    T, D = x.shape
    nq = wq_bf.shape[1]
    te = N_EXPERTS // PEER_HEADS
    blocks_per_mod = tokens_per_mod // PEER_TB
    return pl.pallas_call(
        _peer_route_kernel,
        grid=(T // PEER_TB, PEER_HEADS),
        in_specs=[
            pl.BlockSpec((PEER_TB, D), lambda i, j: (i, 0)),
            pl.BlockSpec((1, D), lambda i, j: (0, 0)),
            pl.BlockSpec((1, 1, D), lambda i, j: (i // blocks_per_mod, 0, 0)),
            pl.BlockSpec((1, 1, D), lambda i, j: (i // blocks_per_mod, 0, 0)),
            pl.BlockSpec((D, nq), lambda i, j: (0, 0)),
            pl.BlockSpec((2 * PEER_HEADS, N_KEYS, PEER_HALF), lambda i, j: (0, 0, 0)),
            pl.BlockSpec((CAND_ROWS, PEER_TB), lambda i, j: (0, 0)),
            pl.BlockSpec((D, te), lambda i, j: (0, j)),
        ],
        out_specs=[
            pl.BlockSpec((N_SLOTS, PEER_TB), lambda i, j: (0, i)),
            pl.BlockSpec((N_SLOTS, PEER_TB), lambda i, j: (0, i)),
            pl.BlockSpec((PEER_TB, te), lambda i, j: (i, j)),
        ],
        out_shape=[
            jax.ShapeDtypeStruct((N_SLOTS, T), jnp.int32),
            jax.ShapeDtypeStruct((N_SLOTS, T), jnp.float32),
            jax.ShapeDtypeStruct((T, N_EXPERTS), jnp.float32),
        ],
        scratch_shapes=[pltpu.VMEM((PEER_TB, D), jnp.bfloat16),
                        pltpu.VMEM((2 * PEER_HEADS, PEER_TB, PEER_HALF), jnp.bfloat16),
                        pltpu.VMEM((2, PEER_TOPK, PEER_TB), jnp.float32),
                        pltpu.VMEM((2, PEER_TOPK, PEER_TB), jnp.float32),
                        pltpu.VMEM((PEER_TOPK, PEER_TB), jnp.float32),
                        pltpu.VMEM((PEER_TOPK, PEER_TB), jnp.float32)],
        compiler_params=pltpu.CompilerParams(dimension_semantics=("arbitrary", "arbitrary"),
                                             vmem_limit_bytes=VMEM_LIMIT),
        name="peer_route",
    )(x, norm_g.reshape(1, D).astype(jnp.float32), shift, scale, wq_bf, keys_bf, _cand_flat_table(), u_t)
SC_LANES = 16
SC_TOKENS = 32
DENSE_TM = 512
DENSE_TE = 2048


def _sc_worker_tokens(n_tokens):
    per_worker = n_tokens // (SC_CORES * SC_SUBCORES)
    blk = min(SC_TOKENS, per_worker)
    return per_worker, blk


def _sc_extract(scores, idx_flat):
    T = scores.shape[0]
    per_worker, blk = _sc_worker_tokens(T)
    mesh = plsc.VectorSubcoreMesh(core_axis_name="c", subcore_axis_name="s")

    def body(s_hbm, idx_hbm, out_hbm, row0_v, row1_v, idx_v, out_v, rsem):
        rows = (row0_v, row1_v)
        wid = lax.axis_index("s") * SC_CORES + lax.axis_index("c")
        t0 = wid * per_worker

        def row_copy(tok, slot):
            return pltpu.make_async_copy(s_hbm.at[tok], rows[slot], rsem.at[slot])

        def block(b, carry):
            tb = t0 + b * blk
            pltpu.sync_copy(idx_hbm.at[pl.ds(tb * N_SLOTS, blk * N_SLOTS)], idx_v)
            row_copy(tb, 0).start()

            def pair(i2, c):
                for slot in range(2):
                    i = i2 * 2 + slot

                    @pl.when(i + 1 < blk)
                    def _():
                        row_copy(tb + i + 1, 1 - slot).start()

                    row_copy(tb + i, slot).wait()
                    for j in range(N_SLOTS // SC_LANES):
                        at = pl.ds(i * N_SLOTS + j * SC_LANES, SC_LANES)
                        out_v[at] = plsc.load_gather(rows[slot], [idx_v[at]])
                return c

            lax.fori_loop(0, blk // 2, pair, 0)
            pltpu.sync_copy(out_v, out_hbm.at[pl.ds(tb * N_SLOTS, blk * N_SLOTS)])
            return carry

        lax.fori_loop(0, per_worker // blk, block, 0)

    return pl.kernel(
        body, mesh=mesh,
        out_type=jax.ShapeDtypeStruct((T * N_SLOTS,), jnp.float32),
        scratch_types=[pltpu.VMEM((N_EXPERTS,), jnp.float32), pltpu.VMEM((N_EXPERTS,), jnp.float32),
                       pltpu.VMEM((blk * N_SLOTS,), jnp.int32), pltpu.VMEM((blk * N_SLOTS,), jnp.float32),
                       pltpu.SemaphoreType.DMA((2,))],
        compiler_params=pltpu.CompilerParams(needs_layout_passes=False),
        name="peer_extract",
    )(scores, idx_flat)


def _sc_scatter(w_flat, idx_flat, n_tokens):
    T = n_tokens
    per_worker, blk = _sc_worker_tokens(T)
    n_vec = N_SLOTS // SC_LANES
    mesh = plsc.VectorSubcoreMesh(core_axis_name="c", subcore_axis_name="s")

    def body(w_hbm, idx_hbm, out_hbm, row0_v, row1_v, idx_v, w_v, wsem):
        rows = (row0_v, row1_v)
        wid = lax.axis_index("s") * SC_CORES + lax.axis_index("c")
        t0 = wid * per_worker
        zero = jnp.zeros((SC_LANES,), jnp.float32)

        def zero_fill(k, c):
            for slot in range(2):
                rows[slot][pl.ds(k * SC_LANES, SC_LANES)] = zero
            return c

        lax.fori_loop(0, N_EXPERTS // SC_LANES, zero_fill, 0)

        def out_copy(tok, slot):
            return pltpu.make_async_copy(rows[slot], out_hbm.at[tok], wsem.at[slot])

        def retire(tok, i, slot):
            out_copy(tok, slot).wait()
            for j in range(n_vec):
                plsc.store_scatter(rows[slot], [idx_v[pl.ds(i * N_SLOTS + j * SC_LANES, SC_LANES)]], zero)

        def block(b, carry):
            tb = t0 + b * blk
            pltpu.sync_copy(idx_hbm.at[pl.ds(tb * N_SLOTS, blk * N_SLOTS)], idx_v)
            pltpu.sync_copy(w_hbm.at[pl.ds(tb * N_SLOTS, blk * N_SLOTS)], w_v)

            def pair(i2, c):
                for slot in range(2):
                    i = i2 * 2 + slot

                    @pl.when(i >= 2)
                    def _():
                        retire(tb + i - 2, i - 2, slot)

                    for j in range(n_vec):
                        at = pl.ds(i * N_SLOTS + j * SC_LANES, SC_LANES)
                        plsc.addupdate_scatter(rows[slot], [idx_v[at]], w_v[at])
                    out_copy(tb + i, slot).start()
                return c

            lax.fori_loop(0, blk // 2, pair, 0)
            for slot in range(2):
                retire(tb + blk - 2 + slot, blk - 2 + slot, slot)
            return carry

        lax.fori_loop(0, per_worker // blk, block, 0)

    return pl.kernel(
        body, mesh=mesh,
        out_type=jax.ShapeDtypeStruct((T, N_EXPERTS), jnp.float32),
        scratch_types=[pltpu.VMEM((N_EXPERTS,), jnp.float32), pltpu.VMEM((N_EXPERTS,), jnp.float32),
                       pltpu.VMEM((blk * N_SLOTS,), jnp.int32), pltpu.VMEM((blk * N_SLOTS,), jnp.float32),
                       pltpu.SemaphoreType.DMA((2,))],
        compiler_params=pltpu.CompilerParams(needs_layout_passes=False),
        name="peer_scatter",
    )(w_flat, idx_flat)


def _peer_gate_kernel(a_ref, g_ref, w_ref):
    w_ref[...] = jax.nn.gelu(a_ref[...]) * g_ref[...]


def _peer_gate(act, gate):
    T = act.shape[0]
    tb = min(2048, T)
    spec = pl.BlockSpec((tb, N_SLOTS), lambda i: (i, 0))
    return pl.pallas_call(
        _peer_gate_kernel, grid=(T // tb,), in_specs=[spec, spec], out_specs=spec,
        out_shape=jax.ShapeDtypeStruct((T, N_SLOTS), jnp.float32),
        compiler_params=pltpu.CompilerParams(dimension_semantics=("arbitrary",)),
        name="peer_gate",
    )(act, gate)


def _peer_combine_kernel(w_ref, v_ref, x_ref, g2_ref, o_ref, acc_ref):
    k = pl.program_id(1)

    @pl.when(k == 0)
    def _():
        acc_ref[...] = jnp.zeros(acc_ref.shape, jnp.float32)

    acc_ref[...] += jnp.dot(w_ref[...].astype(jnp.bfloat16), v_ref[...], preferred_element_type=jnp.float32)

    @pl.when(k == pl.num_programs(1) - 1)
    def _():
        o_ref[...] = x_ref[...] + g2_ref[0] * acc_ref[...]


def _peer_combine(w_dense, v_bf, x, g2, tokens_per_mod):
    T, D = x.shape
    tm = min(DENSE_TM, T)
    blocks_per_mod = tokens_per_mod // tm
    return pl.pallas_call(
        _peer_combine_kernel,
        grid=(T // tm, N_EXPERTS // DENSE_TE),
        in_specs=[pl.BlockSpec((tm, DENSE_TE), lambda i, k: (i, k)),
                  pl.BlockSpec((DENSE_TE, D), lambda i, k: (k, 0)),
                  pl.BlockSpec((tm, D), lambda i, k: (i, 0)),
                  pl.BlockSpec((1, 1, D), lambda i, k: (i // blocks_per_mod, 0, 0))],
        out_specs=pl.BlockSpec((tm, D), lambda i, k: (i, 0)),
        out_shape=jax.ShapeDtypeStruct((T, D), jnp.float32),
        scratch_shapes=[pltpu.VMEM((tm, D), jnp.float32)],
        compiler_params=pltpu.CompilerParams(dimension_semantics=("arbitrary", "arbitrary"),
                                             vmem_limit_bytes=VMEM_LIMIT),
        name="peer_combine",
    )(w_dense, v_bf, x, g2)


def _peer_residual(x, norm_g, shift, scale, gate2, wq_bf, keys_bf, u_t, v_bf, tokens_per_mod):
    T, D = x.shape
    idx_t, gate_t, scores = _peer_route(x, norm_g, shift, scale, wq_bf, keys_bf, u_t, tokens_per_mod)
    idx_flat = jnp.transpose(idx_t).reshape(T * N_SLOTS)
    act = _sc_extract(scores, idx_flat).reshape(T, N_SLOTS)
    w = _peer_gate(act, jnp.transpose(gate_t))
    w_dense = _sc_scatter(w.reshape(T * N_SLOTS), idx_flat, T)
    return _peer_combine(w_dense, v_bf, x, gate2, tokens_per_mod)


def kernel(x, c, ctx, c_ctx, w_ada, b_ada, norm1_g, norm2_g, w_in, w_out,
           qn_a, kn_a, qn_b, kn_b, lam_q1, lam_k1, lam_q2, lam_k2, subln_g,
           s5_a_re, s5_a_im, s5_log_dt, s5_b_re, s5_b_im, s5_c_re, s5_c_im,
           s5_d, s5_w_glu, peer_wq, peer_keys, peer_u, peer_v):
    B, L, D = x.shape
    Lc = ctx.shape[1]
    depth = w_in.shape[0]
    rows = L // GRID_W
    row = jnp.repeat(jnp.arange(rows, dtype=jnp.int32), GRID_W)
    col = jnp.tile(jnp.arange(GRID_W, dtype=jnp.int32), rows)
    s_c = jax.nn.silu(c)
    s_cc = jax.nn.silu(c_ctx)
    mods, cmods, s5_ops, wq_bf, keys_bf, u_t, v_bf = [], [], [], [], [], [], []
    w_in_bf = w_in.astype(jnp.bfloat16)
    w_out_bf = w_out.astype(jnp.bfloat16)
    for l in range(depth):
        mods.append(jnp.split((s_c @ w_ada[l] + b_ada[l])[:, None, :], 6, axis=-1))
        cmods.append([m.reshape(1, 1, D) for m in jnp.split(s_cc @ w_ada[l] + b_ada[l], 6, axis=-1)])
        s5_ops.append(_s5_operators(s5_a_re[l], s5_a_im[l], s5_log_dt[l], s5_b_re[l], s5_b_im[l],
                                    s5_c_re[l], s5_c_im[l]))
        wq_bf.append(peer_wq[l].astype(jnp.bfloat16))
        keys_bf.append(peer_keys[l].reshape(2 * PEER_HEADS, N_KEYS, PEER_HALF).astype(jnp.bfloat16))
        u_t.append(jnp.transpose(peer_u[l].astype(jnp.bfloat16)))
        v_bf.append(peer_v[l].astype(jnp.bfloat16))

    outs = []
    for b in range(B):
        xb = x[b:b + 1]
        cb = ctx[b:b + 1]
        for l in range(depth):
            with_ctx = l < depth - 1
            sh1, sc1, g1, sh2, sc2, g2 = [m[b:b + 1] for m in mods[l]]
            csh1, csc1, cg1, csh2, csc2, cg2 = cmods[l]
            proj = _in_proj(xb.reshape(L, D), norm1_g[l], sh1, sc1, w_in_bf[l]).reshape(1, L, -1)
            projc = _in_proj(cb.reshape(Lc, D), norm1_g[l], csh1, csc1, w_in_bf[l]).reshape(1, Lc, -1)
            mix, mixc = _token_mixers(proj, projc, row, col, l, qn_a[l], kn_a[l], qn_b[l], kn_b[l],
                                      lam_q1[l], lam_k1[l], lam_q2[l], lam_k2[l], subln_g[l],
                                      s5_ops[l], s5_d[l], s5_w_glu[l], with_ctx)
            x2 = _out_proj(mix.reshape(L, -1), w_out_bf[l], xb.reshape(L, D), g1)
            xb = _peer_residual(x2, norm2_g[l], sh2, sc2, g2, wq_bf[l], keys_bf[l], u_t[l], v_bf[l],
                                tokens_per_mod=L).reshape(1, L, D)
            if with_ctx:
                c2 = _out_proj(mixc.reshape(Lc, -1), w_out_bf[l], cb.reshape(Lc, D), cg1)
                cb = _peer_residual(c2, norm2_g[l], csh2, csc2, cg2, wq_bf[l], keys_bf[l],
                                    u_t[l], v_bf[l], tokens_per_mod=Lc).reshape(1, Lc, D)
        outs.append(xb)
    return jnp.concatenate(outs, axis=0)
```

```python
import functools
import math

import jax
import jax.numpy as jnp
from jax import lax
from jax.experimental import pallas as pl
from jax.experimental.pallas import tpu as pltpu
from jax.experimental.pallas import tpu_sc as plsc

D_MODEL = 1024
GRID_W = 64
EPS = 1e-6
ROPE_THETA = 10000.0
HEAD_DIM = 64
A_HEADS = 6
A_KV_HEADS = 2
A_GROUP = A_HEADS // A_KV_HEADS
A_Q = A_HEADS * HEAD_DIM
A_KV = A_KV_HEADS * HEAD_DIM
B_HEADS = 6
B_SUB = 32
B_VDIM = 2 * B_SUB
B_QK = B_HEADS * 2 * B_SUB
B_V = B_HEADS * B_VDIM
C_GROUPS = 16
C_GROUP_CH = 16
C_WIDTH = C_GROUPS * C_GROUP_CH
C_STATE = 64
PEER_HEADS = 8
N_KEYS = 128
PEER_TOPK = 16
PEER_HALF = 128
TOKEN_CHUNK = 128

LANES = 128
SUBLANES = 8
V_EXT = LANES
LOG2E = math.log2(math.e)
NEG_BIG = -1e30
VMEM_LIMIT = 48 * 1024 * 1024


TQ = 1024
ATTN_TK = 2816
K_PACK = LANES


def _flash_unit(s, v_t, m_ref, acc_ref, unit):
    m_old = m_ref[unit]
    m_new = jnp.maximum(m_old, jnp.max(s, axis=0, keepdims=True))
    alpha = jnp.exp2(m_old - m_new)
    p = jnp.exp2(s - m_new).astype(jnp.bfloat16)
    acc_ref[unit] = alpha * acc_ref[unit] + jnp.dot(v_t, p, preferred_element_type=jnp.float32)
    m_ref[unit] = m_new


def _flash_all_units(q_ref, k_ref, vt_ref, m_ref, acc_ref, *, n_units, units_per_group, tk, n_keys):
    m_ref[...] = jnp.full(m_ref.shape, NEG_BIG, jnp.float32)
    acc_ref[...] = jnp.zeros(acc_ref.shape, jnp.float32)

    def body(j, carry):
        off = pl.multiple_of(j * tk, tk)
        k = k_ref[0, pl.ds(off, tk), :]

        def scores(u):
            return jnp.dot(k, q_ref[0, u], preferred_element_type=jnp.float32)

        s = scores(0)
        for u in range(n_units):
            s_next = scores(u + 1) if u + 1 < n_units else None
            _flash_unit(s, vt_ref[0, u // units_per_group, :, pl.ds(off, tk)], m_ref, acc_ref, u)
            s = s_next
        return carry

    lax.fori_loop(0, n_keys // tk, body, 0)


def _gqa_kernel(q_ref, k_ref, vt_ref, o_ref, m_ref, acc_ref, *, tk, n_keys):
    _flash_all_units(q_ref, k_ref, vt_ref, m_ref, acc_ref,
                     n_units=A_HEADS, units_per_group=A_GROUP, tk=tk, n_keys=n_keys)
    for h in range(A_HEADS):
        acc = acc_ref[h]
        o_ref[0, h * HEAD_DIM:(h + 1) * HEAD_DIM, :] = (
            acc[:HEAD_DIM] / acc[HEAD_DIM:HEAD_DIM + 1]).astype(o_ref.dtype)


def _diff_kernel(lam_ref, q_ref, k_ref, vt_ref, g_ref, o_ref, m_ref, acc_ref, *, tk, n_keys, out_scale):
    _flash_all_units(q_ref, k_ref, vt_ref, m_ref, acc_ref,
                     n_units=4, units_per_group=2, tk=tk, n_keys=n_keys)
    lam = lam_ref[0]
    for h in range(2):
        a1 = acc_ref[2 * h]
        a2 = acc_ref[2 * h + 1]
        o = a1[:B_VDIM] / a1[B_VDIM:B_VDIM + 1] - lam * (a2[:B_VDIM] / a2[B_VDIM:B_VDIM + 1])
        o = o * lax.rsqrt(jnp.mean(o * o, axis=0, keepdims=True) + EPS)
        o_ref[0, h * B_VDIM:(h + 1) * B_VDIM, :] = (o * g_ref[...] * out_scale).astype(o_ref.dtype)


def _gqa_attention(q_t, k, v_t):
    B, _, _, L = q_t.shape
    S = k.shape[1]
    tq, tk = min(TQ, L), min(ATTN_TK, S)
    return pl.pallas_call(
        functools.partial(_gqa_kernel, tk=tk, n_keys=S),
        grid=(B, L // tq),
        in_specs=[
            pl.BlockSpec((1, A_HEADS, K_PACK, tq), lambda b, i: (b, 0, 0, i)),
            pl.BlockSpec((1, S, K_PACK), lambda b, i: (b, 0, 0)),
            pl.BlockSpec((1, A_KV_HEADS, V_EXT, S), lambda b, i: (b, 0, 0, 0)),
        ],
        out_specs=pl.BlockSpec((1, A_Q, tq), lambda b, i: (b, 0, i)),
        out_shape=jax.ShapeDtypeStruct((B, A_Q, L), jnp.bfloat16),
        scratch_shapes=[pltpu.VMEM((A_HEADS, 1, tq), jnp.float32),
                        pltpu.VMEM((A_HEADS, V_EXT, tq), jnp.float32)],
        compiler_params=pltpu.CompilerParams(
            dimension_semantics=("arbitrary", "arbitrary"), vmem_limit_bytes=VMEM_LIMIT),
        name="gqa_attention",
    )(q_t, k, v_t)


def _diff_attention(q_t, k, v_t, lam, subln_g, out_scale):
    B, _, _, L = q_t.shape
    S = k.shape[1]
    tq, tk = min(TQ, L), min(ATTN_TK, S)
    return pl.pallas_call(
        functools.partial(_diff_kernel, tk=tk, n_keys=S, out_scale=out_scale),
        grid=(B, B_HEADS // 2, L // tq),
        in_specs=[
            pl.BlockSpec(memory_space=pltpu.SMEM),
            pl.BlockSpec((1, 4, K_PACK, tq), lambda b, h, i: (b, h, 0, i)),
            pl.BlockSpec((1, S, K_PACK), lambda b, h, i: (b, 0, h)),
            pl.BlockSpec((1, 2, V_EXT, S), lambda b, h, i: (b, h, 0, 0)),
            pl.BlockSpec((B_VDIM, 1), lambda b, h, i: (0, 0)),
        ],
        out_specs=pl.BlockSpec((1, 2 * B_VDIM, tq), lambda b, h, i: (b, h, i)),
        out_shape=jax.ShapeDtypeStruct((B, B_V, L), jnp.bfloat16),
        scratch_shapes=[pltpu.VMEM((4, 1, tq), jnp.float32),
                        pltpu.VMEM((4, V_EXT, tq), jnp.float32)],
        compiler_params=pltpu.CompilerParams(
            dimension_semantics=("arbitrary", "arbitrary", "arbitrary"), vmem_limit_bytes=VMEM_LIMIT),
        name="diff_attention",
    )(lam.reshape(1).astype(jnp.float32), q_t, k, v_t, subln_g.reshape(B_VDIM, 1).astype(jnp.float32))


def _value_ext_t(v):
    B, S, H, d = v.shape
    ones = jnp.ones((B, S, H, 1), v.dtype)
    pad = jnp.zeros((B, S, H, V_EXT - d - 1), v.dtype)
    return jnp.transpose(jnp.concatenate([v, ones, pad], axis=-1), (0, 2, 3, 1)).astype(jnp.bfloat16)


def _query_ext_t(q, offsets):
    B, L, U, d = q.shape
    q_t = jnp.transpose(q, (0, 2, 3, 1)).astype(jnp.bfloat16)
    units = [jnp.pad(q_t[:, u], ((0, 0), (off, K_PACK - d - off), (0, 0))) for u, off in enumerate(offsets)]
    return jnp.stack(units, axis=1)


def _rms(x, g):
    xf = x.astype(jnp.float32)
    y = xf * lax.rsqrt(jnp.mean(xf * xf, axis=-1, keepdims=True) + EPS)
    return (y * g.astype(jnp.float32)).astype(x.dtype)


def _rope_2d(x, row, col):
    d = x.shape[-1]
    half = d // 2
    inv = 1.0 / (ROPE_THETA ** (jnp.arange(0, half, 2, dtype=jnp.float32) / half))
    ang = jnp.concatenate([row.astype(jnp.float32)[:, None] * inv[None, :],
                           col.astype(jnp.float32)[:, None] * inv[None, :]], axis=-1)
    cos = jnp.cos(ang)[None, :, None, :]
    sin = jnp.sin(ang)[None, :, None, :]
    xr = x.astype(jnp.float32).reshape(x.shape[:-1] + (half, 2))
    x0, x1 = xr[..., 0], xr[..., 1]
    out = jnp.stack([x0 * cos - x1 * sin, x0 * sin + x1 * cos], axis=-1)
    return out.reshape(x.shape).astype(x.dtype)


PROJ_TM = 512


S5_TC = 64
S5_ROW_TILE = 16
S5_CW = S5_TC * C_GROUP_CH
S5_SW = 2 * C_STATE


def _s5_operators(a_re, a_im, log_dt, b_re, b_im, c_re, c_im):
    f32 = jnp.float32
    tc, P, N, G = S5_TC, C_GROUP_CH, C_STATE, C_GROUPS
    j = jnp.arange(tc + 1, dtype=f32)
    kerns, p_parts, q_parts, a_parts = [], [], [], []
    for dirn in range(2):
        A = lax.complex(a_re[dirn].astype(f32), a_im[dirn].astype(f32))
        adt = A * jnp.exp(log_dt[dirn].astype(f32))[:, None]
        abar = jnp.exp(adt)
        bbar = ((abar - 1) / A)[..., None] * lax.complex(b_re[dirn].astype(f32), b_im[dirn].astype(f32))
        cmat = lax.complex(c_re[dirn].astype(f32), c_im[dirn].astype(f32))
        pw = jnp.exp(adt[None] * j[:, None, None])
        kerns.append(jnp.einsum('gpn,jgn,gnq->gqjp', cmat, pw[:tc], bbar).real)
        inj_pw = pw[:tc] if dirn == 1 else pw[:tc][::-1]
        inj = inj_pw[:, :, :, None] * bbar[None]
        p_parts.append(jnp.transpose(inj, (1, 0, 3, 2)).reshape(G, S5_CW, N))
        out_pw = pw[1:][::-1] if dirn == 1 else pw[1:]
        outm = cmat[None] * out_pw[:, :, None, :]
        q_parts.append(jnp.transpose(outm, (1, 3, 0, 2)).reshape(G, N, S5_CW))
        a_parts.append(pw[tc])
    f_seq = jnp.concatenate([jnp.flip(kerns[1][:, :, 1:], axis=2), kerns[0][:, :, :1] + kerns[1][:, :, :1],
                             kerns[0][:, :, 1:], jnp.zeros((G, P, 1, P), f32)], axis=2)
    rolled = jnp.tile(f_seq.astype(jnp.bfloat16).reshape(G, P, 2 * tc * P), (1, 1, tc))
    rolled = rolled[:, :, :tc * (2 * tc - 1) * P].reshape(G, P, tc, (2 * tc - 1) * P)
    toep = jnp.transpose(rolled[:, :, :, (tc - 1) * P:], (0, 2, 1, 3)).reshape(G, S5_CW, S5_CW)
    zeros = jnp.zeros((G, N, S5_CW), jnp.bfloat16)
    bf = lambda m: m.astype(jnp.bfloat16)
    w_mat = jnp.concatenate([
        toep,
        bf(q_parts[0].real), zeros, bf(-q_parts[0].imag), zeros,
        zeros, bf(q_parts[1].real), zeros, bf(-q_parts[1].imag),
    ], axis=1)
    p_mat = jnp.concatenate([p_parts[0].real, p_parts[1].real, p_parts[0].imag, p_parts[1].imag], axis=-1)
    a_chunk = jnp.stack([jnp.concatenate([a_parts[0].real, a_parts[1].real], axis=-1),
                         jnp.concatenate([a_parts[0].imag, a_parts[1].imag], axis=-1)], axis=1)
    return p_mat.astype(jnp.bfloat16), w_mat, a_chunk.astype(f32)


def _s5_scan_kernel(u_ref, p_ref, w_ref, a_ref, y_ref, s_scr, hs_scr, *, rows, order_f, order_r):
    f32 = jnp.float32
    u = u_ref[0]
    s_scr[...] = jnp.dot(u, p_ref[0], preferred_element_type=f32)
    hs_scr[...] = jnp.zeros(hs_scr.shape, f32)
    a_re = a_ref[0, 0:1, :]
    a_im = a_ref[0, 1:2, :]
    is_fwd = lax.broadcasted_iota(jnp.int32, (rows, S5_SW), 1) < C_STATE
    h_re = jnp.zeros((rows, S5_SW), f32)
    h_im = jnp.zeros((rows, S5_SW), f32)
    for cf, cr in zip(order_f, order_r):
        rf = slice(cf * rows, (cf + 1) * rows)
        rr = slice(cr * rows, (cr + 1) * rows)
        hs_scr[rf, 0:S5_SW] = h_re
        hs_scr[rf, S5_SW:2 * S5_SW] = h_im
        hs_scr[rr, 2 * S5_SW:3 * S5_SW] = h_re
        hs_scr[rr, 3 * S5_SW:4 * S5_SW] = h_im
        s_re = jnp.where(is_fwd, s_scr[rf, 0:S5_SW], s_scr[rr, 0:S5_SW])
        s_im = jnp.where(is_fwd, s_scr[rf, S5_SW:2 * S5_SW], s_scr[rr, S5_SW:2 * S5_SW])
        h_re, h_im = a_re * h_re - a_im * h_im + s_re, a_re * h_im + a_im * h_re + s_im
    y = jnp.dot(u, w_ref[0, 0:S5_CW, :], preferred_element_type=f32)
    y += jnp.dot(hs_scr[...].astype(jnp.bfloat16), w_ref[0, S5_CW:, :], preferred_element_type=f32)
    y_ref[0] = y


def _s5_scan(u_seq, n_ctx_chunks, p_mat, w_mat, a_chunk):
    B, S, _ = u_seq.shape
    G, P = C_GROUPS, C_GROUP_CH
    nc = S // S5_TC
    R = -(-(nc * B) // S5_ROW_TILE) * S5_ROW_TILE
    ug = jnp.transpose(u_seq.reshape(B, nc, S5_TC, G, P), (3, 1, 0, 2, 4))
    ug = jnp.pad(ug.reshape(G, nc * B, S5_CW), ((0, 0), (0, R - nc * B), (0, 0))).astype(jnp.bfloat16)
    order_f = tuple(range(nc))
    order_r = tuple(range(n_ctx_chunks - 1, -1, -1)) + tuple(range(nc - 1, n_ctx_chunks - 1, -1))
    y = pl.pallas_call(
        functools.partial(_s5_scan_kernel, rows=B, order_f=order_f, order_r=order_r),
        grid=(G,),
        in_specs=[
            pl.BlockSpec((1, R, S5_CW), lambda g: (g, 0, 0)),
            pl.BlockSpec((1, S5_CW, 2 * S5_SW), lambda g: (g, 0, 0)),
            pl.BlockSpec((1, S5_CW + 4 * S5_SW, S5_CW), lambda g: (g, 0, 0)),
            pl.BlockSpec((1, 2, S5_SW), lambda g: (g, 0, 0)),
        ],
        out_specs=pl.BlockSpec((1, R, S5_CW), lambda g: (g, 0, 0)),
        out_shape=jax.ShapeDtypeStruct((G, R, S5_CW), jnp.float32),
        scratch_shapes=[pltpu.VMEM((R, 2 * S5_SW), jnp.float32),
                        pltpu.VMEM((R, 4 * S5_SW), jnp.float32)],
        compiler_params=pltpu.CompilerParams(dimension_semantics=("arbitrary",),
                                             vmem_limit_bytes=VMEM_LIMIT),
        name="s5_scan",
    )(ug, p_mat, w_mat, a_chunk)
    y = y[:, :nc * B].reshape(G, nc, B, S5_TC, P)
    return jnp.transpose(y, (2, 1, 3, 0, 4)).reshape(B, S, C_WIDTH)


def _s5_glu_kernel(y_ref, u_ref, d_ref, w_ref, o_ref):
    y = y_ref[...] + d_ref[...] * u_ref[...]
    g = jax.nn.gelu(y)
    z = jnp.dot(g.astype(jnp.bfloat16), w_ref[...], preferred_element_type=jnp.float32)
    o_ref[...] = (g * jax.nn.sigmoid(z)).astype(o_ref.dtype)


def _s5_glu(y, u, d_skip, w_glu):
    T = y.shape[0]
    tb = math.gcd(T, 1024)
    return pl.pallas_call(
        _s5_glu_kernel,
        grid=(T // tb,),
        in_specs=[pl.BlockSpec((tb, C_WIDTH), lambda i: (i, 0)),
                  pl.BlockSpec((tb, C_WIDTH), lambda i: (i, 0)),
                  pl.BlockSpec((1, C_WIDTH), lambda i: (0, 0)),
                  pl.BlockSpec((C_WIDTH, C_WIDTH), lambda i: (0, 0))],
        out_specs=pl.BlockSpec((tb, C_WIDTH), lambda i: (i, 0)),
        out_shape=jax.ShapeDtypeStruct((T, C_WIDTH), jnp.float32),
        compiler_params=pltpu.CompilerParams(dimension_semantics=("arbitrary",)),
        name="s5_glu",
    )(y, u, d_skip.reshape(1, C_WIDTH).astype(jnp.float32), w_glu.astype(jnp.bfloat16))


def _s5_mixer(u, uc, s5_ops, d_skip, w_glu, with_ctx):
    B, L, _ = u.shape
    Lc = uc.shape[1]
    p_mat, w_mat, a_chunk = s5_ops
    u_seq = jnp.concatenate([uc, u], axis=1).astype(jnp.float32)
    y = _s5_scan(u_seq, Lc // S5_TC, p_mat, w_mat, a_chunk)
    if with_ctx:
        out = _s5_glu(y.reshape(B * (Lc + L), C_WIDTH), u_seq.reshape(B * (Lc + L), C_WIDTH), d_skip, w_glu)
        out = out.reshape(B, Lc + L, C_WIDTH)
        return out[:, Lc:], out[:, :Lc]
    out = _s5_glu(y[:, Lc:].reshape(B * L, C_WIDTH), u.astype(jnp.float32).reshape(B * L, C_WIDTH), d_skip, w_glu)
    return out.reshape(B, L, C_WIDTH), None


def _in_proj_kernel(x_ref, g_ref, sh_ref, sc_ref, w_ref, o_ref):
    x = x_ref[...]
    y = x * lax.rsqrt(jnp.mean(x * x, axis=-1, keepdims=True) + EPS) * g_ref[...]
    h = (y * (1.0 + sc_ref[0]) + sh_ref[0]).astype(jnp.bfloat16)
    o_ref[...] = jnp.dot(h, w_ref[...], preferred_element_type=jnp.float32)


def _in_proj(x, norm_g, shift, scale, w_bf):
    T, D = x.shape
    N = w_bf.shape[1]
    tm = min(PROJ_TM, T)
    return pl.pallas_call(
        _in_proj_kernel,
        grid=(T // tm,),
        in_specs=[pl.BlockSpec((tm, D), lambda i: (i, 0)),
                  pl.BlockSpec((1, D), lambda i: (0, 0)),
                  pl.BlockSpec((1, 1, D), lambda i: (0, 0, 0)),
                  pl.BlockSpec((1, 1, D), lambda i: (0, 0, 0)),
                  pl.BlockSpec((D, N), lambda i: (0, 0))],
        out_specs=pl.BlockSpec((tm, N), lambda i: (i, 0)),
        out_shape=jax.ShapeDtypeStruct((T, N), jnp.float32),
        compiler_params=pltpu.CompilerParams(dimension_semantics=("arbitrary",), vmem_limit_bytes=VMEM_LIMIT),
        name="in_proj",
    )(x, norm_g.reshape(1, D).astype(jnp.float32), shift, scale, w_bf)


def _out_proj_kernel(m_ref, w_ref, x_ref, g_ref, o_ref):
    o_ref[...] = x_ref[...] + g_ref[0] * jnp.dot(m_ref[...], w_ref[...], preferred_element_type=jnp.float32)


def _out_proj(mix_bf, w_bf, x, gate):
    T, D = x.shape
    K = mix_bf.shape[1]
    tm = min(PROJ_TM, T)
    return pl.pallas_call(
        _out_proj_kernel,
        grid=(T // tm,),
        in_specs=[pl.BlockSpec((tm, K), lambda i: (i, 0)),
                  pl.BlockSpec((K, D), lambda i: (0, 0)),
                  pl.BlockSpec((tm, D), lambda i: (i, 0)),
                  pl.BlockSpec((1, 1, D), lambda i: (0, 0, 0))],
        out_specs=pl.BlockSpec((tm, D), lambda i: (i, 0)),
        out_shape=jax.ShapeDtypeStruct((T, D), jnp.float32),
        compiler_params=pltpu.CompilerParams(dimension_semantics=("arbitrary",), vmem_limit_bytes=VMEM_LIMIT),
        name="out_proj",
    )(mix_bf, w_bf, x, gate)


def _ada_proj_kernel(s_ref, w_ref, b_ref, o_ref):
    o_ref[...] = jnp.dot(s_ref[...].astype(jnp.bfloat16), w_ref[...], preferred_element_type=jnp.float32) + b_ref[...]


def _ada_proj(s, w_bf, bias):
    M, D = s.shape
    N = w_bf.shape[1]
    rows = -(-M // SUBLANES) * SUBLANES
    tn = N // 6
    out = pl.pallas_call(
        _ada_proj_kernel,
        grid=(N // tn,),
        in_specs=[pl.BlockSpec((rows, D), lambda j: (0, 0)),
                  pl.BlockSpec((D, tn), lambda j: (0, j)),
                  pl.BlockSpec((1, tn), lambda j: (0, j))],
        out_specs=pl.BlockSpec((rows, tn), lambda j: (0, j)),
        out_shape=jax.ShapeDtypeStruct((rows, N), jnp.float32),
        compiler_params=pltpu.CompilerParams(dimension_semantics=("arbitrary",)),
        name="ada_proj",
    )(jnp.pad(s, ((0, rows - M), (0, 0))), w_bf, bias.reshape(1, N).astype(jnp.float32))
    return out[:M]


def _token_mixers(proj, projc, row, col, layer_idx, qn_a, kn_a, qn_b, kn_b,
                  lam_q1, lam_k1, lam_q2, lam_k2, subln_g, s5_ops, d_skip, w_glu, with_ctx):
    B, L, _ = proj.shape
    Lc = projc.shape[1]
    cuts = [A_Q, A_Q + A_KV, A_Q + 2 * A_KV, A_Q + 2 * A_KV + B_QK,
            A_Q + 2 * A_KV + 2 * B_QK, A_Q + 2 * A_KV + 2 * B_QK + B_V]
    qa, ka, va, qb, kb, vb, us = jnp.split(proj, cuts, axis=-1)
    qac, kac, vac, qbc, kbc, vbc, usc = jnp.split(projc, cuts, axis=-1)

    qa = _rope_2d(_rms(qa.reshape(B, L, A_HEADS, HEAD_DIM), qn_a), row, col)
    ka = _rope_2d(_rms(ka.reshape(B, L, A_KV_HEADS, HEAD_DIM), kn_a), row, col)
    va = va.reshape(B, L, A_KV_HEADS, HEAD_DIM)
    kac = _rms(kac.reshape(B, Lc, A_KV_HEADS, HEAD_DIM), kn_a)
    vac = vac.reshape(B, Lc, A_KV_HEADS, HEAD_DIM)
    ka_all = jnp.concatenate([kac, ka], axis=1)
    va_all = jnp.concatenate([vac, va], axis=1)
    qa_t = _query_ext_t(qa * (HEAD_DIM ** -0.5 * LOG2E), [(hq // A_GROUP) * HEAD_DIM for hq in range(A_HEADS)])
    ka_pack = ka_all.reshape(B, Lc + L, A_KV).astype(jnp.bfloat16)
    o_a = jnp.transpose(_gqa_attention(qa_t, ka_pack, _value_ext_t(va_all)), (0, 2, 1))

    lambda_init = 0.8 - 0.6 * math.exp(-0.3 * layer_idx)
    lam = (jnp.exp(jnp.sum(lam_q1.astype(jnp.float32) * lam_k1.astype(jnp.float32)))
           - jnp.exp(jnp.sum(lam_q2.astype(jnp.float32) * lam_k2.astype(jnp.float32))) + lambda_init)
    qb = _rope_2d(_rms(qb.reshape(B, L, 2 * B_HEADS, B_SUB), qn_b), row, col)
    kb = _rope_2d(_rms(kb.reshape(B, L, 2 * B_HEADS, B_SUB), kn_b), row, col)
    vb = vb.reshape(B, L, B_HEADS, B_VDIM)
    kbc = _rms(kbc.reshape(B, Lc, 2 * B_HEADS, B_SUB), kn_b)
    vbc = vbc.reshape(B, Lc, B_HEADS, B_VDIM)
    kb_all = jnp.concatenate([kbc, kb], axis=1)
    vb_all = jnp.concatenate([vbc, vb], axis=1)
    qb_t = _query_ext_t(qb * (B_SUB ** -0.5 * LOG2E), [(j % 4) * B_SUB for j in range(2 * B_HEADS)])
    kb_pack = kb_all.reshape(B, Lc + L, B_QK).astype(jnp.bfloat16)
    o_b = jnp.transpose(_diff_attention(qb_t, kb_pack, _value_ext_t(vb_all), lam, subln_g, 1 - lambda_init),
                        (0, 2, 1))

    o_c, o_cc = _s5_mixer(us, usc, s5_ops, d_skip, w_glu, with_ctx)

    mix = jnp.concatenate([o_a, o_b, o_c.astype(jnp.bfloat16)], axis=-1)
    if not with_ctx:
        return mix, None
    qac = _rms(qac.reshape(B, Lc, A_HEADS, HEAD_DIM), qn_a)
    qac_t = _query_ext_t(qac * (HEAD_DIM ** -0.5 * LOG2E), [(hq // A_GROUP) * HEAD_DIM for hq in range(A_HEADS)])
    o_ac = jnp.transpose(_gqa_attention(qac_t, kac.reshape(B, Lc, A_KV).astype(jnp.bfloat16), _value_ext_t(vac)),
                         (0, 2, 1))
    qbc = _rms(qbc.reshape(B, Lc, 2 * B_HEADS, B_SUB), qn_b)
    qbc_t = _query_ext_t(qbc * (B_SUB ** -0.5 * LOG2E), [(j % 4) * B_SUB for j in range(2 * B_HEADS)])
    o_bc = jnp.transpose(_diff_attention(qbc_t, kbc.reshape(B, Lc, B_QK).astype(jnp.bfloat16), _value_ext_t(vbc),
                                         lam, subln_g, 1 - lambda_init), (0, 2, 1))
    mixc = jnp.concatenate([o_ac, o_bc, o_cc.astype(jnp.bfloat16)], axis=-1)
    return mix, mixc


PEER_TB = 256
N_SLOTS = PEER_HEADS * PEER_TOPK
CAND_ROWS = 80
INVALID_FLAT = 1.0e9
SC_CORES = 2
SC_SUBCORES = 16


def _cand_flat_table():
    f = [float(b) for b in range(16)]
    for a in range(1, 8):
        f += [float(a * 16 + b) if (a + 1) * (b + 1) <= 16 else INVALID_FLAT for b in range(8)]
    f += [float(a * 16) for a in range(8, 16)]
    return jnp.broadcast_to(jnp.asarray(f, jnp.float32)[:, None], (CAND_ROWS, PEER_TB))


def _pair_rows(first, second):
    blocks = [first[0:1] + second]
    blocks += [first[a:a + 1] + second[0:8] for a in range(1, 8)]
    blocks += [first[8:16] + second[0:1]]
    return jnp.concatenate(blocks, axis=0)


N_EXPERTS = N_KEYS * N_KEYS


def _peer_route_kernel(x_ref, g_ref, sh_ref, sc_ref, wq_ref, keys_ref, ftab_ref, ut_ref,
                       idx_ref, gate_ref, s_ref, h_scr, q_scr, val_scr, id_scr, sc_scr, e_scr):
    f32 = jnp.float32
    head = pl.program_id(1)

    @pl.when(head == 0)
    def _():
        x = x_ref[...]
        y = x * lax.rsqrt(jnp.mean(x * x, axis=-1, keepdims=True) + EPS) * g_ref[...]
        h2 = (y * (1.0 + sc_ref[0]) + sh_ref[0]).astype(jnp.bfloat16)
        h_scr[...] = h2
        q = jnp.dot(h2, wq_ref[...], preferred_element_type=f32)
        for ht in range(2 * PEER_HEADS):
            q_scr[ht] = q[:, ht * PEER_HALF:(ht + 1) * PEER_HALF].astype(jnp.bfloat16)

    sub_scores = [lax.dot_general(keys_ref[head * 2 + t], q_scr[head * 2 + t], (((1,), (1,)), ((), ())),
                                  preferred_element_type=f32) for t in range(2)]
    s_ref[...] = jnp.dot(h_scr[...], ut_ref[...], preferred_element_type=f32)

    ftab = ftab_ref[...]
    key_id = lax.broadcasted_iota(jnp.int32, (N_KEYS, PEER_TB), 0).astype(f32)
    neg_inf = -jnp.inf
    for t in range(2):
        s = sub_scores[t]
        for r in range(PEER_TOPK):
            m = jnp.max(s, axis=0, keepdims=True)
            pick = jnp.min(jnp.where(s == m, key_id, float(N_KEYS)), axis=0, keepdims=True)
            s = jnp.where(key_id == pick, neg_inf, s)
            val_scr[t, r:r + 1, :] = m
            id_scr[t, r:r + 1, :] = pick
    cand = _pair_rows(val_scr[0], val_scr[1])
    cand = jnp.where(ftab < float(PEER_TOPK * PEER_TOPK), cand, neg_inf)
    expert = _pair_rows(id_scr[0] * float(N_KEYS), id_scr[1])
    for r in range(PEER_TOPK):
        m = jnp.max(cand, axis=0, keepdims=True)
        pick = jnp.min(jnp.where(cand == m, ftab, INVALID_FLAT), axis=0, keepdims=True)
        hit = ftab == pick
        sc_scr[r:r + 1, :] = m
        e_scr[r:r + 1, :] = jnp.sum(jnp.where(hit, expert, 0.0), axis=0, keepdims=True)
        cand = jnp.where(hit, neg_inf, cand)
    sc = sc_scr[...]
    p = jnp.exp(sc - sc[0:1])
    rows = pl.ds(pl.multiple_of(head * PEER_TOPK, PEER_TOPK), PEER_TOPK)
    idx_ref[rows, :] = e_scr[...].astype(jnp.int32)
    gate_ref[rows, :] = p / jnp.sum(p, axis=0, keepdims=True)


def _peer_route(x, norm_g, shift, scale, wq_bf, keys_bf, u_t, tokens_per_mod):
    T, D = x.shape
    nq = wq_bf.shape[1]
    te = N_EXPERTS // PEER_HEADS
    blocks_per_mod = tokens_per_mod // PEER_TB
    return pl.pallas_call(
        _peer_route_kernel,
        grid=(T // PEER_TB, PEER_HEADS),
        in_specs=[
            pl.BlockSpec((PEER_TB, D), lambda i, j: (i, 0)),
            pl.BlockSpec((1, D), lambda i, j: (0, 0)),
            pl.BlockSpec((1, 1, D), lambda i, j: (i // blocks_per_mod, 0, 0)),
            pl.BlockSpec((1, 1, D), lambda i, j: (i // blocks_per_mod, 0, 0)),
            pl.BlockSpec((D, nq), lambda i, j: (0, 0)),
            pl.BlockSpec((2 * PEER_HEADS, N_KEYS, PEER_HALF), lambda i, j: (0, 0, 0)),
            pl.BlockSpec((CAND_ROWS, PEER_TB), lambda i, j: (0, 0)),
            pl.BlockSpec((D, te), lambda i, j: (0, j)),
        ],
        out_specs=[
            pl.BlockSpec((N_SLOTS, PEER_TB), lambda i, j: (0, i)),
            pl.BlockSpec((N_SLOTS, PEER_TB), lambda i, j: (0, i)),
            pl.BlockSpec((PEER_TB, te), lambda i, j: (i, j)),
        ],
        out_shape=[
            jax.ShapeDtypeStruct((N_SLOTS, T), jnp.int32),
            jax.ShapeDtypeStruct((N_SLOTS, T), jnp.float32),
            jax.ShapeDtypeStruct((T, N_EXPERTS), jnp.float32),
        ],
        scratch_shapes=[pltpu.VMEM((PEER_TB, D), jnp.bfloat16),
                        pltpu.VMEM((2 * PEER_HEADS, PEER_TB, PEER_HALF), jnp.bfloat16),
                        pltpu.VMEM((2, PEER_TOPK, PEER_TB), jnp.float32),
                        pltpu.VMEM((2, PEER_TOPK, PEER_TB), jnp.float32),
                        pltpu.VMEM((PEER_TOPK, PEER_TB), jnp.float32),
                        pltpu.VMEM((PEER_TOPK, PEER_TB), jnp.float32)],
        compiler_params=pltpu.CompilerParams(dimension_semantics=("arbitrary", "arbitrary"),
                                             vmem_limit_bytes=VMEM_LIMIT),
        name="peer_route",
    )(x, norm_g.reshape(1, D).astype(jnp.float32), shift, scale, wq_bf, keys_bf, _cand_flat_table(), u_t)
SC_LANES = 16
SC_TOKENS = 32
DENSE_TM = 512
DENSE_TE = 2048


def _sc_worker_tokens(n_tokens):
    per_worker = n_tokens // (SC_CORES * SC_SUBCORES)
    blk = min(SC_TOKENS, per_worker)
    return per_worker, blk


def _sc_extract(scores, idx_flat):
    T = scores.shape[0]
    per_worker, blk = _sc_worker_tokens(T)
    mesh = plsc.VectorSubcoreMesh(core_axis_name="c", subcore_axis_name="s")

    def body(s_hbm, idx_hbm, out_hbm, row0_v, row1_v, idx_v, out_v, rsem):
        rows = (row0_v, row1_v)
        wid = lax.axis_index("s") * SC_CORES + lax.axis_index("c")
        t0 = wid * per_worker

        def row_copy(tok, slot):
            return pltpu.make_async_copy(s_hbm.at[tok], rows[slot], rsem.at[slot])

        def block(b, carry):
            tb = t0 + b * blk
            pltpu.sync_copy(idx_hbm.at[pl.ds(tb * N_SLOTS, blk * N_SLOTS)], idx_v)
            row_copy(tb, 0).start()

            def pair(i2, c):
                for slot in range(2):
                    i = i2 * 2 + slot

                    @pl.when(i + 1 < blk)
                    def _():
                        row_copy(tb + i + 1, 1 - slot).start()

                    row_copy(tb + i, slot).wait()
                    for j in range(N_SLOTS // SC_LANES):
                        at = pl.ds(i * N_SLOTS + j * SC_LANES, SC_LANES)
                        out_v[at] = plsc.load_gather(rows[slot], [idx_v[at]])
                return c

            lax.fori_loop(0, blk // 2, pair, 0)
            pltpu.sync_copy(out_v, out_hbm.at[pl.ds(tb * N_SLOTS, blk * N_SLOTS)])
            return carry

        lax.fori_loop(0, per_worker // blk, block, 0)

    return pl.kernel(
        body, mesh=mesh,
        out_type=jax.ShapeDtypeStruct((T * N_SLOTS,), jnp.float32),
        scratch_types=[pltpu.VMEM((N_EXPERTS,), jnp.float32), pltpu.VMEM((N_EXPERTS,), jnp.float32),
                       pltpu.VMEM((blk * N_SLOTS,), jnp.int32), pltpu.VMEM((blk * N_SLOTS,), jnp.float32),
                       pltpu.SemaphoreType.DMA((2,))],
        compiler_params=pltpu.CompilerParams(needs_layout_passes=False),
        name="peer_extract",
    )(scores, idx_flat)


def _sc_scatter(w_flat, idx_flat, n_tokens):
    T = n_tokens
    per_worker, blk = _sc_worker_tokens(T)
    n_vec = N_SLOTS // SC_LANES
    mesh = plsc.VectorSubcoreMesh(core_axis_name="c", subcore_axis_name="s")

    def body(w_hbm, idx_hbm, out_hbm, row0_v, row1_v, idx_v, w_v, wsem):
        rows = (row0_v, row1_v)
        wid = lax.axis_index("s") * SC_CORES + lax.axis_index("c")
        t0 = wid * per_worker
        zero = jnp.zeros((SC_LANES,), jnp.float32)

        def zero_fill(k, c):
            for slot in range(2):
                rows[slot][pl.ds(k * SC_LANES, SC_LANES)] = zero
            return c

        lax.fori_loop(0, N_EXPERTS // SC_LANES, zero_fill, 0)

        def out_copy(tok, slot):
            return pltpu.make_async_copy(rows[slot], out_hbm.at[tok], wsem.at[slot])

        def retire(tok, i, slot):
            out_copy(tok, slot).wait()
            for j in range(n_vec):
                plsc.store_scatter(rows[slot], [idx_v[pl.ds(i * N_SLOTS + j * SC_LANES, SC_LANES)]], zero)

        def block(b, carry):
            tb = t0 + b * blk
            pltpu.sync_copy(idx_hbm.at[pl.ds(tb * N_SLOTS, blk * N_SLOTS)], idx_v)
            pltpu.sync_copy(w_hbm.at[pl.ds(tb * N_SLOTS, blk * N_SLOTS)], w_v)

            def pair(i2, c):
                for slot in range(2):
                    i = i2 * 2 + slot

                    @pl.when(i >= 2)
                    def _():
                        retire(tb + i - 2, i - 2, slot)

                    for j in range(n_vec):
                        at = pl.ds(i * N_SLOTS + j * SC_LANES, SC_LANES)
                        plsc.addupdate_scatter(rows[slot], [idx_v[at]], w_v[at])
                    out_copy(tb + i, slot).start()
                return c

            lax.fori_loop(0, blk // 2, pair, 0)
            for slot in range(2):
                retire(tb + blk - 2 + slot, blk - 2 + slot, slot)
            return carry

        lax.fori_loop(0, per_worker // blk, block, 0)

    return pl.kernel(
        body, mesh=mesh,
        out_type=jax.ShapeDtypeStruct((T, N_EXPERTS), jnp.float32),
        scratch_types=[pltpu.VMEM((N_EXPERTS,), jnp.float32), pltpu.VMEM((N_EXPERTS,), jnp.float32),
                       pltpu.VMEM((blk * N_SLOTS,), jnp.int32), pltpu.VMEM((blk * N_SLOTS,), jnp.float32),
                       pltpu.SemaphoreType.DMA((2,))],
        compiler_params=pltpu.CompilerParams(needs_layout_passes=False),
        name="peer_scatter",
    )(w_flat, idx_flat)


def _peer_gate_kernel(a_ref, g_ref, w_ref):
    w_ref[...] = jax.nn.gelu(a_ref[...]) * g_ref[...]


def _peer_gate(act, gate):
    T = act.shape[0]
    tb = min(2048, T)
    spec = pl.BlockSpec((tb, N_SLOTS), lambda i: (i, 0))
    return pl.pallas_call(
        _peer_gate_kernel, grid=(T // tb,), in_specs=[spec, spec], out_specs=spec,
        out_shape=jax.ShapeDtypeStruct((T, N_SLOTS), jnp.float32),
        compiler_params=pltpu.CompilerParams(dimension_semantics=("arbitrary",)),
        name="peer_gate",
    )(act, gate)


def _peer_combine_kernel(w_ref, v_ref, x_ref, g2_ref, o_ref, acc_ref):
    k = pl.program_id(1)

    @pl.when(k == 0)
    def _():
        acc_ref[...] = jnp.zeros(acc_ref.shape, jnp.float32)

    acc_ref[...] += jnp.dot(w_ref[...].astype(jnp.bfloat16), v_ref[...], preferred_element_type=jnp.float32)

    @pl.when(k == pl.num_programs(1) - 1)
    def _():
        o_ref[...] = x_ref[...] + g2_ref[0] * acc_ref[...]


def _peer_combine(w_dense, v_bf, x, g2, tokens_per_mod):
    T, D = x.shape
    tm = min(DENSE_TM, T)
    blocks_per_mod = tokens_per_mod // tm
    return pl.pallas_call(
        _peer_combine_kernel,
        grid=(T // tm, N_EXPERTS // DENSE_TE),
        in_specs=[pl.BlockSpec((tm, DENSE_TE), lambda i, k: (i, k)),
                  pl.BlockSpec((DENSE_TE, D), lambda i, k: (k, 0)),
                  pl.BlockSpec((tm, D), lambda i, k: (i, 0)),
                  pl.BlockSpec((1, 1, D), lambda i, k: (i // blocks_per_mod, 0, 0))],
        out_specs=pl.BlockSpec((tm, D), lambda i, k: (i, 0)),
        out_shape=jax.ShapeDtypeStruct((T, D), jnp.float32),
        scratch_shapes=[pltpu.VMEM((tm, D), jnp.float32)],
        compiler_params=pltpu.CompilerParams(dimension_semantics=("arbitrary", "arbitrary"),
                                             vmem_limit_bytes=VMEM_LIMIT),
        name="peer_combine",
    )(w_dense, v_bf, x, g2)


def _peer_residual(x, norm_g, shift, scale, gate2, wq_bf, keys_bf, u_t, v_bf, tokens_per_mod):
    T, D = x.shape
    idx_t, gate_t, scores = _peer_route(x, norm_g, shift, scale, wq_bf, keys_bf, u_t, tokens_per_mod)
    idx_flat = jnp.transpose(idx_t).reshape(T * N_SLOTS)
    act = _sc_extract(scores, idx_flat).reshape(T, N_SLOTS)
    w = _peer_gate(act, jnp.transpose(gate_t))
    w_dense = _sc_scatter(w.reshape(T * N_SLOTS), idx_flat, T)
    return _peer_combine(w_dense, v_bf, x, gate2, tokens_per_mod)


def kernel(x, c, ctx, c_ctx, w_ada, b_ada, norm1_g, norm2_g, w_in, w_out,
           qn_a, kn_a, qn_b, kn_b, lam_q1, lam_k1, lam_q2, lam_k2, subln_g,
           s5_a_re, s5_a_im, s5_log_dt, s5_b_re, s5_b_im, s5_c_re, s5_c_im,
           s5_d, s5_w_glu, peer_wq, peer_keys, peer_u, peer_v):
    B, L, D = x.shape
    Lc = ctx.shape[1]
    depth = w_in.shape[0]
    rows = L // GRID_W
    row = jnp.repeat(jnp.arange(rows, dtype=jnp.int32), GRID_W)
    col = jnp.tile(jnp.arange(GRID_W, dtype=jnp.int32), rows)
    s_c = jax.nn.silu(c)
    s_cc = jax.nn.silu(c_ctx)
    mods, cmods, s5_ops, wq_bf, keys_bf, u_t, v_bf = [], [], [], [], [], [], []
    w_in_bf = w_in.astype(jnp.bfloat16)
    w_out_bf = w_out.astype(jnp.bfloat16)
    for l in range(depth):
        ada = _ada_proj(jnp.concatenate([s_c, s_cc[None, :]], axis=0), w_ada[l].astype(jnp.bfloat16), b_ada[l])
        mods.append(jnp.split(ada[:B, None, :], 6, axis=-1))
        cmods.append([m.reshape(1, 1, D) for m in jnp.split(ada[B], 6, axis=-1)])
        s5_ops.append(_s5_operators(s5_a_re[l], s5_a_im[l], s5_log_dt[l], s5_b_re[l], s5_b_im[l],
                                    s5_c_re[l], s5_c_im[l]))
        wq_bf.append(peer_wq[l].astype(jnp.bfloat16))
        keys_bf.append(peer_keys[l].reshape(2 * PEER_HEADS, N_KEYS, PEER_HALF).astype(jnp.bfloat16))
        u_t.append(jnp.transpose(peer_u[l].astype(jnp.bfloat16)))
        v_bf.append(peer_v[l].astype(jnp.bfloat16))

    outs = []
    for b in range(B):
        xb = x[b:b + 1]
        cb = ctx[b:b + 1]
        for l in range(depth):
            with_ctx = l < depth - 1
            sh1, sc1, g1, sh2, sc2, g2 = [m[b:b + 1] for m in mods[l]]
            csh1, csc1, cg1, csh2, csc2, cg2 = cmods[l]
            proj = _in_proj(xb.reshape(L, D), norm1_g[l], sh1, sc1, w_in_bf[l]).reshape(1, L, -1)
            projc = _in_proj(cb.reshape(Lc, D), norm1_g[l], csh1, csc1, w_in_bf[l]).reshape(1, Lc, -1)
            mix, mixc = _token_mixers(proj, projc, row, col, l, qn_a[l], kn_a[l], qn_b[l], kn_b[l],
                                      lam_q1[l], lam_k1[l], lam_q2[l], lam_k2[l], subln_g[l],
                                      s5_ops[l], s5_d[l], s5_w_glu[l], with_ctx)
            x2 = _out_proj(mix.reshape(L, -1), w_out_bf[l], xb.reshape(L, D), g1)
            xb = _peer_residual(x2, norm2_g[l], sh2, sc2, g2, wq_bf[l], keys_bf[l], u_t[l], v_bf[l],
                                tokens_per_mod=L).reshape(1, L, D)
            if with_ctx:
                c2 = _out_proj(mixc.reshape(Lc, -1), w_out_bf[l], cb.reshape(Lc, D), cg1)
                cb = _peer_residual(c2, norm2_g[l], csh2, csc2, cg2, wq_bf[l], keys_bf[l],
                                    u_t[l], v_bf[l], tokens_per_mod=Lc).reshape(1, Lc, D)
        outs.append(xb)
    return jnp.concatenate(outs, axis=0)
```

```python
import functools
import math

import jax
import jax.numpy as jnp
from jax import lax
from jax.experimental import pallas as pl
from jax.experimental.pallas import tpu as pltpu
from jax.experimental.pallas import tpu_sc as plsc

D_MODEL = 1024
GRID_W = 64
EPS = 1e-6
ROPE_THETA = 10000.0
HEAD_DIM = 64
A_HEADS = 6
A_KV_HEADS = 2
A_GROUP = A_HEADS // A_KV_HEADS
A_Q = A_HEADS * HEAD_DIM
A_KV = A_KV_HEADS * HEAD_DIM
B_HEADS = 6
B_SUB = 32
B_VDIM = 2 * B_SUB
B_QK = B_HEADS * 2 * B_SUB
B_V = B_HEADS * B_VDIM
C_GROUPS = 16
C_GROUP_CH = 16
C_WIDTH = C_GROUPS * C_GROUP_CH
C_STATE = 64
PEER_HEADS = 8
N_KEYS = 128
PEER_TOPK = 16
PEER_HALF = 128
TOKEN_CHUNK = 128

LANES = 128
SUBLANES = 8
V_EXT = LANES
LOG2E = math.log2(math.e)
NEG_BIG = -1e30
VMEM_LIMIT = 48 * 1024 * 1024


TQ = 1024
ATTN_TK = 2816
K_PACK = LANES


def _flash_unit(s, v_t, m_ref, acc_ref, unit):
    m_old = m_ref[unit]
    m_new = jnp.maximum(m_old, jnp.max(s, axis=0, keepdims=True))
    alpha = jnp.exp2(m_old - m_new)
    p = jnp.exp2(s - m_new).astype(jnp.bfloat16)
    acc_ref[unit] = alpha * acc_ref[unit] + jnp.dot(v_t, p, preferred_element_type=jnp.float32)
    m_ref[unit] = m_new


def _flash_all_units(q_ref, k_ref, vt_ref, m_ref, acc_ref, *, n_units, units_per_group, tk, n_keys):
    m_ref[...] = jnp.full(m_ref.shape, NEG_BIG, jnp.float32)
    acc_ref[...] = jnp.zeros(acc_ref.shape, jnp.float32)

    def body(j, carry):
        off = pl.multiple_of(j * tk, tk)
        k = k_ref[0, pl.ds(off, tk), :]

        def scores(u):
            return jnp.dot(k, q_ref[0, u], preferred_element_type=jnp.float32)

        s = scores(0)
        for u in range(n_units):
            s_next = scores(u + 1) if u + 1 < n_units else None
            _flash_unit(s, vt_ref[0, u // units_per_group, :, pl.ds(off, tk)], m_ref, acc_ref, u)
            s = s_next
        return carry

    lax.fori_loop(0, n_keys // tk, body, 0)


def _gqa_kernel(q_ref, k_ref, vt_ref, o_ref, m_ref, acc_ref, *, tk, n_keys):
    _flash_all_units(q_ref, k_ref, vt_ref, m_ref, acc_ref,
                     n_units=A_HEADS, units_per_group=A_GROUP, tk=tk, n_keys=n_keys)
    for h in range(A_HEADS):
        acc = acc_ref[h]
        o_ref[0, h * HEAD_DIM:(h + 1) * HEAD_DIM, :] = (
            acc[:HEAD_DIM] / acc[HEAD_DIM:HEAD_DIM + 1]).astype(o_ref.dtype)


def _diff_kernel(lam_ref, q_ref, k_ref, vt_ref, g_ref, o_ref, m_ref, acc_ref, *, tk, n_keys, out_scale):
    _flash_all_units(q_ref, k_ref, vt_ref, m_ref, acc_ref,
                     n_units=4, units_per_group=2, tk=tk, n_keys=n_keys)
    lam = lam_ref[0]
    for h in range(2):
        a1 = acc_ref[2 * h]
        a2 = acc_ref[2 * h + 1]
        o = a1[:B_VDIM] / a1[B_VDIM:B_VDIM + 1] - lam * (a2[:B_VDIM] / a2[B_VDIM:B_VDIM + 1])
        o = o * lax.rsqrt(jnp.mean(o * o, axis=0, keepdims=True) + EPS)
        o_ref[0, h * B_VDIM:(h + 1) * B_VDIM, :] = (o * g_ref[...] * out_scale).astype(o_ref.dtype)


def _gqa_attention(q_t, k, v_t):
    B, _, _, L = q_t.shape
    S = k.shape[1]
    tq, tk = min(TQ, L), min(ATTN_TK, S)
    return pl.pallas_call(
        functools.partial(_gqa_kernel, tk=tk, n_keys=S),
        grid=(B, L // tq),
        in_specs=[
            pl.BlockSpec((1, A_HEADS, K_PACK, tq), lambda b, i: (b, 0, 0, i)),
            pl.BlockSpec((1, S, K_PACK), lambda b, i: (b, 0, 0)),
            pl.BlockSpec((1, A_KV_HEADS, V_EXT, S), lambda b, i: (b, 0, 0, 0)),
        ],
        out_specs=pl.BlockSpec((1, A_Q, tq), lambda b, i: (b, 0, i)),
        out_shape=jax.ShapeDtypeStruct((B, A_Q, L), jnp.bfloat16),
        scratch_shapes=[pltpu.VMEM((A_HEADS, 1, tq), jnp.float32),
                        pltpu.VMEM((A_HEADS, V_EXT, tq), jnp.float32)],
        compiler_params=pltpu.CompilerParams(
            dimension_semantics=("arbitrary", "arbitrary"), vmem_limit_bytes=VMEM_LIMIT),
        name="gqa_attention",
    )(q_t, k, v_t)


def _diff_attention(q_t, k, v_t, lam, subln_g, out_scale):
    B, _, _, L = q_t.shape
    S = k.shape[1]
    tq, tk = min(TQ, L), min(ATTN_TK, S)
    return pl.pallas_call(
        functools.partial(_diff_kernel, tk=tk, n_keys=S, out_scale=out_scale),
        grid=(B, B_HEADS // 2, L // tq),
        in_specs=[
            pl.BlockSpec(memory_space=pltpu.SMEM),
            pl.BlockSpec((1, 4, K_PACK, tq), lambda b, h, i: (b, h, 0, i)),
            pl.BlockSpec((1, S, K_PACK), lambda b, h, i: (b, 0, h)),
            pl.BlockSpec((1, 2, V_EXT, S), lambda b, h, i: (b, h, 0, 0)),
            pl.BlockSpec((B_VDIM, 1), lambda b, h, i: (0, 0)),
        ],
        out_specs=pl.BlockSpec((1, 2 * B_VDIM, tq), lambda b, h, i: (b, h, i)),
        out_shape=jax.ShapeDtypeStruct((B, B_V, L), jnp.bfloat16),
        scratch_shapes=[pltpu.VMEM((4, 1, tq), jnp.float32),
                        pltpu.VMEM((4, V_EXT, tq), jnp.float32)],
        compiler_params=pltpu.CompilerParams(
            dimension_semantics=("arbitrary", "arbitrary", "arbitrary"), vmem_limit_bytes=VMEM_LIMIT),
        name="diff_attention",
    )(lam.reshape(1).astype(jnp.float32), q_t, k, v_t, subln_g.reshape(B_VDIM, 1).astype(jnp.float32))


def _value_ext_t(v):
    B, S, H, d = v.shape
    ones = jnp.ones((B, S, H, 1), v.dtype)
    pad = jnp.zeros((B, S, H, V_EXT - d - 1), v.dtype)
    return jnp.transpose(jnp.concatenate([v, ones, pad], axis=-1), (0, 2, 3, 1)).astype(jnp.bfloat16)


def _query_ext_t(q, offsets):
    B, L, U, d = q.shape
    slots = K_PACK // d
    q_t = jnp.transpose(q, (0, 2, 3, 1)).astype(jnp.bfloat16)
    own = jnp.asarray([[off // d == s for s in range(slots)] for off in offsets])
    ext = jnp.where(own[None, :, :, None, None], q_t[:, :, None], jnp.zeros((), jnp.bfloat16))
    return ext.reshape(B, U, K_PACK, L)


def _rms(x, g):
    xf = x.astype(jnp.float32)
    y = xf * lax.rsqrt(jnp.mean(xf * xf, axis=-1, keepdims=True) + EPS)
    return (y * g.astype(jnp.float32)).astype(x.dtype)


def _rope_2d(x, row, col):
    d = x.shape[-1]
    half = d // 2
    inv = 1.0 / (ROPE_THETA ** (jnp.arange(0, half, 2, dtype=jnp.float32) / half))
    ang = jnp.concatenate([row.astype(jnp.float32)[:, None] * inv[None, :],
                           col.astype(jnp.float32)[:, None] * inv[None, :]], axis=-1)
    cos = jnp.cos(ang)[None, :, None, :]
    sin = jnp.sin(ang)[None, :, None, :]
    xr = x.astype(jnp.float32).reshape(x.shape[:-1] + (half, 2))
    x0, x1 = xr[..., 0], xr[..., 1]
    out = jnp.stack([x0 * cos - x1 * sin, x0 * sin + x1 * cos], axis=-1)
    return out.reshape(x.shape).astype(x.dtype)


PROJ_TM = 512


S5_TC = 64
S5_ROW_TILE = 16
S5_CW = S5_TC * C_GROUP_CH
S5_SW = 2 * C_STATE


def _s5_operators(a_re, a_im, log_dt, b_re, b_im, c_re, c_im):
    f32 = jnp.float32
    tc, P, N, G = S5_TC, C_GROUP_CH, C_STATE, C_GROUPS
    j = jnp.arange(tc + 1, dtype=f32)
    kerns, p_parts, q_parts, a_parts = [], [], [], []
    for dirn in range(2):
        A = lax.complex(a_re[dirn].astype(f32), a_im[dirn].astype(f32))
        adt = A * jnp.exp(log_dt[dirn].astype(f32))[:, None]
        abar = jnp.exp(adt)
        bbar = ((abar - 1) / A)[..., None] * lax.complex(b_re[dirn].astype(f32), b_im[dirn].astype(f32))
        cmat = lax.complex(c_re[dirn].astype(f32), c_im[dirn].astype(f32))
        pw = jnp.exp(adt[None] * j[:, None, None])
        kerns.append(jnp.einsum('gpn,jgn,gnq->gqjp', cmat, pw[:tc], bbar).real)
        inj_pw = pw[:tc] if dirn == 1 else pw[:tc][::-1]
        inj = inj_pw[:, :, :, None] * bbar[None]
        p_parts.append(jnp.transpose(inj, (1, 0, 3, 2)).reshape(G, S5_CW, N))
        out_pw = pw[1:][::-1] if dirn == 1 else pw[1:]
        outm = cmat[None] * out_pw[:, :, None, :]
        q_parts.append(jnp.transpose(outm, (1, 3, 0, 2)).reshape(G, N, S5_CW))
        a_parts.append(pw[tc])
    f_seq = jnp.concatenate([jnp.flip(kerns[1][:, :, 1:], axis=2), kerns[0][:, :, :1] + kerns[1][:, :, :1],
                             kerns[0][:, :, 1:], jnp.zeros((G, P, 1, P), f32)], axis=2)
    rolled = jnp.tile(f_seq.astype(jnp.bfloat16).reshape(G, P, 2 * tc * P), (1, 1, tc))
    rolled = rolled[:, :, :tc * (2 * tc - 1) * P].reshape(G, P, tc, (2 * tc - 1) * P)
    toep = jnp.transpose(rolled[:, :, :, (tc - 1) * P:], (0, 2, 1, 3)).reshape(G, S5_CW, S5_CW)
    zeros = jnp.zeros((G, N, S5_CW), jnp.bfloat16)
    bf = lambda m: m.astype(jnp.bfloat16)
    w_mat = jnp.concatenate([
        toep,
        bf(q_parts[0].real), zeros, bf(-q_parts[0].imag), zeros,
        zeros, bf(q_parts[1].real), zeros, bf(-q_parts[1].imag),
    ], axis=1)
    p_mat = jnp.concatenate([p_parts[0].real, p_parts[1].real, p_parts[0].imag, p_parts[1].imag], axis=-1)
    a_chunk = jnp.stack([jnp.concatenate([a_parts[0].real, a_parts[1].real], axis=-1),
                         jnp.concatenate([a_parts[0].imag, a_parts[1].imag], axis=-1)], axis=1)
    return p_mat.astype(jnp.bfloat16), w_mat, a_chunk.astype(f32)


def _s5_scan_kernel(u_ref, p_ref, w_ref, a_ref, y_ref, s_scr, hs_scr, *, rows, order_f, order_r):
    f32 = jnp.float32
    u = u_ref[0]
    s_scr[...] = jnp.dot(u, p_ref[0], preferred_element_type=f32)
    hs_scr[...] = jnp.zeros(hs_scr.shape, f32)
    a_re = a_ref[0, 0:1, :]
    a_im = a_ref[0, 1:2, :]
    is_fwd = lax.broadcasted_iota(jnp.int32, (rows, S5_SW), 1) < C_STATE
    h_re = jnp.zeros((rows, S5_SW), f32)
    h_im = jnp.zeros((rows, S5_SW), f32)
    for cf, cr in zip(order_f, order_r):
        rf = slice(cf * rows, (cf + 1) * rows)
        rr = slice(cr * rows, (cr + 1) * rows)
        hs_scr[rf, 0:S5_SW] = h_re
        hs_scr[rf, S5_SW:2 * S5_SW] = h_im
        hs_scr[rr, 2 * S5_SW:3 * S5_SW] = h_re
        hs_scr[rr, 3 * S5_SW:4 * S5_SW] = h_im
        s_re = jnp.where(is_fwd, s_scr[rf, 0:S5_SW], s_scr[rr, 0:S5_SW])
        s_im = jnp.where(is_fwd, s_scr[rf, S5_SW:2 * S5_SW], s_scr[rr, S5_SW:2 * S5_SW])
        h_re, h_im = a_re * h_re - a_im * h_im + s_re, a_re * h_im + a_im * h_re + s_im
    y = jnp.dot(u, w_ref[0, 0:S5_CW, :], preferred_element_type=f32)
    y += jnp.dot(hs_scr[...].astype(jnp.bfloat16), w_ref[0, S5_CW:, :], preferred_element_type=f32)
    y_ref[0] = y


def _s5_scan(u_seq, n_ctx_chunks, p_mat, w_mat, a_chunk):
    B, S, _ = u_seq.shape
    G, P = C_GROUPS, C_GROUP_CH
    nc = S // S5_TC
    R = -(-(nc * B) // S5_ROW_TILE) * S5_ROW_TILE
    ug = jnp.transpose(u_seq.reshape(B, nc, S5_TC, G, P), (3, 1, 0, 2, 4))
    ug = jnp.pad(ug.reshape(G, nc * B, S5_CW), ((0, 0), (0, R - nc * B), (0, 0))).astype(jnp.bfloat16)
    order_f = tuple(range(nc))
    order_r = tuple(range(n_ctx_chunks - 1, -1, -1)) + tuple(range(nc - 1, n_ctx_chunks - 1, -1))
    y = pl.pallas_call(
        functools.partial(_s5_scan_kernel, rows=B, order_f=order_f, order_r=order_r),
        grid=(G,),
        in_specs=[
            pl.BlockSpec((1, R, S5_CW), lambda g: (g, 0, 0)),
            pl.BlockSpec((1, S5_CW, 2 * S5_SW), lambda g: (g, 0, 0)),
            pl.BlockSpec((1, S5_CW + 4 * S5_SW, S5_CW), lambda g: (g, 0, 0)),
            pl.BlockSpec((1, 2, S5_SW), lambda g: (g, 0, 0)),
        ],
        out_specs=pl.BlockSpec((1, R, S5_CW), lambda g: (g, 0, 0)),
        out_shape=jax.ShapeDtypeStruct((G, R, S5_CW), jnp.float32),
        scratch_shapes=[pltpu.VMEM((R, 2 * S5_SW), jnp.float32),
                        pltpu.VMEM((R, 4 * S5_SW), jnp.float32)],
        compiler_params=pltpu.CompilerParams(dimension_semantics=("arbitrary",),
                                             vmem_limit_bytes=VMEM_LIMIT),
        name="s5_scan",
    )(ug, p_mat, w_mat, a_chunk)
    y = y[:, :nc * B].reshape(G, nc, B, S5_TC, P)
    return jnp.transpose(y, (2, 1, 3, 0, 4)).reshape(B, S, C_WIDTH)


def _s5_glu_kernel(y_ref, u_ref, d_ref, w_ref, o_ref):
    y = y_ref[...] + d_ref[...] * u_ref[...]
    g = jax.nn.gelu(y)
    z = jnp.dot(g.astype(jnp.bfloat16), w_ref[...], preferred_element_type=jnp.float32)
    o_ref[...] = (g * jax.nn.sigmoid(z)).astype(o_ref.dtype)


def _s5_glu(y, u, d_skip, w_glu):
    T = y.shape[0]
    tb = math.gcd(T, 1024)
    return pl.pallas_call(
        _s5_glu_kernel,
        grid=(T // tb,),
        in_specs=[pl.BlockSpec((tb, C_WIDTH), lambda i: (i, 0)),
                  pl.BlockSpec((tb, C_WIDTH), lambda i: (i, 0)),
                  pl.BlockSpec((1, C_WIDTH), lambda i: (0, 0)),
                  pl.BlockSpec((C_WIDTH, C_WIDTH), lambda i: (0, 0))],
        out_specs=pl.BlockSpec((tb, C_WIDTH), lambda i: (i, 0)),
        out_shape=jax.ShapeDtypeStruct((T, C_WIDTH), jnp.float32),
        compiler_params=pltpu.CompilerParams(dimension_semantics=("arbitrary",)),
        name="s5_glu",
    )(y, u, d_skip.reshape(1, C_WIDTH).astype(jnp.float32), w_glu.astype(jnp.bfloat16))


def _s5_mixer(u, uc, s5_ops, d_skip, w_glu, with_ctx):
    B, L, _ = u.shape
    Lc = uc.shape[1]
    p_mat, w_mat, a_chunk = s5_ops
    u_seq = jnp.concatenate([uc, u], axis=1).astype(jnp.float32)
    y = _s5_scan(u_seq, Lc // S5_TC, p_mat, w_mat, a_chunk)
    if with_ctx:
        out = _s5_glu(y.reshape(B * (Lc + L), C_WIDTH), u_seq.reshape(B * (Lc + L), C_WIDTH), d_skip, w_glu)
        out = out.reshape(B, Lc + L, C_WIDTH)
        return out[:, Lc:], out[:, :Lc]
    out = _s5_glu(y[:, Lc:].reshape(B * L, C_WIDTH), u.astype(jnp.float32).reshape(B * L, C_WIDTH), d_skip, w_glu)
    return out.reshape(B, L, C_WIDTH), None


def _in_proj_kernel(x_ref, g_ref, sh_ref, sc_ref, w_ref, o_ref):
    x = x_ref[...]
    y = x * lax.rsqrt(jnp.mean(x * x, axis=-1, keepdims=True) + EPS) * g_ref[...]
    h = (y * (1.0 + sc_ref[0]) + sh_ref[0]).astype(jnp.bfloat16)
    o_ref[...] = jnp.dot(h, w_ref[...], preferred_element_type=jnp.float32)


def _in_proj(x, norm_g, shift, scale, w_bf):
    T, D = x.shape
    N = w_bf.shape[1]
    tm = min(PROJ_TM, T)
    return pl.pallas_call(
        _in_proj_kernel,
        grid=(T // tm,),
        in_specs=[pl.BlockSpec((tm, D), lambda i: (i, 0)),
                  pl.BlockSpec((1, D), lambda i: (0, 0)),
                  pl.BlockSpec((1, 1, D), lambda i: (0, 0, 0)),
                  pl.BlockSpec((1, 1, D), lambda i: (0, 0, 0)),
                  pl.BlockSpec((D, N), lambda i: (0, 0))],
        out_specs=pl.BlockSpec((tm, N), lambda i: (i, 0)),
        out_shape=jax.ShapeDtypeStruct((T, N), jnp.float32),
        compiler_params=pltpu.CompilerParams(dimension_semantics=("arbitrary",), vmem_limit_bytes=VMEM_LIMIT),
        name="in_proj",
    )(x, norm_g.reshape(1, D).astype(jnp.float32), shift, scale, w_bf)


def _out_proj_kernel(m_ref, w_ref, x_ref, g_ref, o_ref):
    o_ref[...] = x_ref[...] + g_ref[0] * jnp.dot(m_ref[...], w_ref[...], preferred_element_type=jnp.float32)


def _out_proj(mix_bf, w_bf, x, gate):
    T, D = x.shape
    K = mix_bf.shape[1]
    tm = min(PROJ_TM, T)
    return pl.pallas_call(
        _out_proj_kernel,
        grid=(T // tm,),
        in_specs=[pl.BlockSpec((tm, K), lambda i: (i, 0)),
                  pl.BlockSpec((K, D), lambda i: (0, 0)),
                  pl.BlockSpec((tm, D), lambda i: (i, 0)),
                  pl.BlockSpec((1, 1, D), lambda i: (0, 0, 0))],
        out_specs=pl.BlockSpec((tm, D), lambda i: (i, 0)),
        out_shape=jax.ShapeDtypeStruct((T, D), jnp.float32),
        compiler_params=pltpu.CompilerParams(dimension_semantics=("arbitrary",), vmem_limit_bytes=VMEM_LIMIT),
        name="out_proj",
    )(mix_bf, w_bf, x, gate)


def _ada_proj_kernel(s_ref, w_ref, b_ref, o_ref):
    o_ref[...] = jnp.dot(s_ref[...].astype(jnp.bfloat16), w_ref[...], preferred_element_type=jnp.float32) + b_ref[...]


def _ada_proj(s, w_bf, bias):
    M, D = s.shape
    N = w_bf.shape[1]
    rows = -(-M // SUBLANES) * SUBLANES
    tn = N // 6
    out = pl.pallas_call(
        _ada_proj_kernel,
        grid=(N // tn,),
        in_specs=[pl.BlockSpec((rows, D), lambda j: (0, 0)),
                  pl.BlockSpec((D, tn), lambda j: (0, j)),
                  pl.BlockSpec((1, tn), lambda j: (0, j))],
        out_specs=pl.BlockSpec((rows, tn), lambda j: (0, j)),
        out_shape=jax.ShapeDtypeStruct((rows, N), jnp.float32),
        compiler_params=pltpu.CompilerParams(dimension_semantics=("arbitrary",)),
        name="ada_proj",
    )(jnp.pad(s, ((0, rows - M), (0, 0))), w_bf, bias.reshape(1, N).astype(jnp.float32))
    return out[:M]


def _token_mixers(proj, projc, row, col, layer_idx, qn_a, kn_a, qn_b, kn_b,
                  lam_q1, lam_k1, lam_q2, lam_k2, subln_g, s5_ops, d_skip, w_glu, with_ctx):
    B, L, _ = proj.shape
    Lc = projc.shape[1]
    cuts = [A_Q, A_Q + A_KV, A_Q + 2 * A_KV, A_Q + 2 * A_KV + B_QK,
            A_Q + 2 * A_KV + 2 * B_QK, A_Q + 2 * A_KV + 2 * B_QK + B_V]
    qa, ka, va, qb, kb, vb, us = jnp.split(proj, cuts, axis=-1)
    qac, kac, vac, qbc, kbc, vbc, usc = jnp.split(projc, cuts, axis=-1)

    qa = _rope_2d(_rms(qa.reshape(B, L, A_HEADS, HEAD_DIM), qn_a), row, col)
    ka = _rope_2d(_rms(ka.reshape(B, L, A_KV_HEADS, HEAD_DIM), kn_a), row, col)
    va = va.reshape(B, L, A_KV_HEADS, HEAD_DIM)
    kac = _rms(kac.reshape(B, Lc, A_KV_HEADS, HEAD_DIM), kn_a)
    vac = vac.reshape(B, Lc, A_KV_HEADS, HEAD_DIM)
    ka_all = jnp.concatenate([kac, ka], axis=1)
    va_all = jnp.concatenate([vac, va], axis=1)
    qa_t = _query_ext_t(qa * (HEAD_DIM ** -0.5 * LOG2E), [(hq // A_GROUP) * HEAD_DIM for hq in range(A_HEADS)])
    ka_pack = ka_all.reshape(B, Lc + L, A_KV).astype(jnp.bfloat16)
    o_a = jnp.transpose(_gqa_attention(qa_t, ka_pack, _value_ext_t(va_all)), (0, 2, 1))

    lambda_init = 0.8 - 0.6 * math.exp(-0.3 * layer_idx)
    lam = (jnp.exp(jnp.sum(lam_q1.astype(jnp.float32) * lam_k1.astype(jnp.float32)))
           - jnp.exp(jnp.sum(lam_q2.astype(jnp.float32) * lam_k2.astype(jnp.float32))) + lambda_init)
    qb = _rope_2d(_rms(qb.reshape(B, L, 2 * B_HEADS, B_SUB), qn_b), row, col)
    kb = _rope_2d(_rms(kb.reshape(B, L, 2 * B_HEADS, B_SUB), kn_b), row, col)
    vb = vb.reshape(B, L, B_HEADS, B_VDIM)
    kbc = _rms(kbc.reshape(B, Lc, 2 * B_HEADS, B_SUB), kn_b)
    vbc = vbc.reshape(B, Lc, B_HEADS, B_VDIM)
    kb_all = jnp.concatenate([kbc, kb], axis=1)
    vb_all = jnp.concatenate([vbc, vb], axis=1)
    qb_t = _query_ext_t(qb * (B_SUB ** -0.5 * LOG2E), [(j % 4) * B_SUB for j in range(2 * B_HEADS)])
    kb_pack = kb_all.reshape(B, Lc + L, B_QK).astype(jnp.bfloat16)
    o_b = jnp.transpose(_diff_attention(qb_t, kb_pack, _value_ext_t(vb_all), lam, subln_g, 1 - lambda_init),
                        (0, 2, 1))

    o_c, o_cc = _s5_mixer(us, usc, s5_ops, d_skip, w_glu, with_ctx)

    mix = jnp.concatenate([o_a, o_b, o_c.astype(jnp.bfloat16)], axis=-1)
    if not with_ctx:
        return mix, None
    qac = _rms(qac.reshape(B, Lc, A_HEADS, HEAD_DIM), qn_a)
    qac_t = _query_ext_t(qac * (HEAD_DIM ** -0.5 * LOG2E), [(hq // A_GROUP) * HEAD_DIM for hq in range(A_HEADS)])
    o_ac = jnp.transpose(_gqa_attention(qac_t, kac.reshape(B, Lc, A_KV).astype(jnp.bfloat16), _value_ext_t(vac)),
                         (0, 2, 1))
    qbc = _rms(qbc.reshape(B, Lc, 2 * B_HEADS, B_SUB), qn_b)
    qbc_t = _query_ext_t(qbc * (B_SUB ** -0.5 * LOG2E), [(j % 4) * B_SUB for j in range(2 * B_HEADS)])
    o_bc = jnp.transpose(_diff_attention(qbc_t, kbc.reshape(B, Lc, B_QK).astype(jnp.bfloat16), _value_ext_t(vbc),
                                         lam, subln_g, 1 - lambda_init), (0, 2, 1))
    mixc = jnp.concatenate([o_ac, o_bc, o_cc.astype(jnp.bfloat16)], axis=-1)
    return mix, mixc


PEER_TB = 256
N_SLOTS = PEER_HEADS * PEER_TOPK
CAND_ROWS = 80
INVALID_FLAT = 1.0e9
SC_CORES = 2
SC_SUBCORES = 16


def _cand_flat_table():
    f = [float(b) for b in range(16)]
    for a in range(1, 8):
        f += [float(a * 16 + b) if (a + 1) * (b + 1) <= 16 else INVALID_FLAT for b in range(8)]
    f += [float(a * 16) for a in range(8, 16)]
    return jnp.broadcast_to(jnp.asarray(f, jnp.float32)[:, None], (CAND_ROWS, PEER_TB))


def _pair_rows(first, second):
    blocks = [first[0:1] + second]
    blocks += [first[a:a + 1] + second[0:8] for a in range(1, 8)]
    blocks += [first[8:16] + second[0:1]]
    return jnp.concatenate(blocks, axis=0)


N_EXPERTS = N_KEYS * N_KEYS


def _peer_route_kernel(x_ref, g_ref, sh_ref, sc_ref, wq_ref, keys_ref, ftab_ref, ut_ref,
                       idx_ref, gate_ref, s_ref, h_scr, q_scr, val_scr, id_scr, sc_scr, e_scr):
    f32 = jnp.float32
    head = pl.program_id(1)

    @pl.when(head == 0)
    def _():
        x = x_ref[...]
        y = x * lax.rsqrt(jnp.mean(x * x, axis=-1, keepdims=True) + EPS) * g_ref[...]
        h2 = (y * (1.0 + sc_ref[0]) + sh_ref[0]).astype(jnp.bfloat16)
        h_scr[...] = h2
        q = jnp.dot(h2, wq_ref[...], preferred_element_type=f32)
        for ht in range(2 * PEER_HEADS):
            q_scr[ht] = q[:, ht * PEER_HALF:(ht + 1) * PEER_HALF].astype(jnp.bfloat16)

    sub_scores = [lax.dot_general(keys_ref[head * 2 + t], q_scr[head * 2 + t], (((1,), (1,)), ((), ())),
                                  preferred_element_type=f32) for t in range(2)]
    s_ref[...] = jnp.dot(h_scr[...], ut_ref[...], preferred_element_type=f32)

    ftab = ftab_ref[...]
    key_id = lax.broadcasted_iota(jnp.int32, (N_KEYS, PEER_TB), 0).astype(f32)
    neg_inf = -jnp.inf
    for t in range(2):
        s = sub_scores[t]
        for r in range(PEER_TOPK):
            m = jnp.max(s, axis=0, keepdims=True)
            pick = jnp.min(jnp.where(s == m, key_id, float(N_KEYS)), axis=0, keepdims=True)
            s = jnp.where(key_id == pick, neg_inf, s)
            val_scr[t, r:r + 1, :] = m
            id_scr[t, r:r + 1, :] = pick
    cand = _pair_rows(val_scr[0], val_scr[1])
    cand = jnp.where(ftab < float(PEER_TOPK * PEER_TOPK), cand, neg_inf)
    expert = _pair_rows(id_scr[0] * float(N_KEYS), id_scr[1])
    for r in range(PEER_TOPK):
        m = jnp.max(cand, axis=0, keepdims=True)
        pick = jnp.min(jnp.where(cand == m, ftab, INVALID_FLAT), axis=0, keepdims=True)
        hit = ftab == pick
        sc_scr[r:r + 1, :] = m
        e_scr[r:r + 1, :] = jnp.sum(jnp.where(hit, expert, 0.0), axis=0, keepdims=True)
        cand = jnp.where(hit, neg_inf, cand)
    sc = sc_scr[...]
    p = jnp.exp(sc - sc[0:1])
    rows = pl.ds(pl.multiple_of(head * PEER_TOPK, PEER_TOPK), PEER_TOPK)
    idx_ref[rows, :] = e_scr[...].astype(jnp.int32)
    gate_ref[rows, :] = p / jnp.sum(p, axis=0, keepdims=True)


def _peer_route(x, norm_g, shift, scale, wq_bf, keys_bf, u_t, tokens_per_mod):
    T, D = x.shape
    nq = wq_bf.shape[1]
    te = N_EXPERTS // PEER_HEADS
    blocks_per_mod = tokens_per_mod // PEER_TB
    return pl.pallas_call(
        _peer_route_kernel,
        grid=(T // PEER_TB, PEER_HEADS),
        in_specs=[
            pl.BlockSpec((PEER_TB, D), lambda i, j: (i, 0)),
            pl.BlockSpec((1, D), lambda i, j: (0, 0)),
            pl.BlockSpec((1, 1, D), lambda i, j: (i // blocks_per_mod, 0, 0)),
            pl.BlockSpec((1, 1, D), lambda i, j: (i // blocks_per_mod, 0, 0)),
            pl.BlockSpec((D, nq), lambda i, j: (0, 0)),
            pl.BlockSpec((2 * PEER_HEADS, N_KEYS, PEER_HALF), lambda i, j: (0, 0, 0)),
            pl.BlockSpec((CAND_ROWS, PEER_TB), lambda i, j: (0, 0)),
            pl.BlockSpec((D, te), lambda i, j: (0, j)),
        ],
        out_specs=[
            pl.BlockSpec((N_SLOTS, PEER_TB), lambda i, j: (0, i)),
            pl.BlockSpec((N_SLOTS, PEER_TB), lambda i, j: (0, i)),
            pl.BlockSpec((PEER_TB, te), lambda i, j: (i, j)),
        ],
        out_shape=[
            jax.ShapeDtypeStruct((N_SLOTS, T), jnp.int32),
            jax.ShapeDtypeStruct((N_SLOTS, T), jnp.float32),
            jax.ShapeDtypeStruct((T, N_EXPERTS), jnp.float32),
        ],
        scratch_shapes=[pltpu.VMEM((PEER_TB, D), jnp.bfloat16),
                        pltpu.VMEM((2 * PEER_HEADS, PEER_TB, PEER_HALF), jnp.bfloat16),
                        pltpu.VMEM((2, PEER_TOPK, PEER_TB), jnp.float32),
                        pltpu.VMEM((2, PEER_TOPK, PEER_TB), jnp.float32),
                        pltpu.VMEM((PEER_TOPK, PEER_TB), jnp.float32),
                        pltpu.VMEM((PEER_TOPK, PEER_TB), jnp.float32)],
        compiler_params=pltpu.CompilerParams(dimension_semantics=("arbitrary", "arbitrary"),
                                             vmem_limit_bytes=VMEM_LIMIT),
        name="peer_route",
    )(x, norm_g.reshape(1, D).astype(jnp.float32), shift, scale, wq_bf, keys_bf, _cand_flat_table(), u_t)
SC_LANES = 16
SC_TOKENS = 32
DENSE_TM = 512
DENSE_TE = 2048


def _sc_worker_tokens(n_tokens):
    per_worker = n_tokens // (SC_CORES * SC_SUBCORES)
    blk = min(SC_TOKENS, per_worker)
    return per_worker, blk


def _sc_extract(scores, idx_flat):
    T = scores.shape[0]
    per_worker, blk = _sc_worker_tokens(T)
    mesh = plsc.VectorSubcoreMesh(core_axis_name="c", subcore_axis_name="s")

    def body(s_hbm, idx_hbm, out_hbm, row0_v, row1_v, idx_v, out_v, rsem):
        rows = (row0_v, row1_v)
        wid = lax.axis_index("s") * SC_CORES + lax.axis_index("c")
        t0 = wid * per_worker

        def row_copy(tok, slot):
            return pltpu.make_async_copy(s_hbm.at[tok], rows[slot], rsem.at[slot])

        def block(b, carry):
            tb = t0 + b * blk
            pltpu.sync_copy(idx_hbm.at[pl.ds(tb * N_SLOTS, blk * N_SLOTS)], idx_v)
            row_copy(tb, 0).start()

            def pair(i2, c):
                for slot in range(2):
                    i = i2 * 2 + slot

                    @pl.when(i + 1 < blk)
                    def _():
                        row_copy(tb + i + 1, 1 - slot).start()

                    row_copy(tb + i, slot).wait()
                    for j in range(N_SLOTS // SC_LANES):
                        at = pl.ds(i * N_SLOTS + j * SC_LANES, SC_LANES)
                        out_v[at] = plsc.load_gather(rows[slot], [idx_v[at]])
                return c

            lax.fori_loop(0, blk // 2, pair, 0)
            pltpu.sync_copy(out_v, out_hbm.at[pl.ds(tb * N_SLOTS, blk * N_SLOTS)])
            return carry

        lax.fori_loop(0, per_worker // blk, block, 0)

    return pl.kernel(
        body, mesh=mesh,
        out_type=jax.ShapeDtypeStruct((T * N_SLOTS,), jnp.float32),
        scratch_types=[pltpu.VMEM((N_EXPERTS,), jnp.float32), pltpu.VMEM((N_EXPERTS,), jnp.float32),
                       pltpu.VMEM((blk * N_SLOTS,), jnp.int32), pltpu.VMEM((blk * N_SLOTS,), jnp.float32),
                       pltpu.SemaphoreType.DMA((2,))],
        compiler_params=pltpu.CompilerParams(needs_layout_passes=False),
        name="peer_extract",
    )(scores, idx_flat)


def _sc_scatter(w_flat, idx_flat, n_tokens):
    T = n_tokens
    per_worker, blk = _sc_worker_tokens(T)
    n_vec = N_SLOTS // SC_LANES
    mesh = plsc.VectorSubcoreMesh(core_axis_name="c", subcore_axis_name="s")

    def body(w_hbm, idx_hbm, out_hbm, row0_v, row1_v, idx_v, w_v, wsem):
        rows = (row0_v, row1_v)
        wid = lax.axis_index("s") * SC_CORES + lax.axis_index("c")
        t0 = wid * per_worker
        zero = jnp.zeros((SC_LANES,), jnp.float32)

        def zero_fill(k, c):
            for slot in range(2):
                rows[slot][pl.ds(k * SC_LANES, SC_LANES)] = zero
            return c

        lax.fori_loop(0, N_EXPERTS // SC_LANES, zero_fill, 0)

        def out_copy(tok, slot):
            return pltpu.make_async_copy(rows[slot], out_hbm.at[tok], wsem.at[slot])

        def retire(tok, i, slot):
            out_copy(tok, slot).wait()
            for j in range(n_vec):
                plsc.store_scatter(rows[slot], [idx_v[pl.ds(i * N_SLOTS + j * SC_LANES, SC_LANES)]], zero)

        def block(b, carry):
            tb = t0 + b * blk
            pltpu.sync_copy(idx_hbm.at[pl.ds(tb * N_SLOTS, blk * N_SLOTS)], idx_v)
            pltpu.sync_copy(w_hbm.at[pl.ds(tb * N_SLOTS, blk * N_SLOTS)], w_v)

            def pair(i2, c):
                for slot in range(2):
                    i = i2 * 2 + slot

                    @pl.when(i >= 2)
                    def _():
                        retire(tb + i - 2, i - 2, slot)

                    for j in range(n_vec):
                        at = pl.ds(i * N_SLOTS + j * SC_LANES, SC_LANES)
                        plsc.addupdate_scatter(rows[slot], [idx_v[at]], w_v[at])
                    out_copy(tb + i, slot).start()
                return c

            lax.fori_loop(0, blk // 2, pair, 0)
            for slot in range(2):
                retire(tb + blk - 2 + slot, blk - 2 + slot, slot)
            return carry

        lax.fori_loop(0, per_worker // blk, block, 0)

    return pl.kernel(
        body, mesh=mesh,
        out_type=jax.ShapeDtypeStruct((T, N_EXPERTS), jnp.float32),
        scratch_types=[pltpu.VMEM((N_EXPERTS,), jnp.float32), pltpu.VMEM((N_EXPERTS,), jnp.float32),
                       pltpu.VMEM((blk * N_SLOTS,), jnp.int32), pltpu.VMEM((blk * N_SLOTS,), jnp.float32),
                       pltpu.SemaphoreType.DMA((2,))],
        compiler_params=pltpu.CompilerParams(needs_layout_passes=False),
        name="peer_scatter",
    )(w_flat, idx_flat)


def _peer_gate_kernel(a_ref, g_ref, w_ref):
    w_ref[...] = jax.nn.gelu(a_ref[...]) * g_ref[...]


def _peer_gate(act, gate):
    T = act.shape[0]
    tb = min(2048, T)
    spec = pl.BlockSpec((tb, N_SLOTS), lambda i: (i, 0))
    return pl.pallas_call(
        _peer_gate_kernel, grid=(T // tb,), in_specs=[spec, spec], out_specs=spec,
        out_shape=jax.ShapeDtypeStruct((T, N_SLOTS), jnp.float32),
        compiler_params=pltpu.CompilerParams(dimension_semantics=("arbitrary",)),
        name="peer_gate",
    )(act, gate)


def _peer_combine_kernel(w_ref, v_ref, x_ref, g2_ref, o_ref, acc_ref):
    k = pl.program_id(1)

    @pl.when(k == 0)
    def _():
        acc_ref[...] = jnp.zeros(acc_ref.shape, jnp.float32)

    acc_ref[...] += jnp.dot(w_ref[...].astype(jnp.bfloat16), v_ref[...], preferred_element_type=jnp.float32)

    @pl.when(k == pl.num_programs(1) - 1)
    def _():
        o_ref[...] = x_ref[...] + g2_ref[0] * acc_ref[...]


def _peer_combine(w_dense, v_bf, x, g2, tokens_per_mod):
    T, D = x.shape
    tm = min(DENSE_TM, T)
    blocks_per_mod = tokens_per_mod // tm
    return pl.pallas_call(
        _peer_combine_kernel,
        grid=(T // tm, N_EXPERTS // DENSE_TE),
        in_specs=[pl.BlockSpec((tm, DENSE_TE), lambda i, k: (i, k)),
                  pl.BlockSpec((DENSE_TE, D), lambda i, k: (k, 0)),
                  pl.BlockSpec((tm, D), lambda i, k: (i, 0)),
                  pl.BlockSpec((1, 1, D), lambda i, k: (i // blocks_per_mod, 0, 0))],
        out_specs=pl.BlockSpec((tm, D), lambda i, k: (i, 0)),
        out_shape=jax.ShapeDtypeStruct((T, D), jnp.float32),
        scratch_shapes=[pltpu.VMEM((tm, D), jnp.float32)],
        compiler_params=pltpu.CompilerParams(dimension_semantics=("arbitrary", "arbitrary"),
                                             vmem_limit_bytes=VMEM_LIMIT),
        name="peer_combine",
    )(w_dense, v_bf, x, g2)


def _peer_residual(x, norm_g, shift, scale, gate2, wq_bf, keys_bf, u_t, v_bf, tokens_per_mod):
    T, D = x.shape
    idx_t, gate_t, scores = _peer_route(x, norm_g, shift, scale, wq_bf, keys_bf, u_t, tokens_per_mod)
    idx_flat = jnp.transpose(idx_t).reshape(T * N_SLOTS)
    act = _sc_extract(scores, idx_flat).reshape(T, N_SLOTS)
    w = _peer_gate(act, jnp.transpose(gate_t))
    w_dense = _sc_scatter(w.reshape(T * N_SLOTS), idx_flat, T)
    return _peer_combine(w_dense, v_bf, x, gate2, tokens_per_mod)


def kernel(x, c, ctx, c_ctx, w_ada, b_ada, norm1_g, norm2_g, w_in, w_out,
           qn_a, kn_a, qn_b, kn_b, lam_q1, lam_k1, lam_q2, lam_k2, subln_g,
           s5_a_re, s5_a_im, s5_log_dt, s5_b_re, s5_b_im, s5_c_re, s5_c_im,
           s5_d, s5_w_glu, peer_wq, peer_keys, peer_u, peer_v):
    B, L, D = x.shape
    Lc = ctx.shape[1]
    depth = w_in.shape[0]
    rows = L // GRID_W
    row = jnp.repeat(jnp.arange(rows, dtype=jnp.int32), GRID_W)
    col = jnp.tile(jnp.arange(GRID_W, dtype=jnp.int32), rows)
    s_c = jax.nn.silu(c)
    s_cc = jax.nn.silu(c_ctx)
    mods, cmods, s5_ops, wq_bf, keys_bf, u_t, v_bf = [], [], [], [], [], [], []
    w_in_bf = w_in.astype(jnp.bfloat16)
    w_out_bf = w_out.astype(jnp.bfloat16)
    for l in range(depth):
        ada = _ada_proj(jnp.concatenate([s_c, s_cc[None, :]], axis=0), w_ada[l].astype(jnp.bfloat16), b_ada[l])
        mods.append(jnp.split(ada[:B, None, :], 6, axis=-1))
        cmods.append([m.reshape(1, 1, D) for m in jnp.split(ada[B], 6, axis=-1)])
        s5_ops.append(_s5_operators(s5_a_re[l], s5_a_im[l], s5_log_dt[l], s5_b_re[l], s5_b_im[l],
                                    s5_c_re[l], s5_c_im[l]))
        wq_bf.append(peer_wq[l].astype(jnp.bfloat16))
        keys_bf.append(peer_keys[l].reshape(2 * PEER_HEADS, N_KEYS, PEER_HALF).astype(jnp.bfloat16))
        u_t.append(jnp.transpose(peer_u[l].astype(jnp.bfloat16)))
        v_bf.append(peer_v[l].astype(jnp.bfloat16))

    outs = []
    for b in range(B):
        xb = x[b:b + 1]
        cb = ctx[b:b + 1]
        for l in range(depth):
            with_ctx = l < depth - 1
            sh1, sc1, g1, sh2, sc2, g2 = [m[b:b + 1] for m in mods[l]]
            csh1, csc1, cg1, csh2, csc2, cg2 = cmods[l]
            proj = _in_proj(xb.reshape(L, D), norm1_g[l], sh1, sc1, w_in_bf[l]).reshape(1, L, -1)
            projc = _in_proj(cb.reshape(Lc, D), norm1_g[l], csh1, csc1, w_in_bf[l]).reshape(1, Lc, -1)
            mix, mixc = _token_mixers(proj, projc, row, col, l, qn_a[l], kn_a[l], qn_b[l], kn_b[l],
                                      lam_q1[l], lam_k1[l], lam_q2[l], lam_k2[l], subln_g[l],
                                      s5_ops[l], s5_d[l], s5_w_glu[l], with_ctx)
            x2 = _out_proj(mix.reshape(L, -1), w_out_bf[l], xb.reshape(L, D), g1)
            xb = _peer_residual(x2, norm2_g[l], sh2, sc2, g2, wq_bf[l], keys_bf[l], u_t[l], v_bf[l],
                                tokens_per_mod=L).reshape(1, L, D)
            if with_ctx:
                c2 = _out_proj(mixc.reshape(Lc, -1), w_out_bf[l], cb.reshape(Lc, D), cg1)
                cb = _peer_residual(c2, norm2_g[l], csh2, csc2, cg2, wq_bf[l], keys_bf[l],
                                    u_t[l], v_bf[l], tokens_per_mod=Lc).reshape(1, Lc, D)
        outs.append(xb)
    return jnp.concatenate(outs, axis=0)
```

```python
import functools
import math

import jax
import jax.numpy as jnp
from jax import lax
from jax.experimental import pallas as pl
from jax.experimental.pallas import tpu as pltpu
from jax.experimental.pallas import tpu_sc as plsc

D_MODEL = 1024
GRID_W = 64
EPS = 1e-6
ROPE_THETA = 10000.0
HEAD_DIM = 64
A_HEADS = 6
A_KV_HEADS = 2
A_GROUP = A_HEADS // A_KV_HEADS
A_Q = A_HEADS * HEAD_DIM
A_KV = A_KV_HEADS * HEAD_DIM
B_HEADS = 6
B_SUB = 32
B_VDIM = 2 * B_SUB
B_QK = B_HEADS * 2 * B_SUB
B_V = B_HEADS * B_VDIM
C_GROUPS = 16
C_GROUP_CH = 16
C_WIDTH = C_GROUPS * C_GROUP_CH
C_STATE = 64
PEER_HEADS = 8
N_KEYS = 128
PEER_TOPK = 16
PEER_HALF = 128
TOKEN_CHUNK = 128

LANES = 128
SUBLANES = 8
V_EXT = LANES
LOG2E = math.log2(math.e)
NEG_BIG = -1e30
VMEM_LIMIT = 48 * 1024 * 1024


TQ = 1024
ATTN_TK = 2816
K_PACK = LANES


def _flash_unit(s, v_t, m_ref, acc_ref, unit):
    m_old = m_ref[unit]
    m_new = jnp.maximum(m_old, jnp.max(s, axis=0, keepdims=True))
    alpha = jnp.exp2(m_old - m_new)
    p = jnp.exp2(s - m_new).astype(jnp.bfloat16)
    acc_ref[unit] = alpha * acc_ref[unit] + jnp.dot(v_t, p, preferred_element_type=jnp.float32)
    m_ref[unit] = m_new


def _flash_all_units(q_ref, k_ref, vt_ref, m_ref, acc_ref, *, n_units, units_per_group, tk, n_keys):
    m_ref[...] = jnp.full(m_ref.shape, NEG_BIG, jnp.float32)
    acc_ref[...] = jnp.zeros(acc_ref.shape, jnp.float32)

    def body(j, carry):
        off = pl.multiple_of(j * tk, tk)
        k = k_ref[0, pl.ds(off, tk), :]

        def scores(u):
            return jnp.dot(k, q_ref[0, u], preferred_element_type=jnp.float32)

        s = scores(0)
        for u in range(n_units):
            s_next = scores(u + 1) if u + 1 < n_units else None
            _flash_unit(s, vt_ref[0, u // units_per_group, :, pl.ds(off, tk)], m_ref, acc_ref, u)
            s = s_next
        return carry

    lax.fori_loop(0, n_keys // tk, body, 0)


def _gqa_kernel(q_ref, k_ref, vt_ref, o_ref, m_ref, acc_ref, *, tk, n_keys):
    _flash_all_units(q_ref, k_ref, vt_ref, m_ref, acc_ref,
                     n_units=A_HEADS, units_per_group=A_GROUP, tk=tk, n_keys=n_keys)
    for h in range(A_HEADS):
        acc = acc_ref[h]
        o_ref[0, h * HEAD_DIM:(h + 1) * HEAD_DIM, :] = (
            acc[:HEAD_DIM] / acc[HEAD_DIM:HEAD_DIM + 1]).astype(o_ref.dtype)


def _diff_kernel(lam_ref, q_ref, k_ref, vt_ref, g_ref, o_ref, m_ref, acc_ref, *, tk, n_keys, out_scale):
    _flash_all_units(q_ref, k_ref, vt_ref, m_ref, acc_ref,
                     n_units=4, units_per_group=2, tk=tk, n_keys=n_keys)
    lam = lam_ref[0]
    for h in range(2):
        a1 = acc_ref[2 * h]
        a2 = acc_ref[2 * h + 1]
        o = a1[:B_VDIM] / a1[B_VDIM:B_VDIM + 1] - lam * (a2[:B_VDIM] / a2[B_VDIM:B_VDIM + 1])
        o = o * lax.rsqrt(jnp.mean(o * o, axis=0, keepdims=True) + EPS)
        o_ref[0, h * B_VDIM:(h + 1) * B_VDIM, :] = (o * g_ref[...] * out_scale).astype(o_ref.dtype)


def _gqa_attention(q_t, k, v_t):
    B, _, _, L = q_t.shape
    S = k.shape[1]
    tq, tk = min(TQ, L), min(ATTN_TK, S)
    return pl.pallas_call(
        functools.partial(_gqa_kernel, tk=tk, n_keys=S),
        grid=(B, L // tq),
        in_specs=[
            pl.BlockSpec((1, A_HEADS, K_PACK, tq), lambda b, i: (b, 0, 0, i)),
            pl.BlockSpec((1, S, K_PACK), lambda b, i: (b, 0, 0)),
            pl.BlockSpec((1, A_KV_HEADS, V_EXT, S), lambda b, i: (b, 0, 0, 0)),
        ],
        out_specs=pl.BlockSpec((1, A_Q, tq), lambda b, i: (b, 0, i)),
        out_shape=jax.ShapeDtypeStruct((B, A_Q, L), jnp.bfloat16),
        scratch_shapes=[pltpu.VMEM((A_HEADS, 1, tq), jnp.float32),
                        pltpu.VMEM((A_HEADS, V_EXT, tq), jnp.float32)],
        compiler_params=pltpu.CompilerParams(
            dimension_semantics=("arbitrary", "arbitrary"), vmem_limit_bytes=VMEM_LIMIT),
        name="gqa_attention",
    )(q_t, k, v_t)


def _diff_attention(q_t, k, v_t, lam, subln_g, out_scale):
    B, _, _, L = q_t.shape
    S = k.shape[1]
    tq, tk = min(TQ, L), min(ATTN_TK, S)
    return pl.pallas_call(
        functools.partial(_diff_kernel, tk=tk, n_keys=S, out_scale=out_scale),
        grid=(B, B_HEADS // 2, L // tq),
        in_specs=[
            pl.BlockSpec(memory_space=pltpu.SMEM),
            pl.BlockSpec((1, 4, K_PACK, tq), lambda b, h, i: (b, h, 0, i)),
            pl.BlockSpec((1, S, K_PACK), lambda b, h, i: (b, 0, h)),
            pl.BlockSpec((1, 2, V_EXT, S), lambda b, h, i: (b, h, 0, 0)),
            pl.BlockSpec((B_VDIM, 1), lambda b, h, i: (0, 0)),
        ],
        out_specs=pl.BlockSpec((1, 2 * B_VDIM, tq), lambda b, h, i: (b, h, i)),
        out_shape=jax.ShapeDtypeStruct((B, B_V, L), jnp.bfloat16),
        scratch_shapes=[pltpu.VMEM((4, 1, tq), jnp.float32),
                        pltpu.VMEM((4, V_EXT, tq), jnp.float32)],
        compiler_params=pltpu.CompilerParams(
            dimension_semantics=("arbitrary", "arbitrary", "arbitrary"), vmem_limit_bytes=VMEM_LIMIT),
        name="diff_attention",
    )(lam.reshape(1).astype(jnp.float32), q_t, k, v_t, subln_g.reshape(B_VDIM, 1).astype(jnp.float32))


def _value_ext_t(v):
    B, S, H, d = v.shape
    ones = jnp.ones((B, S, H, 1), v.dtype)
    pad = jnp.zeros((B, S, H, V_EXT - d - 1), v.dtype)
    return jnp.transpose(jnp.concatenate([v, ones, pad], axis=-1), (0, 2, 3, 1)).astype(jnp.bfloat16)


def _query_ext_t(q, offsets):
    B, L, U, d = q.shape
    q_t = jnp.transpose(q, (0, 2, 3, 1)).astype(jnp.bfloat16)
    units = [jnp.pad(q_t[:, u], ((0, 0), (off, K_PACK - d - off), (0, 0))) for u, off in enumerate(offsets)]
    return jnp.stack(units, axis=1)


def _rms(x, g):
    xf = x.astype(jnp.float32)
    y = xf * lax.rsqrt(jnp.mean(xf * xf, axis=-1, keepdims=True) + EPS)
    return (y * g.astype(jnp.float32)).astype(x.dtype)


def _rope_2d(x, row, col):
    d = x.shape[-1]
    half = d // 2
    inv = 1.0 / (ROPE_THETA ** (jnp.arange(0, half, 2, dtype=jnp.float32) / half))
    ang = jnp.concatenate([row.astype(jnp.float32)[:, None] * inv[None, :],
                           col.astype(jnp.float32)[:, None] * inv[None, :]], axis=-1)
    cos = jnp.cos(ang)[None, :, None, :]
    sin = jnp.sin(ang)[None, :, None, :]
    xr = x.astype(jnp.float32).reshape(x.shape[:-1] + (half, 2))
    x0, x1 = xr[..., 0], xr[..., 1]
    out = jnp.stack([x0 * cos - x1 * sin, x0 * sin + x1 * cos], axis=-1)
    return out.reshape(x.shape).astype(x.dtype)


PROJ_TM = 512


S5_TC = 64
S5_ROW_TILE = 16
S5_CW = S5_TC * C_GROUP_CH
S5_SW = 2 * C_STATE


def _s5_operators(a_re, a_im, log_dt, b_re, b_im, c_re, c_im):
    f32 = jnp.float32
    tc, P, N, G = S5_TC, C_GROUP_CH, C_STATE, C_GROUPS
    j = jnp.arange(tc + 1, dtype=f32)
    kerns, p_parts, q_parts, a_parts = [], [], [], []
    for dirn in range(2):
        A = lax.complex(a_re[dirn].astype(f32), a_im[dirn].astype(f32))
        adt = A * jnp.exp(log_dt[dirn].astype(f32))[:, None]
        abar = jnp.exp(adt)
        bbar = ((abar - 1) / A)[..., None] * lax.complex(b_re[dirn].astype(f32), b_im[dirn].astype(f32))
        cmat = lax.complex(c_re[dirn].astype(f32), c_im[dirn].astype(f32))
        pw = jnp.exp(adt[None] * j[:, None, None])
        kerns.append(jnp.einsum('gpn,jgn,gnq->gqjp', cmat, pw[:tc], bbar).real)
        inj_pw = pw[:tc] if dirn == 1 else pw[:tc][::-1]
        inj = inj_pw[:, :, :, None] * bbar[None]
        p_parts.append(jnp.transpose(inj, (1, 0, 3, 2)).reshape(G, S5_CW, N))
        out_pw = pw[1:][::-1] if dirn == 1 else pw[1:]
        outm = cmat[None] * out_pw[:, :, None, :]
        q_parts.append(jnp.transpose(outm, (1, 3, 0, 2)).reshape(G, N, S5_CW))
        a_parts.append(pw[tc])
    f_seq = jnp.concatenate([jnp.flip(kerns[1][:, :, 1:], axis=2), kerns[0][:, :, :1] + kerns[1][:, :, :1],
                             kerns[0][:, :, 1:], jnp.zeros((G, P, 1, P), f32)], axis=2)
    rolled = jnp.tile(f_seq.astype(jnp.bfloat16).reshape(G, P, 2 * tc * P), (1, 1, tc))
    rolled = rolled[:, :, :tc * (2 * tc - 1) * P].reshape(G, P, tc, (2 * tc - 1) * P)
    toep = jnp.transpose(rolled[:, :, :, (tc - 1) * P:], (0, 2, 1, 3)).reshape(G, S5_CW, S5_CW)
    zeros = jnp.zeros((G, N, S5_CW), jnp.bfloat16)
    bf = lambda m: m.astype(jnp.bfloat16)
    w_mat = jnp.concatenate([
        toep,
        bf(q_parts[0].real), zeros, bf(-q_parts[0].imag), zeros,
        zeros, bf(q_parts[1].real), zeros, bf(-q_parts[1].imag),
    ], axis=1)
    p_mat = jnp.concatenate([p_parts[0].real, p_parts[1].real, p_parts[0].imag, p_parts[1].imag], axis=-1)
    a_chunk = jnp.stack([jnp.concatenate([a_parts[0].real, a_parts[1].real], axis=-1),
                         jnp.concatenate([a_parts[0].imag, a_parts[1].imag], axis=-1)], axis=1)
    return p_mat.astype(jnp.bfloat16), w_mat, a_chunk.astype(f32)


def _s5_scan_kernel(u_ref, p_ref, w_ref, a_ref, y_ref, s_scr, hs_scr, *, rows, order_f, order_r):
    f32 = jnp.float32
    u = u_ref[0]
    s_scr[...] = jnp.dot(u, p_ref[0], preferred_element_type=f32)
    hs_scr[...] = jnp.zeros(hs_scr.shape, f32)
    a_re = a_ref[0, 0:1, :]
    a_im = a_ref[0, 1:2, :]
    is_fwd = lax.broadcasted_iota(jnp.int32, (rows, S5_SW), 1) < C_STATE
    h_re = jnp.zeros((rows, S5_SW), f32)
    h_im = jnp.zeros((rows, S5_SW), f32)
    for cf, cr in zip(order_f, order_r):
        rf = slice(cf * rows, (cf + 1) * rows)
        rr = slice(cr * rows, (cr + 1) * rows)
        hs_scr[rf, 0:S5_SW] = h_re
        hs_scr[rf, S5_SW:2 * S5_SW] = h_im
        hs_scr[rr, 2 * S5_SW:3 * S5_SW] = h_re
        hs_scr[rr, 3 * S5_SW:4 * S5_SW] = h_im
        s_re = jnp.where(is_fwd, s_scr[rf, 0:S5_SW], s_scr[rr, 0:S5_SW])
        s_im = jnp.where(is_fwd, s_scr[rf, S5_SW:2 * S5_SW], s_scr[rr, S5_SW:2 * S5_SW])
        h_re, h_im = a_re * h_re - a_im * h_im + s_re, a_re * h_im + a_im * h_re + s_im
    y = jnp.dot(u, w_ref[0, 0:S5_CW, :], preferred_element_type=f32)
    y += jnp.dot(hs_scr[...].astype(jnp.bfloat16), w_ref[0, S5_CW:, :], preferred_element_type=f32)
    y_ref[0] = y


def _s5_scan(u_seq, n_ctx_chunks, p_mat, w_mat, a_chunk):
    B, S, _ = u_seq.shape
    G, P = C_GROUPS, C_GROUP_CH
    nc = S // S5_TC
    R = -(-(nc * B) // S5_ROW_TILE) * S5_ROW_TILE
    ug = jnp.transpose(u_seq.reshape(B, nc, S5_TC, G, P), (3, 1, 0, 2, 4))
    ug = jnp.pad(ug.reshape(G, nc * B, S5_CW), ((0, 0), (0, R - nc * B), (0, 0))).astype(jnp.bfloat16)
    order_f = tuple(range(nc))
    order_r = tuple(range(n_ctx_chunks - 1, -1, -1)) + tuple(range(nc - 1, n_ctx_chunks - 1, -1))
    y = pl.pallas_call(
        functools.partial(_s5_scan_kernel, rows=B, order_f=order_f, order_r=order_r),
        grid=(G,),
        in_specs=[
            pl.BlockSpec((1, R, S5_CW), lambda g: (g, 0, 0)),
            pl.BlockSpec((1, S5_CW, 2 * S5_SW), lambda g: (g, 0, 0)),
            pl.BlockSpec((1, S5_CW + 4 * S5_SW, S5_CW), lambda g: (g, 0, 0)),
            pl.BlockSpec((1, 2, S5_SW), lambda g: (g, 0, 0)),
        ],
        out_specs=pl.BlockSpec((1, R, S5_CW), lambda g: (g, 0, 0)),
        out_shape=jax.ShapeDtypeStruct((G, R, S5_CW), jnp.float32),
        scratch_shapes=[pltpu.VMEM((R, 2 * S5_SW), jnp.float32),
                        pltpu.VMEM((R, 4 * S5_SW), jnp.float32)],
        compiler_params=pltpu.CompilerParams(dimension_semantics=("arbitrary",),
                                             vmem_limit_bytes=VMEM_LIMIT),
        name="s5_scan",
    )(ug, p_mat, w_mat, a_chunk)
    y = y[:, :nc * B].reshape(G, nc, B, S5_TC, P)
    return jnp.transpose(y, (2, 1, 3, 0, 4)).reshape(B, S, C_WIDTH)


def _s5_glu_kernel(y_ref, u_ref, d_ref, w_ref, o_ref):
    y = y_ref[...] + d_ref[...] * u_ref[...]
    g = jax.nn.gelu(y)
    z = jnp.dot(g.astype(jnp.bfloat16), w_ref[...], preferred_element_type=jnp.float32)
    o_ref[...] = (g * jax.nn.sigmoid(z)).astype(o_ref.dtype)


def _s5_glu(y, u, d_skip, w_glu):
    T = y.shape[0]
    tb = math.gcd(T, 1024)
    return pl.pallas_call(
        _s5_glu_kernel,
        grid=(T // tb,),
        in_specs=[pl.BlockSpec((tb, C_WIDTH), lambda i: (i, 0)),
                  pl.BlockSpec((tb, C_WIDTH), lambda i: (i, 0)),
                  pl.BlockSpec((1, C_WIDTH), lambda i: (0, 0)),
                  pl.BlockSpec((C_WIDTH, C_WIDTH), lambda i: (0, 0))],
        out_specs=pl.BlockSpec((tb, C_WIDTH), lambda i: (i, 0)),
        out_shape=jax.ShapeDtypeStruct((T, C_WIDTH), jnp.float32),
        compiler_params=pltpu.CompilerParams(dimension_semantics=("arbitrary",)),
        name="s5_glu",
    )(y, u, d_skip.reshape(1, C_WIDTH).astype(jnp.float32), w_glu.astype(jnp.bfloat16))


def _s5_mixer(u, uc, s5_ops, d_skip, w_glu, with_ctx):
    B, L, _ = u.shape
    Lc = uc.shape[1]
    p_mat, w_mat, a_chunk = s5_ops
    u_seq = jnp.concatenate([uc, u], axis=1).astype(jnp.float32)
    y = _s5_scan(u_seq, Lc // S5_TC, p_mat, w_mat, a_chunk)
    if with_ctx:
        out = _s5_glu(y.reshape(B * (Lc + L), C_WIDTH), u_seq.reshape(B * (Lc + L), C_WIDTH), d_skip, w_glu)
        out = out.reshape(B, Lc + L, C_WIDTH)
        return out[:, Lc:], out[:, :Lc]
    out = _s5_glu(y[:, Lc:].reshape(B * L, C_WIDTH), u.astype(jnp.float32).reshape(B * L, C_WIDTH), d_skip, w_glu)
    return out.reshape(B, L, C_WIDTH), None


def _in_proj_kernel(x_ref, g_ref, sh_ref, sc_ref, w_ref, o_ref):
    x = x_ref[...]
    y = x * lax.rsqrt(jnp.mean(x * x, axis=-1, keepdims=True) + EPS) * g_ref[...]
    h = (y * (1.0 + sc_ref[0]) + sh_ref[0]).astype(jnp.bfloat16)
    o_ref[...] = jnp.dot(h, w_ref[...], preferred_element_type=jnp.float32)


def _in_proj(x, norm_g, shift, scale, w_bf):
    T, D = x.shape
    N = w_bf.shape[1]
    tm = min(PROJ_TM, T)
    return pl.pallas_call(
        _in_proj_kernel,
        grid=(T // tm,),
        in_specs=[pl.BlockSpec((tm, D), lambda i: (i, 0)),
                  pl.BlockSpec((1, D), lambda i: (0, 0)),
                  pl.BlockSpec((1, 1, D), lambda i: (0, 0, 0)),
                  pl.BlockSpec((1, 1, D), lambda i: (0, 0, 0)),
                  pl.BlockSpec((D, N), lambda i: (0, 0))],
        out_specs=pl.BlockSpec((tm, N), lambda i: (i, 0)),
        out_shape=jax.ShapeDtypeStruct((T, N), jnp.float32),
        compiler_params=pltpu.CompilerParams(dimension_semantics=("arbitrary",), vmem_limit_bytes=VMEM_LIMIT),
        name="in_proj",
    )(x, norm_g.reshape(1, D).astype(jnp.float32), shift, scale, w_bf)


def _out_proj_kernel(m_ref, w_ref, x_ref, g_ref, o_ref):
    o_ref[...] = x_ref[...] + g_ref[0] * jnp.dot(m_ref[...], w_ref[...], preferred_element_type=jnp.float32)


def _out_proj(mix_bf, w_bf, x, gate):
    T, D = x.shape
    K = mix_bf.shape[1]
    tm = min(PROJ_TM, T)
    return pl.pallas_call(
        _out_proj_kernel,
        grid=(T // tm,),
        in_specs=[pl.BlockSpec((tm, K), lambda i: (i, 0)),
                  pl.BlockSpec((K, D), lambda i: (0, 0)),
                  pl.BlockSpec((tm, D), lambda i: (i, 0)),
                  pl.BlockSpec((1, 1, D), lambda i: (0, 0, 0))],
        out_specs=pl.BlockSpec((tm, D), lambda i: (i, 0)),
        out_shape=jax.ShapeDtypeStruct((T, D), jnp.float32),
        compiler_params=pltpu.CompilerParams(dimension_semantics=("arbitrary",), vmem_limit_bytes=VMEM_LIMIT),
        name="out_proj",
    )(mix_bf, w_bf, x, gate)


def _ada_proj_kernel(s_ref, w_ref, b_ref, o_ref):
    o_ref[...] = jnp.dot(s_ref[...].astype(jnp.bfloat16), w_ref[...], preferred_element_type=jnp.float32) + b_ref[...]


def _ada_proj(s, w_bf, bias):
    M, D = s.shape
    N = w_bf.shape[1]
    rows = -(-M // SUBLANES) * SUBLANES
    tn = N // 6
    out = pl.pallas_call(
        _ada_proj_kernel,
        grid=(N // tn,),
        in_specs=[pl.BlockSpec((rows, D), lambda j: (0, 0)),
                  pl.BlockSpec((D, tn), lambda j: (0, j)),
                  pl.BlockSpec((1, tn), lambda j: (0, j))],
        out_specs=pl.BlockSpec((rows, tn), lambda j: (0, j)),
        out_shape=jax.ShapeDtypeStruct((rows, N), jnp.float32),
        compiler_params=pltpu.CompilerParams(dimension_semantics=("arbitrary",)),
        name="ada_proj",
    )(jnp.pad(s, ((0, rows - M), (0, 0))), w_bf, bias.reshape(1, N).astype(jnp.float32))
    return out[:M]


def _token_mixers(proj, projc, row, col, layer_idx, qn_a, kn_a, qn_b, kn_b,
                  lam_q1, lam_k1, lam_q2, lam_k2, subln_g, s5_ops, d_skip, w_glu, with_ctx):
    B, L, _ = proj.shape
    Lc = projc.shape[1]
    cuts = [A_Q, A_Q + A_KV, A_Q + 2 * A_KV, A_Q + 2 * A_KV + B_QK,
            A_Q + 2 * A_KV + 2 * B_QK, A_Q + 2 * A_KV + 2 * B_QK + B_V]
    qa, ka, va, qb, kb, vb, us = jnp.split(proj, cuts, axis=-1)
    qac, kac, vac, qbc, kbc, vbc, usc = jnp.split(projc, cuts, axis=-1)

    qa = _rope_2d(_rms(qa.reshape(B, L, A_HEADS, HEAD_DIM), qn_a), row, col)
    ka = _rope_2d(_rms(ka.reshape(B, L, A_KV_HEADS, HEAD_DIM), kn_a), row, col)
    va = va.reshape(B, L, A_KV_HEADS, HEAD_DIM)
    kac = _rms(kac.reshape(B, Lc, A_KV_HEADS, HEAD_DIM), kn_a)
    vac = vac.reshape(B, Lc, A_KV_HEADS, HEAD_DIM)
    ka_all = jnp.concatenate([kac, ka], axis=1)
    va_all = jnp.concatenate([vac, va], axis=1)
    qa_t = _query_ext_t(qa * (HEAD_DIM ** -0.5 * LOG2E), [(hq // A_GROUP) * HEAD_DIM for hq in range(A_HEADS)])
    ka_pack = ka_all.reshape(B, Lc + L, A_KV).astype(jnp.bfloat16)
    o_a = jnp.transpose(_gqa_attention(qa_t, ka_pack, _value_ext_t(va_all)), (0, 2, 1))

    lambda_init = 0.8 - 0.6 * math.exp(-0.3 * layer_idx)
    lam = (jnp.exp(jnp.sum(lam_q1.astype(jnp.float32) * lam_k1.astype(jnp.float32)))
           - jnp.exp(jnp.sum(lam_q2.astype(jnp.float32) * lam_k2.astype(jnp.float32))) + lambda_init)
    qb = _rope_2d(_rms(qb.reshape(B, L, 2 * B_HEADS, B_SUB), qn_b), row, col)
    kb = _rope_2d(_rms(kb.reshape(B, L, 2 * B_HEADS, B_SUB), kn_b), row, col)
    vb = vb.reshape(B, L, B_HEADS, B_VDIM)
    kbc = _rms(kbc.reshape(B, Lc, 2 * B_HEADS, B_SUB), kn_b)
    vbc = vbc.reshape(B, Lc, B_HEADS, B_VDIM)
    kb_all = jnp.concatenate([kbc, kb], axis=1)
    vb_all = jnp.concatenate([vbc, vb], axis=1)
    qb_t = _query_ext_t(qb * (B_SUB ** -0.5 * LOG2E), [(j % 4) * B_SUB for j in range(2 * B_HEADS)])
    kb_pack = kb_all.reshape(B, Lc + L, B_QK).astype(jnp.bfloat16)
    o_b = jnp.transpose(_diff_attention(qb_t, kb_pack, _value_ext_t(vb_all), lam, subln_g, 1 - lambda_init),
                        (0, 2, 1))

    o_c, o_cc = _s5_mixer(us, usc, s5_ops, d_skip, w_glu, with_ctx)

    mix = jnp.concatenate([o_a, o_b, o_c.astype(jnp.bfloat16)], axis=-1)
    if not with_ctx:
        return mix, None
    qac = _rms(qac.reshape(B, Lc, A_HEADS, HEAD_DIM), qn_a)
    qac_t = _query_ext_t(qac * (HEAD_DIM ** -0.5 * LOG2E), [(hq // A_GROUP) * HEAD_DIM for hq in range(A_HEADS)])
    o_ac = jnp.transpose(_gqa_attention(qac_t, kac.reshape(B, Lc, A_KV).astype(jnp.bfloat16), _value_ext_t(vac)),
                         (0, 2, 1))
    qbc = _rms(qbc.reshape(B, Lc, 2 * B_HEADS, B_SUB), qn_b)
    qbc_t = _query_ext_t(qbc * (B_SUB ** -0.5 * LOG2E), [(j % 4) * B_SUB for j in range(2 * B_HEADS)])
    o_bc = jnp.transpose(_diff_attention(qbc_t, kbc.reshape(B, Lc, B_QK).astype(jnp.bfloat16), _value_ext_t(vbc),
                                         lam, subln_g, 1 - lambda_init), (0, 2, 1))
    mixc = jnp.concatenate([o_ac, o_bc, o_cc.astype(jnp.bfloat16)], axis=-1)
    return mix, mixc


PEER_TB = 256
N_SLOTS = PEER_HEADS * PEER_TOPK
CAND_ROWS = 80
INVALID_FLAT = 1.0e9
SC_CORES = 2
SC_SUBCORES = 16


def _cand_flat_table():
    f = [float(b) for b in range(16)]
    for a in range(1, 8):
        f += [float(a * 16 + b) if (a + 1) * (b + 1) <= 16 else INVALID_FLAT for b in range(8)]
    f += [float(a * 16) for a in range(8, 16)]
    return jnp.broadcast_to(jnp.asarray(f, jnp.float32)[:, None], (CAND_ROWS, PEER_TB))


def _pair_rows(first, second):
    blocks = [first[0:1] + second]
    blocks += [first[a:a + 1] + second[0:8] for a in range(1, 8)]
    blocks += [first[8:16] + second[0:1]]
    return jnp.concatenate(blocks, axis=0)


N_EXPERTS = N_KEYS * N_KEYS


def _peer_route_kernel(x_ref, g_ref, sh_ref, sc_ref, wq_ref, keys_ref, ftab_ref, ut_ref,
                       idx_ref, gate_ref, s_ref, h_scr, q_scr, val_scr, id_scr, sc_scr, e_scr):
    f32 = jnp.float32
    head = pl.program_id(1)

    @pl.when(head == 0)
    def _():
        x = x_ref[...]
        y = x * lax.rsqrt(jnp.mean(x * x, axis=-1, keepdims=True) + EPS) * g_ref[...]
        h2 = (y * (1.0 + sc_ref[0]) + sh_ref[0]).astype(jnp.bfloat16)
        h_scr[...] = h2
        q = jnp.dot(h2, wq_ref[...], preferred_element_type=f32)
        for ht in range(2 * PEER_HEADS):
            q_scr[ht] = q[:, ht * PEER_HALF:(ht + 1) * PEER_HALF].astype(jnp.bfloat16)

    sub_scores = [lax.dot_general(keys_ref[head * 2 + t], q_scr[head * 2 + t], (((1,), (1,)), ((), ())),
                                  preferred_element_type=f32) for t in range(2)]
    s_ref[...] = jnp.dot(h_scr[...], ut_ref[...], preferred_element_type=f32)

    ftab = ftab_ref[...]
    key_id = lax.broadcasted_iota(jnp.int32, (N_KEYS, PEER_TB), 0).astype(f32)
    neg_inf = -jnp.inf
    for t in range(2):
        s = sub_scores[t]
        for r in range(PEER_TOPK):
            m = jnp.max(s, axis=0, keepdims=True)
            pick = jnp.min(jnp.where(s == m, key_id, float(N_KEYS)), axis=0, keepdims=True)
            s = jnp.where(key_id == pick, neg_inf, s)
            val_scr[t, r:r + 1, :] = m
            id_scr[t, r:r + 1, :] = pick
    cand = _pair_rows(val_scr[0], val_scr[1])
    cand = jnp.where(ftab < float(PEER_TOPK * PEER_TOPK), cand, neg_inf)
    expert = _pair_rows(id_scr[0] * float(N_KEYS), id_scr[1])
    for r in range(PEER_TOPK):
        m = jnp.max(cand, axis=0, keepdims=True)
        pick = jnp.min(jnp.where(cand == m, ftab, INVALID_FLAT), axis=0, keepdims=True)
        hit = ftab == pick
        sc_scr[r:r + 1, :] = m
        e_scr[r:r + 1, :] = jnp.sum(jnp.where(hit, expert, 0.0), axis=0, keepdims=True)
        cand = jnp.where(hit, neg_inf, cand)
    sc = sc_scr[...]
    p = jnp.exp(sc - sc[0:1])
    rows = pl.ds(pl.multiple_of(head * PEER_TOPK, PEER_TOPK), PEER_TOPK)
    idx_ref[rows, :] = e_scr[...].astype(jnp.int32)
    gate_ref[rows, :] = p / jnp.sum(p, axis=0, keepdims=True)


def _peer_route(x, norm_g, shift, scale, wq_bf, keys_bf, u_t, tokens_per_mod):
    T, D = x.shape
    nq = wq_bf.shape[1]
    te = N_EXPERTS // PEER_HEADS
    blocks_per_mod = tokens_per_mod // PEER_TB
    return pl.pallas_call(
        _peer_route_kernel,
        grid=(T // PEER_TB, PEER_HEADS),
        in_specs=[
            pl.BlockSpec((PEER_TB, D), lambda i, j: (i, 0)),
            pl.BlockSpec((1, D), lambda i, j: (0, 0)),
            pl.BlockSpec((1, 1, D), lambda i, j: (i // blocks_per_mod, 0, 0)),
            pl.BlockSpec((1, 1, D), lambda i, j: (i // blocks_per_mod, 0, 0)),
            pl.BlockSpec((D, nq), lambda i, j: (0, 0)),
            pl.BlockSpec((2 * PEER_HEADS, N_KEYS, PEER_HALF), lambda i, j: (0, 0, 0)),
            pl.BlockSpec((CAND_ROWS, PEER_TB), lambda i, j: (0, 0)),
            pl.BlockSpec((D, te), lambda i, j: (0, j)),
        ],
        out_specs=[
            pl.BlockSpec((N_SLOTS, PEER_TB), lambda i, j: (0, i)),
            pl.BlockSpec((N_SLOTS, PEER_TB), lambda i, j: (0, i)),
            pl.BlockSpec((PEER_TB, te), lambda i, j: (i, j)),
        ],
        out_shape=[
            jax.ShapeDtypeStruct((N_SLOTS, T), jnp.int32),
            jax.ShapeDtypeStruct((N_SLOTS, T), jnp.float32),
            jax.ShapeDtypeStruct((T, N_EXPERTS), jnp.float32),
        ],
        scratch_shapes=[pltpu.VMEM((PEER_TB, D), jnp.bfloat16),
                        pltpu.VMEM((2 * PEER_HEADS, PEER_TB, PEER_HALF), jnp.bfloat16),
                        pltpu.VMEM((2, PEER_TOPK, PEER_TB), jnp.float32),
                        pltpu.VMEM((2, PEER_TOPK, PEER_TB), jnp.float32),
                        pltpu.VMEM((PEER_TOPK, PEER_TB), jnp.float32),
                        pltpu.VMEM((PEER_TOPK, PEER_TB), jnp.float32)],
        compiler_params=pltpu.CompilerParams(dimension_semantics=("arbitrary", "arbitrary"),
                                             vmem_limit_bytes=VMEM_LIMIT),
        name="peer_route",
    )(x, norm_g.reshape(1, D).astype(jnp.float32), shift, scale, wq_bf, keys_bf, _cand_flat_table(), u_t)
SC_LANES = 16
SC_TOKENS = 32
DENSE_TM = 512
DENSE_TE = 2048


def _sc_worker_tokens(n_tokens):
    per_worker = n_tokens // (SC_CORES * SC_SUBCORES)
    blk = min(SC_TOKENS, per_worker)
    return per_worker, blk


def _sc_extract(scores, idx_flat):
    T = scores.shape[0]
    per_worker, blk = _sc_worker_tokens(T)
    mesh = plsc.VectorSubcoreMesh(core_axis_name="c", subcore_axis_name="s")

    def body(s_hbm, idx_hbm, out_hbm, row0_v, row1_v, idx_v, out_v, rsem):
        rows = (row0_v, row1_v)
        wid = lax.axis_index("s") * SC_CORES + lax.axis_index("c")
        t0 = wid * per_worker

        def row_copy(tok, slot):
            return pltpu.make_async_copy(s_hbm.at[tok], rows[slot], rsem.at[slot])

        def block(b, carry):
            tb = t0 + b * blk
            pltpu.sync_copy(idx_hbm.at[pl.ds(tb * N_SLOTS, blk * N_SLOTS)], idx_v)
            row_copy(tb, 0).start()

            def pair(i2, c):
                for slot in range(2):
                    i = i2 * 2 + slot

                    @pl.when(i + 1 < blk)
                    def _():
                        row_copy(tb + i + 1, 1 - slot).start()

                    row_copy(tb + i, slot).wait()
                    for j in range(N_SLOTS // SC_LANES):
                        at = pl.ds(i * N_SLOTS + j * SC_LANES, SC_LANES)
                        out_v[at] = plsc.load_gather(rows[slot], [idx_v[at]])
                return c

            lax.fori_loop(0, blk // 2, pair, 0)
            pltpu.sync_copy(out_v, out_hbm.at[pl.ds(tb * N_SLOTS, blk * N_SLOTS)])
            return carry

        lax.fori_loop(0, per_worker // blk, block, 0)

    return pl.kernel(
        body, mesh=mesh,
        out_type=jax.ShapeDtypeStruct((T * N_SLOTS,), jnp.float32),
        scratch_types=[pltpu.VMEM((N_EXPERTS,), jnp.float32), pltpu.VMEM((N_EXPERTS,), jnp.float32),
                       pltpu.VMEM((blk * N_SLOTS,), jnp.int32), pltpu.VMEM((blk * N_SLOTS,), jnp.float32),
                       pltpu.SemaphoreType.DMA((2,))],
        compiler_params=pltpu.CompilerParams(needs_layout_passes=False),
        name="peer_extract",
    )(scores, idx_flat)


def _sc_scatter(w_flat, idx_flat, n_tokens):
    T = n_tokens
    per_worker, blk = _sc_worker_tokens(T)
    n_vec = N_SLOTS // SC_LANES
    mesh = plsc.VectorSubcoreMesh(core_axis_name="c", subcore_axis_name="s")

    def body(w_hbm, idx_hbm, out_hbm, row0_v, row1_v, idx_v, w_v, wsem):
        rows = (row0_v, row1_v)
        wid = lax.axis_index("s") * SC_CORES + lax.axis_index("c")
        t0 = wid * per_worker
        zero = jnp.zeros((SC_LANES,), jnp.float32)

        def zero_fill(k, c):
            for slot in range(2):
                rows[slot][pl.ds(k * SC_LANES, SC_LANES)] = zero
            return c

        lax.fori_loop(0, N_EXPERTS // SC_LANES, zero_fill, 0)

        def out_copy(tok, slot):
            return pltpu.make_async_copy(rows[slot], out_hbm.at[tok], wsem.at[slot])

        def retire(tok, i, slot):
            out_copy(tok, slot).wait()
            for j in range(n_vec):
                plsc.store_scatter(rows[slot], [idx_v[pl.ds(i * N_SLOTS + j * SC_LANES, SC_LANES)]], zero)

        def block(b, carry):
            tb = t0 + b * blk
            pltpu.sync_copy(idx_hbm.at[pl.ds(tb * N_SLOTS, blk * N_SLOTS)], idx_v)
            pltpu.sync_copy(w_hbm.at[pl.ds(tb * N_SLOTS, blk * N_SLOTS)], w_v)

            def pair(i2, c):
                for slot in range(2):
                    i = i2 * 2 + slot

                    @pl.when(i >= 2)
                    def _():
                        retire(tb + i - 2, i - 2, slot)

                    for j in range(n_vec):
                        at = pl.ds(i * N_SLOTS + j * SC_LANES, SC_LANES)
                        plsc.addupdate_scatter(rows[slot], [idx_v[at]], w_v[at])
                    out_copy(tb + i, slot).start()
                return c

            lax.fori_loop(0, blk // 2, pair, 0)
            for slot in range(2):
                retire(tb + blk - 2 + slot, blk - 2 + slot, slot)
            return carry

        lax.fori_loop(0, per_worker // blk, block, 0)

    return pl.kernel(
        body, mesh=mesh,
        out_type=jax.ShapeDtypeStruct((T, N_EXPERTS), jnp.float32),
        scratch_types=[pltpu.VMEM((N_EXPERTS,), jnp.float32), pltpu.VMEM((N_EXPERTS,), jnp.float32),
                       pltpu.VMEM((blk * N_SLOTS,), jnp.int32), pltpu.VMEM((blk * N_SLOTS,), jnp.float32),
                       pltpu.SemaphoreType.DMA((2,))],
        compiler_params=pltpu.CompilerParams(needs_layout_passes=False),
        name="peer_scatter",
    )(w_flat, idx_flat)


def _peer_gate_kernel(a_ref, g_ref, w_ref):
    w_ref[...] = jax.nn.gelu(a_ref[...]) * g_ref[...]


def _peer_gate(act, gate):
    T = act.shape[0]
    tb = min(2048, T)
    spec = pl.BlockSpec((tb, N_SLOTS), lambda i: (i, 0))
    return pl.pallas_call(
        _peer_gate_kernel, grid=(T // tb,), in_specs=[spec, spec], out_specs=spec,
        out_shape=jax.ShapeDtypeStruct((T, N_SLOTS), jnp.float32),
        compiler_params=pltpu.CompilerParams(dimension_semantics=("arbitrary",)),
        name="peer_gate",
    )(act, gate)


def _peer_combine_kernel(w_ref, v_ref, x_ref, g2_ref, o_ref, acc_ref):
    k = pl.program_id(1)

    @pl.when(k == 0)
    def _():
        acc_ref[...] = jnp.zeros(acc_ref.shape, jnp.float32)

    acc_ref[...] += jnp.dot(w_ref[...].astype(jnp.bfloat16), v_ref[...], preferred_element_type=jnp.float32)

    @pl.when(k == pl.num_programs(1) - 1)
    def _():
        o_ref[...] = x_ref[...] + g2_ref[0] * acc_ref[...]


def _peer_combine(w_dense, v_bf, x, g2, tokens_per_mod):
    T, D = x.shape
    tm = min(DENSE_TM, T)
    blocks_per_mod = tokens_per_mod // tm
    return pl.pallas_call(
        _peer_combine_kernel,
        grid=(T // tm, N_EXPERTS // DENSE_TE),
        in_specs=[pl.BlockSpec((tm, DENSE_TE), lambda i, k: (i, k)),
                  pl.BlockSpec((DENSE_TE, D), lambda i, k: (k, 0)),
                  pl.BlockSpec((tm, D), lambda i, k: (i, 0)),
                  pl.BlockSpec((1, 1, D), lambda i, k: (i // blocks_per_mod, 0, 0))],
        out_specs=pl.BlockSpec((tm, D), lambda i, k: (i, 0)),
        out_shape=jax.ShapeDtypeStruct((T, D), jnp.float32),
        scratch_shapes=[pltpu.VMEM((tm, D), jnp.float32)],
        compiler_params=pltpu.CompilerParams(dimension_semantics=("arbitrary", "arbitrary"),
                                             vmem_limit_bytes=VMEM_LIMIT),
        name="peer_combine",
    )(w_dense, v_bf, x, g2)


def _peer_residual(x, norm_g, shift, scale, gate2, wq_bf, keys_bf, u_t, v_bf, tokens_per_mod):
    T, D = x.shape
    idx_t, gate_t, scores = _peer_route(x, norm_g, shift, scale, wq_bf, keys_bf, u_t, tokens_per_mod)
    idx_flat = jnp.transpose(idx_t).reshape(T * N_SLOTS)
    act = _sc_extract(scores, idx_flat).reshape(T, N_SLOTS)
    w = _peer_gate(act, jnp.transpose(gate_t))
    w_dense = _sc_scatter(w.reshape(T * N_SLOTS), idx_flat, T)
    return _peer_combine(w_dense, v_bf, x, gate2, tokens_per_mod)


def kernel(x, c, ctx, c_ctx, w_ada, b_ada, norm1_g, norm2_g, w_in, w_out,
           qn_a, kn_a, qn_b, kn_b, lam_q1, lam_k1, lam_q2, lam_k2, subln_g,
           s5_a_re, s5_a_im, s5_log_dt, s5_b_re, s5_b_im, s5_c_re, s5_c_im,
           s5_d, s5_w_glu, peer_wq, peer_keys, peer_u, peer_v):
    B, L, D = x.shape
    Lc = ctx.shape[1]
    depth = w_in.shape[0]
    rows = L // GRID_W
    row = jnp.repeat(jnp.arange(rows, dtype=jnp.int32), GRID_W)
    col = jnp.tile(jnp.arange(GRID_W, dtype=jnp.int32), rows)
    s_c = jax.nn.silu(c)
    s_cc = jax.nn.silu(c_ctx)
    mods, cmods, s5_ops, wq_bf, keys_bf, u_t, v_bf = [], [], [], [], [], [], []
    w_in_bf = w_in.astype(jnp.bfloat16)
    w_out_bf = w_out.astype(jnp.bfloat16)
    for l in range(depth):
        ada = _ada_proj(jnp.concatenate([s_c, s_cc[None, :]], axis=0), w_ada[l].astype(jnp.bfloat16), b_ada[l])
        mods.append(jnp.split(ada[:B, None, :], 6, axis=-1))
        cmods.append([m.reshape(1, 1, D) for m in jnp.split(ada[B], 6, axis=-1)])
        s5_ops.append(_s5_operators(s5_a_re[l], s5_a_im[l], s5_log_dt[l], s5_b_re[l], s5_b_im[l],
                                    s5_c_re[l], s5_c_im[l]))
        wq_bf.append(peer_wq[l].astype(jnp.bfloat16))
        keys_bf.append(peer_keys[l].reshape(2 * PEER_HEADS, N_KEYS, PEER_HALF).astype(jnp.bfloat16))
        u_t.append(jnp.transpose(peer_u[l].astype(jnp.bfloat16)))
        v_bf.append(peer_v[l].astype(jnp.bfloat16))

    outs = []
    for b in range(B):
        xb = x[b:b + 1]
        cb = ctx[b:b + 1]
        for l in range(depth):
            with_ctx = l < depth - 1
            sh1, sc1, g1, sh2, sc2, g2 = [m[b:b + 1] for m in mods[l]]
            csh1, csc1, cg1, csh2, csc2, cg2 = cmods[l]
            proj = _in_proj(xb.reshape(L, D), norm1_g[l], sh1, sc1, w_in_bf[l]).reshape(1, L, -1)
            projc = _in_proj(cb.reshape(Lc, D), norm1_g[l], csh1, csc1, w_in_bf[l]).reshape(1, Lc, -1)
            mix, mixc = _token_mixers(proj, projc, row, col, l, qn_a[l], kn_a[l], qn_b[l], kn_b[l],
                                      lam_q1[l], lam_k1[l], lam_q2[l], lam_k2[l], subln_g[l],
                                      s5_ops[l], s5_d[l], s5_w_glu[l], with_ctx)
            x2 = _out_proj(mix.reshape(L, -1), w_out_bf[l], xb.reshape(L, D), g1)
            xb = _peer_residual(x2, norm2_g[l], sh2, sc2, g2, wq_bf[l], keys_bf[l], u_t[l], v_bf[l],
                                tokens_per_mod=L).reshape(1, L, D)
            if with_ctx:
                c2 = _out_proj(mixc.reshape(Lc, -1), w_out_bf[l], cb.reshape(Lc, D), cg1)
                cb = _peer_residual(c2, norm2_g[l], csh2, csc2, cg2, wq_bf[l], keys_bf[l],
                                    u_t[l], v_bf[l], tokens_per_mod=Lc).reshape(1, Lc, D)
        outs.append(xb)
    return jnp.concatenate(outs, axis=0)
```

```python
import functools
import math

import jax
import jax.numpy as jnp
from jax import lax
from jax.experimental import pallas as pl
from jax.experimental.pallas import tpu as pltpu
from jax.experimental.pallas import tpu_sc as plsc

D_MODEL = 1024
GRID_W = 64
EPS = 1e-6
ROPE_THETA = 10000.0
HEAD_DIM = 64
A_HEADS = 6
A_KV_HEADS = 2
A_GROUP = A_HEADS // A_KV_HEADS
A_Q = A_HEADS * HEAD_DIM
A_KV = A_KV_HEADS * HEAD_DIM
B_HEADS = 6
B_SUB = 32
B_VDIM = 2 * B_SUB
B_QK = B_HEADS * 2 * B_SUB
B_V = B_HEADS * B_VDIM
C_GROUPS = 16
C_GROUP_CH = 16
C_WIDTH = C_GROUPS * C_GROUP_CH
C_STATE = 64
PEER_HEADS = 8
N_KEYS = 128
PEER_TOPK = 16
PEER_HALF = 128
TOKEN_CHUNK = 128

LANES = 128
SUBLANES = 8
V_EXT = LANES
LOG2E = math.log2(math.e)
NEG_BIG = -1e30
VMEM_LIMIT = 48 * 1024 * 1024


TQ = 1024
ATTN_TK = 2816
K_PACK = LANES


def _flash_unit(s, v_t, m_ref, acc_ref, unit):
    m_old = m_ref[unit]
    m_new = jnp.maximum(m_old, jnp.max(s, axis=0, keepdims=True))
    alpha = jnp.exp2(m_old - m_new)
    p = jnp.exp2(s - m_new).astype(jnp.bfloat16)
    acc_ref[unit] = alpha * acc_ref[unit] + jnp.dot(v_t, p, preferred_element_type=jnp.float32)
    m_ref[unit] = m_new


def _flash_all_units(q_ref, k_ref, vt_ref, m_ref, acc_ref, *, n_units, units_per_group, tk, n_keys):
    m_ref[...] = jnp.full(m_ref.shape, NEG_BIG, jnp.float32)
    acc_ref[...] = jnp.zeros(acc_ref.shape, jnp.float32)

    def body(j, carry):
        off = pl.multiple_of(j * tk, tk)
        k = k_ref[0, pl.ds(off, tk), :]

        def scores(u):
            return jnp.dot(k, q_ref[0, u], preferred_element_type=jnp.float32)

        s = scores(0)
        for u in range(n_units):
            s_next = scores(u + 1) if u + 1 < n_units else None
            _flash_unit(s, vt_ref[0, u // units_per_group, :, pl.ds(off, tk)], m_ref, acc_ref, u)
            s = s_next
        return carry

    lax.fori_loop(0, n_keys // tk, body, 0)


def _gqa_kernel(q_ref, k_ref, vt_ref, o_ref, m_ref, acc_ref, *, tk, n_keys):
    _flash_all_units(q_ref, k_ref, vt_ref, m_ref, acc_ref,
                     n_units=A_HEADS, units_per_group=A_GROUP, tk=tk, n_keys=n_keys)
    for h in range(A_HEADS):
        acc = acc_ref[h]
        o_ref[0, h * HEAD_DIM:(h + 1) * HEAD_DIM, :] = (
            acc[:HEAD_DIM] / acc[HEAD_DIM:HEAD_DIM + 1]).astype(o_ref.dtype)


def _diff_kernel(lam_ref, q_ref, k_ref, vt_ref, g_ref, o_ref, m_ref, acc_ref, *, tk, n_keys, out_scale):
    _flash_all_units(q_ref, k_ref, vt_ref, m_ref, acc_ref,
                     n_units=4, units_per_group=2, tk=tk, n_keys=n_keys)
    lam = lam_ref[0]
    for h in range(2):
        a1 = acc_ref[2 * h]
        a2 = acc_ref[2 * h + 1]
        o = a1[:B_VDIM] / a1[B_VDIM:B_VDIM + 1] - lam * (a2[:B_VDIM] / a2[B_VDIM:B_VDIM + 1])
        o = o * lax.rsqrt(jnp.mean(o * o, axis=0, keepdims=True) + EPS)
        o_ref[0, h * B_VDIM:(h + 1) * B_VDIM, :] = (o * g_ref[...] * out_scale).astype(o_ref.dtype)


def _gqa_attention(q_t, k, v_t):
    B, _, _, L = q_t.shape
    S = k.shape[1]
    tq, tk = min(TQ, L), min(ATTN_TK, S)
    return pl.pallas_call(
        functools.partial(_gqa_kernel, tk=tk, n_keys=S),
        grid=(B, L // tq),
        in_specs=[
            pl.BlockSpec((1, A_HEADS, K_PACK, tq), lambda b, i: (b, 0, 0, i)),
            pl.BlockSpec((1, S, K_PACK), lambda b, i: (b, 0, 0)),
            pl.BlockSpec((1, A_KV_HEADS, V_EXT, S), lambda b, i: (b, 0, 0, 0)),
        ],
        out_specs=pl.BlockSpec((1, A_Q, tq), lambda b, i: (b, 0, i)),
        out_shape=jax.ShapeDtypeStruct((B, A_Q, L), jnp.bfloat16),
        scratch_shapes=[pltpu.VMEM((A_HEADS, 1, tq), jnp.float32),
                        pltpu.VMEM((A_HEADS, V_EXT, tq), jnp.float32)],
        compiler_params=pltpu.CompilerParams(
            dimension_semantics=("arbitrary", "arbitrary"), vmem_limit_bytes=VMEM_LIMIT),
        name="gqa_attention",
    )(q_t, k, v_t)


def _diff_attention(q_t, k, v_t, lam, subln_g, out_scale):
    B, _, _, L = q_t.shape
    S = k.shape[1]
    tq, tk = min(TQ, L), min(ATTN_TK, S)
    return pl.pallas_call(
        functools.partial(_diff_kernel, tk=tk, n_keys=S, out_scale=out_scale),
        grid=(B, B_HEADS // 2, L // tq),
        in_specs=[
            pl.BlockSpec(memory_space=pltpu.SMEM),
            pl.BlockSpec((1, 4, K_PACK, tq), lambda b, h, i: (b, h, 0, i)),
            pl.BlockSpec((1, S, K_PACK), lambda b, h, i: (b, 0, h)),
            pl.BlockSpec((1, 2, V_EXT, S), lambda b, h, i: (b, h, 0, 0)),
            pl.BlockSpec((B_VDIM, 1), lambda b, h, i: (0, 0)),
        ],
        out_specs=pl.BlockSpec((1, 2 * B_VDIM, tq), lambda b, h, i: (b, h, i)),
        out_shape=jax.ShapeDtypeStruct((B, B_V, L), jnp.bfloat16),
        scratch_shapes=[pltpu.VMEM((4, 1, tq), jnp.float32),
                        pltpu.VMEM((4, V_EXT, tq), jnp.float32)],
        compiler_params=pltpu.CompilerParams(
            dimension_semantics=("arbitrary", "arbitrary", "arbitrary"), vmem_limit_bytes=VMEM_LIMIT),
        name="diff_attention",
    )(lam.reshape(1).astype(jnp.float32), q_t, k, v_t, subln_g.reshape(B_VDIM, 1).astype(jnp.float32))


def _value_ext_t(v):
    B, S, H, d = v.shape
    ones = jnp.ones((B, S, H, 1), v.dtype)
    pad = jnp.zeros((B, S, H, V_EXT - d - 1), v.dtype)
    return jnp.transpose(jnp.concatenate([v, ones, pad], axis=-1), (0, 2, 3, 1)).astype(jnp.bfloat16)


def _query_ext_t(q, offsets):
    B, L, U, d = q.shape
    q_t = jnp.transpose(q, (0, 2, 3, 1)).astype(jnp.bfloat16)
    units = [jnp.pad(q_t[:, u], ((0, 0), (off, K_PACK - d - off), (0, 0))) for u, off in enumerate(offsets)]
    return jnp.stack(units, axis=1)


def _rms(x, g):
    xf = x.astype(jnp.float32)
    y = xf * lax.rsqrt(jnp.mean(xf * xf, axis=-1, keepdims=True) + EPS)
    return (y * g.astype(jnp.float32)).astype(x.dtype)


def _rope_2d(x, row, col):
    d = x.shape[-1]
    half = d // 2
    inv = 1.0 / (ROPE_THETA ** (jnp.arange(0, half, 2, dtype=jnp.float32) / half))
    ang = jnp.concatenate([row.astype(jnp.float32)[:, None] * inv[None, :],
                           col.astype(jnp.float32)[:, None] * inv[None, :]], axis=-1)
    cos = jnp.cos(ang)[None, :, None, :]
    sin = jnp.sin(ang)[None, :, None, :]
    xr = x.astype(jnp.float32).reshape(x.shape[:-1] + (half, 2))
    x0, x1 = xr[..., 0], xr[..., 1]
    out = jnp.stack([x0 * cos - x1 * sin, x0 * sin + x1 * cos], axis=-1)
    return out.reshape(x.shape).astype(x.dtype)


PROJ_TM = 512


S5_TC = 64
S5_ROW_TILE = 16
S5_CW = S5_TC * C_GROUP_CH
S5_SW = 2 * C_STATE


def _s5_operators(a_re, a_im, log_dt, b_re, b_im, c_re, c_im):
    f32 = jnp.float32
    tc, P, N, G = S5_TC, C_GROUP_CH, C_STATE, C_GROUPS
    j = jnp.arange(tc + 1, dtype=f32)
    kerns, p_parts, q_parts, a_parts = [], [], [], []
    for dirn in range(2):
        A = lax.complex(a_re[dirn].astype(f32), a_im[dirn].astype(f32))
        adt = A * jnp.exp(log_dt[dirn].astype(f32))[:, None]
        abar = jnp.exp(adt)
        bbar = ((abar - 1) / A)[..., None] * lax.complex(b_re[dirn].astype(f32), b_im[dirn].astype(f32))
        cmat = lax.complex(c_re[dirn].astype(f32), c_im[dirn].astype(f32))
        pw = jnp.exp(adt[None] * j[:, None, None])
        kerns.append(jnp.einsum('gpn,jgn,gnq->gqjp', cmat, pw[:tc], bbar).real)
        inj_pw = pw[:tc] if dirn == 1 else pw[:tc][::-1]
        inj = inj_pw[:, :, :, None] * bbar[None]
        p_parts.append(jnp.transpose(inj, (1, 0, 3, 2)).reshape(G, S5_CW, N))
        out_pw = pw[1:][::-1] if dirn == 1 else pw[1:]
        outm = cmat[None] * out_pw[:, :, None, :]
        q_parts.append(jnp.transpose(outm, (1, 3, 0, 2)).reshape(G, N, S5_CW))
        a_parts.append(pw[tc])
    f_seq = jnp.concatenate([jnp.flip(kerns[1][:, :, 1:], axis=2), kerns[0][:, :, :1] + kerns[1][:, :, :1],
                             kerns[0][:, :, 1:], jnp.zeros((G, P, 1, P), f32)], axis=2)
    rolled = jnp.tile(f_seq.astype(jnp.bfloat16).reshape(G, P, 2 * tc * P), (1, 1, tc))
    rolled = rolled[:, :, :tc * (2 * tc - 1) * P].reshape(G, P, tc, (2 * tc - 1) * P)
    toep = jnp.transpose(rolled[:, :, :, (tc - 1) * P:], (0, 2, 1, 3)).reshape(G, S5_CW, S5_CW)
    zeros = jnp.zeros((G, N, S5_CW), jnp.bfloat16)
    bf = lambda m: m.astype(jnp.bfloat16)
    w_mat = jnp.concatenate([
        toep,
        bf(q_parts[0].real), zeros, bf(-q_parts[0].imag), zeros,
        zeros, bf(q_parts[1].real), zeros, bf(-q_parts[1].imag),
    ], axis=1)
    p_mat = jnp.concatenate([p_parts[0].real, p_parts[1].real, p_parts[0].imag, p_parts[1].imag], axis=-1)
    a_chunk = jnp.stack([jnp.concatenate([a_parts[0].real, a_parts[1].real], axis=-1),
                         jnp.concatenate([a_parts[0].imag, a_parts[1].imag], axis=-1)], axis=1)
    return p_mat.astype(jnp.bfloat16), w_mat, a_chunk.astype(f32)


def _s5_scan_kernel(u_ref, p_ref, w_ref, a_ref, y_ref, s_scr, hs_scr, *, rows, order_f, order_r):
    f32 = jnp.float32
    u = u_ref[0]
    s_scr[...] = jnp.dot(u, p_ref[0], preferred_element_type=f32)
    hs_scr[...] = jnp.zeros(hs_scr.shape, f32)
    a_re = a_ref[0, 0:1, :]
    a_im = a_ref[0, 1:2, :]
    is_fwd = lax.broadcasted_iota(jnp.int32, (rows, S5_SW), 1) < C_STATE
    h_re = jnp.zeros((rows, S5_SW), f32)
    h_im = jnp.zeros((rows, S5_SW), f32)
    for cf, cr in zip(order_f, order_r):
        rf = slice(cf * rows, (cf + 1) * rows)
        rr = slice(cr * rows, (cr + 1) * rows)
        hs_scr[rf, 0:S5_SW] = h_re
        hs_scr[rf, S5_SW:2 * S5_SW] = h_im
        hs_scr[rr, 2 * S5_SW:3 * S5_SW] = h_re
        hs_scr[rr, 3 * S5_SW:4 * S5_SW] = h_im
        s_re = jnp.where(is_fwd, s_scr[rf, 0:S5_SW], s_scr[rr, 0:S5_SW])
        s_im = jnp.where(is_fwd, s_scr[rf, S5_SW:2 * S5_SW], s_scr[rr, S5_SW:2 * S5_SW])
        h_re, h_im = a_re * h_re - a_im * h_im + s_re, a_re * h_im + a_im * h_re + s_im
    y = jnp.dot(u, w_ref[0, 0:S5_CW, :], preferred_element_type=f32)
    y += jnp.dot(hs_scr[...].astype(jnp.bfloat16), w_ref[0, S5_CW:, :], preferred_element_type=f32)
    y_ref[0] = y


def _s5_scan(u_seq, n_ctx_chunks, p_mat, w_mat, a_chunk):
    B, S, _ = u_seq.shape
    G, P = C_GROUPS, C_GROUP_CH
    nc = S // S5_TC
    R = -(-(nc * B) // S5_ROW_TILE) * S5_ROW_TILE
    ug = jnp.transpose(u_seq.reshape(B, nc, S5_TC, G, P), (3, 1, 0, 2, 4))
    ug = jnp.pad(ug.reshape(G, nc * B, S5_CW), ((0, 0), (0, R - nc * B), (0, 0))).astype(jnp.bfloat16)
    order_f = tuple(range(nc))
    order_r = tuple(range(n_ctx_chunks - 1, -1, -1)) + tuple(range(nc - 1, n_ctx_chunks - 1, -1))
    y = pl.pallas_call(
        functools.partial(_s5_scan_kernel, rows=B, order_f=order_f, order_r=order_r),
        grid=(G,),
        in_specs=[
            pl.BlockSpec((1, R, S5_CW), lambda g: (g, 0, 0)),
            pl.BlockSpec((1, S5_CW, 2 * S5_SW), lambda g: (g, 0, 0)),
            pl.BlockSpec((1, S5_CW + 4 * S5_SW, S5_CW), lambda g: (g, 0, 0)),
            pl.BlockSpec((1, 2, S5_SW), lambda g: (g, 0, 0)),
        ],
        out_specs=pl.BlockSpec((1, R, S5_CW), lambda g: (g, 0, 0)),
        out_shape=jax.ShapeDtypeStruct((G, R, S5_CW), jnp.float32),
        scratch_shapes=[pltpu.VMEM((R, 2 * S5_SW), jnp.float32),
                        pltpu.VMEM((R, 4 * S5_SW), jnp.float32)],
        compiler_params=pltpu.CompilerParams(dimension_semantics=("arbitrary",),
                                             vmem_limit_bytes=VMEM_LIMIT),
        name="s5_scan",
    )(ug, p_mat, w_mat, a_chunk)
    y = y[:, :nc * B].reshape(G, nc, B, S5_TC, P)
    return jnp.transpose(y, (2, 1, 3, 0, 4)).reshape(B, S, C_WIDTH)


def _s5_glu_kernel(y_ref, u_ref, d_ref, w_ref, o_ref):
    y = y_ref[...] + d_ref[...] * u_ref[...]
    g = jax.nn.gelu(y)
    z = jnp.dot(g.astype(jnp.bfloat16), w_ref[...], preferred_element_type=jnp.float32)
    o_ref[...] = (g * jax.nn.sigmoid(z)).astype(o_ref.dtype)


def _s5_glu(y, u, d_skip, w_glu):
    T = y.shape[0]
    tb = math.gcd(T, 1024)
    return pl.pallas_call(
        _s5_glu_kernel,
        grid=(T // tb,),
        in_specs=[pl.BlockSpec((tb, C_WIDTH), lambda i: (i, 0)),
                  pl.BlockSpec((tb, C_WIDTH), lambda i: (i, 0)),
                  pl.BlockSpec((1, C_WIDTH), lambda i: (0, 0)),
                  pl.BlockSpec((C_WIDTH, C_WIDTH), lambda i: (0, 0))],
        out_specs=pl.BlockSpec((tb, C_WIDTH), lambda i: (i, 0)),
        out_shape=jax.ShapeDtypeStruct((T, C_WIDTH), jnp.float32),
        compiler_params=pltpu.CompilerParams(dimension_semantics=("arbitrary",)),
        name="s5_glu",
    )(y, u, d_skip.reshape(1, C_WIDTH).astype(jnp.float32), w_glu.astype(jnp.bfloat16))


def _s5_mixer(u, uc, s5_ops, d_skip, w_glu, with_ctx):
    B, L, _ = u.shape
    Lc = uc.shape[1]
    p_mat, w_mat, a_chunk = s5_ops
    u_seq = jnp.concatenate([uc, u], axis=1).astype(jnp.float32)
    y = _s5_scan(u_seq, Lc // S5_TC, p_mat, w_mat, a_chunk)
    if with_ctx:
        out = _s5_glu(y.reshape(B * (Lc + L), C_WIDTH), u_seq.reshape(B * (Lc + L), C_WIDTH), d_skip, w_glu)
        out = out.reshape(B, Lc + L, C_WIDTH)
        return out[:, Lc:], out[:, :Lc]
    out = _s5_glu(y[:, Lc:].reshape(B * L, C_WIDTH), u.astype(jnp.float32).reshape(B * L, C_WIDTH), d_skip, w_glu)
    return out.reshape(B, L, C_WIDTH), None


def _in_proj_kernel(x_ref, g_ref, sh_ref, sc_ref, w_ref, o_ref):
    x = x_ref[...]
    y = x * lax.rsqrt(jnp.mean(x * x, axis=-1, keepdims=True) + EPS) * g_ref[...]
    h = (y * (1.0 + sc_ref[0]) + sh_ref[0]).astype(jnp.bfloat16)
    o_ref[...] = jnp.dot(h, w_ref[...], preferred_element_type=jnp.float32)


def _in_proj(x, norm_g, shift, scale, w_bf):
    T, D = x.shape
    N = w_bf.shape[1]
    tm = min(PROJ_TM, T)
    return pl.pallas_call(
        _in_proj_kernel,
        grid=(T // tm,),
        in_specs=[pl.BlockSpec((tm, D), lambda i: (i, 0)),
                  pl.BlockSpec((1, D), lambda i: (0, 0)),
                  pl.BlockSpec((1, 1, D), lambda i: (0, 0, 0)),
                  pl.BlockSpec((1, 1, D), lambda i: (0, 0, 0)),
                  pl.BlockSpec((D, N), lambda i: (0, 0))],
        out_specs=pl.BlockSpec((tm, N), lambda i: (i, 0)),
        out_shape=jax.ShapeDtypeStruct((T, N), jnp.float32),
        compiler_params=pltpu.CompilerParams(dimension_semantics=("arbitrary",), vmem_limit_bytes=VMEM_LIMIT),
        name="in_proj",
    )(x, norm_g.reshape(1, D).astype(jnp.float32), shift, scale, w_bf)


def _out_proj_kernel(m_ref, w_ref, x_ref, g_ref, o_ref):
    o_ref[...] = x_ref[...] + g_ref[0] * jnp.dot(m_ref[...], w_ref[...], preferred_element_type=jnp.float32)


def _out_proj(mix_bf, w_bf, x, gate):
    T, D = x.shape
    K = mix_bf.shape[1]
    tm = min(PROJ_TM, T)
    return pl.pallas_call(
        _out_proj_kernel,
        grid=(T // tm,),
        in_specs=[pl.BlockSpec((tm, K), lambda i: (i, 0)),
                  pl.BlockSpec((K, D), lambda i: (0, 0)),
                  pl.BlockSpec((tm, D), lambda i: (i, 0)),
                  pl.BlockSpec((1, 1, D), lambda i: (0, 0, 0))],
        out_specs=pl.BlockSpec((tm, D), lambda i: (i, 0)),
        out_shape=jax.ShapeDtypeStruct((T, D), jnp.float32),
        compiler_params=pltpu.CompilerParams(dimension_semantics=("arbitrary",), vmem_limit_bytes=VMEM_LIMIT),
        name="out_proj",
    )(mix_bf, w_bf, x, gate)


def _ada_proj_kernel(s_ref, w_ref, b_ref, o_ref):
    o_ref[...] = jnp.dot(s_ref[...].astype(jnp.bfloat16), w_ref[...], preferred_element_type=jnp.float32) + b_ref[...]


def _ada_proj(s, w_bf, bias):
    M, D = s.shape
    N = w_bf.shape[1]
    rows = -(-M // SUBLANES) * SUBLANES
    tn = N // 6
    out = pl.pallas_call(
        _ada_proj_kernel,
        grid=(N // tn,),
        in_specs=[pl.BlockSpec((rows, D), lambda j: (0, 0)),
                  pl.BlockSpec((D, tn), lambda j: (0, j)),
                  pl.BlockSpec((1, tn), lambda j: (0, j))],
        out_specs=pl.BlockSpec((rows, tn), lambda j: (0, j)),
        out_shape=jax.ShapeDtypeStruct((rows, N), jnp.float32),
        compiler_params=pltpu.CompilerParams(dimension_semantics=("arbitrary",)),
        name="ada_proj",
    )(jnp.pad(s, ((0, rows - M), (0, 0))), w_bf, bias.reshape(1, N).astype(jnp.float32))
    return out[:M]


def _token_mixers(proj, projc, row, col, layer_idx, qn_a, kn_a, qn_b, kn_b,
                  lam_q1, lam_k1, lam_q2, lam_k2, subln_g, s5_ops, d_skip, w_glu, with_ctx):
    B, L, _ = proj.shape
    Lc = projc.shape[1]
    cuts = [A_Q, A_Q + A_KV, A_Q + 2 * A_KV, A_Q + 2 * A_KV + B_QK,
            A_Q + 2 * A_KV + 2 * B_QK, A_Q + 2 * A_KV + 2 * B_QK + B_V]
    qa, ka, va, qb, kb, vb, us = jnp.split(proj, cuts, axis=-1)
    qac, kac, vac, qbc, kbc, vbc, usc = jnp.split(projc, cuts, axis=-1)

    qa = _rope_2d(_rms(qa.reshape(B, L, A_HEADS, HEAD_DIM), qn_a), row, col)
    ka = _rope_2d(_rms(ka.reshape(B, L, A_KV_HEADS, HEAD_DIM), kn_a), row, col)
    va = va.reshape(B, L, A_KV_HEADS, HEAD_DIM)
    kac = _rms(kac.reshape(B, Lc, A_KV_HEADS, HEAD_DIM), kn_a)
    vac = vac.reshape(B, Lc, A_KV_HEADS, HEAD_DIM)
    ka_all = jnp.concatenate([kac, ka], axis=1)
    va_all = jnp.concatenate([vac, va], axis=1)
    qa_t = _query_ext_t(qa * (HEAD_DIM ** -0.5 * LOG2E), [(hq // A_GROUP) * HEAD_DIM for hq in range(A_HEADS)])
    ka_pack = ka_all.reshape(B, Lc + L, A_KV).astype(jnp.bfloat16)
    o_a = jnp.transpose(_gqa_attention(qa_t, ka_pack, _value_ext_t(va_all)), (0, 2, 1))

    lambda_init = 0.8 - 0.6 * math.exp(-0.3 * layer_idx)
    lam = (jnp.exp(jnp.sum(lam_q1.astype(jnp.float32) * lam_k1.astype(jnp.float32)))
           - jnp.exp(jnp.sum(lam_q2.astype(jnp.float32) * lam_k2.astype(jnp.float32))) + lambda_init)
    qb = _rope_2d(_rms(qb.reshape(B, L, 2 * B_HEADS, B_SUB), qn_b), row, col)
    kb = _rope_2d(_rms(kb.reshape(B, L, 2 * B_HEADS, B_SUB), kn_b), row, col)
    vb = vb.reshape(B, L, B_HEADS, B_VDIM)
    kbc = _rms(kbc.reshape(B, Lc, 2 * B_HEADS, B_SUB), kn_b)
    vbc = vbc.reshape(B, Lc, B_HEADS, B_VDIM)
    kb_all = jnp.concatenate([kbc, kb], axis=1)
    vb_all = jnp.concatenate([vbc, vb], axis=1)
    qb_t = _query_ext_t(qb * (B_SUB ** -0.5 * LOG2E), [(j % 4) * B_SUB for j in range(2 * B_HEADS)])
    kb_pack = kb_all.reshape(B, Lc + L, B_QK).astype(jnp.bfloat16)
    o_b = jnp.transpose(_diff_attention(qb_t, kb_pack, _value_ext_t(vb_all), lam, subln_g, 1 - lambda_init),
                        (0, 2, 1))

    o_c, o_cc = _s5_mixer(us, usc, s5_ops, d_skip, w_glu, with_ctx)

    mix = jnp.concatenate([o_a, o_b, o_c.astype(jnp.bfloat16)], axis=-1)
    if not with_ctx:
        return mix, None
    qac = _rms(qac.reshape(B, Lc, A_HEADS, HEAD_DIM), qn_a)
    qac_t = _query_ext_t(qac * (HEAD_DIM ** -0.5 * LOG2E), [(hq // A_GROUP) * HEAD_DIM for hq in range(A_HEADS)])
    o_ac = jnp.transpose(_gqa_attention(qac_t, kac.reshape(B, Lc, A_KV).astype(jnp.bfloat16), _value_ext_t(vac)),
                         (0, 2, 1))
    qbc = _rms(qbc.reshape(B, Lc, 2 * B_HEADS, B_SUB), qn_b)
    qbc_t = _query_ext_t(qbc * (B_SUB ** -0.5 * LOG2E), [(j % 4) * B_SUB for j in range(2 * B_HEADS)])
    o_bc = jnp.transpose(_diff_attention(qbc_t, kbc.reshape(B, Lc, B_QK).astype(jnp.bfloat16), _value_ext_t(vbc),
                                         lam, subln_g, 1 - lambda_init), (0, 2, 1))
    mixc = jnp.concatenate([o_ac, o_bc, o_cc.astype(jnp.bfloat16)], axis=-1)
    return mix, mixc


PEER_TB = 256
N_SLOTS = PEER_HEADS * PEER_TOPK
CAND_ROWS = 80
INVALID_FLAT = 1.0e9
SC_CORES = 2
SC_SUBCORES = 16


def _cand_flat_table():
    f = [float(b) for b in range(16)]
    for a in range(1, 8):
        f += [float(a * 16 + b) if (a + 1) * (b + 1) <= 16 else INVALID_FLAT for b in range(8)]
    f += [float(a * 16) for a in range(8, 16)]
    return jnp.broadcast_to(jnp.asarray(f, jnp.float32)[:, None], (CAND_ROWS, PEER_TB))


def _pair_rows(first, second):
    blocks = [first[0:1] + second]
    blocks += [first[a:a + 1] + second[0:8] for a in range(1, 8)]
    blocks += [first[8:16] + second[0:1]]
    return jnp.concatenate(blocks, axis=0)


N_EXPERTS = N_KEYS * N_KEYS


def _peer_route_kernel(x_ref, g_ref, sh_ref, sc_ref, wq_ref, keys_ref, ftab_ref, ut_ref,
                       idx_ref, gate_ref, s_ref, h_scr, q_scr, val_scr, id_scr, sc_scr, e_scr):
    f32 = jnp.float32
    head = pl.program_id(1)

    @pl.when(head == 0)
    def _():
        x = x_ref[...]
        y = x * lax.rsqrt(jnp.mean(x * x, axis=-1, keepdims=True) + EPS) * g_ref[...]
        h2 = (y * (1.0 + sc_ref[0]) + sh_ref[0]).astype(jnp.bfloat16)
        h_scr[...] = h2
        q = jnp.dot(h2, wq_ref[...], preferred_element_type=f32)
        for ht in range(2 * PEER_HEADS):
            q_scr[ht] = q[:, ht * PEER_HALF:(ht + 1) * PEER_HALF].astype(jnp.bfloat16)

    sub_scores = [lax.dot_general(keys_ref[head * 2 + t], q_scr[head * 2 + t], (((1,), (1,)), ((), ())),
                                  preferred_element_type=f32) for t in range(2)]
    s_ref[...] = jnp.dot(h_scr[...], ut_ref[...], preferred_element_type=f32)

    ftab = ftab_ref[...]
    key_id = lax.broadcasted_iota(jnp.int32, (N_KEYS, PEER_TB), 0).astype(f32)
    neg_inf = -jnp.inf
    for t in range(2):
        s = sub_scores[t]
        for r in range(PEER_TOPK):
            m = jnp.max(s, axis=0, keepdims=True)
            pick = jnp.min(jnp.where(s == m, key_id, float(N_KEYS)), axis=0, keepdims=True)
            s = jnp.where(key_id == pick, neg_inf, s)
            val_scr[t, r:r + 1, :] = m
            id_scr[t, r:r + 1, :] = pick
    cand = _pair_rows(val_scr[0], val_scr[1])
    cand = jnp.where(ftab < float(PEER_TOPK * PEER_TOPK), cand, neg_inf)
    expert = _pair_rows(id_scr[0] * float(N_KEYS), id_scr[1])
    for r in range(PEER_TOPK):
        m = jnp.max(cand, axis=0, keepdims=True)
        pick = jnp.min(jnp.where(cand == m, ftab, INVALID_FLAT), axis=0, keepdims=True)
        hit = ftab == pick
        sc_scr[r:r + 1, :] = m
        e_scr[r:r + 1, :] = jnp.sum(jnp.where(hit, expert, 0.0), axis=0, keepdims=True)
        cand = jnp.where(hit, neg_inf, cand)
    sc = sc_scr[...]
    p = jnp.exp(sc - sc[0:1])
    rows = pl.ds(pl.multiple_of(head * PEER_TOPK, PEER_TOPK), PEER_TOPK)
    idx_ref[rows, :] = e_scr[...].astype(jnp.int32)
    gate_ref[rows, :] = p / jnp.sum(p, axis=0, keepdims=True)


def _peer_route(x, norm_g, shift, scale, wq_bf, keys_bf, u_t, tokens_per_mod):
    T, D = x.shape
    nq = wq_bf.shape[1]
    te = N_EXPERTS // PEER_HEADS
    blocks_per_mod = tokens_per_mod // PEER_TB
    return pl.pallas_call(
        _peer_route_kernel,
        grid=(T // PEER_TB, PEER_HEADS),
        in_specs=[
            pl.BlockSpec((PEER_TB, D), lambda i, j: (i, 0)),
            pl.BlockSpec((1, D), lambda i, j: (0, 0)),
            pl.BlockSpec((1, 1, D), lambda i, j: (i // blocks_per_mod, 0, 0)),
            pl.BlockSpec((1, 1, D), lambda i, j: (i // blocks_per_mod, 0, 0)),
            pl.BlockSpec((D, nq), lambda i, j: (0, 0)),
            pl.BlockSpec((2 * PEER_HEADS, N_KEYS, PEER_HALF), lambda i, j: (0, 0, 0)),
            pl.BlockSpec((CAND_ROWS, PEER_TB), lambda i, j: (0, 0)),
            pl.BlockSpec((D, te), lambda i, j: (0, j)),
        ],
        out_specs=[
            pl.BlockSpec((N_SLOTS, PEER_TB), lambda i, j: (0, i)),
            pl.BlockSpec((N_SLOTS, PEER_TB), lambda i, j: (0, i)),
            pl.BlockSpec((PEER_TB, te), lambda i, j: (i, j)),
        ],
        out_shape=[
            jax.ShapeDtypeStruct((N_SLOTS, T), jnp.int32),
            jax.ShapeDtypeStruct((N_SLOTS, T), jnp.float32),
            jax.ShapeDtypeStruct((T, N_EXPERTS), jnp.float32),
        ],
        scratch_shapes=[pltpu.VMEM((PEER_TB, D), jnp.bfloat16),
                        pltpu.VMEM((2 * PEER_HEADS, PEER_TB, PEER_HALF), jnp.bfloat16),
                        pltpu.VMEM((2, PEER_TOPK, PEER_TB), jnp.float32),
                        pltpu.VMEM((2, PEER_TOPK, PEER_TB), jnp.float32),
                        pltpu.VMEM((PEER_TOPK, PEER_TB), jnp.float32),
                        pltpu.VMEM((PEER_TOPK, PEER_TB), jnp.float32)],
        compiler_params=pltpu.CompilerParams(dimension_semantics=("arbitrary", "arbitrary"),
                                             vmem_limit_bytes=VMEM_LIMIT),
        name="peer_route",
    )(x, norm_g.reshape(1, D).astype(jnp.float32), shift, scale, wq_bf, keys_bf, _cand_flat_table(), u_t)
SC_LANES = 16
SC_TOKENS = 32
DENSE_TM = 1024
DENSE_TE = 2048


def _sc_worker_tokens(n_tokens):
    per_worker = n_tokens // (SC_CORES * SC_SUBCORES)
    blk = min(SC_TOKENS, per_worker)
    return per_worker, blk


def _sc_extract(scores, idx_flat):
    T = scores.shape[0]
    per_worker, blk = _sc_worker_tokens(T)
    mesh = plsc.VectorSubcoreMesh(core_axis_name="c", subcore_axis_name="s")

    def body(s_hbm, idx_hbm, out_hbm, row0_v, row1_v, idx_v, out_v, rsem):
        rows = (row0_v, row1_v)
        wid = lax.axis_index("s") * SC_CORES + lax.axis_index("c")
        t0 = wid * per_worker

        def row_copy(tok, slot):
            return pltpu.make_async_copy(s_hbm.at[tok], rows[slot], rsem.at[slot])

        def block(b, carry):
            tb = t0 + b * blk
            pltpu.sync_copy(idx_hbm.at[pl.ds(tb * N_SLOTS, blk * N_SLOTS)], idx_v)
            row_copy(tb, 0).start()

            def pair(i2, c):
                for slot in range(2):
                    i = i2 * 2 + slot

                    @pl.when(i + 1 < blk)
                    def _():
                        row_copy(tb + i + 1, 1 - slot).start()

                    row_copy(tb + i, slot).wait()
                    for j in range(N_SLOTS // SC_LANES):
                        at = pl.ds(i * N_SLOTS + j * SC_LANES, SC_LANES)
                        out_v[at] = plsc.load_gather(rows[slot], [idx_v[at]])
                return c

            lax.fori_loop(0, blk // 2, pair, 0)
            pltpu.sync_copy(out_v, out_hbm.at[pl.ds(tb * N_SLOTS, blk * N_SLOTS)])
            return carry

        lax.fori_loop(0, per_worker // blk, block, 0)

    return pl.kernel(
        body, mesh=mesh,
        out_type=jax.ShapeDtypeStruct((T * N_SLOTS,), jnp.float32),
        scratch_types=[pltpu.VMEM((N_EXPERTS,), jnp.float32), pltpu.VMEM((N_EXPERTS,), jnp.float32),
                       pltpu.VMEM((blk * N_SLOTS,), jnp.int32), pltpu.VMEM((blk * N_SLOTS,), jnp.float32),
                       pltpu.SemaphoreType.DMA((2,))],
        compiler_params=pltpu.CompilerParams(needs_layout_passes=False),
        name="peer_extract",
    )(scores, idx_flat)


def _sc_scatter(w_flat, idx_flat, n_tokens):
    T = n_tokens
    per_worker, blk = _sc_worker_tokens(T)
    n_vec = N_SLOTS // SC_LANES
    mesh = plsc.VectorSubcoreMesh(core_axis_name="c", subcore_axis_name="s")

    def body(w_hbm, idx_hbm, out_hbm, row0_v, row1_v, idx_v, w_v, wsem):
        rows = (row0_v, row1_v)
        wid = lax.axis_index("s") * SC_CORES + lax.axis_index("c")
        t0 = wid * per_worker
        zero = jnp.zeros((SC_LANES,), jnp.float32)

        def zero_fill(k, c):
            for slot in range(2):
                rows[slot][pl.ds(k * SC_LANES, SC_LANES)] = zero
            return c

        lax.fori_loop(0, N_EXPERTS // SC_LANES, zero_fill, 0)

        def out_copy(tok, slot):
            return pltpu.make_async_copy(rows[slot], out_hbm.at[tok], wsem.at[slot])

        def retire(tok, i, slot):
            out_copy(tok, slot).wait()
            for j in range(n_vec):
                plsc.store_scatter(rows[slot], [idx_v[pl.ds(i * N_SLOTS + j * SC_LANES, SC_LANES)]], zero)

        def block(b, carry):
            tb = t0 + b * blk
            pltpu.sync_copy(idx_hbm.at[pl.ds(tb * N_SLOTS, blk * N_SLOTS)], idx_v)
            pltpu.sync_copy(w_hbm.at[pl.ds(tb * N_SLOTS, blk * N_SLOTS)], w_v)

            def pair(i2, c):
                for slot in range(2):
                    i = i2 * 2 + slot

                    @pl.when(i >= 2)
                    def _():
                        retire(tb + i - 2, i - 2, slot)

                    for j in range(n_vec):
                        at = pl.ds(i * N_SLOTS + j * SC_LANES, SC_LANES)
                        plsc.addupdate_scatter(rows[slot], [idx_v[at]], w_v[at])
                    out_copy(tb + i, slot).start()
                return c

            lax.fori_loop(0, blk // 2, pair, 0)
            for slot in range(2):
                retire(tb + blk - 2 + slot, blk - 2 + slot, slot)
            return carry

        lax.fori_loop(0, per_worker // blk, block, 0)

    return pl.kernel(
        body, mesh=mesh,
        out_type=jax.ShapeDtypeStruct((T, N_EXPERTS), jnp.float32),
        scratch_types=[pltpu.VMEM((N_EXPERTS,), jnp.float32), pltpu.VMEM((N_EXPERTS,), jnp.float32),
                       pltpu.VMEM((blk * N_SLOTS,), jnp.int32), pltpu.VMEM((blk * N_SLOTS,), jnp.float32),
                       pltpu.SemaphoreType.DMA((2,))],
        compiler_params=pltpu.CompilerParams(needs_layout_passes=False),
        name="peer_scatter",
    )(w_flat, idx_flat)


def _peer_gate_kernel(a_ref, g_ref, w_ref):
    w_ref[...] = jax.nn.gelu(a_ref[...]) * g_ref[...]


def _peer_gate(act, gate):
    T = act.shape[0]
    tb = min(2048, T)
    spec = pl.BlockSpec((tb, N_SLOTS), lambda i: (i, 0))
    return pl.pallas_call(
        _peer_gate_kernel, grid=(T // tb,), in_specs=[spec, spec], out_specs=spec,
        out_shape=jax.ShapeDtypeStruct((T, N_SLOTS), jnp.float32),
        compiler_params=pltpu.CompilerParams(dimension_semantics=("arbitrary",)),
        name="peer_gate",
    )(act, gate)


def _peer_combine_kernel(w_ref, v_ref, x_ref, g2_ref, o_ref, acc_ref):
    k = pl.program_id(1)

    @pl.when(k == 0)
    def _():
        acc_ref[...] = jnp.zeros(acc_ref.shape, jnp.float32)

    acc_ref[...] += jnp.dot(w_ref[...].astype(jnp.bfloat16), v_ref[...], preferred_element_type=jnp.float32)

    @pl.when(k == pl.num_programs(1) - 1)
    def _():
        o_ref[...] = x_ref[...] + g2_ref[0] * acc_ref[...]


def _peer_combine(w_dense, v_bf, x, g2, tokens_per_mod):
    T, D = x.shape
    tm = min(DENSE_TM, T)
    blocks_per_mod = tokens_per_mod // tm
    return pl.pallas_call(
        _peer_combine_kernel,
        grid=(T // tm, N_EXPERTS // DENSE_TE),
        in_specs=[pl.BlockSpec((tm, DENSE_TE), lambda i, k: (i, k)),
                  pl.BlockSpec((DENSE_TE, D), lambda i, k: (k, 0)),
                  pl.BlockSpec((tm, D), lambda i, k: (i, 0)),
                  pl.BlockSpec((1, 1, D), lambda i, k: (i // blocks_per_mod, 0, 0))],
        out_specs=pl.BlockSpec((tm, D), lambda i, k: (i, 0)),
        out_shape=jax.ShapeDtypeStruct((T, D), jnp.float32),
        scratch_shapes=[pltpu.VMEM((tm, D), jnp.float32)],
        compiler_params=pltpu.CompilerParams(dimension_semantics=("arbitrary", "arbitrary"),
                                             vmem_limit_bytes=VMEM_LIMIT),
        name="peer_combine",
    )(w_dense, v_bf, x, g2)


def _peer_residual(x, norm_g, shift, scale, gate2, wq_bf, keys_bf, u_t, v_bf, tokens_per_mod):
    T, D = x.shape
    idx_t, gate_t, scores = _peer_route(x, norm_g, shift, scale, wq_bf, keys_bf, u_t, tokens_per_mod)
    idx_flat = jnp.transpose(idx_t).reshape(T * N_SLOTS)
    act = _sc_extract(scores, idx_flat).reshape(T, N_SLOTS)
    w = _peer_gate(act, jnp.transpose(gate_t))
    w_dense = _sc_scatter(w.reshape(T * N_SLOTS), idx_flat, T)
    return _peer_combine(w_dense, v_bf, x, gate2, tokens_per_mod)


def kernel(x, c, ctx, c_ctx, w_ada, b_ada, norm1_g, norm2_g, w_in, w_out,
           qn_a, kn_a, qn_b, kn_b, lam_q1, lam_k1, lam_q2, lam_k2, subln_g,
           s5_a_re, s5_a_im, s5_log_dt, s5_b_re, s5_b_im, s5_c_re, s5_c_im,
           s5_d, s5_w_glu, peer_wq, peer_keys, peer_u, peer_v):
    B, L, D = x.shape
    Lc = ctx.shape[1]
    depth = w_in.shape[0]
    rows = L // GRID_W
    row = jnp.repeat(jnp.arange(rows, dtype=jnp.int32), GRID_W)
    col = jnp.tile(jnp.arange(GRID_W, dtype=jnp.int32), rows)
    s_c = jax.nn.silu(c)
    s_cc = jax.nn.silu(c_ctx)
    mods, cmods, s5_ops, wq_bf, keys_bf, u_t, v_bf = [], [], [], [], [], [], []
    w_in_bf = w_in.astype(jnp.bfloat16)
    w_out_bf = w_out.astype(jnp.bfloat16)
    for l in range(depth):
        ada = _ada_proj(jnp.concatenate([s_c, s_cc[None, :]], axis=0), w_ada[l].astype(jnp.bfloat16), b_ada[l])
        mods.append(jnp.split(ada[:B, None, :], 6, axis=-1))
        cmods.append([m.reshape(1, 1, D) for m in jnp.split(ada[B], 6, axis=-1)])
        s5_ops.append(_s5_operators(s5_a_re[l], s5_a_im[l], s5_log_dt[l], s5_b_re[l], s5_b_im[l],
                                    s5_c_re[l], s5_c_im[l]))
        wq_bf.append(peer_wq[l].astype(jnp.bfloat16))
        keys_bf.append(peer_keys[l].reshape(2 * PEER_HEADS, N_KEYS, PEER_HALF).astype(jnp.bfloat16))
        u_t.append(jnp.transpose(peer_u[l].astype(jnp.bfloat16)))
        v_bf.append(peer_v[l].astype(jnp.bfloat16))

    outs = []
    for b in range(B):
        xb = x[b:b + 1]
        cb = ctx[b:b + 1]
        for l in range(depth):
            with_ctx = l < depth - 1
            sh1, sc1, g1, sh2, sc2, g2 = [m[b:b + 1] for m in mods[l]]
            csh1, csc1, cg1, csh2, csc2, cg2 = cmods[l]
            proj = _in_proj(xb.reshape(L, D), norm1_g[l], sh1, sc1, w_in_bf[l]).reshape(1, L, -1)
            projc = _in_proj(cb.reshape(Lc, D), norm1_g[l], csh1, csc1, w_in_bf[l]).reshape(1, Lc, -1)
            mix, mixc = _token_mixers(proj, projc, row, col, l, qn_a[l], kn_a[l], qn_b[l], kn_b[l],
                                      lam_q1[l], lam_k1[l], lam_q2[l], lam_k2[l], subln_g[l],
                                      s5_ops[l], s5_d[l], s5_w_glu[l], with_ctx)
            x2 = _out_proj(mix.reshape(L, -1), w_out_bf[l], xb.reshape(L, D), g1)
            xb = _peer_residual(x2, norm2_g[l], sh2, sc2, g2, wq_bf[l], keys_bf[l], u_t[l], v_bf[l],
                                tokens_per_mod=L).reshape(1, L, D)
            if with_ctx:
                c2 = _out_proj(mixc.reshape(Lc, -1), w_out_bf[l], cb.reshape(Lc, D), cg1)
                cb = _peer_residual(c2, norm2_g[l], csh2, csc2, cg2, wq_bf[l], keys_bf[l],
                                    u_t[l], v_bf[l], tokens_per_mod=Lc).reshape(1, Lc, D)
        outs.append(xb)
    return jnp.concatenate(outs, axis=0)
```
